```python
import math
import jax, jax.numpy as jnp
from jax import lax
import numpy as np

D_MODEL = 2048
BATCH = 8
SEQ = 4096
DEPTH = 4

HEAD_DIM = 128
A_PAIRS = ((128, 1), (512, 4), (2048, 16))
A_GROUPS = len(A_PAIRS)
A_HEADS = D_MODEL // 256
A_OUT = A_HEADS * HEAD_DIM
A_COLS = A_GROUPS * 3 * A_HEADS * HEAD_DIM
B_CHUNK = 128
B_GROUPS = D_MODEL // 256
B_WIDTH = B_GROUPS * HEAD_DIM
B_COLS = 2 * B_WIDTH
MEM_TOKENS = 256
MEM_HEADS = 4
MEM_WIDTH = MEM_HEADS * HEAD_DIM
MIX_OUT = A_OUT + MEM_WIDTH
D_FF = 4 * D_MODEL
N_LAYERS_A = (DEPTH + 1) // 2
N_LAYERS_B = DEPTH // 2
ALPHA = (2.0 * DEPTH) ** 0.25
BETA = (8.0 * DEPTH) ** -0.25
LN_EPS = 1e-5

kernel_name = "hybrid_dilated_gmlp_memory_deepnorm"


def _layer_norm(x, g, b):
    xf = x.astype(jnp.float32)
    mu = jnp.mean(xf, axis=-1, keepdims=True)
    var = jnp.mean(jnp.square(xf - mu), axis=-1, keepdims=True)
    y = (xf - mu) * lax.rsqrt(var + LN_EPS) * g.astype(jnp.float32) + b.astype(jnp.float32)
    return y.astype(x.dtype)


def _dilated_branch(q, k, v, window, dilation):
    b, s, h, dh = q.shape
    w_sub = window // dilation
    blk = w_sub
    L = s // dilation
    nb = -(-L // blk)
    Lp = nb * blk

    def to_sub(t):
        t = t.reshape(b, L, dilation, h, dh).transpose(0, 2, 1, 3, 4)
        t = jnp.pad(t, ((0, 0), (0, 0), (0, Lp - L), (0, 0), (0, 0)))
        return t.reshape(b, dilation, nb, blk, h, dh)

    qb, kb, vb = to_sub(q), to_sub(k), to_sub(v)

    def with_prev(t):
        prev = jnp.pad(t[:, :, :-1], ((0, 0), (0, 0), (1, 0), (0, 0), (0, 0), (0, 0)))
        return jnp.concatenate([prev, t], axis=3)

    kk, vv = with_prev(kb), with_prev(vb)
    scores = jnp.einsum('brnqhd,brnkhd->brnhqk', qb, kk).astype(jnp.float32) * (dh ** -0.5)
    qi = jnp.arange(blk)[:, None]
    kj = jnp.arange(2 * blk)[None, :]
    dist = qi + blk - kj
    band = (dist >= 0) & (dist <= w_sub)
    exists = (jnp.arange(nb) > 0)[:, None, None] | (kj >= blk)[None]
    mask = band[None] & exists
    scores = jnp.where(mask[None, None, :, None], scores, -jnp.inf)
    m = jnp.max(scores, axis=-1, keepdims=True)
    p = jnp.exp(scores - m)
    l = jnp.sum(p, axis=-1, keepdims=True)
    o = jnp.einsum('brnhqk,brnkhd->brnqhd', p, vv.astype(jnp.float32))
    o = o / jnp.moveaxis(l, 3, 4)
    lse = jnp.moveaxis((m + jnp.log(l))[..., 0], 3, 4)

    def from_sub(t):
        t = t.reshape((b, dilation, Lp) + t.shape[4:])[:, :, :L]
        t = jnp.moveaxis(t, 1, 2)
        return t.reshape((b, s) + t.shape[3:])

    return from_sub(o), from_sub(lse)


def _mixer_dilated(x, w_in):
    b, s, _ = x.shape
    proj = x @ w_in
    qkv = proj[..., :A_COLS].reshape(b, s, A_GROUPS, 3, A_HEADS, HEAD_DIM)
    mem_q = proj[..., A_COLS:]
    outs, lses = [], []
    for g, (window, dilation) in enumerate(A_PAIRS):
        o, lse = _dilated_branch(qkv[:, :, g, 0], qkv[:, :, g, 1], qkv[:, :, g, 2], window, dilation)
        outs.append(o)
        lses.append(lse)
    wts = jax.nn.softmax(jnp.stack(lses, axis=0), axis=0)
    o = jnp.sum(wts[..., None] * jnp.stack(outs, axis=0), axis=0)
    return o.reshape(b, s, A_OUT).astype(x.dtype), mem_q


def _mixer_gmlp(x, w_in, w_s, b_s, vnorm_g, vnorm_b):
    b, s, _ = x.shape
    proj = x @ w_in
    z = jax.nn.gelu(proj[..., :B_COLS], approximate=False)
    u, v = z[..., :B_WIDTH], z[..., B_WIDTH:]
    mem_q = proj[..., B_COLS:]
    v = _layer_norm(v, vnorm_g, vnorm_b)
    v = v.reshape(b, s // B_CHUNK, B_CHUNK, B_GROUPS, HEAD_DIM)
    causal = jnp.tril(jnp.ones((B_CHUNK, B_CHUNK), dtype=w_s.dtype))
    ws = w_s * causal[None]
    sg = jnp.einsum('gts,bcsgd->bctgd', ws, v) + b_s.T[:, :, None]
    return u * sg.reshape(b, s, B_WIDTH), mem_q


def _memory_attention(mem_q, mem, w_mem_kv):
    b, s, _ = mem_q.shape
    q = mem_q.reshape(b, s, MEM_HEADS, HEAD_DIM)
    kv = (mem @ w_mem_kv).reshape(b, MEM_TOKENS, 2, MEM_HEADS, HEAD_DIM)
    k, v = kv[:, :, 0], kv[:, :, 1]
    scores = jnp.einsum('bshd,bmhd->bhsm', q, k).astype(jnp.float32) * (HEAD_DIM ** -0.5)
    p = jax.nn.softmax(scores, axis=-1)
    o = jnp.einsum('bhsm,bmhd->bshd', p, v.astype(jnp.float32))
    return o.reshape(b, s, MEM_WIDTH).astype(mem_q.dtype)


def _fwd_setup_inputs(seed: int = 0) -> dict:
    key = jax.random.key(seed)
    ks = jax.random.split(key, 20)

    def nrm(k, shape, std):
        return jax.random.normal(k, shape, jnp.float32) * std

    x = nrm(ks[0], (BATCH, SEQ, D_MODEL), 1.0)
    mem = nrm(ks[1], (BATCH, MEM_TOKENS, D_MODEL), 1.0)
    a_scale = np.ones((A_GROUPS, 3, A_HEADS * HEAD_DIM), np.float32)
    a_scale[:, 2, :] = BETA
    a_scale = jnp.asarray(np.concatenate([a_scale.reshape(-1), np.ones(MEM_WIDTH, np.float32)]))
    w_in_a = nrm(ks[2], (N_LAYERS_A, D_MODEL, A_COLS + MEM_WIDTH), D_MODEL ** -0.5) * a_scale
    b_scale = jnp.asarray(np.concatenate([np.full(B_COLS, BETA, np.float32), np.ones(MEM_WIDTH, np.float32)]))
    w_in_b = nrm(ks[3], (N_LAYERS_B, D_MODEL, B_COLS + MEM_WIDTH), D_MODEL ** -0.5) * b_scale
    w_s = nrm(ks[4], (N_LAYERS_B, B_GROUPS, B_CHUNK, B_CHUNK), B_CHUNK ** -0.5)
    b_s = 1.0 + nrm(ks[5], (N_LAYERS_B, B_GROUPS, B_CHUNK), 0.02)
    vnorm_g = 1.0 + nrm(ks[6], (N_LAYERS_B, B_WIDTH), 0.02)
    vnorm_b = nrm(ks[7], (N_LAYERS_B, B_WIDTH), 0.02)
    kv_scale = jnp.concatenate([jnp.ones((MEM_WIDTH,), jnp.float32), jnp.full((MEM_WIDTH,), BETA, jnp.float32)])
    w_mem_kv = nrm(ks[8], (DEPTH, D_MODEL, 2 * MEM_WIDTH), D_MODEL ** -0.5) * kv_scale
    w_out = nrm(ks[9], (DEPTH, MIX_OUT, D_MODEL), BETA * MIX_OUT ** -0.5)
    ln1_g = 1.0 + nrm(ks[10], (DEPTH, D_MODEL), 0.02)
    ln1_b = nrm(ks[11], (DEPTH, D_MODEL), 0.02)
    w_ff1 = nrm(ks[12], (DEPTH, D_MODEL, D_FF), BETA * D_MODEL ** -0.5)
    w_ff2 = nrm(ks[13], (DEPTH, D_FF, D_MODEL), BETA * D_FF ** -0.5)
    ln2_g = 1.0 + nrm(ks[14], (DEPTH, D_MODEL), 0.02)
    ln2_b = nrm(ks[15], (DEPTH, D_MODEL), 0.02)
    return {"x": x, "mem": mem, "w_in_a": w_in_a, "w_in_b": w_in_b, "w_s": w_s, "b_s": b_s,
            "vnorm_g": vnorm_g, "vnorm_b": vnorm_b, "w_mem_kv": w_mem_kv, "w_out": w_out,
            "ln1_g": ln1_g, "ln1_b": ln1_b, "w_ff1": w_ff1, "w_ff2": w_ff2,
            "ln2_g": ln2_g, "ln2_b": ln2_b}


def _fwd_reference(x, mem, w_in_a, w_in_b, w_s, b_s, vnorm_g, vnorm_b, w_mem_kv, w_out,
              ln1_g, ln1_b, w_ff1, w_ff2, ln2_g, ln2_b):
    for i in range(DEPTH):
        j = i // 2
        if i % 2 == 0:
            mix, mem_q = _mixer_dilated(x, w_in_a[j])
        else:
            mix, mem_q = _mixer_gmlp(x, w_in_b[j], w_s[j], b_s[j], vnorm_g[j], vnorm_b[j])
        mem_o = _memory_attention(mem_q, mem, w_mem_kv[i])
        y = jnp.concatenate([mix, mem_o], axis=-1) @ w_out[i]
        x = _layer_norm(ALPHA * x + y, ln1_g[i], ln1_b[i])
        hidden = jnp.square(jax.nn.relu(x @ w_ff1[i]))
        x = _layer_norm(ALPHA * x + hidden @ w_ff2[i], ln2_g[i], ln2_b[i])
    return x


import jax as _jax
import jax.numpy as _jnp

TWIN_FORMAT = 'train_step'
FWD_PARAMS = ['x', 'mem', 'w_in_a', 'w_in_b', 'w_s', 'b_s', 'vnorm_g', 'vnorm_b', 'w_mem_kv', 'w_out', 'ln1_g', 'ln1_b', 'w_ff1', 'w_ff2', 'ln2_g', 'ln2_b']
TWIN_WEIGHTS = ['w_in_a', 'w_in_b', 'w_s', 'b_s', 'vnorm_g', 'vnorm_b', 'w_mem_kv', 'w_out', 'ln1_g', 'ln1_b', 'w_ff1', 'w_ff2', 'ln2_g', 'ln2_b']
TWIN_DIFF_INPUT = 'x'
TWIN_INPUTS = ['x', 'mem', 'w_in_a', 'w_in_b', 'w_s', 'b_s', 'vnorm_g', 'vnorm_b', 'w_mem_kv', 'w_out', 'ln1_g', 'ln1_b', 'w_ff1', 'w_ff2', 'ln2_g', 'ln2_b', 'loss_target', 'm_w_in_a', 'm_w_in_b', 'm_w_s', 'm_b_s', 'm_vnorm_g', 'm_vnorm_b', 'm_w_mem_kv', 'm_w_out', 'm_ln1_g', 'm_ln1_b', 'm_w_ff1', 'm_w_ff2', 'm_ln2_g', 'm_ln2_b', 'v_w_in_a', 'v_w_in_b', 'v_w_s', 'v_b_s', 'v_vnorm_g', 'v_vnorm_b', 'v_w_mem_kv', 'v_w_out', 'v_ln1_g', 'v_ln1_b', 'v_w_ff1', 'v_w_ff2', 'v_ln2_g', 'v_ln2_b']
TWIN_OUTPUTS = ['loss', 'grad_x', 'grad_w_in_a', 'grad_w_in_b', 'grad_w_s', 'grad_b_s', 'grad_vnorm_g', 'grad_vnorm_b', 'grad_w_mem_kv', 'grad_w_out', 'grad_ln1_g', 'grad_ln1_b', 'grad_w_ff1', 'grad_w_ff2', 'grad_ln2_g', 'grad_ln2_b', 'delta_w_in_a', 'delta_w_in_b', 'delta_w_s', 'delta_b_s', 'delta_vnorm_g', 'delta_vnorm_b', 'delta_w_mem_kv', 'delta_w_out', 'delta_ln1_g', 'delta_ln1_b', 'delta_w_ff1', 'delta_w_ff2', 'delta_ln2_g', 'delta_ln2_b', 'new_m_w_in_a', 'new_m_w_in_b', 'new_m_w_s', 'new_m_b_s', 'new_m_vnorm_g', 'new_m_vnorm_b', 'new_m_w_mem_kv', 'new_m_w_out', 'new_m_ln1_g', 'new_m_ln1_b', 'new_m_w_ff1', 'new_m_w_ff2', 'new_m_ln2_g', 'new_m_ln2_b', 'new_v_w_in_a', 'new_v_w_in_b', 'new_v_w_s', 'new_v_b_s', 'new_v_vnorm_g', 'new_v_vnorm_b', 'new_v_w_mem_kv', 'new_v_w_out', 'new_v_ln1_g', 'new_v_ln1_b', 'new_v_w_ff1', 'new_v_w_ff2', 'new_v_ln2_g', 'new_v_ln2_b']
TWIN_LEAF_KINDS = {'loss': 'loss', 'grad_x': 'grad_x', 'grad_w_in_a': 'grad_w', 'grad_w_in_b': 'grad_w', 'grad_w_s': 'grad_w', 'grad_b_s': 'grad_w', 'grad_vnorm_g': 'grad_w', 'grad_vnorm_b': 'grad_w', 'grad_w_mem_kv': 'grad_w', 'grad_w_out': 'grad_w', 'grad_ln1_g': 'grad_w', 'grad_ln1_b': 'grad_w', 'grad_w_ff1': 'grad_w', 'grad_w_ff2': 'grad_w', 'grad_ln2_g': 'grad_w', 'grad_ln2_b': 'grad_w', 'delta_w_in_a': 'delta_w', 'delta_w_in_b': 'delta_w', 'delta_w_s': 'delta_w', 'delta_b_s': 'delta_w', 'delta_vnorm_g': 'delta_w', 'delta_vnorm_b': 'delta_w', 'delta_w_mem_kv': 'delta_w', 'delta_w_out': 'delta_w', 'delta_ln1_g': 'delta_w', 'delta_ln1_b': 'delta_w', 'delta_w_ff1': 'delta_w', 'delta_w_ff2': 'delta_w', 'delta_ln2_g': 'delta_w', 'delta_ln2_b': 'delta_w', 'new_m_w_in_a': 'new_m', 'new_m_w_in_b': 'new_m', 'new_m_w_s': 'new_m', 'new_m_b_s': 'new_m', 'new_m_vnorm_g': 'new_m', 'new_m_vnorm_b': 'new_m', 'new_m_w_mem_kv': 'new_m', 'new_m_w_out': 'new_m', 'new_m_ln1_g': 'new_m', 'new_m_ln1_b': 'new_m', 'new_m_w_ff1': 'new_m', 'new_m_w_ff2': 'new_m', 'new_m_ln2_g': 'new_m', 'new_m_ln2_b': 'new_m', 'new_v_w_in_a': 'new_v', 'new_v_w_in_b': 'new_v', 'new_v_w_s': 'new_v', 'new_v_b_s': 'new_v', 'new_v_vnorm_g': 'new_v', 'new_v_vnorm_b': 'new_v', 'new_v_w_mem_kv': 'new_v', 'new_v_w_out': 'new_v', 'new_v_ln1_g': 'new_v', 'new_v_ln1_b': 'new_v', 'new_v_w_ff1': 'new_v', 'new_v_w_ff2': 'new_v', 'new_v_ln2_g': 'new_v', 'new_v_ln2_b': 'new_v'}


def _forward(args):
    return _fwd_reference(*[args[k] for k in FWD_PARAMS])


def _output_shape():
    def fwd():
        inp = _fwd_setup_inputs(0)
        return _fwd_reference(*[inp[k] for k in FWD_PARAMS])
    out = _jax.eval_shape(fwd)
    return out.shape, out.dtype

N_MICROBATCH = 1
ADAM_LR = 0.001
ADAM_B1 = 0.9
ADAM_B2 = 0.999
ADAM_EPS = 1e-08
ADAM_WD = 0.01
ADAM_STEP = 10
PER_EXAMPLE_BATCH_AXIS = {'x': 0, 'mem': 0, 'loss_target': 0}
SHARED_INPUTS = []
_WEIGHT_DTYPES = {'w_in_a': _jnp.float32, 'w_in_b': _jnp.float32, 'w_s': _jnp.float32, 'b_s': _jnp.float32, 'vnorm_g': _jnp.float32, 'vnorm_b': _jnp.float32, 'w_mem_kv': _jnp.float32, 'w_out': _jnp.float32, 'ln1_g': _jnp.float32, 'ln1_b': _jnp.float32, 'w_ff1': _jnp.float32, 'w_ff2': _jnp.float32, 'ln2_g': _jnp.float32, 'ln2_b': _jnp.float32}
MOMENT_SCALE = {'w_in_a': 1.818123e-03, 'w_in_b': 1.340444e-02, 'w_s': 4.263943e-03, 'b_s': 6.115870e-03, 'vnorm_g': 4.211895e-03, 'vnorm_b': 4.400076e-03, 'w_mem_kv': 2.254386e-03, 'w_out': 1.026456e-02, 'ln1_g': 6.005251e-01, 'ln1_b': 2.934849e-01, 'w_ff1': 6.614391e-03, 'w_ff2': 1.690717e-02, 'ln2_g': 8.054872e+00, 'ln2_b': 6.121262e-01}


def _to_microbatches(a, axis):
    t = _jnp.moveaxis(a, axis, 0)
    t = t.reshape((N_MICROBATCH, t.shape[0] // N_MICROBATCH) + t.shape[1:])
    return _jnp.moveaxis(t, 1, axis + 1)


def setup_inputs(seed: int = 0) -> dict:
    inp = _fwd_setup_inputs(seed)
    key = _jax.random.fold_in(_jax.random.key(seed), 7919)
    shape, _ = _output_shape()
    out = dict(inp)
    out["loss_target"] = _jax.random.normal(_jax.random.fold_in(key, 0), shape, _jnp.float32)
    for i, name in enumerate(TWIN_WEIGHTS):
        w = inp[name].astype(_jnp.float32)
        if MOMENT_SCALE is None:
            s = _jnp.sqrt(_jnp.mean(_jnp.square(w)) + 1e-30)
        else:
            s = MOMENT_SCALE[name]
        km, kv = _jax.random.split(_jax.random.fold_in(key, i + 1))
        out[name] = w
        out["m_" + name] = s * _jax.random.normal(km, w.shape, _jnp.float32)
        out["v_" + name] = (s * s) * _jax.random.uniform(kv, w.shape, _jnp.float32, 0.5, 1.5)
    if N_MICROBATCH > 1:
        for name, axis in PER_EXAMPLE_BATCH_AXIS.items():
            out[name] = _to_microbatches(out[name], axis)
    return {'x': out['x'], 'mem': out['mem'], 'w_in_a': out['w_in_a'], 'w_in_b': out['w_in_b'], 'w_s': out['w_s'], 'b_s': out['b_s'], 'vnorm_g': out['vnorm_g'], 'vnorm_b': out['vnorm_b'], 'w_mem_kv': out['w_mem_kv'], 'w_out': out['w_out'], 'ln1_g': out['ln1_g'], 'ln1_b': out['ln1_b'], 'w_ff1': out['w_ff1'], 'w_ff2': out['w_ff2'], 'ln2_g': out['ln2_g'], 'ln2_b': out['ln2_b'], 'loss_target': out['loss_target'], 'm_w_in_a': out['m_w_in_a'], 'm_w_in_b': out['m_w_in_b'], 'm_w_s': out['m_w_s'], 'm_b_s': out['m_b_s'], 'm_vnorm_g': out['m_vnorm_g'], 'm_vnorm_b': out['m_vnorm_b'], 'm_w_mem_kv': out['m_w_mem_kv'], 'm_w_out': out['m_w_out'], 'm_ln1_g': out['m_ln1_g'], 'm_ln1_b': out['m_ln1_b'], 'm_w_ff1': out['m_w_ff1'], 'm_w_ff2': out['m_w_ff2'], 'm_ln2_g': out['m_ln2_g'], 'm_ln2_b': out['m_ln2_b'], 'v_w_in_a': out['v_w_in_a'], 'v_w_in_b': out['v_w_in_b'], 'v_w_s': out['v_w_s'], 'v_b_s': out['v_b_s'], 'v_vnorm_g': out['v_vnorm_g'], 'v_vnorm_b': out['v_vnorm_b'], 'v_w_mem_kv': out['v_w_mem_kv'], 'v_w_out': out['v_w_out'], 'v_ln1_g': out['v_ln1_g'], 'v_ln1_b': out['v_ln1_b'], 'v_w_ff1': out['v_w_ff1'], 'v_w_ff2': out['v_w_ff2'], 'v_ln2_g': out['v_ln2_g'], 'v_ln2_b': out['v_ln2_b']}


def _loss(weights, diff, rest, loss_target):
    with _jax.named_scope("forward"):
        args = {**rest, TWIN_DIFF_INPUT: diff, **{k: w.astype(_WEIGHT_DTYPES[k]) for k, w in weights.items()}}
        y = _forward(args)
    with _jax.named_scope("loss_head"):
        err = _jnp.square(y.astype(_jnp.float32) - loss_target)
        return 0.5 * _jnp.sum(_jnp.mean(err, axis=-1)) if err.ndim else 0.5 * err


def _adamw(w, g, m, v):
    m = ADAM_B1 * m + (1.0 - ADAM_B1) * g
    v = ADAM_B2 * v + (1.0 - ADAM_B2) * _jnp.square(g)
    m_hat = m / (1.0 - ADAM_B1 ** ADAM_STEP)
    v_hat = v / (1.0 - ADAM_B2 ** ADAM_STEP)
    delta = -ADAM_LR * (m_hat / (_jnp.sqrt(v_hat) + ADAM_EPS) + ADAM_WD * w)
    return delta, m, v


def reference(x, mem, w_in_a, w_in_b, w_s, b_s, vnorm_g, vnorm_b, w_mem_kv, w_out, ln1_g, ln1_b, w_ff1, w_ff2, ln2_g, ln2_b, loss_target, m_w_in_a, m_w_in_b, m_w_s, m_b_s, m_vnorm_g, m_vnorm_b, m_w_mem_kv, m_w_out, m_ln1_g, m_ln1_b, m_w_ff1, m_w_ff2, m_ln2_g, m_ln2_b, v_w_in_a, v_w_in_b, v_w_s, v_b_s, v_vnorm_g, v_vnorm_b, v_w_mem_kv, v_w_out, v_ln1_g, v_ln1_b, v_w_ff1, v_w_ff2, v_ln2_g, v_ln2_b):
    given = dict(x=x, mem=mem, w_in_a=w_in_a, w_in_b=w_in_b, w_s=w_s, b_s=b_s, vnorm_g=vnorm_g, vnorm_b=vnorm_b, w_mem_kv=w_mem_kv, w_out=w_out, ln1_g=ln1_g, ln1_b=ln1_b, w_ff1=w_ff1, w_ff2=w_ff2, ln2_g=ln2_g, ln2_b=ln2_b, loss_target=loss_target, m_w_in_a=m_w_in_a, m_w_in_b=m_w_in_b, m_w_s=m_w_s, m_b_s=m_b_s, m_vnorm_g=m_vnorm_g, m_vnorm_b=m_vnorm_b, m_w_mem_kv=m_w_mem_kv, m_w_out=m_w_out, m_ln1_g=m_ln1_g, m_ln1_b=m_ln1_b, m_w_ff1=m_w_ff1, m_w_ff2=m_w_ff2, m_ln2_g=m_ln2_g, m_ln2_b=m_ln2_b, v_w_in_a=v_w_in_a, v_w_in_b=v_w_in_b, v_w_s=v_w_s, v_b_s=v_b_s, v_vnorm_g=v_vnorm_g, v_vnorm_b=v_vnorm_b, v_w_mem_kv=v_w_mem_kv, v_w_out=v_w_out, v_ln1_g=v_ln1_g, v_ln1_b=v_ln1_b, v_w_ff1=v_w_ff1, v_w_ff2=v_w_ff2, v_ln2_g=v_ln2_g, v_ln2_b=v_ln2_b)
    weights = {n: given[n] for n in TWIN_WEIGHTS}
    shared = {n: given[n] for n in SHARED_INPUTS}
    per_example = {n: given[n] for n in ['x', 'mem']}
    grad_fn = _jax.value_and_grad(_loss, argnums=(0, 1))

    def one_microbatch(ex, loss_target):
        ex = dict(ex)
        diff = ex.pop(TWIN_DIFF_INPUT)
        return grad_fn(weights, diff, {**shared, **ex}, loss_target)

    if N_MICROBATCH == 1:
        loss, (grad_w, grad_x) = one_microbatch(per_example, given["loss_target"])
    else:
        def body(carry, xs):
            loss_sum, grad_sum = carry
            l_k, (gw_k, gx_k) = one_microbatch(xs[0], xs[1])
            with _jax.named_scope("update"):
                return (loss_sum + l_k, _jax.tree.map(_jnp.add, grad_sum, gw_k)), gx_k

        init = (_jnp.zeros((), _jnp.float32), _jax.tree.map(_jnp.zeros_like, weights))
        (loss, grad_w), grad_x = _jax.lax.scan(body, init, (per_example, given["loss_target"]))
    with _jax.named_scope("update"):
        delta_w, new_m, new_v = {}, {}, {}
        for n in TWIN_WEIGHTS:
            delta_w[n], new_m[n], new_v[n] = _adamw(weights[n], grad_w[n], given["m_" + n], given["v_" + n])
    return (loss, grad_x, *[grad_w[n] for n in TWIN_WEIGHTS], *[delta_w[n] for n in TWIN_WEIGHTS],
            *[new_m[n] for n in TWIN_WEIGHTS], *[new_v[n] for n in TWIN_WEIGHTS])
```

```python
import math

import jax
import jax.numpy as jnp
from jax import lax
from jax.experimental import pallas as pl
from jax.experimental.pallas import tpu as pltpu

F32 = jnp.float32
BF16 = jnp.bfloat16
N_DEV = 8
HEAD_DIM = 128
MEM_HEADS = 4
MEM_WIDTH = MEM_HEADS * HEAD_DIM
A_DILATIONS = (1, 4, 16)
LN_EPS = 1e-5
ADAM_LR, ADAM_B1, ADAM_B2, ADAM_EPS, ADAM_WD, ADAM_STEP = 0.001, 0.9, 0.999, 1e-08, 0.01, 10
VMEM_LIMIT_BYTES = 56 * 1024 * 1024
MESH = pl.DeviceIdType.MESH
ANY = pl.BlockSpec(memory_space=pl.ANY)


def _params(*sem):
    return pltpu.CompilerParams(dimension_semantics=sem, vmem_limit_bytes=VMEM_LIMIT_BYTES)


def _tile(dim, pref):
    if dim <= pref:
        return dim
    best = None
    for t in range(128, pref + 1, 128):
        if dim % t == 0:
            best = t
    return best if best is not None else dim


def _dot(a, b, dims):
    return lax.dot_general(a, b, (dims, ((), ())), preferred_element_type=F32)


NN = ((1,), (0,))
NT = ((1,), (1,))
TN = ((0,), (0,))


def matmul(a, b, mode, name, epilogue=None, extras=(), out_dtypes=(F32,), tm_pref=512, tn_pref=1024, tk_pref=512):
    if mode == "nn":
        (m, k), (k2, n) = a.shape, b.shape
    elif mode == "nt":
        (m, k), (n, k2) = a.shape, b.shape
    else:
        (k, m), (k2, n) = a.shape, b.shape
    assert k == k2, (a.shape, b.shape, mode)
    tm, tn, tk = _tile(m, tm_pref), _tile(n, tn_pref), _tile(k, tk_pref)
    nk = k // tk
    dims = {"nn": NN, "nt": NT, "tn": TN}[mode]
    a_spec = pl.BlockSpec((tk, tm), lambda i, j, kk: (kk, i)) if mode == "tn" else pl.BlockSpec((tm, tk), lambda i, j, kk: (i, kk))
    b_spec = pl.BlockSpec((tn, tk), lambda i, j, kk: (j, kk)) if mode == "nt" else pl.BlockSpec((tk, tn), lambda i, j, kk: (kk, j))
    tile_spec = pl.BlockSpec((tm, tn), lambda i, j, kk: (i, j))
    n_ex, n_out = len(extras), len(out_dtypes)
    if epilogue is None:
        epilogue = lambda acc: (acc,) * n_out

    def body(a_ref, b_ref, *rest):
        ex_refs, out_refs, acc_ref = rest[:n_ex], rest[n_ex:n_ex + n_out], rest[-1]
        kk = pl.program_id(2)

        @pl.when(kk == 0)
        def _():
            acc_ref[...] = jnp.zeros_like(acc_ref)

        acc_ref[...] += _dot(a_ref[...].astype(BF16), b_ref[...].astype(BF16), dims)

        @pl.when(kk == nk - 1)
        def _():
            outs = epilogue(acc_ref[...], *[e[...] for e in ex_refs])
            for o_ref, val in zip(out_refs, outs):
                o_ref[...] = val.astype(o_ref.dtype)

    outs = pl.pallas_call(
        body,
        grid=(m // tm, n // tn, nk),
        in_specs=[a_spec, b_spec] + [tile_spec] * n_ex,
        out_specs=[tile_spec] * n_out,
        out_shape=[jax.ShapeDtypeStruct((m, n), dt) for dt in out_dtypes],
        scratch_shapes=[pltpu.VMEM((tm, tn), F32)],
        compiler_params=_params("parallel", "parallel", "arbitrary"),
        name=name,
    )(a, b, *extras)
    return outs[0] if n_out == 1 else outs


def ln_residual(x, y, g, b, alpha, name):
    s, w = x.shape
    tb = _tile(s, 256)
    row = pl.BlockSpec((tb, w), lambda i: (i, 0))
    vec = pl.BlockSpec((1, w), lambda i: (0, 0))

    def body(x_ref, y_ref, g_ref, b_ref, r_ref, xn_ref, xnb_ref):
        r = alpha * x_ref[...] + y_ref[...]
        mu = jnp.mean(r, axis=-1, keepdims=True)
        var = jnp.mean(jnp.square(r - mu), axis=-1, keepdims=True)
        xn = (r - mu) * lax.rsqrt(var + LN_EPS) * g_ref[...] + b_ref[...]
        r_ref[...] = r
        xn_ref[...] = xn
        xnb_ref[...] = xn.astype(BF16)

    return pl.pallas_call(
        body, grid=(s // tb,), in_specs=[row, row, vec, vec], out_specs=[row, row, row],
        out_shape=[jax.ShapeDtypeStruct((s, w), F32), jax.ShapeDtypeStruct((s, w), F32), jax.ShapeDtypeStruct((s, w), BF16)],
        compiler_params=_params("parallel"), name=name,
    )(x, y, g.reshape(1, w), b.reshape(1, w))


def _ln_bwd_tile(r, g, dxn):
    mu = jnp.mean(r, axis=-1, keepdims=True)
    cen = r - mu
    rstd = lax.rsqrt(jnp.mean(jnp.square(cen), axis=-1, keepdims=True) + LN_EPS)
    xhat = cen * rstd
    dxh = dxn * g
    dr = rstd * (dxh - jnp.mean(dxh, axis=-1, keepdims=True) - xhat * jnp.mean(dxh * xhat, axis=-1, keepdims=True))
    return dr, jnp.sum(dxn * xhat, axis=0, keepdims=True)


def ln_backward(r, g, dxn, name):
    s, w = r.shape
    tb = _tile(s, 256)
    row = pl.BlockSpec((tb, w), lambda i: (i, 0))
    vec = pl.BlockSpec((1, w), lambda i: (0, 0))

    def body(r_ref, g_ref, d_ref, dr_ref, drb_ref, dg_ref, db_ref):
        @pl.when(pl.program_id(0) == 0)
        def _():
            dg_ref[...] = jnp.zeros_like(dg_ref)
            db_ref[...] = jnp.zeros_like(db_ref)

        dxn = d_ref[...]
        dr, dg = _ln_bwd_tile(r_ref[...], g_ref[...], dxn)
        dr_ref[...] = dr
        drb_ref[...] = dr.astype(BF16)
        dg_ref[...] += dg
        db_ref[...] += jnp.sum(dxn, axis=0, keepdims=True)

    return pl.pallas_call(
        body, grid=(s // tb,), in_specs=[row, vec, row], out_specs=[row, row, vec, vec],
        out_shape=[jax.ShapeDtypeStruct((s, w), F32), jax.ShapeDtypeStruct((s, w), BF16),
                   jax.ShapeDtypeStruct((1, w), F32), jax.ShapeDtypeStruct((1, w), F32)],
        compiler_params=_params("arbitrary"), name=name,
    )(r, g.reshape(1, w), dxn)


def _causal_masks():
    qi = lax.broadcasted_iota(jnp.int32, (128, 128), 0)
    kj = lax.broadcasted_iota(jnp.int32, (128, 128), 1)
    return kj <= qi, kj >= qi


def _sub_rows(d, r):
    return pl.ds(r, 128, stride=d) if d > 1 else pl.ds(0, 128)


def _for_each_residue(d, fn):
    if d <= 4:
        for r in range(d):
            fn(r)
    else:
        def step(r, carry):
            fn(r)
            return carry
        lax.fori_loop(0, d, step, 0)


def attention_forward(proj, group, d, heads, name):
    s = proj.shape[0]
    rows = 128 * d
    nb = s // rows
    scale = HEAD_DIM ** -0.5
    qc, kc, vc = (group * 3) * heads, (group * 3 + 1) * heads, (group * 3 + 2) * heads

    def cur(col):
        return pl.BlockSpec((rows, 128), lambda h, n: (n, col + h))

    def prev(col):
        return pl.BlockSpec((rows, 128), lambda h, n: (jnp.maximum(n - 1, 0), col + h))

    out = pl.BlockSpec((rows, 128), lambda h, n: (n, h))

    def body(q_ref, kc_ref, kp_ref, vc_ref, vp_ref, o_ref, l_ref):
        mask_c, band_p = _causal_masks()
        mask_p = band_p & (pl.program_id(1) > 0)

        def one(r):
            sl = _sub_rows(d, r)
            q = q_ref[sl, :].astype(BF16)
            s_c = jnp.where(mask_c, _dot(q, kc_ref[sl, :].astype(BF16), NT) * scale, -jnp.inf)
            s_p = jnp.where(mask_p, _dot(q, kp_ref[sl, :].astype(BF16), NT) * scale, -jnp.inf)
            m = jnp.maximum(jnp.max(s_c, axis=-1, keepdims=True), jnp.max(s_p, axis=-1, keepdims=True))
            e_c, e_p = jnp.exp(s_c - m), jnp.exp(s_p - m)
            l = jnp.sum(e_c, axis=-1, keepdims=True) + jnp.sum(e_p, axis=-1, keepdims=True)
            o = _dot(e_c.astype(BF16), vc_ref[sl, :].astype(BF16), NN) + _dot(e_p.astype(BF16), vp_ref[sl, :].astype(BF16), NN)
            o_ref[sl, :] = o / l
            l_ref[sl, :] = jnp.broadcast_to(m + jnp.log(l), (128, 128))

        _for_each_residue(d, one)

    return pl.pallas_call(
        body, grid=(heads, nb),
        in_specs=[cur(qc), cur(kc), prev(kc), cur(vc), prev(vc)], out_specs=[out, out],
        out_shape=[jax.ShapeDtypeStruct((s, heads * 128), F32)] * 2,
        compiler_params=_params("parallel", "parallel"), name=name,
    )(proj, proj, proj, proj, proj)


def attention_combine(os_, ls_, name):
    s, w = os_[0].shape
    tb = _tile(s, 256)
    row = pl.BlockSpec((tb, w), lambda i: (i, 0))

    def body(o0, o1, o2, l0, l1, l2, o_ref, ob_ref, lse_ref):
        a, b, c = l0[...], l1[...], l2[...]
        m = jnp.maximum(jnp.maximum(a, b), c)
        ea, eb, ec = jnp.exp(a - m), jnp.exp(b - m), jnp.exp(c - m)
        tot = ea + eb + ec
        o = (ea / tot) * o0[...] + (eb / tot) * o1[...] + (ec / tot) * o2[...]
        o_ref[...] = o
        ob_ref[...] = o.astype(BF16)
        lse_ref[...] = m + jnp.log(tot)

    return pl.pallas_call(
        body, grid=(s // tb,), in_specs=[row] * 6, out_specs=[row] * 3,
        out_shape=[jax.ShapeDtypeStruct((s, w), F32), jax.ShapeDtypeStruct((s, w), BF16), jax.ShapeDtypeStruct((s, w), F32)],
        compiler_params=_params("parallel"), name=name,
    )(*os_, *ls_)


def attention_backward(proj, o, lse, dcat, group, d, heads, name):
    s = proj.shape[0]
    rows = 128 * d
    nb = s // rows
    scale = HEAD_DIM ** -0.5
    qc, kc, vc = (group * 3) * heads, (group * 3 + 1) * heads, (group * 3 + 2) * heads

    def at(col, shift):
        return pl.BlockSpec((rows, 128), lambda h, n: (jnp.clip(n + shift, 0, nb - 1), col + h))

    def out(col):
        return pl.BlockSpec((rows, 128), lambda h, n: (n, col + h))

    def body(qc_ref, qn_ref, kc_ref, kp_ref, vc_ref, vp_ref, doc_ref, don_ref, oc_ref, on_ref, lc_ref, ln_ref,
             dq_ref, dk_ref, dv_ref):
        n = pl.program_id(1)
        mask_c, band_p = _causal_masks()
        mask_b = band_p & (n > 0)
        mask_n = band_p & (n < nb - 1)

        def tile(q, k, v, do, lse_t, dsum, mask):
            p = jnp.where(mask, jnp.exp(_dot(q, k, NT) * scale - lse_t), 0.0)
            ds = (p * (_dot(do, v, NT) - dsum) * scale).astype(BF16)
            return p.astype(BF16), ds

        def one(r):
            sl = _sub_rows(d, r)
            q_c, q_n = qc_ref[sl, :].astype(BF16), qn_ref[sl, :].astype(BF16)
            k_c, k_p = kc_ref[sl, :].astype(BF16), kp_ref[sl, :].astype(BF16)
            v_c, v_p = vc_ref[sl, :].astype(BF16), vp_ref[sl, :].astype(BF16)
            do_c, do_n = doc_ref[sl, :], don_ref[sl, :]
            dsum_c = jnp.sum(do_c * oc_ref[sl, :], axis=-1, keepdims=True)
            dsum_n = jnp.sum(do_n * on_ref[sl, :], axis=-1, keepdims=True)
            do_c, do_n = do_c.astype(BF16), do_n.astype(BF16)
            p_a, ds_a = tile(q_c, k_c, v_c, do_c, lc_ref[sl, :], dsum_c, mask_c)
            _, ds_b = tile(q_c, k_p, v_p, do_c, lc_ref[sl, :], dsum_c, mask_b)
            p_n, ds_n = tile(q_n, k_c, v_c, do_n, ln_ref[sl, :], dsum_n, mask_n)
            dq_ref[sl, :] = _dot(ds_a, k_c, NN) + _dot(ds_b, k_p, NN)
            dk_ref[sl, :] = _dot(ds_a, q_c, TN) + _dot(ds_n, q_n, TN)
            dv_ref[sl, :] = _dot(p_a, do_c, TN) + _dot(p_n, do_n, TN)

        _for_each_residue(d, one)

    w = heads * 128
    return pl.pallas_call(
        body, grid=(heads, nb),
        in_specs=[at(qc, 0), at(qc, 1), at(kc, 0), at(kc, -1), at(vc, 0), at(vc, -1),
                  at(0, 0), at(0, 1), at(0, 0), at(0, 1), at(0, 0), at(0, 1)],
        out_specs=[out(0)] * 3,
        out_shape=[jax.ShapeDtypeStruct((s, w), F32)] * 3,
        compiler_params=_params("parallel", "parallel"), name=name,
    )(proj, proj, proj, proj, proj, proj, dcat, dcat, o, o, lse, lse)


def _mem_softmax(q, kv, h, scale):
    k = kv[:, h * 128:(h + 1) * 128].astype(BF16)
    v = kv[:, MEM_WIDTH + h * 128:MEM_WIDTH + (h + 1) * 128].astype(BF16)
    sc = _dot(q, k, NT) * scale
    e = jnp.exp(sc - jnp.max(sc, axis=-1, keepdims=True))
    return e / jnp.sum(e, axis=-1, keepdims=True), k, v


def memory_attention(proj, qcol, kv, name):
    s = proj.shape[0]
    tb = _tile(s, 512)
    scale = HEAD_DIM ** -0.5

    def body(q_ref, kv_ref, o_ref):
        kv_t = kv_ref[...]
        for h in range(MEM_HEADS):
            p, _, v = _mem_softmax(q_ref[:, h * 128:(h + 1) * 128].astype(BF16), kv_t, h, scale)
            o_ref[:, h * 128:(h + 1) * 128] = _dot(p.astype(BF16), v, NN).astype(BF16)

    return pl.pallas_call(
        body, grid=(s // tb,),
        in_specs=[pl.BlockSpec((tb, MEM_WIDTH), lambda i: (i, qcol // MEM_WIDTH)), pl.BlockSpec(kv.shape, lambda i: (0, 0))],
        out_specs=pl.BlockSpec((tb, MEM_WIDTH), lambda i: (i, 0)),
        out_shape=jax.ShapeDtypeStruct((s, MEM_WIDTH), BF16),
        compiler_params=_params("parallel"), name=name,
    )(proj, kv)


def memory_attention_backward(proj, qcol, kv, dcat, dcol, name):
    s = proj.shape[0]
    tb = _tile(s, 512)
    scale = HEAD_DIM ** -0.5

    def body(q_ref, kv_ref, do_ref, dq_ref, dkv_ref):
        @pl.when(pl.program_id(0) == 0)
        def _():
            dkv_ref[...] = jnp.zeros_like(dkv_ref)

        kv_t = kv_ref[...]
        for h in range(MEM_HEADS):
            cols = slice(h * 128, (h + 1) * 128)
            q = q_ref[:, cols].astype(BF16)
            p, k, v = _mem_softmax(q, kv_t, h, scale)
            do = do_ref[:, cols].astype(BF16)
            dp = _dot(do, v, NT)
            ds = (p * (dp - jnp.sum(dp * p, axis=-1, keepdims=True)) * scale).astype(BF16)
            dq_ref[:, cols] = _dot(ds, k, NN).astype(BF16)
            dkv_ref[:, cols] += _dot(ds, q, TN)
            dkv_ref[:, MEM_WIDTH + h * 128:MEM_WIDTH + (h + 1) * 128] += _dot(p.astype(BF16), do, TN)

    return pl.pallas_call(
        body, grid=(s // tb,),
        in_specs=[pl.BlockSpec((tb, MEM_WIDTH), lambda i: (i, qcol // MEM_WIDTH)), pl.BlockSpec(kv.shape, lambda i: (0, 0)),
                  pl.BlockSpec((tb, MEM_WIDTH), lambda i: (i, dcol // MEM_WIDTH))],
        out_specs=[pl.BlockSpec((tb, MEM_WIDTH), lambda i: (i, 0)), pl.BlockSpec(kv.shape, lambda i: (0, 0))],
        out_shape=[jax.ShapeDtypeStruct((s, MEM_WIDTH), BF16), jax.ShapeDtypeStruct(kv.shape, F32)],
        compiler_params=_params("arbitrary"), name=name,
    )(proj, kv, dcat)


_SQRT_HALF = math.sqrt(0.5)
_INV_SQRT_2PI = 1.0 / math.sqrt(2.0 * math.pi)


def _gelu(x):
    return 0.5 * x * (1.0 + lax.erf(x * _SQRT_HALF))


def _gelu_grad(x):
    return 0.5 * (1.0 + lax.erf(x * _SQRT_HALF)) + x * (_INV_SQRT_2PI * jnp.exp(-0.5 * x * x))


def _gmlp_specs(s, wd, groups, tb):
    half = lambda c: pl.BlockSpec((tb, wd), lambda i: (i, c))
    ws_spec = pl.BlockSpec((groups, 128, 128), lambda i: (0, 0, 0))
    bs_spec = pl.BlockSpec((groups, 128, 1), lambda i: (0, 0, 0))
    vec = pl.BlockSpec((1, wd), lambda i: (0, 0))
    return half, ws_spec, bs_spec, vec


def _vnorm(zv, g, b):
    mu = jnp.mean(zv, axis=-1, keepdims=True)
    var = jnp.mean(jnp.square(zv - mu), axis=-1, keepdims=True)
    return (zv - mu) * lax.rsqrt(var + LN_EPS) * g + b


def gmlp_forward(proj, ws, bs, vg, vb, name):
    s = proj.shape[0]
    groups = ws.shape[0]
    wd = groups * 128
    tb = _tile(s, 512)
    half, ws_spec, bs_spec, vec = _gmlp_specs(s, wd, groups, tb)

    def body(pu_ref, pv_ref, ws_ref, bs_ref, vg_ref, vb_ref, o_ref, vn_ref):
        causal, _ = _causal_masks()
        vn_ref[...] = _vnorm(_gelu(pv_ref[...]), vg_ref[...], vb_ref[...]).astype(BF16)
        for g in range(groups):
            cols = slice(g * 128, (g + 1) * 128)
            wm = jnp.where(causal, ws_ref[g], 0.0).astype(BF16)
            for c in range(tb // 128):
                rws = slice(c * 128, (c + 1) * 128)
                sg = _dot(wm, vn_ref[rws, cols], NN) + bs_ref[g]
                o_ref[rws, cols] = (_gelu(pu_ref[rws, cols]) * sg).astype(BF16)

    return pl.pallas_call(
        body, grid=(s // tb,),
        in_specs=[half(0), half(1), ws_spec, bs_spec, vec, vec], out_specs=pl.BlockSpec((tb, wd), lambda i: (i, 0)),
        out_shape=jax.ShapeDtypeStruct((s, wd), BF16),
        scratch_shapes=[pltpu.VMEM((tb, wd), BF16)],
        compiler_params=_params("parallel"), name=name,
    )(proj, proj, ws, bs.reshape(groups, 128, 1), vg.reshape(1, wd), vb.reshape(1, wd))


def gmlp_backward(proj, ws, bs, vg, vb, dcat, name):
    s = proj.shape[0]
    groups = ws.shape[0]
    wd = groups * 128
    tb = _tile(s, 512)
    half, ws_spec, bs_spec, vec = _gmlp_specs(s, wd, groups, tb)

    def body(pu_ref, pv_ref, ws_ref, bs_ref, vg_ref, vb_ref, do_ref, dpu_ref, dpv_ref, dws_ref, dbs_ref, dvg_ref, dvb_ref,
             vn_ref, dvn_ref):
        @pl.when(pl.program_id(0) == 0)
        def _():
            dws_ref[...] = jnp.zeros_like(dws_ref)
            dbs_ref[...] = jnp.zeros_like(dbs_ref)
            dvg_ref[...] = jnp.zeros_like(dvg_ref)
            dvb_ref[...] = jnp.zeros_like(dvb_ref)

        causal, _ = _causal_masks()
        pv = pv_ref[...]
        zv = _gelu(pv)
        vn_ref[...] = _vnorm(zv, vg_ref[...], vb_ref[...]).astype(BF16)
        for g in range(groups):
            cols = slice(g * 128, (g + 1) * 128)
            wm = jnp.where(causal, ws_ref[g], 0.0).astype(BF16)
            dws_g = jnp.zeros((128, 128), F32)
            dbs_g = jnp.zeros((128, 1), F32)
            for c in range(tb // 128):
                rws = slice(c * 128, (c + 1) * 128)
                vn = vn_ref[rws, cols]
                pu = pu_ref[rws, cols]
                do = do_ref[rws, cols]
                sg = _dot(wm, vn, NN) + bs_ref[g]
                dpu_ref[rws, cols] = (do * sg * _gelu_grad(pu)).astype(BF16)
                dsg = do * _gelu(pu)
                dsg_b = dsg.astype(BF16)
                dws_g += _dot(dsg_b, vn, NT)
                dbs_g += jnp.sum(dsg, axis=-1, keepdims=True)
                dvn_ref[rws, cols] = _dot(wm, dsg_b, TN)
            dws_ref[g] += jnp.where(causal, dws_g, 0.0)
            dbs_ref[g] += dbs_g
        dvn = dvn_ref[...]
        dzv, dvg = _ln_bwd_tile(zv, vg_ref[...], dvn)
        dvg_ref[...] += dvg
        dvb_ref[...] += jnp.sum(dvn, axis=0, keepdims=True)
        dpv_ref[...] = (dzv * _gelu_grad(pv)).astype(BF16)

    row = pl.BlockSpec((tb, wd), lambda i: (i, 0))
    dpu, dpv, dws, dbs, dvg, dvb = pl.pallas_call(
        body, grid=(s // tb,),
        in_specs=[half(0), half(1), ws_spec, bs_spec, vec, vec, row],
        out_specs=[row, row, ws_spec, bs_spec, vec, vec],
        out_shape=[jax.ShapeDtypeStruct((s, wd), BF16), jax.ShapeDtypeStruct((s, wd), BF16),
                   jax.ShapeDtypeStruct((groups, 128, 128), F32), jax.ShapeDtypeStruct((groups, 128, 1), F32),
                   jax.ShapeDtypeStruct((1, wd), F32), jax.ShapeDtypeStruct((1, wd), F32)],
        scratch_shapes=[pltpu.VMEM((tb, wd), BF16), pltpu.VMEM((tb, wd), F32)],
        compiler_params=_params("arbitrary"), name=name,
    )(proj, proj, ws, bs.reshape(groups, 128, 1), vg.reshape(1, wd), vb.reshape(1, wd), dcat)
    return dpu, dpv, dws, dbs.reshape(groups, 128), dvg, dvb


def loss_head(y, target, name):
    s, w = y.shape
    tb = _tile(s, 256)
    row = pl.BlockSpec((tb, w), lambda i: (i, 0))
    nsteps = s // tb

    def body(y_ref, t_ref, loss_ref, dy_ref, acc_ref):
        i = pl.program_id(0)

        @pl.when(i == 0)
        def _():
            acc_ref[...] = jnp.zeros_like(acc_ref)

        err = y_ref[...] - t_ref[...]
        dy_ref[...] = err / w
        acc_ref[...] += jnp.sum(jnp.mean(jnp.square(err), axis=-1, keepdims=True), axis=0, keepdims=True)

        @pl.when(i == nsteps - 1)
        def _():
            loss_ref[...] = jnp.broadcast_to(0.5 * acc_ref[...], loss_ref.shape)

    return pl.pallas_call(
        body, grid=(nsteps,), in_specs=[row, row],
        out_specs=[pl.BlockSpec((8, 128), lambda i: (0, 0)), row],
        out_shape=[jax.ShapeDtypeStruct((8, 128), F32), jax.ShapeDtypeStruct((s, w), F32)],
        scratch_shapes=[pltpu.VMEM((1, 1), F32)],
        compiler_params=_params("arbitrary"), name=name,
    )(y, target)


def _my_position():
    return lax.axis_index("x"), lax.axis_index("y"), lax.axis_index("c")


def all_gather(v, name):
    def body(v_ref, out_ref, send_sems, recv_sems, local_sem):
        x, y, c = _my_position()
        me, sibling = (x, y, c), (x, y, 1 - c)
        chips = [(1 - x, y), (x, 1 - y), (1 - x, 1 - y)]

        def slot(px, py, pc):
            return out_ref.at[4 * px + 2 * py + pc]

        def copy(k, block, to, src=None):
            return pltpu.make_async_remote_copy(
                src_ref=slot(*block) if src is None else src, dst_ref=slot(*block),
                send_sem=send_sems.at[k], recv_sem=recv_sems.at[k], device_id=to, device_id_type=MESH)

        mine = pltpu.make_async_copy(v_ref, slot(*me), local_sem)
        mine.start()
        first = [copy(0, me, sibling, src=v_ref)] + [copy(1 + j, me, (*chip, c), src=v_ref) for j, chip in enumerate(chips)]
        for cp in first:
            cp.start()
        passed = [copy(4 + j, (*chip, c), sibling) for j, chip in enumerate(chips)]
        for j, chip in enumerate(chips):
            copy(1 + j, (*chip, c), me).wait_recv()
            passed[j].start()
        copy(0, sibling, me).wait_recv()
        for j, chip in enumerate(chips):
            copy(4 + j, (*chip, 1 - c), me).wait_recv()
        for cp in first + passed:
            cp.wait_send()
        mine.wait()

    return pl.pallas_call(
        body, out_shape=jax.ShapeDtypeStruct((N_DEV,) + v.shape, v.dtype), in_specs=[ANY], out_specs=ANY,
        scratch_shapes=[pltpu.SemaphoreType.DMA((7,)), pltpu.SemaphoreType.DMA((7,)), pltpu.SemaphoreType.DMA(())],
        name=name,
    )(v)


def all_to_all(p, name):
    relations = [(dx, dy, dc) for dx in (0, 1) for dy in (0, 1) for dc in (0, 1)][1:]

    def body(p_ref, out_ref, send_sems, recv_sems, local_sem):
        x, y, c = _my_position()
        me = 4 * x + 2 * y + c
        mine = pltpu.make_async_copy(p_ref.at[me], out_ref.at[me], local_sem)
        mine.start()
        copies = []
        for k, (dx, dy, dc) in enumerate(relations):
            px, py, pc = (x + dx) % 2, (y + dy) % 2, (c + dc) % 2
            peer = 4 * px + 2 * py + pc
            copies.append((pltpu.make_async_remote_copy(
                src_ref=p_ref.at[peer], dst_ref=out_ref.at[me], send_sem=send_sems.at[k], recv_sem=recv_sems.at[k],
                device_id=(px, py, pc), device_id_type=MESH), pltpu.make_async_remote_copy(
                src_ref=p_ref.at[peer], dst_ref=out_ref.at[peer], send_sem=send_sems.at[k], recv_sem=recv_sems.at[k],
                device_id=(px, py, pc), device_id_type=MESH)))
        for send, _ in copies:
            send.start()
        for _, arrival in copies:
            arrival.wait_recv()
        for send, _ in copies:
            send.wait_send()
        mine.wait()

    return pl.pallas_call(
        body, out_shape=jax.ShapeDtypeStruct(p.shape, p.dtype), in_specs=[ANY], out_specs=ANY,
        scratch_shapes=[pltpu.SemaphoreType.DMA((7,)), pltpu.SemaphoreType.DMA((7,)), pltpu.SemaphoreType.DMA(())],
        name=name,
    )(p)


def adamw(parts, w, m, v, row_block_offset, name):
    _, r, c = parts.shape
    tr = r
    while tr * c > 160 * 1024 and tr % 16 == 0:
        tr //= 2
    off = row_block_offset * (r // tr)
    c1 = 1.0 - ADAM_B1 ** ADAM_STEP
    c2 = 1.0 - ADAM_B2 ** ADAM_STEP

    def body(p_ref, w_ref, m_ref, v_ref, g_ref, d_ref, nm_ref, nv_ref):
        g = p_ref[0]
        for j in range(1, N_DEV):
            g = g + p_ref[j]
        nm = ADAM_B1 * m_ref[...] + (1.0 - ADAM_B1) * g
        nv = ADAM_B2 * v_ref[...] + (1.0 - ADAM_B2) * jnp.square(g)
        m_hat = nm / c1
        v_hat = nv / c2
        g_ref[...] = g
        d_ref[...] = -ADAM_LR * (m_hat / (jnp.sqrt(v_hat) + ADAM_EPS) + ADAM_WD * w_ref[...])
        nm_ref[...] = nm
        nv_ref[...] = nv

    state = pl.BlockSpec((tr, c), lambda i: (off + i, 0))
    out = pl.BlockSpec((tr, c), lambda i: (i, 0))
    return pl.pallas_call(
        body, grid=(r // tr,),
        in_specs=[pl.BlockSpec((N_DEV, tr, c), lambda i: (0, i, 0)), state, state, state], out_specs=[out] * 4,
        out_shape=[jax.ShapeDtypeStruct((r, c), F32)] * 4,
        compiler_params=_params("parallel"), name=name,
    )(parts, w, m, v)


def kernel(x, mem, w_in_a, w_in_b, w_s, b_s, vnorm_g, vnorm_b, w_mem_kv, w_out, ln1_g, ln1_b, w_ff1, w_ff2, ln2_g, ln2_b, loss_target, m_w_in_a, m_w_in_b, m_w_s, m_b_s, m_vnorm_g, m_vnorm_b, m_w_mem_kv, m_w_out, m_ln1_g, m_ln1_b, m_w_ff1, m_w_ff2, m_ln2_g, m_ln2_b, v_w_in_a, v_w_in_b, v_w_s, v_b_s, v_vnorm_g, v_vnorm_b, v_w_mem_kv, v_w_out, v_ln1_g, v_ln1_b, v_w_ff1, v_w_ff2, v_ln2_g, v_ln2_b):
    depth = w_ff1.shape[0]
    d_model = x.shape[-1]
    d_ff = w_ff2.shape[1] * N_DEV
    groups = w_s.shape[1]
    mix_w = groups * HEAD_DIM
    alpha = (2.0 * depth) ** 0.25
    me = 4 * lax.axis_index("x") + 2 * lax.axis_index("y") + lax.axis_index("c")
    x0 = x[0]
    mem_b = mem[0].astype(BF16)
    target = loss_target[0]

    def layer_shards(i):
        w_in = (w_in_a if i % 2 == 0 else w_in_b)[i // 2]
        return [w_in, w_mem_kv[i], w_out[i], w_ff1[i], w_ff2[i]]

    weights = []
    for i in range(depth):
        shards = layer_shards(i)
        packed = jnp.concatenate([t.astype(BF16).reshape(-1) for t in shards]).reshape(-1, 1024)
        gathered = all_gather(packed, name=f"gather_weights_{i % 2}").reshape(N_DEV, -1)
        full, at = [], 0
        for t, sharded_cols in zip(shards, (True, False, True, True, False)):
            piece = gathered[:, at:at + t.size].reshape((N_DEV,) + t.shape)
            at += t.size
            if sharded_cols:
                full.append(jnp.transpose(piece, (1, 0, 2)).reshape(t.shape[0], N_DEV * t.shape[1]))
            else:
                full.append(piece.reshape(N_DEV * t.shape[0], t.shape[1]))
        weights.append(full)
    vnorm = all_gather(jnp.concatenate([vnorm_g, vnorm_b], axis=0), name="gather_vnorm")
    n_b = vnorm_g.shape[0]
    vnorm = jnp.transpose(vnorm, (1, 0, 2)).reshape(2 * n_b, mix_w)
    vg_full, vb_full = vnorm[:n_b], vnorm[n_b:]

    saved = []
    xf, xb = x0, x0.astype(BF16)
    for i in range(depth):
        w_in, w_kv, w_o, w_1, w_2 = weights[i]
        j = i // 2
        proj = matmul(xb, w_in, "nn", name=f"in_proj_{i % 2}")
        kv = matmul(mem_b, w_kv, "nn", name="mem_kv")
        if i % 2 == 0:
            os_, ls_ = [], []
            for g, d in enumerate(A_DILATIONS):
                o_g, l_g = attention_forward(proj, g, d, groups, name=f"attn_fwd_d{d}")
                os_.append(o_g)
                ls_.append(l_g)
            o, mix_b, lse = attention_combine(os_, ls_, name="attn_combine")
            qcol = 9 * mix_w
            extra = (o, lse)
        else:
            mix_b = gmlp_forward(proj, w_s[j], b_s[j], vg_full[j], vb_full[j], name="gmlp_fwd")
            qcol = 2 * mix_w
            extra = ()
        mem_o = memory_attention(proj, qcol, kv, name=f"mem_attn_{i % 2}")
        cat = jnp.concatenate([mix_b, mem_o], axis=1)
        y1 = matmul(cat, w_o, "nn", name="out_proj")
        r1, x1, x1b = ln_residual(xf, y1, ln1_g[i], ln1_b[i], alpha, name="ln1")
        h, hid = matmul(x1b, w_1, "nn", name="ff1", out_dtypes=(F32, BF16),
                        epilogue=lambda acc: (acc, jnp.square(jnp.maximum(acc, 0.0))))
        y2 = matmul(hid, w_2, "nn", name="ff2")
        r2, x2, x2b = ln_residual(x1, y2, ln2_g[i], ln2_b[i], alpha, name="ln2")
        saved.append((xb, proj, kv, extra, qcol, cat, r1, x1b, h, hid, r2))
        xf, xb = x2, x2b

    loss_tile, dx = loss_head(xf, target, name="loss_head")
    loss = lax.psum(loss_tile[0, 0], ("x", "y", "c"))

    def exchange_cols(dw):
        k, n = dw.shape
        return all_to_all(jnp.transpose(dw.reshape(k, N_DEV, n // N_DEV), (1, 0, 2)), name=f"exchange_{k}x{n}")

    def exchange_rows(dw):
        k, n = dw.shape
        return all_to_all(dw.reshape(N_DEV, k // N_DEV, n), name=f"exchange_{k}x{n}")

    recv = {n: [None] * depth for n in ("w_in", "w_mem_kv", "w_out", "w_ff1", "w_ff2")}
    small = {n: [None] * depth for n in ("ln1_g", "ln1_b", "ln2_g", "ln2_b")}
    small_b = {n: [None] * n_b for n in ("w_s", "b_s", "vnorm_g", "vnorm_b")}
    for i in reversed(range(depth)):
        w_in, w_kv, w_o, w_1, w_2 = weights[i]
        xb, proj, kv, extra, qcol, cat, r1, x1b, h, hid, r2 = saved[i]
        j = i // 2
        dr2, dr2b, small["ln2_g"][i], small["ln2_b"][i] = ln_backward(r2, ln2_g[i], dx, name="ln2_bwd")
        dh = matmul(dr2b, w_2, "nt", name="ff2_dx", extras=(h,), out_dtypes=(BF16,),
                    epilogue=lambda acc, h_t: (acc * (2.0 * jnp.maximum(h_t, 0.0)),))
        recv["w_ff2"][i] = exchange_rows(matmul(hid, dr2b, "tn", name="ff2_dw"))
        dx1 = matmul(dh, w_1, "nt", name="ff1_dx", extras=(dr2,), epilogue=lambda acc, res: (acc + alpha * res,))
        recv["w_ff1"][i] = exchange_cols(matmul(x1b, dh, "tn", name="ff1_dw"))
        dr1, dr1b, small["ln1_g"][i], small["ln1_b"][i] = ln_backward(r1, ln1_g[i], dx1, name="ln1_bwd")
        dcat = matmul(dr1b, w_o, "nt", name="out_proj_dx")
        recv["w_out"][i] = exchange_cols(matmul(cat, dr1b, "tn", name="out_proj_dw"))
        dq_mem, dkv = memory_attention_backward(proj, qcol, kv, dcat, mix_w, name=f"mem_attn_bwd_{i % 2}")
        recv["w_mem_kv"][i] = exchange_rows(matmul(mem_b, dkv, "tn", name="mem_kv_dw"))
        if i % 2 == 0:
            o, lse = extra
            pieces = []
            for g, d in enumerate(A_DILATIONS):
                pieces += [t.astype(BF16) for t in attention_backward(proj, o, lse, dcat, g, d, groups, name=f"attn_bwd_d{d}")]
            dproj = jnp.concatenate(pieces + [dq_mem], axis=1)
        else:
            dpu, dpv, small_b["w_s"][j], small_b["b_s"][j], small_b["vnorm_g"][j], small_b["vnorm_b"][j] = gmlp_backward(
                proj, w_s[j], b_s[j], vg_full[j], vb_full[j], dcat, name="gmlp_bwd")
            dproj = jnp.concatenate([dpu, dpv, dq_mem], axis=1)
        recv["w_in"][i] = exchange_cols(matmul(xb, dproj, "tn", name=f"in_proj_dw_{i % 2}"))
        dx = matmul(dproj, w_in, "nt", name=f"in_proj_dx_{i % 2}", extras=(dr1,), epilogue=lambda acc, res: (acc + alpha * res,))
    grad_x = dx[None]

    flat_small = jnp.concatenate(
        [jnp.concatenate(small[n], axis=0).reshape(-1) for n in ("ln1_g", "ln1_b", "ln2_g", "ln2_b")]
        + [jnp.stack(small_b[n]).reshape(-1) for n in ("w_s", "b_s", "vnorm_g", "vnorm_b")])
    pad = (-flat_small.size) % 1024
    gathered_small = all_gather(jnp.pad(flat_small, (0, pad)).reshape(-1, 1024), name="gather_small_grads").reshape(N_DEV, -1)

    results = {}

    def take(at, shape):
        size = math.prod(shape)
        return gathered_small[:, at:at + size].reshape((N_DEV,) + shape), at + size

    at = 0
    for n, w_, m_, v_ in (("ln1_g", ln1_g, m_ln1_g, v_ln1_g), ("ln1_b", ln1_b, m_ln1_b, v_ln1_b),
                          ("ln2_g", ln2_g, m_ln2_g, v_ln2_g), ("ln2_b", ln2_b, m_ln2_b, v_ln2_b)):
        parts, at = take(at, w_.shape)
        results[n] = adamw(parts, w_, m_, v_, 0, name="adamw_ln")
    parts, at = take(at, (n_b * groups * 128, 128))
    results["w_s"] = [t.reshape(w_s.shape) for t in adamw(parts, w_s.reshape(-1, 128), m_w_s.reshape(-1, 128), v_w_s.reshape(-1, 128), 0, name="adamw_w_s")]
    parts, at = take(at, (n_b * groups, 128))
    results["b_s"] = [t.reshape(b_s.shape) for t in adamw(parts, b_s.reshape(-1, 128), m_b_s.reshape(-1, 128), v_b_s.reshape(-1, 128), 0, name="adamw_b_s")]
    for n, w_, m_, v_ in (("vnorm_g", vnorm_g, m_vnorm_g, v_vnorm_g), ("vnorm_b", vnorm_b, m_vnorm_b, v_vnorm_b)):
        parts, at = take(at, (n_b, mix_w))
        parts = lax.dynamic_slice_in_dim(parts, me * w_.shape[1], w_.shape[1], axis=2)
        results[n] = adamw(parts, w_, m_, v_, 0, name="adamw_vnorm")

    def per_layer(name, layers, w_, m_, v_):
        rows = w_.shape[1]
        flat = lambda t: t.reshape(-1, t.shape[-1])
        outs = [adamw(recv[name][i], flat(w_), flat(m_), flat(v_), k, name=f"adamw_{rows}x{w_.shape[-1]}") for k, i in enumerate(layers)]
        return [jnp.stack([o[q] for o in outs]) for q in range(4)]

    results["w_in_a"] = per_layer("w_in", range(0, depth, 2), w_in_a, m_w_in_a, v_w_in_a)
    results["w_in_b"] = per_layer("w_in", range(1, depth, 2), w_in_b, m_w_in_b, v_w_in_b)
    results["w_mem_kv"] = per_layer("w_mem_kv", range(depth), w_mem_kv, m_w_mem_kv, v_w_mem_kv)
    results["w_out"] = per_layer("w_out", range(depth), w_out, m_w_out, v_w_out)
    results["w_ff1"] = per_layer("w_ff1", range(depth), w_ff1, m_w_ff1, v_w_ff1)
    results["w_ff2"] = per_layer("w_ff2", range(depth), w_ff2, m_w_ff2, v_w_ff2)

    order = ("w_in_a", "w_in_b", "w_s", "b_s", "vnorm_g", "vnorm_b", "w_mem_kv", "w_out", "ln1_g", "ln1_b", "w_ff1", "w_ff2", "ln2_g", "ln2_b")
    return (loss, grad_x, *[results[n][0] for n in order], *[results[n][1] for n in order],
            *[results[n][2] for n in order], *[results[n][3] for n in order])
```

```python
import math

import jax
import jax.numpy as jnp
from jax import lax
from jax.experimental import pallas as pl
from jax.experimental.pallas import tpu as pltpu

F32 = jnp.float32
BF16 = jnp.bfloat16
N_DEV = 8
HEAD_DIM = 128
MEM_HEADS = 4
MEM_WIDTH = MEM_HEADS * HEAD_DIM
A_DILATIONS = (1, 4, 16)
LN_EPS = 1e-5
ADAM_LR, ADAM_B1, ADAM_B2, ADAM_EPS, ADAM_WD, ADAM_STEP = 0.001, 0.9, 0.999, 1e-08, 0.01, 10
VMEM_LIMIT_BYTES = 56 * 1024 * 1024
MESH = pl.DeviceIdType.MESH
ANY = pl.BlockSpec(memory_space=pl.ANY)


def _params(*sem):
    return pltpu.CompilerParams(dimension_semantics=sem, vmem_limit_bytes=VMEM_LIMIT_BYTES)


def _tile(dim, pref):
    if dim <= pref:
        return dim
    best = None
    for t in range(128, pref + 1, 128):
        if dim % t == 0:
            best = t
    return best if best is not None else dim


def _dot(a, b, dims):
    return lax.dot_general(a, b, (dims, ((), ())), preferred_element_type=F32)


NN = ((1,), (0,))
NT = ((1,), (1,))
TN = ((0,), (0,))


MATMUL_TILES = {"nn": (1024, 512, 2560), "nt": (1024, 512, 2560), "tn": (1024, 1024, 512)}


def matmul(a, b, mode, name, epilogue=None, extras=(), out_dtypes=(F32,)):
    if mode == "nn":
        (m, k), (k2, n) = a.shape, b.shape
    elif mode == "nt":
        (m, k), (n, k2) = a.shape, b.shape
    else:
        (k, m), (k2, n) = a.shape, b.shape
    assert k == k2, (a.shape, b.shape, mode)
    tm_pref, tn_pref, tk_pref = MATMUL_TILES[mode]
    tm, tn, tk = _tile(m, tm_pref), _tile(n, tn_pref), _tile(k, tk_pref)
    nk = k // tk
    dims = {"nn": NN, "nt": NT, "tn": TN}[mode]
    a_spec = pl.BlockSpec((tk, tm), lambda i, j, kk: (kk, i)) if mode == "tn" else pl.BlockSpec((tm, tk), lambda i, j, kk: (i, kk))
    b_spec = pl.BlockSpec((tn, tk), lambda i, j, kk: (j, kk)) if mode == "nt" else pl.BlockSpec((tk, tn), lambda i, j, kk: (kk, j))
    tile_spec = pl.BlockSpec((tm, tn), lambda i, j, kk: (i, j))
    n_ex, n_out = len(extras), len(out_dtypes)
    if epilogue is None:
        epilogue = lambda acc: (acc,) * n_out

    def body(a_ref, b_ref, *rest):
        ex_refs, out_refs, acc_ref = rest[:n_ex], rest[n_ex:n_ex + n_out], rest[-1]
        kk = pl.program_id(2)

        @pl.when(kk == 0)
        def _():
            acc_ref[...] = jnp.zeros_like(acc_ref)

        acc_ref[...] += _dot(a_ref[...].astype(BF16), b_ref[...].astype(BF16), dims)

        @pl.when(kk == nk - 1)
        def _():
            outs = epilogue(acc_ref[...], *[e[...] for e in ex_refs])
            for o_ref, val in zip(out_refs, outs):
                o_ref[...] = val.astype(o_ref.dtype)

    outs = pl.pallas_call(
        body,
        grid=(m // tm, n // tn, nk),
        in_specs=[a_spec, b_spec] + [tile_spec] * n_ex,
        out_specs=[tile_spec] * n_out,
        out_shape=[jax.ShapeDtypeStruct((m, n), dt) for dt in out_dtypes],
        scratch_shapes=[pltpu.VMEM((tm, tn), F32)],
        compiler_params=_params("parallel", "parallel", "arbitrary"),
        name=name,
    )(a, b, *extras)
    return outs[0] if n_out == 1 else outs


def ln_residual(x, y, g, b, alpha, name):
    s, w = x.shape
    tb = _tile(s, 256)
    row = pl.BlockSpec((tb, w), lambda i: (i, 0))
    vec = pl.BlockSpec((1, w), lambda i: (0, 0))

    def body(x_ref, y_ref, g_ref, b_ref, r_ref, xn_ref, xnb_ref):
        r = alpha * x_ref[...] + y_ref[...]
        mu = jnp.mean(r, axis=-1, keepdims=True)
        var = jnp.mean(jnp.square(r - mu), axis=-1, keepdims=True)
        xn = (r - mu) * lax.rsqrt(var + LN_EPS) * g_ref[...] + b_ref[...]
        r_ref[...] = r
        xn_ref[...] = xn
        xnb_ref[...] = xn.astype(BF16)

    return pl.pallas_call(
        body, grid=(s // tb,), in_specs=[row, row, vec, vec], out_specs=[row, row, row],
        out_shape=[jax.ShapeDtypeStruct((s, w), F32), jax.ShapeDtypeStruct((s, w), F32), jax.ShapeDtypeStruct((s, w), BF16)],
        compiler_params=_params("parallel"), name=name,
    )(x, y, g.reshape(1, w), b.reshape(1, w))


def _ln_bwd_tile(r, g, dxn):
    mu = jnp.mean(r, axis=-1, keepdims=True)
    cen = r - mu
    rstd = lax.rsqrt(jnp.mean(jnp.square(cen), axis=-1, keepdims=True) + LN_EPS)
    xhat = cen * rstd
    dxh = dxn * g
    dr = rstd * (dxh - jnp.mean(dxh, axis=-1, keepdims=True) - xhat * jnp.mean(dxh * xhat, axis=-1, keepdims=True))
    return dr, jnp.sum(dxn * xhat, axis=0, keepdims=True)


def ln_backward(r, g, dxn, name):
    s, w = r.shape
    tb = _tile(s, 256)
    row = pl.BlockSpec((tb, w), lambda i: (i, 0))
    vec = pl.BlockSpec((1, w), lambda i: (0, 0))

    def body(r_ref, g_ref, d_ref, dr_ref, drb_ref, dg_ref, db_ref):
        @pl.when(pl.program_id(0) == 0)
        def _():
            dg_ref[...] = jnp.zeros_like(dg_ref)
            db_ref[...] = jnp.zeros_like(db_ref)

        dxn = d_ref[...]
        dr, dg = _ln_bwd_tile(r_ref[...], g_ref[...], dxn)
        dr_ref[...] = dr
        drb_ref[...] = dr.astype(BF16)
        dg_ref[...] += dg
        db_ref[...] += jnp.sum(dxn, axis=0, keepdims=True)

    return pl.pallas_call(
        body, grid=(s // tb,), in_specs=[row, vec, row], out_specs=[row, row, vec, vec],
        out_shape=[jax.ShapeDtypeStruct((s, w), F32), jax.ShapeDtypeStruct((s, w), BF16),
                   jax.ShapeDtypeStruct((1, w), F32), jax.ShapeDtypeStruct((1, w), F32)],
        compiler_params=_params("arbitrary"), name=name,
    )(r, g.reshape(1, w), dxn)


def _causal_masks():
    qi = lax.broadcasted_iota(jnp.int32, (128, 128), 0)
    kj = lax.broadcasted_iota(jnp.int32, (128, 128), 1)
    return kj <= qi, kj >= qi


def _sub_rows(d, r):
    return pl.ds(r, 128, stride=d) if d > 1 else pl.ds(0, 128)


def _for_each_residue(d, fn):
    if d <= 4:
        for r in range(d):
            fn(r)
    else:
        def step(r, carry):
            fn(r)
            return carry
        lax.fori_loop(0, d, step, 0)


def attention_forward(proj, group, d, heads, name):
    s = proj.shape[0]
    rows = 128 * d
    nb = s // rows
    scale = HEAD_DIM ** -0.5
    qc, kc, vc = (group * 3) * heads, (group * 3 + 1) * heads, (group * 3 + 2) * heads

    def cur(col):
        return pl.BlockSpec((rows, 128), lambda h, n: (n, col + h))

    def prev(col):
        return pl.BlockSpec((rows, 128), lambda h, n: (jnp.maximum(n - 1, 0), col + h))

    out = pl.BlockSpec((rows, 128), lambda h, n: (n, h))

    def body(q_ref, kc_ref, kp_ref, vc_ref, vp_ref, o_ref, l_ref):
        mask_c, band_p = _causal_masks()
        mask_p = band_p & (pl.program_id(1) > 0)

        def one(r):
            sl = _sub_rows(d, r)
            q = q_ref[sl, :].astype(BF16)
            s_c = jnp.where(mask_c, _dot(q, kc_ref[sl, :].astype(BF16), NT) * scale, -jnp.inf)
            s_p = jnp.where(mask_p, _dot(q, kp_ref[sl, :].astype(BF16), NT) * scale, -jnp.inf)
            m = jnp.maximum(jnp.max(s_c, axis=-1, keepdims=True), jnp.max(s_p, axis=-1, keepdims=True))
            e_c, e_p = jnp.exp(s_c - m), jnp.exp(s_p - m)
            l = jnp.sum(e_c, axis=-1, keepdims=True) + jnp.sum(e_p, axis=-1, keepdims=True)
            o = _dot(e_c.astype(BF16), vc_ref[sl, :].astype(BF16), NN) + _dot(e_p.astype(BF16), vp_ref[sl, :].astype(BF16), NN)
            o_ref[sl, :] = o / l
            l_ref[sl, :] = jnp.broadcast_to(m + jnp.log(l), (128, 128))

        _for_each_residue(d, one)

    return pl.pallas_call(
        body, grid=(heads, nb),
        in_specs=[cur(qc), cur(kc), prev(kc), cur(vc), prev(vc)], out_specs=[out, out],
        out_shape=[jax.ShapeDtypeStruct((s, heads * 128), F32)] * 2,
        compiler_params=_params("parallel", "parallel"), name=name,
    )(proj, proj, proj, proj, proj)


def attention_combine(os_, ls_, name):
    s, w = os_[0].shape
    tb = _tile(s, 256)
    row = pl.BlockSpec((tb, w), lambda i: (i, 0))

    def body(o0, o1, o2, l0, l1, l2, o_ref, ob_ref, lse_ref):
        a, b, c = l0[...], l1[...], l2[...]
        m = jnp.maximum(jnp.maximum(a, b), c)
        ea, eb, ec = jnp.exp(a - m), jnp.exp(b - m), jnp.exp(c - m)
        tot = ea + eb + ec
        o = (ea / tot) * o0[...] + (eb / tot) * o1[...] + (ec / tot) * o2[...]
        o_ref[...] = o
        ob_ref[...] = o.astype(BF16)
        lse_ref[...] = m + jnp.log(tot)

    return pl.pallas_call(
        body, grid=(s // tb,), in_specs=[row] * 6, out_specs=[row] * 3,
        out_shape=[jax.ShapeDtypeStruct((s, w), F32), jax.ShapeDtypeStruct((s, w + MEM_WIDTH), BF16), jax.ShapeDtypeStruct((s, w), F32)],
        compiler_params=_params("parallel"), name=name,
    )(*os_, *ls_)


def attention_backward(proj, o, lse, dcat, group, d, heads, name):
    s = proj.shape[0]
    rows = 128 * d
    nb = s // rows
    scale = HEAD_DIM ** -0.5
    qc, kc, vc = (group * 3) * heads, (group * 3 + 1) * heads, (group * 3 + 2) * heads

    def at(col, shift):
        return pl.BlockSpec((rows, 128), lambda h, n: (jnp.clip(n + shift, 0, nb - 1), col + h))

    def out(col):
        return pl.BlockSpec((rows, 128), lambda h, n: (n, col + h))

    def body(qc_ref, qn_ref, kc_ref, kp_ref, vc_ref, vp_ref, doc_ref, don_ref, oc_ref, on_ref, lc_ref, ln_ref,
             dq_ref, dk_ref, dv_ref):
        n = pl.program_id(1)
        mask_c, band_p = _causal_masks()
        mask_b = band_p & (n > 0)
        mask_n = band_p & (n < nb - 1)

        def tile(q, k, v, do, lse_t, dsum, mask):
            p = jnp.where(mask, jnp.exp(_dot(q, k, NT) * scale - lse_t), 0.0)
            ds = (p * (_dot(do, v, NT) - dsum) * scale).astype(BF16)
            return p.astype(BF16), ds

        def one(r):
            sl = _sub_rows(d, r)
            q_c, q_n = qc_ref[sl, :].astype(BF16), qn_ref[sl, :].astype(BF16)
            k_c, k_p = kc_ref[sl, :].astype(BF16), kp_ref[sl, :].astype(BF16)
            v_c, v_p = vc_ref[sl, :].astype(BF16), vp_ref[sl, :].astype(BF16)
            do_c, do_n = doc_ref[sl, :], don_ref[sl, :]
            dsum_c = jnp.sum(do_c * oc_ref[sl, :], axis=-1, keepdims=True)
            dsum_n = jnp.sum(do_n * on_ref[sl, :], axis=-1, keepdims=True)
            do_c, do_n = do_c.astype(BF16), do_n.astype(BF16)
            p_a, ds_a = tile(q_c, k_c, v_c, do_c, lc_ref[sl, :], dsum_c, mask_c)
            _, ds_b = tile(q_c, k_p, v_p, do_c, lc_ref[sl, :], dsum_c, mask_b)
            p_n, ds_n = tile(q_n, k_c, v_c, do_n, ln_ref[sl, :], dsum_n, mask_n)
            dq_ref[sl, :] = _dot(ds_a, k_c, NN) + _dot(ds_b, k_p, NN)
            dk_ref[sl, :] = _dot(ds_a, q_c, TN) + _dot(ds_n, q_n, TN)
            dv_ref[sl, :] = _dot(p_a, do_c, TN) + _dot(p_n, do_n, TN)

        _for_each_residue(d, one)

    w = heads * 128
    return pl.pallas_call(
        body, grid=(heads, nb),
        in_specs=[at(qc, 0), at(qc, 1), at(kc, 0), at(kc, -1), at(vc, 0), at(vc, -1),
                  at(0, 0), at(0, 1), at(0, 0), at(0, 1), at(0, 0), at(0, 1)],
        out_specs=[out(0)] * 3,
        out_shape=[jax.ShapeDtypeStruct((s, w), F32)] * 3,
        compiler_params=_params("parallel", "parallel"), name=name,
    )(proj, proj, proj, proj, proj, proj, dcat, dcat, o, o, lse, lse)


def _mem_softmax(q, kv, h, scale):
    k = kv[:, h * 128:(h + 1) * 128].astype(BF16)
    v = kv[:, MEM_WIDTH + h * 128:MEM_WIDTH + (h + 1) * 128].astype(BF16)
    sc = _dot(q, k, NT) * scale
    e = jnp.exp(sc - jnp.max(sc, axis=-1, keepdims=True))
    return e / jnp.sum(e, axis=-1, keepdims=True), k, v


def memory_attention(proj, qcol, kv, cat, name):
    s = proj.shape[0]
    tb = _tile(s, 512)
    scale = HEAD_DIM ** -0.5

    def body(q_ref, kv_ref, cat_ref, o_ref):
        kv_t = kv_ref[...]
        for h in range(MEM_HEADS):
            p, _, v = _mem_softmax(q_ref[:, h * 128:(h + 1) * 128].astype(BF16), kv_t, h, scale)
            o_ref[:, h * 128:(h + 1) * 128] = _dot(p.astype(BF16), v, NN).astype(BF16)

    return pl.pallas_call(
        body, grid=(s // tb,),
        in_specs=[pl.BlockSpec((tb, MEM_WIDTH), lambda i: (i, qcol // MEM_WIDTH)), pl.BlockSpec(kv.shape, lambda i: (0, 0)), ANY],
        out_specs=pl.BlockSpec((tb, MEM_WIDTH), lambda i: (i, cat.shape[1] // MEM_WIDTH - 1)),
        out_shape=jax.ShapeDtypeStruct(cat.shape, BF16), input_output_aliases={2: 0},
        compiler_params=_params("parallel"), name=name,
    )(proj, kv, cat)


def memory_attention_backward(proj, qcol, kv, dcat, dcol, name):
    s = proj.shape[0]
    tb = _tile(s, 512)
    scale = HEAD_DIM ** -0.5

    def body(q_ref, kv_ref, do_ref, dq_ref, dkv_ref):
        @pl.when(pl.program_id(0) == 0)
        def _():
            dkv_ref[...] = jnp.zeros_like(dkv_ref)

        kv_t = kv_ref[...]
        for h in range(MEM_HEADS):
            cols = slice(h * 128, (h + 1) * 128)
            q = q_ref[:, cols].astype(BF16)
            p, k, v = _mem_softmax(q, kv_t, h, scale)
            do = do_ref[:, cols].astype(BF16)
            dp = _dot(do, v, NT)
            ds = (p * (dp - jnp.sum(dp * p, axis=-1, keepdims=True)) * scale).astype(BF16)
            dq_ref[:, cols] = _dot(ds, k, NN).astype(BF16)
            dkv_ref[:, cols] += _dot(ds, q, TN)
            dkv_ref[:, MEM_WIDTH + h * 128:MEM_WIDTH + (h + 1) * 128] += _dot(p.astype(BF16), do, TN)

    return pl.pallas_call(
        body, grid=(s // tb,),
        in_specs=[pl.BlockSpec((tb, MEM_WIDTH), lambda i: (i, qcol // MEM_WIDTH)), pl.BlockSpec(kv.shape, lambda i: (0, 0)),
                  pl.BlockSpec((tb, MEM_WIDTH), lambda i: (i, dcol // MEM_WIDTH))],
        out_specs=[pl.BlockSpec((tb, MEM_WIDTH), lambda i: (i, 0)), pl.BlockSpec(kv.shape, lambda i: (0, 0))],
        out_shape=[jax.ShapeDtypeStruct((s, MEM_WIDTH), BF16), jax.ShapeDtypeStruct(kv.shape, F32)],
        compiler_params=_params("arbitrary"), name=name,
    )(proj, kv, dcat)


_SQRT_HALF = math.sqrt(0.5)
_INV_SQRT_2PI = 1.0 / math.sqrt(2.0 * math.pi)


def _gelu(x):
    return 0.5 * x * (1.0 + lax.erf(x * _SQRT_HALF))


def _gelu_grad(x):
    return 0.5 * (1.0 + lax.erf(x * _SQRT_HALF)) + x * (_INV_SQRT_2PI * jnp.exp(-0.5 * x * x))


def _gmlp_specs(s, wd, groups, tb):
    half = lambda c: pl.BlockSpec((tb, wd), lambda i: (i, c))
    ws_spec = pl.BlockSpec((groups, 128, 128), lambda i: (0, 0, 0))
    bs_spec = pl.BlockSpec((groups, 128, 1), lambda i: (0, 0, 0))
    vec = pl.BlockSpec((1, wd), lambda i: (0, 0))
    return half, ws_spec, bs_spec, vec


def _vnorm(zv, g, b):
    mu = jnp.mean(zv, axis=-1, keepdims=True)
    var = jnp.mean(jnp.square(zv - mu), axis=-1, keepdims=True)
    return (zv - mu) * lax.rsqrt(var + LN_EPS) * g + b


def gmlp_forward(proj, ws, bs, vg, vb, name):
    s = proj.shape[0]
    groups = ws.shape[0]
    wd = groups * 128
    tb = _tile(s, 512)
    half, ws_spec, bs_spec, vec = _gmlp_specs(s, wd, groups, tb)

    def body(pu_ref, pv_ref, ws_ref, bs_ref, vg_ref, vb_ref, o_ref, vn_ref):
        causal, _ = _causal_masks()
        vn_ref[...] = _vnorm(_gelu(pv_ref[...]), vg_ref[...], vb_ref[...]).astype(BF16)
        for g in range(groups):
            cols = slice(g * 128, (g + 1) * 128)
            wm = jnp.where(causal, ws_ref[g], 0.0).astype(BF16)
            for c in range(tb // 128):
                rws = slice(c * 128, (c + 1) * 128)
                sg = _dot(wm, vn_ref[rws, cols], NN) + bs_ref[g]
                o_ref[rws, cols] = (_gelu(pu_ref[rws, cols]) * sg).astype(BF16)

    return pl.pallas_call(
        body, grid=(s // tb,),
        in_specs=[half(0), half(1), ws_spec, bs_spec, vec, vec], out_specs=pl.BlockSpec((tb, wd), lambda i: (i, 0)),
        out_shape=jax.ShapeDtypeStruct((s, wd + MEM_WIDTH), BF16),
        scratch_shapes=[pltpu.VMEM((tb, wd), BF16)],
        compiler_params=_params("parallel"), name=name,
    )(proj, proj, ws, bs.reshape(groups, 128, 1), vg.reshape(1, wd), vb.reshape(1, wd))


def gmlp_backward(proj, ws, bs, vg, vb, dcat, name):
    s = proj.shape[0]
    groups = ws.shape[0]
    wd = groups * 128
    tb = _tile(s, 512)
    half, ws_spec, bs_spec, vec = _gmlp_specs(s, wd, groups, tb)

    def body(pu_ref, pv_ref, ws_ref, bs_ref, vg_ref, vb_ref, do_ref, dpu_ref, dpv_ref, dws_ref, dbs_ref, dvg_ref, dvb_ref,
             vn_ref, dvn_ref):
        @pl.when(pl.program_id(0) == 0)
        def _():
            dws_ref[...] = jnp.zeros_like(dws_ref)
            dbs_ref[...] = jnp.zeros_like(dbs_ref)
            dvg_ref[...] = jnp.zeros_like(dvg_ref)
            dvb_ref[...] = jnp.zeros_like(dvb_ref)

        causal, _ = _causal_masks()
        pv = pv_ref[...]
        zv = _gelu(pv)
        vn_ref[...] = _vnorm(zv, vg_ref[...], vb_ref[...]).astype(BF16)
        for g in range(groups):
            cols = slice(g * 128, (g + 1) * 128)
            wm = jnp.where(causal, ws_ref[g], 0.0).astype(BF16)
            dws_g = jnp.zeros((128, 128), F32)
            dbs_g = jnp.zeros((128, 1), F32)
            for c in range(tb // 128):
                rws = slice(c * 128, (c + 1) * 128)
                vn = vn_ref[rws, cols]
                pu = pu_ref[rws, cols]
                do = do_ref[rws, cols]
                sg = _dot(wm, vn, NN) + bs_ref[g]
                dpu_ref[rws, cols] = (do * sg * _gelu_grad(pu)).astype(BF16)
                dsg = do * _gelu(pu)
                dsg_b = dsg.astype(BF16)
                dws_g += _dot(dsg_b, vn, NT)
                dbs_g += jnp.sum(dsg, axis=-1, keepdims=True)
                dvn_ref[rws, cols] = _dot(wm, dsg_b, TN)
            dws_ref[g] += jnp.where(causal, dws_g, 0.0)
            dbs_ref[g] += dbs_g
        dvn = dvn_ref[...]
        dzv, dvg = _ln_bwd_tile(zv, vg_ref[...], dvn)
        dvg_ref[...] += dvg
        dvb_ref[...] += jnp.sum(dvn, axis=0, keepdims=True)
        dpv_ref[...] = (dzv * _gelu_grad(pv)).astype(BF16)

    row = pl.BlockSpec((tb, wd), lambda i: (i, 0))
    dpu, dpv, dws, dbs, dvg, dvb = pl.pallas_call(
        body, grid=(s // tb,),
        in_specs=[half(0), half(1), ws_spec, bs_spec, vec, vec, row],
        out_specs=[row, row, ws_spec, bs_spec, vec, vec],
        out_shape=[jax.ShapeDtypeStruct((s, wd), BF16), jax.ShapeDtypeStruct((s, wd), BF16),
                   jax.ShapeDtypeStruct((groups, 128, 128), F32), jax.ShapeDtypeStruct((groups, 128, 1), F32),
                   jax.ShapeDtypeStruct((1, wd), F32), jax.ShapeDtypeStruct((1, wd), F32)],
        scratch_shapes=[pltpu.VMEM((tb, wd), BF16), pltpu.VMEM((tb, wd), F32)],
        compiler_params=_params("arbitrary"), name=name,
    )(proj, proj, ws, bs.reshape(groups, 128, 1), vg.reshape(1, wd), vb.reshape(1, wd), dcat)
    return dpu, dpv, dws, dbs.reshape(groups, 128), dvg, dvb


def loss_head(y, target, name):
    s, w = y.shape
    tb = _tile(s, 256)
    row = pl.BlockSpec((tb, w), lambda i: (i, 0))
    nsteps = s // tb

    def body(y_ref, t_ref, loss_ref, dy_ref, acc_ref):
        i = pl.program_id(0)

        @pl.when(i == 0)
        def _():
            acc_ref[...] = jnp.zeros_like(acc_ref)

        err = y_ref[...] - t_ref[...]
        dy_ref[...] = err / w
        acc_ref[...] += jnp.sum(jnp.mean(jnp.square(err), axis=-1, keepdims=True), axis=0, keepdims=True)

        @pl.when(i == nsteps - 1)
        def _():
            loss_ref[...] = jnp.broadcast_to(0.5 * acc_ref[...], loss_ref.shape)

    return pl.pallas_call(
        body, grid=(nsteps,), in_specs=[row, row],
        out_specs=[pl.BlockSpec((8, 128), lambda i: (0, 0)), row],
        out_shape=[jax.ShapeDtypeStruct((8, 128), F32), jax.ShapeDtypeStruct((s, w), F32)],
        scratch_shapes=[pltpu.VMEM((1, 1), F32)],
        compiler_params=_params("arbitrary"), name=name,
    )(y, target)


def _my_position():
    return lax.axis_index("x"), lax.axis_index("y"), lax.axis_index("c")


def _comm_call(body, arrays, out_shapes, n_sems, name):
    return pl.pallas_call(
        body, out_shape=out_shapes, in_specs=[ANY] * len(arrays), out_specs=[ANY] * len(out_shapes),
        scratch_shapes=[pltpu.SemaphoreType.DMA((n_sems,)), pltpu.SemaphoreType.DMA((n_sems,)),
                        pltpu.SemaphoreType.DMA((len(arrays),))],
        name=name,
    )(*arrays)


def all_gather(arrays, name):
    n = len(arrays)

    def body(*refs):
        v_refs, out_refs, (send_sems, recv_sems, local_sems) = refs[:n], refs[n:2 * n], refs[2 * n:]
        x, y, c = _my_position()
        me, sibling = (x, y, c), (x, y, 1 - c)
        chips = [(1 - x, y), (x, 1 - y), (1 - x, 1 - y)]

        def copy(a, k, block, to, from_input=False):
            px, py, pc = block
            slot = out_refs[a].at[4 * px + 2 * py + pc]
            return pltpu.make_async_remote_copy(
                src_ref=v_refs[a] if from_input else slot, dst_ref=slot,
                send_sem=send_sems.at[7 * a + k], recv_sem=recv_sems.at[7 * a + k], device_id=to, device_id_type=MESH)

        mine = [pltpu.make_async_copy(v_refs[a], out_refs[a].at[4 * x + 2 * y + c], local_sems.at[a]) for a in range(n)]
        for cp in mine:
            cp.start()
        first = []
        for a in range(n):
            first.append(copy(a, 0, me, sibling, True))
            first += [copy(a, 1 + j, me, (*chip, c), True) for j, chip in enumerate(chips)]
        for cp in first:
            cp.start()
        passed = []
        for j, chip in enumerate(chips):
            for a in range(n):
                copy(a, 1 + j, (*chip, c), me).wait_recv()
                passed.append(copy(a, 4 + j, (*chip, c), sibling))
                passed[-1].start()
        for a in range(n):
            copy(a, 0, sibling, me).wait_recv()
            for j, chip in enumerate(chips):
                copy(a, 4 + j, (*chip, 1 - c), me).wait_recv()
        for cp in first + passed:
            cp.wait_send()
        for cp in mine:
            cp.wait()

    return _comm_call(body, arrays, [jax.ShapeDtypeStruct((N_DEV,) + v.shape, v.dtype) for v in arrays], 7 * n, name)


def all_to_all(arrays, name):
    n = len(arrays)
    relations = [(dx, dy, dc) for dx in (0, 1) for dy in (0, 1) for dc in (0, 1)][1:]

    def body(*refs):
        p_refs, out_refs, (send_sems, recv_sems, local_sems) = refs[:n], refs[n:2 * n], refs[2 * n:]
        x, y, c = _my_position()
        me = 4 * x + 2 * y + c
        mine = [pltpu.make_async_copy(p_refs[a].at[me], out_refs[a].at[me], local_sems.at[a]) for a in range(n)]
        for cp in mine:
            cp.start()
        sends, arrivals = [], []
        for a in range(n):
            for k, (dx, dy, dc) in enumerate(relations):
                px, py, pc = (x + dx) % 2, (y + dy) % 2, (c + dc) % 2
                peer = 4 * px + 2 * py + pc
                for dst, group in ((out_refs[a].at[me], sends), (out_refs[a].at[peer], arrivals)):
                    group.append(pltpu.make_async_remote_copy(
                        src_ref=p_refs[a].at[peer], dst_ref=dst, send_sem=send_sems.at[7 * a + k], recv_sem=recv_sems.at[7 * a + k],
                        device_id=(px, py, pc), device_id_type=MESH))
        for cp in sends:
            cp.start()
        for cp in arrivals:
            cp.wait_recv()
        for cp in sends:
            cp.wait_send()
        for cp in mine:
            cp.wait()

    return _comm_call(body, arrays, [jax.ShapeDtypeStruct(p.shape, p.dtype) for p in arrays], 7 * n, name)


def _row_tile(r, c, budget):
    tr = r
    while tr * c > budget and tr % 16 == 0:
        tr //= 2
    return tr


def _sum_in_device_order(p_ref):
    g = p_ref[0].astype(F32)
    for j in range(1, p_ref.shape[0]):
        g = g + p_ref[j].astype(F32)
    return g


def _adamw_update(g, w, m, v):
    nm = ADAM_B1 * m + (1.0 - ADAM_B1) * g
    nv = ADAM_B2 * v + (1.0 - ADAM_B2) * jnp.square(g)
    m_hat = nm / (1.0 - ADAM_B1 ** ADAM_STEP)
    v_hat = nv / (1.0 - ADAM_B2 ** ADAM_STEP)
    return -ADAM_LR * (m_hat / (jnp.sqrt(v_hat) + ADAM_EPS) + ADAM_WD * w), nm, nv


def sum_parts(parts, name):
    p, r, c = parts.shape
    tr = _row_tile(r, c, 256 * 1024)

    def body(p_ref, g_ref):
        g_ref[...] = _sum_in_device_order(p_ref)

    return pl.pallas_call(
        body, grid=(r // tr,), in_specs=[pl.BlockSpec((p, tr, c), lambda i: (0, i, 0))],
        out_specs=pl.BlockSpec((tr, c), lambda i: (i, 0)), out_shape=jax.ShapeDtypeStruct((r, c), F32),
        compiler_params=_params("parallel"), name=name,
    )(parts)


def adamw(parts, w, m, v, name, emit_grad=True):
    p, r, c = parts.shape
    tr = _row_tile(r, c, 160 * 1024)
    n_out = 4 if emit_grad else 3

    def body(p_ref, w_ref, m_ref, v_ref, *out_refs):
        g = _sum_in_device_order(p_ref)
        vals = _adamw_update(g, w_ref[...], m_ref[...], v_ref[...])
        for o_ref, val in zip(out_refs, ((g,) + vals) if emit_grad else vals):
            o_ref[...] = val

    row = pl.BlockSpec((tr, c), lambda i: (i, 0))
    return pl.pallas_call(
        body, grid=(r // tr,), in_specs=[pl.BlockSpec((p, tr, c), lambda i: (0, i, 0)), row, row, row],
        out_specs=[row] * n_out, out_shape=[jax.ShapeDtypeStruct((r, c), F32)] * n_out,
        compiler_params=_params("parallel"), name=name,
    )(parts, w, m, v)


def adamw_layers(parts_list, w, m, v, name):
    n_layers, r, c = w.shape
    p = parts_list[0].shape[0]
    tr = _row_tile(r, c, 160 * 1024)

    def parts_spec(q):
        return pl.BlockSpec((p, tr, c), lambda l, i: (0, jnp.where(l == q, i, 0), 0))

    def body(*refs):
        p_refs, (w_ref, m_ref, v_ref), out_refs = refs[:n_layers], refs[n_layers:n_layers + 3], refs[n_layers + 3:]
        for q in range(n_layers):
            @pl.when(pl.program_id(0) == q)
            def _():
                g = _sum_in_device_order(p_refs[q])
                vals = _adamw_update(g, w_ref[...], m_ref[...], v_ref[...])
                for o_ref, val in zip(out_refs, (g,) + vals):
                    o_ref[...] = val

    state = pl.BlockSpec((None, tr, c), lambda l, i: (l, i, 0))
    return pl.pallas_call(
        body, grid=(n_layers, r // tr), in_specs=[parts_spec(q) for q in range(n_layers)] + [state] * 3,
        out_specs=[state] * 4, out_shape=[jax.ShapeDtypeStruct(w.shape, F32)] * 4,
        compiler_params=_params("arbitrary", "arbitrary"), name=name,
    )(*parts_list, w, m, v)


def kernel(x, mem, w_in_a, w_in_b, w_s, b_s, vnorm_g, vnorm_b, w_mem_kv, w_out, ln1_g, ln1_b, w_ff1, w_ff2, ln2_g, ln2_b, loss_target, m_w_in_a, m_w_in_b, m_w_s, m_b_s, m_vnorm_g, m_vnorm_b, m_w_mem_kv, m_w_out, m_ln1_g, m_ln1_b, m_w_ff1, m_w_ff2, m_ln2_g, m_ln2_b, v_w_in_a, v_w_in_b, v_w_s, v_b_s, v_vnorm_g, v_vnorm_b, v_w_mem_kv, v_w_out, v_ln1_g, v_ln1_b, v_w_ff1, v_w_ff2, v_ln2_g, v_ln2_b):
    depth = w_ff1.shape[0]
    groups = w_s.shape[1]
    mix_w = groups * HEAD_DIM
    n_b = vnorm_g.shape[0]
    alpha = (2.0 * depth) ** 0.25
    me = 4 * lax.axis_index("x") + 2 * lax.axis_index("y") + lax.axis_index("c")
    x0 = x[0]
    mem_b = mem[0].astype(BF16)
    target = loss_target[0]

    def gather_layer(i):
        w_in = (w_in_a if i % 2 == 0 else w_in_b)[i // 2]
        shards = [w_in.T, w_out[i].T, w_ff1[i].T, w_ff2[i], w_mem_kv[i]]
        gathered = all_gather([t.astype(BF16) for t in shards], name=f"gather_weights_{i % 2}")
        return [g.reshape(N_DEV * t.shape[0], t.shape[1]) for g, t in zip(gathered, shards)]

    weights = [gather_layer(i) for i in range(depth)]
    (vnorm,) = all_gather([jnp.concatenate([vnorm_g, vnorm_b], axis=0)], name="gather_vnorm")
    vnorm = jnp.transpose(vnorm, (1, 0, 2)).reshape(2 * n_b, mix_w)
    vg_full, vb_full = vnorm[:n_b], vnorm[n_b:]

    saved = []
    xf, xb = x0, x0.astype(BF16)
    for i in range(depth):
        w_in_t, w_o_t, w_1_t, w_2, w_kv = weights[i]
        j = i // 2
        proj = matmul(xb, w_in_t, "nt", name=f"in_proj_{i % 2}")
        kv = matmul(mem_b, w_kv, "nn", name="mem_kv")
        if i % 2 == 0:
            os_, ls_ = [], []
            for g, d in enumerate(A_DILATIONS):
                o_g, l_g = attention_forward(proj, g, d, groups, name=f"attn_fwd_d{d}")
                os_.append(o_g)
                ls_.append(l_g)
            o, cat, lse = attention_combine(os_, ls_, name="attn_combine")
            qcol = 9 * mix_w
            extra = (o, lse)
        else:
            cat = gmlp_forward(proj, w_s[j], b_s[j], vg_full[j], vb_full[j], name="gmlp_fwd")
            qcol = 2 * mix_w
            extra = ()
        cat = memory_attention(proj, qcol, kv, cat, name=f"mem_attn_{i % 2}")
        y1 = matmul(cat, w_o_t, "nt", name="out_proj")
        r1, x1, x1b = ln_residual(xf, y1, ln1_g[i], ln1_b[i], alpha, name="ln1")
        h, hid = matmul(x1b, w_1_t, "nt", name="ff1", out_dtypes=(F32, BF16),
                        epilogue=lambda acc: (acc, jnp.square(jnp.maximum(acc, 0.0))))
        y2 = matmul(hid, w_2, "nn", name="ff2")
        r2, x2, x2b = ln_residual(x1, y2, ln2_g[i], ln2_b[i], alpha, name="ln2")
        saved.append((xb, proj, kv, extra, qcol, cat, r1, x1b, h, hid, r2))
        xf, xb = x2, x2b

    loss_tile, dx = loss_head(xf, target, name="loss_head")
    loss = lax.psum(loss_tile[0, 0], ("x", "y", "c"))

    def exchange(dw, tag):
        rows, cols = dw.shape
        (got,) = all_to_all([dw.reshape(N_DEV, rows // N_DEV, cols)], name=f"exchange_{tag}")
        return got

    recv = {n: [None] * depth for n in ("w_in", "w_mem_kv", "w_out", "w_ff1", "w_ff2")}
    small = {n: [None] * depth for n in ("ln1_g", "ln1_b", "ln2_g", "ln2_b")}
    small_b = {n: [None] * n_b for n in ("w_s", "b_s", "vnorm_g", "vnorm_b")}
    for i in reversed(range(depth)):
        w_in_t, w_o_t, w_1_t, w_2, w_kv = weights[i]
        xb, proj, kv, extra, qcol, cat, r1, x1b, h, hid, r2 = saved[i]
        j = i // 2
        dr2, dr2b, small["ln2_g"][i], small["ln2_b"][i] = ln_backward(r2, ln2_g[i], dx, name="ln2_bwd")
        dh = matmul(dr2b, w_2, "nt", name="ff2_dx", extras=(h,), out_dtypes=(BF16,),
                    epilogue=lambda acc, h_t: (acc * (2.0 * jnp.maximum(h_t, 0.0)),))
        recv["w_ff2"][i] = exchange(matmul(hid, dr2b, "tn", name="ff2_dw", out_dtypes=(BF16,)), "w_ff2")
        dx1 = matmul(dh, w_1_t, "nn", name="ff1_dx", extras=(dr2,), epilogue=lambda acc, res: (acc + alpha * res,))
        recv["w_ff1"][i] = exchange(matmul(dh, x1b, "tn", name="ff1_dw", out_dtypes=(BF16,)), "w_ff1")
        dr1, dr1b, small["ln1_g"][i], small["ln1_b"][i] = ln_backward(r1, ln1_g[i], dx1, name="ln1_bwd")
        dcat = matmul(dr1b, w_o_t, "nn", name="out_proj_dx")
        recv["w_out"][i] = exchange(matmul(dr1b, cat, "tn", name="out_proj_dw", out_dtypes=(BF16,)), "w_out")
        dq_mem, dkv = memory_attention_backward(proj, qcol, kv, dcat, mix_w, name=f"mem_attn_bwd_{i % 2}")
        recv["w_mem_kv"][i] = exchange(matmul(mem_b, dkv, "tn", name="mem_kv_dw", out_dtypes=(BF16,)), "w_mem_kv")
        if i % 2 == 0:
            o, lse = extra
            pieces = []
            for g, d in enumerate(A_DILATIONS):
                pieces += [t.astype(BF16) for t in attention_backward(proj, o, lse, dcat, g, d, groups, name=f"attn_bwd_d{d}")]
            dproj = jnp.concatenate(pieces + [dq_mem], axis=1)
        else:
            dpu, dpv, small_b["w_s"][j], small_b["b_s"][j], small_b["vnorm_g"][j], small_b["vnorm_b"][j] = gmlp_backward(
                proj, w_s[j], b_s[j], vg_full[j], vb_full[j], dcat, name="gmlp_bwd")
            dproj = jnp.concatenate([dpu, dpv, dq_mem], axis=1)
        recv["w_in"][i] = exchange(matmul(dproj, xb, "tn", name=f"in_proj_dw_{i % 2}", out_dtypes=(BF16,)), f"w_in_{i % 2}")
        dx = matmul(dproj, w_in_t, "nn", name=f"in_proj_dx_{i % 2}", extras=(dr1,), epilogue=lambda acc, res: (acc + alpha * res,))
    grad_x = dx[None]

    flat_small = jnp.concatenate(
        [jnp.concatenate(small[n], axis=0).reshape(-1) for n in ("ln1_g", "ln1_b", "ln2_g", "ln2_b")]
        + [jnp.stack(small_b[n]).reshape(-1) for n in ("w_s", "b_s", "vnorm_g", "vnorm_b")])
    pad = (-flat_small.size) % 1024
    (gathered_small,) = all_gather([jnp.pad(flat_small, (0, pad)).reshape(-1, 1024)], name="gather_small_grads")
    gathered_small = gathered_small.reshape(N_DEV, -1)

    results = {}
    at = 0

    def take(shape):
        nonlocal at
        size = math.prod(shape)
        at += size
        return gathered_small[:, at - size:at].reshape((N_DEV,) + shape)

    for n, w_, m_, v_ in (("ln1_g", ln1_g, m_ln1_g, v_ln1_g), ("ln1_b", ln1_b, m_ln1_b, v_ln1_b),
                          ("ln2_g", ln2_g, m_ln2_g, v_ln2_g), ("ln2_b", ln2_b, m_ln2_b, v_ln2_b)):
        results[n] = adamw(take(w_.shape), w_, m_, v_, name="adamw_ln")
    lanes = lambda t: t.reshape(-1, 128)
    results["w_s"] = [t.reshape(w_s.shape) for t in adamw(take((w_s.size // 128, 128)), lanes(w_s), lanes(m_w_s), lanes(v_w_s), name="adamw_w_s")]
    results["b_s"] = [t.reshape(b_s.shape) for t in adamw(take((b_s.size // 128, 128)), lanes(b_s), lanes(m_b_s), lanes(v_b_s), name="adamw_b_s")]
    for n, w_, m_, v_ in (("vnorm_g", vnorm_g, m_vnorm_g, v_vnorm_g), ("vnorm_b", vnorm_b, m_vnorm_b, v_vnorm_b)):
        parts = lax.dynamic_slice_in_dim(take((n_b, mix_w)), me * w_.shape[1], w_.shape[1], axis=2)
        results[n] = adamw(parts, w_, m_, v_, name="adamw_vnorm")

    def transposed(name, layers, w_, m_, v_):
        grad = jnp.stack([sum_parts(recv[name][i], name=f"sum_{name}_{i % 2}").T for i in layers])
        flat = lambda t: t.reshape(-1, t.shape[-1])
        upd = adamw(flat(grad)[None], flat(w_), flat(m_), flat(v_), name=f"adamw_{name}_{layers[0] % 2}", emit_grad=False)
        return [grad] + [t.reshape(w_.shape) for t in upd]

    results["w_in_a"] = transposed("w_in", list(range(0, depth, 2)), w_in_a, m_w_in_a, v_w_in_a)
    results["w_in_b"] = transposed("w_in", list(range(1, depth, 2)), w_in_b, m_w_in_b, v_w_in_b)
    results["w_out"] = transposed("w_out", list(range(depth)), w_out, m_w_out, v_w_out)
    results["w_ff1"] = transposed("w_ff1", list(range(depth)), w_ff1, m_w_ff1, v_w_ff1)
    results["w_mem_kv"] = adamw_layers(recv["w_mem_kv"], w_mem_kv, m_w_mem_kv, v_w_mem_kv, name="adamw_w_mem_kv")
    results["w_ff2"] = adamw_layers(recv["w_ff2"], w_ff2, m_w_ff2, v_w_ff2, name="adamw_w_ff2")

    order = ("w_in_a", "w_in_b", "w_s", "b_s", "vnorm_g", "vnorm_b", "w_mem_kv", "w_out", "ln1_g", "ln1_b", "w_ff1", "w_ff2", "ln2_g", "ln2_b")
    return (loss, grad_x, *[results[n][0] for n in order], *[results[n][1] for n in order],
            *[results[n][2] for n in order], *[results[n][3] for n in order])
```

```python
import math

import jax
import jax.numpy as jnp
from jax import lax
from jax.experimental import pallas as pl
from jax.experimental.pallas import tpu as pltpu

F32 = jnp.float32
BF16 = jnp.bfloat16
N_DEV = 8
HEAD_DIM = 128
MEM_HEADS = 4
MEM_WIDTH = MEM_HEADS * HEAD_DIM
A_DILATIONS = (1, 4, 16)
LN_EPS = 1e-5
ADAM_LR, ADAM_B1, ADAM_B2, ADAM_EPS, ADAM_WD, ADAM_STEP = 0.001, 0.9, 0.999, 1e-08, 0.01, 10
VMEM_LIMIT_BYTES = 56 * 1024 * 1024
MESH = pl.DeviceIdType.MESH
ANY = pl.BlockSpec(memory_space=pl.ANY)


def _params(*sem):
    return pltpu.CompilerParams(dimension_semantics=sem, vmem_limit_bytes=VMEM_LIMIT_BYTES)


def _tile(dim, pref):
    if dim <= pref:
        return dim
    best = None
    for t in range(128, pref + 1, 128):
        if dim % t == 0:
            best = t
    return best if best is not None else dim


def _dot(a, b, dims):
    return lax.dot_general(a, b, (dims, ((), ())), preferred_element_type=F32)


NN = ((1,), (0,))
NT = ((1,), (1,))
TN = ((0,), (0,))


def _my_position():
    return lax.axis_index("x"), lax.axis_index("y"), lax.axis_index("c")


class GatherJob:
    US_PER_MB = 46.0

    def __init__(self, arrays, tag):
        self.arrays, self.tag, self.outs = list(arrays), tag, None
        self.out_shapes = [jax.ShapeDtypeStruct((N_DEV,) + v.shape, v.dtype) for v in self.arrays]
        self.n_sems, self.n_local = 7 * len(self.arrays), len(self.arrays)
        self.est_us = self.US_PER_MB * sum(v.size * v.dtype.itemsize for v in self.arrays) / 1e6

    def phases(self, v_refs, out_refs, send_sems, recv_sems, local_sems, sem0, loc0):
        n = len(self.arrays)
        x, y, c = _my_position()
        me, sibling = (x, y, c), (x, y, 1 - c)
        chips = [(1 - x, y), (x, 1 - y), (1 - x, 1 - y)]

        def copy(a, k, block, to, from_input=False):
            px, py, pc = block
            slot = out_refs[a].at[4 * px + 2 * py + pc]
            return pltpu.make_async_remote_copy(
                src_ref=v_refs[a] if from_input else slot, dst_ref=slot, send_sem=send_sems.at[sem0 + 7 * a + k],
                recv_sem=recv_sems.at[sem0 + 7 * a + k], device_id=to, device_id_type=MESH)

        def mine(a):
            return pltpu.make_async_copy(v_refs[a], out_refs[a].at[4 * x + 2 * y + c], local_sems.at[loc0 + a])

        def first(a):
            return [copy(a, 0, me, sibling, True)] + [copy(a, 1 + j, me, (*chip, c), True) for j, chip in enumerate(chips)]

        def start():
            for a in range(n):
                mine(a).start()
                for cp in first(a):
                    cp.start()

        def middle():
            for j, chip in enumerate(chips):
                for a in range(n):
                    copy(a, 1 + j, (*chip, c), me).wait_recv()
                    copy(a, 4 + j, (*chip, c), sibling).start()

        def finish():
            for a in range(n):
                copy(a, 0, sibling, me).wait_recv()
                for j, chip in enumerate(chips):
                    copy(a, 4 + j, (*chip, 1 - c), me).wait_recv()
                for cp in first(a) + [copy(a, 4 + j, (*chip, c), sibling) for j, chip in enumerate(chips)]:
                    cp.wait_send()
                mine(a).wait()

        return start, middle, finish


class ExchangeJob:
    US_PER_MB = 11.0
    RELATIONS = [(dx, dy, dc) for dx in (0, 1) for dy in (0, 1) for dc in (0, 1)][1:]

    def __init__(self, p, row0, rows, tag):
        self.arrays, self.tag, self.outs = [p], tag, None
        self.row0, self.rows = row0, rows
        self.out_shapes = [jax.ShapeDtypeStruct((N_DEV, rows, p.shape[2]), p.dtype)]
        self.n_sems, self.n_local = 7, 1
        self.est_us = self.US_PER_MB * N_DEV * rows * p.shape[2] * p.dtype.itemsize / 1e6

    def phases(self, p_refs, out_refs, send_sems, recv_sems, local_sems, sem0, loc0):
        (p_ref,), (out_ref,) = p_refs, out_refs
        x, y, c = _my_position()
        me = 4 * x + 2 * y + c
        chunk = pl.ds(self.row0, self.rows)

        def mine():
            return pltpu.make_async_copy(p_ref.at[me, chunk], out_ref.at[me], local_sems.at[loc0])

        def copies(arriving):
            made = []
            for k, (dx, dy, dc) in enumerate(self.RELATIONS):
                px, py, pc = (x + dx) % 2, (y + dy) % 2, (c + dc) % 2
                peer = 4 * px + 2 * py + pc
                made.append(pltpu.make_async_remote_copy(
                    src_ref=p_ref.at[peer, chunk], dst_ref=out_ref.at[peer if arriving else me],
                    send_sem=send_sems.at[sem0 + k], recv_sem=recv_sems.at[sem0 + k],
                    device_id=(px, py, pc), device_id_type=MESH))
            return made

        def start():
            mine().start()
            for send in copies(False):
                send.start()

        def finish():
            for arrival in copies(True):
                arrival.wait_recv()
            for send in copies(False):
                send.wait_send()
            mine().wait()

        return start, (lambda: None), finish


class Exchanges:
    MIN_HOST_US = 80.0
    OVERSHOOT = 1.3

    def __init__(self):
        self.queue = []

    def submit(self, job):
        self.queue.append(job)
        return job

    def take(self, host_us):
        jobs, used = [], 0.0
        if host_us >= self.MIN_HOST_US:
            while self.queue and used + self.queue[0].est_us <= self.OVERSHOOT * host_us:
                used += self.queue[0].est_us
                jobs.append(self.queue.pop(0))
        return jobs

    def require(self, job):
        if job.outs is None:
            at = self.queue.index(job)
            jobs, self.queue = self.queue[:at + 1], self.queue[at + 1:]
            _call(lambda: None, grid=(), in_specs=[], out_specs=[], out_shape=[], scratch_shapes=[], args=[],
                  name="exchange", semantics=(), jobs=jobs)
        return job.outs

    def flush(self):
        if self.queue:
            self.require(self.queue[-1])


def _call(body, *, grid, in_specs, out_specs, out_shape, scratch_shapes, args, name, semantics, jobs=()):
    if not jobs:
        return pl.pallas_call(body, grid=grid, in_specs=in_specs, out_specs=out_specs, out_shape=out_shape,
                              scratch_shapes=scratch_shapes, compiler_params=_params(*semantics), name=name)(*args)
    n_in, n_out, n_scr = len(in_specs), len(out_shape), len(scratch_shapes)
    j_in = [a for job in jobs for a in job.arrays]
    j_out = [s for job in jobs for s in job.out_shapes]
    n_sems, n_local = sum(job.n_sems for job in jobs), sum(job.n_local for job in jobs)
    steps = math.prod(grid)
    middle_step = (3 * steps) // 4

    def wrapped(*refs):
        ins, refs = refs[:n_in], refs[n_in:]
        j_ins, refs = refs[:len(j_in)], refs[len(j_in):]
        outs, refs = refs[:n_out], refs[n_out:]
        j_outs, refs = refs[:len(j_out)], refs[len(j_out):]
        scratch, (send_sems, recv_sems, local_sems) = refs[:n_scr], refs[n_scr:]
        step = 0
        for axis, extent in enumerate(grid):
            step = step * extent + pl.program_id(axis)
        phases, at_in, at_out, sem0, loc0 = [], 0, 0, 0, 0
        for job in jobs:
            k_in, k_out = len(job.arrays), len(job.out_shapes)
            phases.append(job.phases(j_ins[at_in:at_in + k_in], j_outs[at_out:at_out + k_out], send_sems, recv_sems, local_sems, sem0, loc0))
            at_in, at_out, sem0, loc0 = at_in + k_in, at_out + k_out, sem0 + job.n_sems, loc0 + job.n_local

        def run(which):
            for ph in phases:
                ph[which]()

        if steps == 1:
            run(0)
            body(*ins, *outs, *scratch)
            run(1)
            run(2)
        else:
            pl.when(step == 0)(lambda: run(0))
            body(*ins, *outs, *scratch)
            pl.when(step == middle_step)(lambda: run(1))
            pl.when(step == steps - 1)(lambda: run(2))

    res = pl.pallas_call(
        wrapped, grid=grid, in_specs=list(in_specs) + [ANY] * len(j_in), out_specs=list(out_specs) + [ANY] * len(j_out),
        out_shape=list(out_shape) + j_out,
        scratch_shapes=list(scratch_shapes) + [pltpu.SemaphoreType.DMA((n_sems,)), pltpu.SemaphoreType.DMA((n_sems,)),
                                               pltpu.SemaphoreType.DMA((n_local,))],
        compiler_params=_params(*(["arbitrary"] * len(grid))), name=name + "".join("__" + job.tag for job in jobs),
    )(*args, *j_in)
    at = n_out
    for job in jobs:
        job.outs = list(res[at:at + len(job.out_shapes)])
        at += len(job.out_shapes)
    return list(res[:n_out])


MATMUL_TILES = {"nn": (1024, 512, 2560), "nt": (1024, 512, 2560), "tn": (1024, 1024, 512)}
MATMUL_FLOPS_PER_US = 6.5e8


def matmul(a, b, mode, name, epilogue=None, extras=(), out_dtypes=(F32,), comm=None):
    if mode == "nn":
        (m, k), (k2, n) = a.shape, b.shape
    elif mode == "nt":
        (m, k), (n, k2) = a.shape, b.shape
    else:
        (k, m), (k2, n) = a.shape, b.shape
    assert k == k2, (a.shape, b.shape, mode)
    tm_pref, tn_pref, tk_pref = MATMUL_TILES[mode]
    tm, tn, tk = _tile(m, tm_pref), _tile(n, tn_pref), _tile(k, tk_pref)
    nk = k // tk
    dims = {"nn": NN, "nt": NT, "tn": TN}[mode]
    a_spec = pl.BlockSpec((tk, tm), lambda i, j, kk: (kk, i)) if mode == "tn" else pl.BlockSpec((tm, tk), lambda i, j, kk: (i, kk))
    b_spec = pl.BlockSpec((tn, tk), lambda i, j, kk: (j, kk)) if mode == "nt" else pl.BlockSpec((tk, tn), lambda i, j, kk: (kk, j))
    tile_spec = pl.BlockSpec((tm, tn), lambda i, j, kk: (i, j))
    n_ex, n_out = len(extras), len(out_dtypes)
    if epilogue is None:
        epilogue = lambda acc: (acc,) * n_out

    def body(a_ref, b_ref, *rest):
        ex_refs, out_refs, acc_ref = rest[:n_ex], rest[n_ex:n_ex + n_out], rest[-1]
        kk = pl.program_id(2)

        @pl.when(kk == 0)
        def _():
            acc_ref[...] = jnp.zeros_like(acc_ref)

        acc_ref[...] += _dot(a_ref[...].astype(BF16), b_ref[...].astype(BF16), dims)

        @pl.when(kk == nk - 1)
        def _():
            outs = epilogue(acc_ref[...], *[e[...] for e in ex_refs])
            for o_ref, val in zip(out_refs, outs):
                o_ref[...] = val.astype(o_ref.dtype)

    host_us = 2.0 * m * n * k / MATMUL_FLOPS_PER_US
    outs = _call(
        body,
        grid=(m // tm, n // tn, nk),
        in_specs=[a_spec, b_spec] + [tile_spec] * n_ex,
        out_specs=[tile_spec] * n_out,
        out_shape=[jax.ShapeDtypeStruct((m, n), dt) for dt in out_dtypes],
        scratch_shapes=[pltpu.VMEM((tm, tn), F32)],
        args=[a, b, *extras], name=name, semantics=("parallel", "parallel", "arbitrary"),
        jobs=comm.take(host_us) if comm is not None else (),
    )
    return outs[0] if n_out == 1 else outs


def ln_residual(x, y, g, b, alpha, name):
    s, w = x.shape
    tb = _tile(s, 256)
    row = pl.BlockSpec((tb, w), lambda i: (i, 0))
    vec = pl.BlockSpec((1, w), lambda i: (0, 0))

    def body(x_ref, y_ref, g_ref, b_ref, r_ref, xn_ref, xnb_ref):
        r = alpha * x_ref[...] + y_ref[...]
        mu = jnp.mean(r, axis=-1, keepdims=True)
        var = jnp.mean(jnp.square(r - mu), axis=-1, keepdims=True)
        xn = (r - mu) * lax.rsqrt(var + LN_EPS) * g_ref[...] + b_ref[...]
        r_ref[...] = r
        xn_ref[...] = xn
        xnb_ref[...] = xn.astype(BF16)

    return pl.pallas_call(
        body, grid=(s // tb,), in_specs=[row, row, vec, vec], out_specs=[row, row, row],
        out_shape=[jax.ShapeDtypeStruct((s, w), F32), jax.ShapeDtypeStruct((s, w), F32), jax.ShapeDtypeStruct((s, w), BF16)],
        compiler_params=_params("parallel"), name=name,
    )(x, y, g.reshape(1, w), b.reshape(1, w))


def _ln_bwd_tile(r, g, dxn):
    mu = jnp.mean(r, axis=-1, keepdims=True)
    cen = r - mu
    rstd = lax.rsqrt(jnp.mean(jnp.square(cen), axis=-1, keepdims=True) + LN_EPS)
    xhat = cen * rstd
    dxh = dxn * g
    dr = rstd * (dxh - jnp.mean(dxh, axis=-1, keepdims=True) - xhat * jnp.mean(dxh * xhat, axis=-1, keepdims=True))
    return dr, jnp.sum(dxn * xhat, axis=0, keepdims=True)


def ln_backward(r, g, dxn, name):
    s, w = r.shape
    tb = _tile(s, 256)
    row = pl.BlockSpec((tb, w), lambda i: (i, 0))
    vec = pl.BlockSpec((1, w), lambda i: (0, 0))

    def body(r_ref, g_ref, d_ref, dr_ref, drb_ref, dg_ref, db_ref):
        @pl.when(pl.program_id(0) == 0)
        def _():
            dg_ref[...] = jnp.zeros_like(dg_ref)
            db_ref[...] = jnp.zeros_like(db_ref)

        dxn = d_ref[...]
        dr, dg = _ln_bwd_tile(r_ref[...], g_ref[...], dxn)
        dr_ref[...] = dr
        drb_ref[...] = dr.astype(BF16)
        dg_ref[...] += dg
        db_ref[...] += jnp.sum(dxn, axis=0, keepdims=True)

    return pl.pallas_call(
        body, grid=(s // tb,), in_specs=[row, vec, row], out_specs=[row, row, vec, vec],
        out_shape=[jax.ShapeDtypeStruct((s, w), F32), jax.ShapeDtypeStruct((s, w), BF16),
                   jax.ShapeDtypeStruct((1, w), F32), jax.ShapeDtypeStruct((1, w), F32)],
        compiler_params=_params("arbitrary"), name=name,
    )(r, g.reshape(1, w), dxn)


ATTENTION_US = {("fwd", 1): 198.0, ("fwd", 4): 97.0, ("fwd", 16): 125.0, ("bwd", 1): 247.0, ("bwd", 4): 133.0, ("bwd", 16): 203.0}


def _causal_masks():
    qi = lax.broadcasted_iota(jnp.int32, (128, 128), 0)
    kj = lax.broadcasted_iota(jnp.int32, (128, 128), 1)
    return kj <= qi, kj >= qi


def _sub_rows(d, r):
    return pl.ds(r, 128, stride=d) if d > 1 else pl.ds(0, 128)


def _for_each_residue(d, fn):
    if d <= 4:
        for r in range(d):
            fn(r)
    else:
        def step(r, carry):
            fn(r)
            return carry
        lax.fori_loop(0, d, step, 0)


def attention_forward(proj, group, d, heads, name, comm=None):
    s = proj.shape[0]
    rows = 128 * d
    nb = s // rows
    scale = HEAD_DIM ** -0.5
    qc, kc, vc = (group * 3) * heads, (group * 3 + 1) * heads, (group * 3 + 2) * heads

    def cur(col):
        return pl.BlockSpec((rows, 128), lambda h, n: (n, col + h))

    def prev(col):
        return pl.BlockSpec((rows, 128), lambda h, n: (jnp.maximum(n - 1, 0), col + h))

    out = pl.BlockSpec((rows, 128), lambda h, n: (n, h))

    def body(q_ref, kc_ref, kp_ref, vc_ref, vp_ref, o_ref, l_ref):
        mask_c, band_p = _causal_masks()
        mask_p = band_p & (pl.program_id(1) > 0)

        def one(r):
            sl = _sub_rows(d, r)
            q = q_ref[sl, :].astype(BF16)
            s_c = jnp.where(mask_c, _dot(q, kc_ref[sl, :].astype(BF16), NT) * scale, -jnp.inf)
            s_p = jnp.where(mask_p, _dot(q, kp_ref[sl, :].astype(BF16), NT) * scale, -jnp.inf)
            m = jnp.maximum(jnp.max(s_c, axis=-1, keepdims=True), jnp.max(s_p, axis=-1, keepdims=True))
            e_c, e_p = jnp.exp(s_c - m), jnp.exp(s_p - m)
            l = jnp.sum(e_c, axis=-1, keepdims=True) + jnp.sum(e_p, axis=-1, keepdims=True)
            o = _dot(e_c.astype(BF16), vc_ref[sl, :].astype(BF16), NN) + _dot(e_p.astype(BF16), vp_ref[sl, :].astype(BF16), NN)
            o_ref[sl, :] = o / l
            l_ref[sl, :] = jnp.broadcast_to(m + jnp.log(l), (128, 128))

        _for_each_residue(d, one)

    return _call(
        body, grid=(heads, nb),
        in_specs=[cur(qc), cur(kc), prev(kc), cur(vc), prev(vc)], out_specs=[out, out],
        out_shape=[jax.ShapeDtypeStruct((s, heads * 128), F32)] * 2, scratch_shapes=[],
        args=[proj] * 5, name=name, semantics=("parallel", "parallel"),
        jobs=comm.take(ATTENTION_US["fwd", d] * s * heads / (4096 * 8)) if comm is not None else (),
    )


def attention_combine(os_, ls_, name):
    s, w = os_[0].shape
    tb = _tile(s, 256)
    row = pl.BlockSpec((tb, w), lambda i: (i, 0))

    def body(o0, o1, o2, l0, l1, l2, o_ref, ob_ref, lse_ref):
        a, b, c = l0[...], l1[...], l2[...]
        m = jnp.maximum(jnp.maximum(a, b), c)
        ea, eb, ec = jnp.exp(a - m), jnp.exp(b - m), jnp.exp(c - m)
        tot = ea + eb + ec
        o = (ea / tot) * o0[...] + (eb / tot) * o1[...] + (ec / tot) * o2[...]
        o_ref[...] = o
        ob_ref[...] = o.astype(BF16)
        lse_ref[...] = m + jnp.log(tot)

    return pl.pallas_call(
        body, grid=(s // tb,), in_specs=[row] * 6, out_specs=[row] * 3,
        out_shape=[jax.ShapeDtypeStruct((s, w), F32), jax.ShapeDtypeStruct((s, w + MEM_WIDTH), BF16), jax.ShapeDtypeStruct((s, w), F32)],
        compiler_params=_params("parallel"), name=name,
    )(*os_, *ls_)


def attention_backward(proj, o, lse, dcat, group, d, heads, name, comm=None):
    s = proj.shape[0]
    rows = 128 * d
    nb = s // rows
    scale = HEAD_DIM ** -0.5
    qc, kc, vc = (group * 3) * heads, (group * 3 + 1) * heads, (group * 3 + 2) * heads

    def at(col, shift):
        return pl.BlockSpec((rows, 128), lambda h, n: (jnp.clip(n + shift, 0, nb - 1), col + h))

    def out(col):
        return pl.BlockSpec((rows, 128), lambda h, n: (n, col + h))

    def body(qc_ref, qn_ref, kc_ref, kp_ref, vc_ref, vp_ref, doc_ref, don_ref, oc_ref, on_ref, lc_ref, ln_ref,
             dq_ref, dk_ref, dv_ref):
        n = pl.program_id(1)
        mask_c, band_p = _causal_masks()
        mask_b = band_p & (n > 0)
        mask_n = band_p & (n < nb - 1)

        def tile(q, k, v, do, lse_t, dsum, mask):
            p = jnp.where(mask, jnp.exp(_dot(q, k, NT) * scale - lse_t), 0.0)
            ds = (p * (_dot(do, v, NT) - dsum) * scale).astype(BF16)
            return p.astype(BF16), ds

        def one(r):
            sl = _sub_rows(d, r)
            q_c, q_n = qc_ref[sl, :].astype(BF16), qn_ref[sl, :].astype(BF16)
            k_c, k_p = kc_ref[sl, :].astype(BF16), kp_ref[sl, :].astype(BF16)
            v_c, v_p = vc_ref[sl, :].astype(BF16), vp_ref[sl, :].astype(BF16)
            do_c, do_n = doc_ref[sl, :], don_ref[sl, :]
            dsum_c = jnp.sum(do_c * oc_ref[sl, :], axis=-1, keepdims=True)
            dsum_n = jnp.sum(do_n * on_ref[sl, :], axis=-1, keepdims=True)
            do_c, do_n = do_c.astype(BF16), do_n.astype(BF16)
            p_a, ds_a = tile(q_c, k_c, v_c, do_c, lc_ref[sl, :], dsum_c, mask_c)
            _, ds_b = tile(q_c, k_p, v_p, do_c, lc_ref[sl, :], dsum_c, mask_b)
            p_n, ds_n = tile(q_n, k_c, v_c, do_n, ln_ref[sl, :], dsum_n, mask_n)
            dq_ref[sl, :] = _dot(ds_a, k_c, NN) + _dot(ds_b, k_p, NN)
            dk_ref[sl, :] = _dot(ds_a, q_c, TN) + _dot(ds_n, q_n, TN)
            dv_ref[sl, :] = _dot(p_a, do_c, TN) + _dot(p_n, do_n, TN)

        _for_each_residue(d, one)

    w = heads * 128
    return _call(
        body, grid=(heads, nb),
        in_specs=[at(qc, 0), at(qc, 1), at(kc, 0), at(kc, -1), at(vc, 0), at(vc, -1),
                  at(0, 0), at(0, 1), at(0, 0), at(0, 1), at(0, 0), at(0, 1)],
        out_specs=[out(0)] * 3,
        out_shape=[jax.ShapeDtypeStruct((s, w), F32)] * 3, scratch_shapes=[],
        args=[proj] * 6 + [dcat, dcat, o, o, lse, lse], name=name, semantics=("parallel", "parallel"),
        jobs=comm.take(ATTENTION_US["bwd", d] * s * heads / (4096 * 8)) if comm is not None else (),
    )


def _mem_softmax(q, kv, h, scale):
    k = kv[:, h * 128:(h + 1) * 128].astype(BF16)
    v = kv[:, MEM_WIDTH + h * 128:MEM_WIDTH + (h + 1) * 128].astype(BF16)
    sc = _dot(q, k, NT) * scale
    e = jnp.exp(sc - jnp.max(sc, axis=-1, keepdims=True))
    return e / jnp.sum(e, axis=-1, keepdims=True), k, v


def memory_attention(proj, qcol, kv, cat, name):
    s = proj.shape[0]
    tb = _tile(s, 512)
    scale = HEAD_DIM ** -0.5

    def body(q_ref, kv_ref, cat_ref, o_ref):
        kv_t = kv_ref[...]
        for h in range(MEM_HEADS):
            p, _, v = _mem_softmax(q_ref[:, h * 128:(h + 1) * 128].astype(BF16), kv_t, h, scale)
            o_ref[:, h * 128:(h + 1) * 128] = _dot(p.astype(BF16), v, NN).astype(BF16)

    return pl.pallas_call(
        body, grid=(s // tb,),
        in_specs=[pl.BlockSpec((tb, MEM_WIDTH), lambda i: (i, qcol // MEM_WIDTH)), pl.BlockSpec(kv.shape, lambda i: (0, 0)), ANY],
        out_specs=pl.BlockSpec((tb, MEM_WIDTH), lambda i: (i, cat.shape[1] // MEM_WIDTH - 1)),
        out_shape=jax.ShapeDtypeStruct(cat.shape, BF16), input_output_aliases={2: 0},
        compiler_params=_params("parallel"), name=name,
    )(proj, kv, cat)


def memory_attention_backward(proj, qcol, kv, dcat, dcol, name):
    s = proj.shape[0]
    tb = _tile(s, 512)
    scale = HEAD_DIM ** -0.5

    def body(q_ref, kv_ref, do_ref, dq_ref, dkv_ref):
        @pl.when(pl.program_id(0) == 0)
        def _():
            dkv_ref[...] = jnp.zeros_like(dkv_ref)

        kv_t = kv_ref[...]
        for h in range(MEM_HEADS):
            cols = slice(h * 128, (h + 1) * 128)
            q = q_ref[:, cols].astype(BF16)
            p, k, v = _mem_softmax(q, kv_t, h, scale)
            do = do_ref[:, cols].astype(BF16)
            dp = _dot(do, v, NT)
            ds = (p * (dp - jnp.sum(dp * p, axis=-1, keepdims=True)) * scale).astype(BF16)
            dq_ref[:, cols] = _dot(ds, k, NN).astype(BF16)
            dkv_ref[:, cols] += _dot(ds, q, TN)
            dkv_ref[:, MEM_WIDTH + h * 128:MEM_WIDTH + (h + 1) * 128] += _dot(p.astype(BF16), do, TN)

    return pl.pallas_call(
        body, grid=(s // tb,),
        in_specs=[pl.BlockSpec((tb, MEM_WIDTH), lambda i: (i, qcol // MEM_WIDTH)), pl.BlockSpec(kv.shape, lambda i: (0, 0)),
                  pl.BlockSpec((tb, MEM_WIDTH), lambda i: (i, dcol // MEM_WIDTH))],
        out_specs=[pl.BlockSpec((tb, MEM_WIDTH), lambda i: (i, 0)), pl.BlockSpec(kv.shape, lambda i: (0, 0))],
        out_shape=[jax.ShapeDtypeStruct((s, MEM_WIDTH), BF16), jax.ShapeDtypeStruct(kv.shape, F32)],
        compiler_params=_params("arbitrary"), name=name,
    )(proj, kv, dcat)


_SQRT_HALF = math.sqrt(0.5)
_INV_SQRT_2PI = 1.0 / math.sqrt(2.0 * math.pi)


def _gelu(x):
    return 0.5 * x * (1.0 + lax.erf(x * _SQRT_HALF))


def _gelu_grad(x):
    return 0.5 * (1.0 + lax.erf(x * _SQRT_HALF)) + x * (_INV_SQRT_2PI * jnp.exp(-0.5 * x * x))


def _gmlp_specs(s, wd, groups, tb):
    half = lambda c: pl.BlockSpec((tb, wd), lambda i: (i, c))
    ws_spec = pl.BlockSpec((groups, 128, 128), lambda i: (0, 0, 0))
    bs_spec = pl.BlockSpec((groups, 128, 1), lambda i: (0, 0, 0))
    vec = pl.BlockSpec((1, wd), lambda i: (0, 0))
    return half, ws_spec, bs_spec, vec


def _vnorm(zv, g, b):
    mu = jnp.mean(zv, axis=-1, keepdims=True)
    var = jnp.mean(jnp.square(zv - mu), axis=-1, keepdims=True)
    return (zv - mu) * lax.rsqrt(var + LN_EPS) * g + b


def gmlp_forward(proj, ws, bs, vg, vb, name):
    s = proj.shape[0]
    groups = ws.shape[0]
    wd = groups * 128
    tb = _tile(s, 512)
    half, ws_spec, bs_spec, vec = _gmlp_specs(s, wd, groups, tb)

    def body(pu_ref, pv_ref, ws_ref, bs_ref, vg_ref, vb_ref, o_ref, vn_ref):
        causal, _ = _causal_masks()
        vn_ref[...] = _vnorm(_gelu(pv_ref[...]), vg_ref[...], vb_ref[...]).astype(BF16)
        for g in range(groups):
            cols = slice(g * 128, (g + 1) * 128)
            wm = jnp.where(causal, ws_ref[g], 0.0).astype(BF16)
            for c in range(tb // 128):
                rws = slice(c * 128, (c + 1) * 128)
                sg = _dot(wm, vn_ref[rws, cols], NN) + bs_ref[g]
                o_ref[rws, cols] = (_gelu(pu_ref[rws, cols]) * sg).astype(BF16)

    return pl.pallas_call(
        body, grid=(s // tb,),
        in_specs=[half(0), half(1), ws_spec, bs_spec, vec, vec], out_specs=pl.BlockSpec((tb, wd), lambda i: (i, 0)),
        out_shape=jax.ShapeDtypeStruct((s, wd + MEM_WIDTH), BF16),
        scratch_shapes=[pltpu.VMEM((tb, wd), BF16)],
        compiler_params=_params("parallel"), name=name,
    )(proj, proj, ws, bs.reshape(groups, 128, 1), vg.reshape(1, wd), vb.reshape(1, wd))


def gmlp_backward(proj, ws, bs, vg, vb, dcat, name):
    s = proj.shape[0]
    groups = ws.shape[0]
    wd = groups * 128
    tb = _tile(s, 512)
    half, ws_spec, bs_spec, vec = _gmlp_specs(s, wd, groups, tb)

    def body(pu_ref, pv_ref, ws_ref, bs_ref, vg_ref, vb_ref, do_ref, dpu_ref, dpv_ref, dws_ref, dbs_ref, dvg_ref, dvb_ref,
             vn_ref, dvn_ref):
        @pl.when(pl.program_id(0) == 0)
        def _():
            dws_ref[...] = jnp.zeros_like(dws_ref)
            dbs_ref[...] = jnp.zeros_like(dbs_ref)
            dvg_ref[...] = jnp.zeros_like(dvg_ref)
            dvb_ref[...] = jnp.zeros_like(dvb_ref)

        causal, _ = _causal_masks()
        pv = pv_ref[...]
        zv = _gelu(pv)
        vn_ref[...] = _vnorm(zv, vg_ref[...], vb_ref[...]).astype(BF16)
        for g in range(groups):
            cols = slice(g * 128, (g + 1) * 128)
            wm = jnp.where(causal, ws_ref[g], 0.0).astype(BF16)
            dws_g = jnp.zeros((128, 128), F32)
            dbs_g = jnp.zeros((128, 1), F32)
            for c in range(tb // 128):
                rws = slice(c * 128, (c + 1) * 128)
                vn = vn_ref[rws, cols]
                pu = pu_ref[rws, cols]
                do = do_ref[rws, cols]
                sg = _dot(wm, vn, NN) + bs_ref[g]
                dpu_ref[rws, cols] = (do * sg * _gelu_grad(pu)).astype(BF16)
                dsg = do * _gelu(pu)
                dsg_b = dsg.astype(BF16)
                dws_g += _dot(dsg_b, vn, NT)
                dbs_g += jnp.sum(dsg, axis=-1, keepdims=True)
                dvn_ref[rws, cols] = _dot(wm, dsg_b, TN)
            dws_ref[g] += jnp.where(causal, dws_g, 0.0)
            dbs_ref[g] += dbs_g
        dvn = dvn_ref[...]
        dzv, dvg = _ln_bwd_tile(zv, vg_ref[...], dvn)
        dvg_ref[...] += dvg
        dvb_ref[...] += jnp.sum(dvn, axis=0, keepdims=True)
        dpv_ref[...] = (dzv * _gelu_grad(pv)).astype(BF16)

    row = pl.BlockSpec((tb, wd), lambda i: (i, 0))
    dpu, dpv, dws, dbs, dvg, dvb = pl.pallas_call(
        body, grid=(s // tb,),
        in_specs=[half(0), half(1), ws_spec, bs_spec, vec, vec, row],
        out_specs=[row, row, ws_spec, bs_spec, vec, vec],
        out_shape=[jax.ShapeDtypeStruct((s, wd), BF16), jax.ShapeDtypeStruct((s, wd), BF16),
                   jax.ShapeDtypeStruct((groups, 128, 128), F32), jax.ShapeDtypeStruct((groups, 128, 1), F32),
                   jax.ShapeDtypeStruct((1, wd), F32), jax.ShapeDtypeStruct((1, wd), F32)],
        scratch_shapes=[pltpu.VMEM((tb, wd), BF16), pltpu.VMEM((tb, wd), F32)],
        compiler_params=_params("arbitrary"), name=name,
    )(proj, proj, ws, bs.reshape(groups, 128, 1), vg.reshape(1, wd), vb.reshape(1, wd), dcat)
    return dpu, dpv, dws, dbs.reshape(groups, 128), dvg, dvb


def loss_head(y, target, name):
    s, w = y.shape
    tb = _tile(s, 256)
    row = pl.BlockSpec((tb, w), lambda i: (i, 0))
    nsteps = s // tb

    def body(y_ref, t_ref, loss_ref, dy_ref, acc_ref):
        i = pl.program_id(0)

        @pl.when(i == 0)
        def _():
            acc_ref[...] = jnp.zeros_like(acc_ref)

        err = y_ref[...] - t_ref[...]
        dy_ref[...] = err / w
        acc_ref[...] += jnp.sum(jnp.mean(jnp.square(err), axis=-1, keepdims=True), axis=0, keepdims=True)

        @pl.when(i == nsteps - 1)
        def _():
            loss_ref[...] = jnp.broadcast_to(0.5 * acc_ref[...], loss_ref.shape)

    return pl.pallas_call(
        body, grid=(nsteps,), in_specs=[row, row],
        out_specs=[pl.BlockSpec((8, 128), lambda i: (0, 0)), row],
        out_shape=[jax.ShapeDtypeStruct((8, 128), F32), jax.ShapeDtypeStruct((s, w), F32)],
        scratch_shapes=[pltpu.VMEM((1, 1), F32)],
        compiler_params=_params("arbitrary"), name=name,
    )(y, target)


PARTS_WINDOW_ELEMS = 512 * 1024


def _row_tile(r, c, budget):
    if r * c <= budget or r % 16:
        return r
    fits = [t for t in range(16, r, 16) if r % t == 0 and t * c <= budget]
    return max(fits) if fits else 16


def _sum_in_device_order(p_ref):
    g = p_ref[0].astype(F32)
    for j in range(1, p_ref.shape[0]):
        g = g + p_ref[j].astype(F32)
    return g


def _adamw_update(g, w, m, v):
    nm = ADAM_B1 * m + (1.0 - ADAM_B1) * g
    nv = ADAM_B2 * v + (1.0 - ADAM_B2) * jnp.square(g)
    m_hat = nm / (1.0 - ADAM_B1 ** ADAM_STEP)
    v_hat = nv / (1.0 - ADAM_B2 ** ADAM_STEP)
    return -ADAM_LR * (m_hat / (jnp.sqrt(v_hat) + ADAM_EPS) + ADAM_WD * w), nm, nv


def sum_parts(chunks, name):
    p, r, c = chunks[0].shape
    tr = _row_tile(r, c, PARTS_WINDOW_ELEMS // len(chunks))
    nb = r // tr

    def chunk_spec(q):
        return pl.BlockSpec((p, tr, c), lambda ch, i: (0, jnp.where(ch == q, i, 0), 0))

    def body(*refs):
        for q in range(len(chunks)):
            @pl.when(pl.program_id(0) == q)
            def _():
                refs[-1][...] = _sum_in_device_order(refs[q])

    return pl.pallas_call(
        body, grid=(len(chunks), nb), in_specs=[chunk_spec(q) for q in range(len(chunks))],
        out_specs=pl.BlockSpec((tr, c), lambda ch, i: (ch * nb + i, 0)),
        out_shape=jax.ShapeDtypeStruct((len(chunks) * r, c), F32),
        compiler_params=_params("arbitrary", "arbitrary"), name=name,
    )(*chunks)


def adamw(parts, w, m, v, name, emit_grad=True):
    p, r, c = parts.shape
    tr = _row_tile(r, c, 160 * 1024)
    n_out = 4 if emit_grad else 3

    def body(p_ref, w_ref, m_ref, v_ref, *out_refs):
        g = _sum_in_device_order(p_ref)
        vals = _adamw_update(g, w_ref[...], m_ref[...], v_ref[...])
        for o_ref, val in zip(out_refs, ((g,) + vals) if emit_grad else vals):
            o_ref[...] = val

    row = pl.BlockSpec((tr, c), lambda i: (i, 0))
    return pl.pallas_call(
        body, grid=(r // tr,), in_specs=[pl.BlockSpec((p, tr, c), lambda i: (0, i, 0)), row, row, row],
        out_specs=[row] * n_out, out_shape=[jax.ShapeDtypeStruct((r, c), F32)] * n_out,
        compiler_params=_params("parallel"), name=name,
    )(parts, w, m, v)


def adamw_layers(chunks, w, m, v, name):
    n_layers, r, c = w.shape
    n_chunks = len(chunks[0])
    p, rc, _ = chunks[0][0].shape
    tr = _row_tile(rc, c, PARTS_WINDOW_ELEMS // (n_layers * n_chunks))
    nb = rc // tr
    flat = [ch for layer in chunks for ch in layer]

    def chunk_spec(q):
        return pl.BlockSpec((p, tr, c), lambda l, ch, i: (0, jnp.where(l * n_chunks + ch == q, i, 0), 0))

    def body(*refs):
        p_refs, (w_ref, m_ref, v_ref), out_refs = refs[:len(flat)], refs[len(flat):len(flat) + 3], refs[len(flat) + 3:]
        for q in range(len(flat)):
            @pl.when(pl.program_id(0) * n_chunks + pl.program_id(1) == q)
            def _():
                g = _sum_in_device_order(p_refs[q])
                vals = _adamw_update(g, w_ref[...], m_ref[...], v_ref[...])
                for o_ref, val in zip(out_refs, (g,) + vals):
                    o_ref[...] = val

    state = pl.BlockSpec((None, tr, c), lambda l, ch, i: (l, ch * nb + i, 0))
    return pl.pallas_call(
        body, grid=(n_layers, n_chunks, nb), in_specs=[chunk_spec(q) for q in range(len(flat))] + [state] * 3,
        out_specs=[state] * 4, out_shape=[jax.ShapeDtypeStruct(w.shape, F32)] * 4,
        compiler_params=_params("arbitrary", "arbitrary", "arbitrary"), name=name,
    )(*flat, w, m, v)


def kernel(x, mem, w_in_a, w_in_b, w_s, b_s, vnorm_g, vnorm_b, w_mem_kv, w_out, ln1_g, ln1_b, w_ff1, w_ff2, ln2_g, ln2_b, loss_target, m_w_in_a, m_w_in_b, m_w_s, m_b_s, m_vnorm_g, m_vnorm_b, m_w_mem_kv, m_w_out, m_ln1_g, m_ln1_b, m_w_ff1, m_w_ff2, m_ln2_g, m_ln2_b, v_w_in_a, v_w_in_b, v_w_s, v_b_s, v_vnorm_g, v_vnorm_b, v_w_mem_kv, v_w_out, v_ln1_g, v_ln1_b, v_w_ff1, v_w_ff2, v_ln2_g, v_ln2_b):
    depth = w_ff1.shape[0]
    groups = w_s.shape[1]
    mix_w = groups * HEAD_DIM
    n_b = vnorm_g.shape[0]
    alpha = (2.0 * depth) ** 0.25
    me = 4 * lax.axis_index("x") + 2 * lax.axis_index("y") + lax.axis_index("c")
    x0 = x[0]
    mem_b = mem[0].astype(BF16)
    target = loss_target[0]

    comm = Exchanges()
    weight_jobs = []
    for i in range(depth):
        w_in = (w_in_a if i % 2 == 0 else w_in_b)[i // 2]
        groups_of_shards = ([w_in.T, w_mem_kv[i]], [w_out[i].T], [w_ff1[i].T], [w_ff2[i]])
        weight_jobs.append([GatherJob([t.astype(BF16) for t in shards], f"g{i}{tag}")
                            for shards, tag in zip(groups_of_shards, ("in", "out", "ff1", "ff2"))])
    weight_jobs[0][0] = GatherJob(weight_jobs[0][0].arrays + [jnp.concatenate([vnorm_g, vnorm_b], axis=0)], "g0in")
    for jobs in weight_jobs:
        for job in jobs:
            comm.submit(job)

    def gathered(job):
        return [g.reshape(N_DEV * g.shape[1], g.shape[2]) for g in comm.require(job)]

    vnorm = comm.require(weight_jobs[0][0])[2]
    vnorm = jnp.transpose(vnorm, (1, 0, 2)).reshape(2 * n_b, mix_w)
    vg_full, vb_full = vnorm[:n_b], vnorm[n_b:]

    saved = []
    weights = []
    xf, xb = x0, x0.astype(BF16)
    for i in range(depth):
        j = i // 2
        w_in_t, w_kv = gathered(weight_jobs[i][0])[:2]
        proj = matmul(xb, w_in_t, "nt", name=f"in_proj_{i % 2}", comm=comm)
        kv = matmul(mem_b, w_kv, "nn", name="mem_kv")
        if i % 2 == 0:
            os_, ls_ = [], []
            for g, d in enumerate(A_DILATIONS):
                o_g, l_g = attention_forward(proj, g, d, groups, name=f"attn_fwd_d{d}", comm=comm)
                os_.append(o_g)
                ls_.append(l_g)
            o, cat, lse = attention_combine(os_, ls_, name="attn_combine")
            qcol = 9 * mix_w
            extra = (o, lse)
        else:
            cat = gmlp_forward(proj, w_s[j], b_s[j], vg_full[j], vb_full[j], name="gmlp_fwd")
            qcol = 2 * mix_w
            extra = ()
        cat = memory_attention(proj, qcol, kv, cat, name=f"mem_attn_{i % 2}")
        (w_o_t,) = gathered(weight_jobs[i][1])
        y1 = matmul(cat, w_o_t, "nt", name="out_proj", comm=comm)
        r1, x1, x1b = ln_residual(xf, y1, ln1_g[i], ln1_b[i], alpha, name="ln1")
        (w_1_t,) = gathered(weight_jobs[i][2])
        h, hid = matmul(x1b, w_1_t, "nt", name="ff1", out_dtypes=(F32, BF16), comm=comm,
                        epilogue=lambda acc: (acc, jnp.square(jnp.maximum(acc, 0.0))))
        (w_2,) = gathered(weight_jobs[i][3])
        y2 = matmul(hid, w_2, "nn", name="ff2", comm=comm)
        r2, x2, x2b = ln_residual(x1, y2, ln2_g[i], ln2_b[i], alpha, name="ln2")
        weights.append((w_in_t, w_o_t, w_1_t, w_2, w_kv))
        saved.append((xb, proj, kv, extra, qcol, cat, r1, x1b, h, hid, r2))
        xf, xb = x2, x2b

    loss_tile, dx = loss_head(xf, target, name="loss_head")
    loss = lax.psum(loss_tile[0, 0], ("x", "y", "c"))

    def exchange(dw, tag):
        rows, cols = dw.shape[0] // N_DEV, dw.shape[1]
        p = dw.reshape(N_DEV, rows, cols)
        n_chunks = 1
        while ExchangeJob.US_PER_MB * dw.size * 2 / 1e6 > 110.0 * n_chunks and rows % (32 * n_chunks) == 0:
            n_chunks *= 2
        step = rows // n_chunks
        return [comm.submit(ExchangeJob(p, q * step, step, f"x{tag}{q}")) for q in range(n_chunks)]

    recv = {n: [None] * depth for n in ("w_in", "w_mem_kv", "w_out", "w_ff1", "w_ff2")}
    small = {n: [None] * depth for n in ("ln1_g", "ln1_b", "ln2_g", "ln2_b")}
    small_b = {n: [None] * n_b for n in ("w_s", "b_s", "vnorm_g", "vnorm_b")}
    for i in reversed(range(depth)):
        w_in_t, w_o_t, w_1_t, w_2, w_kv = weights[i]
        xb, proj, kv, extra, qcol, cat, r1, x1b, h, hid, r2 = saved[i]
        j = i // 2
        dr2, dr2b, small["ln2_g"][i], small["ln2_b"][i] = ln_backward(r2, ln2_g[i], dx, name="ln2_bwd")
        dh = matmul(dr2b, w_2, "nt", name="ff2_dx", extras=(h,), out_dtypes=(BF16,), comm=comm,
                    epilogue=lambda acc, h_t: (acc * (2.0 * jnp.maximum(h_t, 0.0)),))
        recv["w_ff2"][i] = exchange(matmul(hid, dr2b, "tn", name="ff2_dw", out_dtypes=(BF16,), comm=comm), f"ff2{i}")
        dx1 = matmul(dh, w_1_t, "nn", name="ff1_dx", extras=(dr2,), comm=comm, epilogue=lambda acc, res: (acc + alpha * res,))
        recv["w_ff1"][i] = exchange(matmul(dh, x1b, "tn", name="ff1_dw", out_dtypes=(BF16,), comm=comm), f"ff1{i}")
        dr1, dr1b, small["ln1_g"][i], small["ln1_b"][i] = ln_backward(r1, ln1_g[i], dx1, name="ln1_bwd")
        dcat = matmul(dr1b, w_o_t, "nn", name="out_proj_dx", comm=comm)
        recv["w_out"][i] = exchange(matmul(dr1b, cat, "tn", name="out_proj_dw", out_dtypes=(BF16,), comm=comm), f"out{i}")
        dq_mem, dkv = memory_attention_backward(proj, qcol, kv, dcat, mix_w, name=f"mem_attn_bwd_{i % 2}")
        recv["w_mem_kv"][i] = exchange(matmul(mem_b, dkv, "tn", name="mem_kv_dw", out_dtypes=(BF16,)), f"kv{i}")
        if i % 2 == 0:
            o, lse = extra
            pieces = []
            for g, d in enumerate(A_DILATIONS):
                pieces += [t.astype(BF16) for t in attention_backward(proj, o, lse, dcat, g, d, groups, name=f"attn_bwd_d{d}", comm=comm)]
            dproj = jnp.concatenate(pieces + [dq_mem], axis=1)
        else:
            dpu, dpv, small_b["w_s"][j], small_b["b_s"][j], small_b["vnorm_g"][j], small_b["vnorm_b"][j] = gmlp_backward(
                proj, w_s[j], b_s[j], vg_full[j], vb_full[j], dcat, name="gmlp_bwd")
            dproj = jnp.concatenate([dpu, dpv, dq_mem], axis=1)
        recv["w_in"][i] = exchange(matmul(dproj, xb, "tn", name=f"in_proj_dw_{i % 2}", out_dtypes=(BF16,), comm=comm), f"in{i}")
        dx = matmul(dproj, w_in_t, "nn", name=f"in_proj_dx_{i % 2}", extras=(dr1,), comm=comm, epilogue=lambda acc, res: (acc + alpha * res,))
    grad_x = dx[None]

    flat_small = jnp.concatenate(
        [jnp.concatenate(small[n], axis=0).reshape(-1) for n in ("ln1_g", "ln1_b", "ln2_g", "ln2_b")]
        + [jnp.stack(small_b[n]).reshape(-1) for n in ("w_s", "b_s", "vnorm_g", "vnorm_b")])
    pad = (-flat_small.size) % 1024
    small_job = comm.submit(GatherJob([jnp.pad(flat_small, (0, pad)).reshape(-1, 1024)], "gsmall"))
    comm.flush()
    gathered_small = small_job.outs[0].reshape(N_DEV, -1)
    recv = {n: [[job.outs[0] for job in jobs] for jobs in per_layer] for n, per_layer in recv.items()}

    results = {}
    at = 0

    def take(shape):
        nonlocal at
        size = math.prod(shape)
        at += size
        return gathered_small[:, at - size:at].reshape((N_DEV,) + shape)

    for n, w_, m_, v_ in (("ln1_g", ln1_g, m_ln1_g, v_ln1_g), ("ln1_b", ln1_b, m_ln1_b, v_ln1_b),
                          ("ln2_g", ln2_g, m_ln2_g, v_ln2_g), ("ln2_b", ln2_b, m_ln2_b, v_ln2_b)):
        results[n] = adamw(take(w_.shape), w_, m_, v_, name="adamw_ln")
    lanes = lambda t: t.reshape(-1, 128)
    results["w_s"] = [t.reshape(w_s.shape) for t in adamw(take((w_s.size // 128, 128)), lanes(w_s), lanes(m_w_s), lanes(v_w_s), name="adamw_w_s")]
    results["b_s"] = [t.reshape(b_s.shape) for t in adamw(take((b_s.size // 128, 128)), lanes(b_s), lanes(m_b_s), lanes(v_b_s), name="adamw_b_s")]
    for n, w_, m_, v_ in (("vnorm_g", vnorm_g, m_vnorm_g, v_vnorm_g), ("vnorm_b", vnorm_b, m_vnorm_b, v_vnorm_b)):
        parts = lax.dynamic_slice_in_dim(take((n_b, mix_w)), me * w_.shape[1], w_.shape[1], axis=2)
        results[n] = adamw(parts, w_, m_, v_, name="adamw_vnorm")

    def transposed(name, layers, w_, m_, v_):
        grad = jnp.stack([sum_parts(recv[name][i], name=f"sum_{name}_{i % 2}").T for i in layers])
        flat = lambda t: t.reshape(-1, t.shape[-1])
        upd = adamw(flat(grad)[None], flat(w_), flat(m_), flat(v_), name=f"adamw_{name}_{layers[0] % 2}", emit_grad=False)
        return [grad] + [t.reshape(w_.shape) for t in upd]

    results["w_in_a"] = transposed("w_in", list(range(0, depth, 2)), w_in_a, m_w_in_a, v_w_in_a)
    results["w_in_b"] = transposed("w_in", list(range(1, depth, 2)), w_in_b, m_w_in_b, v_w_in_b)
    results["w_out"] = transposed("w_out", list(range(depth)), w_out, m_w_out, v_w_out)
    results["w_ff1"] = transposed("w_ff1", list(range(depth)), w_ff1, m_w_ff1, v_w_ff1)
    results["w_mem_kv"] = adamw_layers(recv["w_mem_kv"], w_mem_kv, m_w_mem_kv, v_w_mem_kv, name="adamw_w_mem_kv")
    results["w_ff2"] = adamw_layers(recv["w_ff2"], w_ff2, m_w_ff2, v_w_ff2, name="adamw_w_ff2")

    order = ("w_in_a", "w_in_b", "w_s", "b_s", "vnorm_g", "vnorm_b", "w_mem_kv", "w_out", "ln1_g", "ln1_b", "w_ff1", "w_ff2", "ln2_g", "ln2_b")
    return (loss, grad_x, *[results[n][0] for n in order], *[results[n][1] for n in order],
            *[results[n][2] for n in order], *[results[n][3] for n in order])
```

```python
import math

import jax
import jax.numpy as jnp
from jax import lax
from jax.experimental import pallas as pl
from jax.experimental.pallas import tpu as pltpu

F32 = jnp.float32
BF16 = jnp.bfloat16
N_DEV = 8
HEAD_DIM = 128
MEM_HEADS = 4
MEM_WIDTH = MEM_HEADS * HEAD_DIM
A_DILATIONS = (1, 4, 16)
LN_EPS = 1e-5
ADAM_LR, ADAM_B1, ADAM_B2, ADAM_EPS, ADAM_WD, ADAM_STEP = 0.001, 0.9, 0.999, 1e-08, 0.01, 10
VMEM_LIMIT_BYTES = 56 * 1024 * 1024
MESH = pl.DeviceIdType.MESH
ANY = pl.BlockSpec(memory_space=pl.ANY)


def _params(*sem):
    return pltpu.CompilerParams(dimension_semantics=sem, vmem_limit_bytes=VMEM_LIMIT_BYTES)


def _tile(dim, pref):
    if dim <= pref:
        return dim
    best = None
    for t in range(128, pref + 1, 128):
        if dim % t == 0:
            best = t
    return best if best is not None else dim


def _dot(a, b, dims):
    return lax.dot_general(a, b, (dims, ((), ())), preferred_element_type=F32)


NN = ((1,), (0,))
NT = ((1,), (1,))
TN = ((0,), (0,))


def _my_position():
    return lax.axis_index("x"), lax.axis_index("y"), lax.axis_index("c")


class GatherJob:
    US_PER_MB = 46.0

    def __init__(self, arrays, tag):
        self.arrays, self.tag, self.outs = list(arrays), tag, None
        self.out_shapes = [jax.ShapeDtypeStruct((N_DEV,) + v.shape, v.dtype) for v in self.arrays]
        self.n_sems, self.n_local = 7 * len(self.arrays), len(self.arrays)
        self.est_us = self.US_PER_MB * sum(v.size * v.dtype.itemsize for v in self.arrays) / 1e6

    def phases(self, v_refs, out_refs, send_sems, recv_sems, local_sems, sem0, loc0):
        n = len(self.arrays)
        x, y, c = _my_position()
        me, sibling = (x, y, c), (x, y, 1 - c)
        chips = [(1 - x, y), (x, 1 - y), (1 - x, 1 - y)]

        def copy(a, k, block, to, from_input=False):
            px, py, pc = block
            slot = out_refs[a].at[4 * px + 2 * py + pc]
            return pltpu.make_async_remote_copy(
                src_ref=v_refs[a] if from_input else slot, dst_ref=slot, send_sem=send_sems.at[sem0 + 7 * a + k],
                recv_sem=recv_sems.at[sem0 + 7 * a + k], device_id=to, device_id_type=MESH)

        def mine(a):
            return pltpu.make_async_copy(v_refs[a], out_refs[a].at[4 * x + 2 * y + c], local_sems.at[loc0 + a])

        def first(a):
            return [copy(a, 0, me, sibling, True)] + [copy(a, 1 + j, me, (*chip, c), True) for j, chip in enumerate(chips)]

        def start():
            for a in range(n):
                mine(a).start()
                for cp in first(a):
                    cp.start()

        def middle():
            for j, chip in enumerate(chips):
                for a in range(n):
                    copy(a, 1 + j, (*chip, c), me).wait_recv()
                    copy(a, 4 + j, (*chip, c), sibling).start()

        def finish():
            for a in range(n):
                copy(a, 0, sibling, me).wait_recv()
                for j, chip in enumerate(chips):
                    copy(a, 4 + j, (*chip, 1 - c), me).wait_recv()
                for cp in first(a) + [copy(a, 4 + j, (*chip, c), sibling) for j, chip in enumerate(chips)]:
                    cp.wait_send()
                mine(a).wait()

        return start, middle, finish


class ExchangeJob:
    US_PER_MB = 11.0
    RELATIONS = [(dx, dy, dc) for dx in (0, 1) for dy in (0, 1) for dc in (0, 1)][1:]

    def __init__(self, p, row0, rows, tag):
        self.arrays, self.tag, self.outs = [p], tag, None
        self.row0, self.rows = row0, rows
        self.out_shapes = [jax.ShapeDtypeStruct((N_DEV, rows, p.shape[2]), p.dtype)]
        self.n_sems, self.n_local = 7, 1
        self.est_us = self.US_PER_MB * N_DEV * rows * p.shape[2] * p.dtype.itemsize / 1e6

    def phases(self, p_refs, out_refs, send_sems, recv_sems, local_sems, sem0, loc0):
        (p_ref,), (out_ref,) = p_refs, out_refs
        x, y, c = _my_position()
        me = 4 * x + 2 * y + c
        chunk = pl.ds(self.row0, self.rows)

        def mine():
            return pltpu.make_async_copy(p_ref.at[me, chunk], out_ref.at[me], local_sems.at[loc0])

        def copies(arriving):
            made = []
            for k, (dx, dy, dc) in enumerate(self.RELATIONS):
                px, py, pc = (x + dx) % 2, (y + dy) % 2, (c + dc) % 2
                peer = 4 * px + 2 * py + pc
                made.append(pltpu.make_async_remote_copy(
                    src_ref=p_ref.at[peer, chunk], dst_ref=out_ref.at[peer if arriving else me],
                    send_sem=send_sems.at[sem0 + k], recv_sem=recv_sems.at[sem0 + k],
                    device_id=(px, py, pc), device_id_type=MESH))
            return made

        def start():
            mine().start()
            for send in copies(False):
                send.start()

        def finish():
            for arrival in copies(True):
                arrival.wait_recv()
            for send in copies(False):
                send.wait_send()
            mine().wait()

        return start, (lambda: None), finish


class Exchanges:
    MIN_HOST_US = 80.0
    OVERSHOOT = 1.1

    def __init__(self):
        self.queue = []

    def submit(self, job):
        self.queue.append(job)
        return job

    def take(self, host_us):
        jobs, used = [], 0.0
        if host_us >= self.MIN_HOST_US:
            while self.queue and used + self.queue[0].est_us <= self.OVERSHOOT * host_us:
                used += self.queue[0].est_us
                jobs.append(self.queue.pop(0))
        return jobs

    def require(self, job):
        if job.outs is None:
            at = self.queue.index(job)
            jobs, self.queue = self.queue[:at + 1], self.queue[at + 1:]
            _call(lambda: None, grid=(), in_specs=[], out_specs=[], out_shape=[], scratch_shapes=[], args=[],
                  name="exchange", semantics=(), jobs=jobs)
        return job.outs

    def flush(self):
        if self.queue:
            self.require(self.queue[-1])


def _call(body, *, grid, in_specs, out_specs, out_shape, scratch_shapes, args, name, semantics, jobs=()):
    if not jobs:
        return pl.pallas_call(body, grid=grid, in_specs=in_specs, out_specs=out_specs, out_shape=out_shape,
                              scratch_shapes=scratch_shapes, compiler_params=_params(*semantics), name=name)(*args)
    n_in, n_out, n_scr = len(in_specs), len(out_shape), len(scratch_shapes)
    j_in = [a for job in jobs for a in job.arrays]
    j_out = [s for job in jobs for s in job.out_shapes]
    n_sems, n_local = sum(job.n_sems for job in jobs), sum(job.n_local for job in jobs)
    steps = math.prod(grid)
    middle_step = (7 * steps) // 8

    def wrapped(*refs):
        ins, refs = refs[:n_in], refs[n_in:]
        j_ins, refs = refs[:len(j_in)], refs[len(j_in):]
        outs, refs = refs[:n_out], refs[n_out:]
        j_outs, refs = refs[:len(j_out)], refs[len(j_out):]
        scratch, (send_sems, recv_sems, local_sems) = refs[:n_scr], refs[n_scr:]
        step = 0
        for axis, extent in enumerate(grid):
            step = step * extent + pl.program_id(axis)
        phases, at_in, at_out, sem0, loc0 = [], 0, 0, 0, 0
        for job in jobs:
            k_in, k_out = len(job.arrays), len(job.out_shapes)
            phases.append(job.phases(j_ins[at_in:at_in + k_in], j_outs[at_out:at_out + k_out], send_sems, recv_sems, local_sems, sem0, loc0))
            at_in, at_out, sem0, loc0 = at_in + k_in, at_out + k_out, sem0 + job.n_sems, loc0 + job.n_local

        def run(which):
            for ph in phases:
                ph[which]()

        if steps == 1:
            run(0)
            body(*ins, *outs, *scratch)
            run(1)
            run(2)
        else:
            pl.when(step == 0)(lambda: run(0))
            body(*ins, *outs, *scratch)
            pl.when(step == middle_step)(lambda: run(1))
            pl.when(step == steps - 1)(lambda: run(2))

    res = pl.pallas_call(
        wrapped, grid=grid, in_specs=list(in_specs) + [ANY] * len(j_in), out_specs=list(out_specs) + [ANY] * len(j_out),
        out_shape=list(out_shape) + j_out,
        scratch_shapes=list(scratch_shapes) + [pltpu.SemaphoreType.DMA((n_sems,)), pltpu.SemaphoreType.DMA((n_sems,)),
                                               pltpu.SemaphoreType.DMA((n_local,))],
        compiler_params=_params(*(["arbitrary"] * len(grid))), name=name + "".join("__" + job.tag for job in jobs),
    )(*args, *j_in)
    at = n_out
    for job in jobs:
        job.outs = list(res[at:at + len(job.out_shapes)])
        at += len(job.out_shapes)
    return list(res[:n_out])


MATMUL_TILES = {"nn": (1024, 512, 2560), "nt": (1024, 512, 2560), "tn": (1024, 1024, 512)}
MATMUL_FLOPS_PER_US = {"nn": 7.5e8, "nt": 7.5e8, "tn": 6.2e8}


def matmul(a, b, mode, name, epilogue=None, extras=(), out_dtypes=(F32,), comm=None):
    if mode == "nn":
        (m, k), (k2, n) = a.shape, b.shape
    elif mode == "nt":
        (m, k), (n, k2) = a.shape, b.shape
    else:
        (k, m), (k2, n) = a.shape, b.shape
    assert k == k2, (a.shape, b.shape, mode)
    tm_pref, tn_pref, tk_pref = MATMUL_TILES[mode]
    tm = _tile(m, tm_pref)
    if mode == "tn" and tm < tm_pref:
        tn_pref *= 2
    tn, tk = _tile(n, tn_pref), _tile(k, tk_pref)
    nk = k // tk
    dims = {"nn": NN, "nt": NT, "tn": TN}[mode]
    a_spec = pl.BlockSpec((tk, tm), lambda i, j, kk: (kk, i)) if mode == "tn" else pl.BlockSpec((tm, tk), lambda i, j, kk: (i, kk))
    b_spec = pl.BlockSpec((tn, tk), lambda i, j, kk: (j, kk)) if mode == "nt" else pl.BlockSpec((tk, tn), lambda i, j, kk: (kk, j))
    tile_spec = pl.BlockSpec((tm, tn), lambda i, j, kk: (i, j))
    n_ex, n_out = len(extras), len(out_dtypes)
    if epilogue is None:
        epilogue = lambda acc: (acc,) * n_out

    def body(a_ref, b_ref, *rest):
        ex_refs, out_refs, acc_ref = rest[:n_ex], rest[n_ex:n_ex + n_out], rest[-1]
        kk = pl.program_id(2)

        @pl.when(kk == 0)
        def _():
            acc_ref[...] = jnp.zeros_like(acc_ref)

        acc_ref[...] += _dot(a_ref[...].astype(BF16), b_ref[...].astype(BF16), dims)

        @pl.when(kk == nk - 1)
        def _():
            outs = epilogue(acc_ref[...], *[e[...] for e in ex_refs])
            for o_ref, val in zip(out_refs, outs):
                o_ref[...] = val.astype(o_ref.dtype)

    host_us = 2.0 * m * n * k / MATMUL_FLOPS_PER_US[mode]
    outs = _call(
        body,
        grid=(m // tm, n // tn, nk),
        in_specs=[a_spec, b_spec] + [tile_spec] * n_ex,
        out_specs=[tile_spec] * n_out,
        out_shape=[jax.ShapeDtypeStruct((m, n), dt) for dt in out_dtypes],
        scratch_shapes=[pltpu.VMEM((tm, tn), F32)],
        args=[a, b, *extras], name=name, semantics=("parallel", "parallel", "arbitrary"),
        jobs=comm.take(host_us) if comm is not None else (),
    )
    return outs[0] if n_out == 1 else outs


def ln_residual(x, y, g, b, alpha, name):
    s, w = x.shape
    tb = _tile(s, 256)
    row = pl.BlockSpec((tb, w), lambda i: (i, 0))
    vec = pl.BlockSpec((1, w), lambda i: (0, 0))

    def body(x_ref, y_ref, g_ref, b_ref, r_ref, xn_ref, xnb_ref):
        r = alpha * x_ref[...] + y_ref[...]
        mu = jnp.mean(r, axis=-1, keepdims=True)
        var = jnp.mean(jnp.square(r - mu), axis=-1, keepdims=True)
        xn = (r - mu) * lax.rsqrt(var + LN_EPS) * g_ref[...] + b_ref[...]
        r_ref[...] = r
        xn_ref[...] = xn
        xnb_ref[...] = xn.astype(BF16)

    return pl.pallas_call(
        body, grid=(s // tb,), in_specs=[row, row, vec, vec], out_specs=[row, row, row],
        out_shape=[jax.ShapeDtypeStruct((s, w), F32), jax.ShapeDtypeStruct((s, w), F32), jax.ShapeDtypeStruct((s, w), BF16)],
        compiler_params=_params("parallel"), name=name,
    )(x, y, g.reshape(1, w), b.reshape(1, w))


def _ln_bwd_tile(r, g, dxn):
    mu = jnp.mean(r, axis=-1, keepdims=True)
    cen = r - mu
    rstd = lax.rsqrt(jnp.mean(jnp.square(cen), axis=-1, keepdims=True) + LN_EPS)
    xhat = cen * rstd
    dxh = dxn * g
    dr = rstd * (dxh - jnp.mean(dxh, axis=-1, keepdims=True) - xhat * jnp.mean(dxh * xhat, axis=-1, keepdims=True))
    return dr, jnp.sum(dxn * xhat, axis=0, keepdims=True)


def ln_backward(r, g, dxn, name):
    s, w = r.shape
    tb = _tile(s, 256)
    row = pl.BlockSpec((tb, w), lambda i: (i, 0))
    vec = pl.BlockSpec((1, w), lambda i: (0, 0))

    def body(r_ref, g_ref, d_ref, dr_ref, drb_ref, dg_ref, db_ref):
        @pl.when(pl.program_id(0) == 0)
        def _():
            dg_ref[...] = jnp.zeros_like(dg_ref)
            db_ref[...] = jnp.zeros_like(db_ref)

        dxn = d_ref[...]
        dr, dg = _ln_bwd_tile(r_ref[...], g_ref[...], dxn)
        dr_ref[...] = dr
        drb_ref[...] = dr.astype(BF16)
        dg_ref[...] += dg
        db_ref[...] += jnp.sum(dxn, axis=0, keepdims=True)

    return pl.pallas_call(
        body, grid=(s // tb,), in_specs=[row, vec, row], out_specs=[row, row, vec, vec],
        out_shape=[jax.ShapeDtypeStruct((s, w), F32), jax.ShapeDtypeStruct((s, w), BF16),
                   jax.ShapeDtypeStruct((1, w), F32), jax.ShapeDtypeStruct((1, w), F32)],
        compiler_params=_params("arbitrary"), name=name,
    )(r, g.reshape(1, w), dxn)


ATTENTION_US = {("fwd", 1): 120.0, ("fwd", 4): 95.0, ("fwd", 16): 120.0, ("bwd", 1): 150.0, ("bwd", 4): 130.0, ("bwd", 16): 200.0}
ATTENTION_HEADS_PER_STEP = {("fwd", 1): 8, ("fwd", 4): 1, ("fwd", 16): 1, ("bwd", 1): 8, ("bwd", 4): 1, ("bwd", 16): 1}


def _causal_masks():
    qi = lax.broadcasted_iota(jnp.int32, (128, 128), 0)
    kj = lax.broadcasted_iota(jnp.int32, (128, 128), 1)
    return kj <= qi, kj >= qi


def _sub_rows(d, r):
    return pl.ds(r, 128, stride=d) if d > 1 else pl.ds(0, 128)


def _for_each_residue(d, fn):
    if d <= 4:
        for r in range(d):
            fn(r)
    else:
        def step(r, carry):
            fn(r)
            return carry
        lax.fori_loop(0, d, step, 0, unroll=2)


def attention_forward(proj, group, d, heads, name, comm=None):
    s = proj.shape[0]
    rows = 128 * d
    nb = s // rows
    scale = HEAD_DIM ** -0.5
    hp = math.gcd(heads, ATTENTION_HEADS_PER_STEP["fwd", d])
    wide = 128 * hp
    qc, kc, vc = (group * 3) * heads // hp, (group * 3 + 1) * heads // hp, (group * 3 + 2) * heads // hp

    def cur(col):
        return pl.BlockSpec((rows, wide), lambda h, n: (n, col + h))

    def prev(col):
        return pl.BlockSpec((rows, wide), lambda h, n: (jnp.maximum(n - 1, 0), col + h))

    out = pl.BlockSpec((rows, wide), lambda h, n: (n, h))

    def body(q_ref, kc_ref, kp_ref, vc_ref, vp_ref, o_ref, l_ref):
        mask_c, band_p = _causal_masks()
        mask_p = band_p & (pl.program_id(1) > 0)

        def one(r):
            rws = _sub_rows(d, r)
            for hh in range(hp):
                sl = (rws, pl.ds(hh * 128, 128))
                q = q_ref[sl].astype(BF16)
                s_c = jnp.where(mask_c, _dot(q, kc_ref[sl].astype(BF16), NT) * scale, -jnp.inf)
                s_p = jnp.where(mask_p, _dot(q, kp_ref[sl].astype(BF16), NT) * scale, -jnp.inf)
                m = jnp.maximum(jnp.max(s_c, axis=-1, keepdims=True), jnp.max(s_p, axis=-1, keepdims=True))
                e_c, e_p = jnp.exp(s_c - m), jnp.exp(s_p - m)
                l = jnp.sum(e_c, axis=-1, keepdims=True) + jnp.sum(e_p, axis=-1, keepdims=True)
                o = _dot(e_c.astype(BF16), vc_ref[sl].astype(BF16), NN) + _dot(e_p.astype(BF16), vp_ref[sl].astype(BF16), NN)
                o_ref[sl] = o / l
                l_ref[sl] = jnp.broadcast_to(m + jnp.log(l), (128, 128))

        _for_each_residue(d, one)

    return _call(
        body, grid=(heads // hp, nb),
        in_specs=[cur(qc), cur(kc), prev(kc), cur(vc), prev(vc)], out_specs=[out, out],
        out_shape=[jax.ShapeDtypeStruct((s, heads * 128), F32)] * 2, scratch_shapes=[],
        args=[proj] * 5, name=name, semantics=("parallel", "parallel"),
        jobs=comm.take(ATTENTION_US["fwd", d] * s * heads / (4096 * 8)) if comm is not None else (),
    )


def attention_combine(os_, ls_, name):
    s, w = os_[0].shape
    tb = _tile(s, 256)
    row = pl.BlockSpec((tb, w), lambda i: (i, 0))

    def body(o0, o1, o2, l0, l1, l2, o_ref, ob_ref, lse_ref):
        a, b, c = l0[...], l1[...], l2[...]
        m = jnp.maximum(jnp.maximum(a, b), c)
        ea, eb, ec = jnp.exp(a - m), jnp.exp(b - m), jnp.exp(c - m)
        tot = ea + eb + ec
        o = (ea / tot) * o0[...] + (eb / tot) * o1[...] + (ec / tot) * o2[...]
        o_ref[...] = o
        ob_ref[...] = o.astype(BF16)
        lse_ref[...] = m + jnp.log(tot)

    return pl.pallas_call(
        body, grid=(s // tb,), in_specs=[row] * 6, out_specs=[row] * 3,
        out_shape=[jax.ShapeDtypeStruct((s, w), F32), jax.ShapeDtypeStruct((s, w + MEM_WIDTH), BF16), jax.ShapeDtypeStruct((s, w), F32)],
        compiler_params=_params("parallel"), name=name,
    )(*os_, *ls_)


def attention_backward(proj, o, lse, dcat, group, d, heads, name, comm=None):
    s = proj.shape[0]
    rows = 128 * d
    nb = s // rows
    scale = HEAD_DIM ** -0.5
    hp = math.gcd(heads, ATTENTION_HEADS_PER_STEP["bwd", d])
    wide = 128 * hp
    qc, kc, vc = (group * 3) * heads // hp, (group * 3 + 1) * heads // hp, (group * 3 + 2) * heads // hp

    def at(col, shift):
        return pl.BlockSpec((rows, wide), lambda h, n: (jnp.clip(n + shift, 0, nb - 1), col + h))

    def out(col):
        return pl.BlockSpec((rows, wide), lambda h, n: (n, col + h))

    def body(qc_ref, qn_ref, kc_ref, kp_ref, vc_ref, vp_ref, doc_ref, don_ref, oc_ref, on_ref, lc_ref, ln_ref,
             dq_ref, dk_ref, dv_ref):
        n = pl.program_id(1)
        mask_c, band_p = _causal_masks()
        mask_b = band_p & (n > 0)
        mask_n = band_p & (n < nb - 1)

        def tile(q, k, v, do, lse_t, dsum, mask):
            p = jnp.where(mask, jnp.exp(_dot(q, k, NT) * scale - lse_t), 0.0)
            ds = (p * (_dot(do, v, NT) - dsum) * scale).astype(BF16)
            return p.astype(BF16), ds

        def one(r):
            rws = _sub_rows(d, r)
            for hh in range(hp):
                sl = (rws, pl.ds(hh * 128, 128))
                q_c, q_n = qc_ref[sl].astype(BF16), qn_ref[sl].astype(BF16)
                k_c, k_p = kc_ref[sl].astype(BF16), kp_ref[sl].astype(BF16)
                v_c, v_p = vc_ref[sl].astype(BF16), vp_ref[sl].astype(BF16)
                do_c, do_n = doc_ref[sl], don_ref[sl]
                dsum_c = jnp.sum(do_c * oc_ref[sl], axis=-1, keepdims=True)
                dsum_n = jnp.sum(do_n * on_ref[sl], axis=-1, keepdims=True)
                do_c, do_n = do_c.astype(BF16), do_n.astype(BF16)
                p_a, ds_a = tile(q_c, k_c, v_c, do_c, lc_ref[sl], dsum_c, mask_c)
                _, ds_b = tile(q_c, k_p, v_p, do_c, lc_ref[sl], dsum_c, mask_b)
                p_n, ds_n = tile(q_n, k_c, v_c, do_n, ln_ref[sl], dsum_n, mask_n)
                dq_ref[sl] = _dot(ds_a, k_c, NN) + _dot(ds_b, k_p, NN)
                dk_ref[sl] = _dot(ds_a, q_c, TN) + _dot(ds_n, q_n, TN)
                dv_ref[sl] = _dot(p_a, do_c, TN) + _dot(p_n, do_n, TN)

        _for_each_residue(d, one)

    w = heads * 128
    return _call(
        body, grid=(heads // hp, nb),
        in_specs=[at(qc, 0), at(qc, 1), at(kc, 0), at(kc, -1), at(vc, 0), at(vc, -1),
                  at(0, 0), at(0, 1), at(0, 0), at(0, 1), at(0, 0), at(0, 1)],
        out_specs=[out(0)] * 3,
        out_shape=[jax.ShapeDtypeStruct((s, w), F32)] * 3, scratch_shapes=[],
        args=[proj] * 6 + [dcat, dcat, o, o, lse, lse], name=name, semantics=("parallel", "parallel"),
        jobs=comm.take(ATTENTION_US["bwd", d] * s * heads / (4096 * 8)) if comm is not None else (),
    )


def _mem_softmax(q, kv, h, scale):
    k = kv[:, h * 128:(h + 1) * 128].astype(BF16)
    v = kv[:, MEM_WIDTH + h * 128:MEM_WIDTH + (h + 1) * 128].astype(BF16)
    sc = _dot(q, k, NT) * scale
    e = jnp.exp(sc - jnp.max(sc, axis=-1, keepdims=True))
    return e / jnp.sum(e, axis=-1, keepdims=True), k, v


def memory_attention(proj, qcol, kv, cat, name):
    s = proj.shape[0]
    tb = _tile(s, 512)
    scale = HEAD_DIM ** -0.5

    def body(q_ref, kv_ref, cat_ref, o_ref):
        kv_t = kv_ref[...]
        for h in range(MEM_HEADS):
            p, _, v = _mem_softmax(q_ref[:, h * 128:(h + 1) * 128].astype(BF16), kv_t, h, scale)
            o_ref[:, h * 128:(h + 1) * 128] = _dot(p.astype(BF16), v, NN).astype(BF16)

    return pl.pallas_call(
        body, grid=(s // tb,),
        in_specs=[pl.BlockSpec((tb, MEM_WIDTH), lambda i: (i, qcol // MEM_WIDTH)), pl.BlockSpec(kv.shape, lambda i: (0, 0)), ANY],
        out_specs=pl.BlockSpec((tb, MEM_WIDTH), lambda i: (i, cat.shape[1] // MEM_WIDTH - 1)),
        out_shape=jax.ShapeDtypeStruct(cat.shape, BF16), input_output_aliases={2: 0},
        compiler_params=_params("parallel"), name=name,
    )(proj, kv, cat)


def memory_attention_backward(proj, qcol, kv, dcat, dcol, name):
    s = proj.shape[0]
    tb = _tile(s, 512)
    scale = HEAD_DIM ** -0.5

    def body(q_ref, kv_ref, do_ref, dq_ref, dkv_ref):
        @pl.when(pl.program_id(0) == 0)
        def _():
            dkv_ref[...] = jnp.zeros_like(dkv_ref)

        kv_t = kv_ref[...]
        for h in range(MEM_HEADS):
            cols = slice(h * 128, (h + 1) * 128)
            q = q_ref[:, cols].astype(BF16)
            p, k, v = _mem_softmax(q, kv_t, h, scale)
            do = do_ref[:, cols].astype(BF16)
            dp = _dot(do, v, NT)
            ds = (p * (dp - jnp.sum(dp * p, axis=-1, keepdims=True)) * scale).astype(BF16)
            dq_ref[:, cols] = _dot(ds, k, NN).astype(BF16)
            dkv_ref[:, cols] += _dot(ds, q, TN)
            dkv_ref[:, MEM_WIDTH + h * 128:MEM_WIDTH + (h + 1) * 128] += _dot(p.astype(BF16), do, TN)

    return pl.pallas_call(
        body, grid=(s // tb,),
        in_specs=[pl.BlockSpec((tb, MEM_WIDTH), lambda i: (i, qcol // MEM_WIDTH)), pl.BlockSpec(kv.shape, lambda i: (0, 0)),
                  pl.BlockSpec((tb, MEM_WIDTH), lambda i: (i, dcol // MEM_WIDTH))],
        out_specs=[pl.BlockSpec((tb, MEM_WIDTH), lambda i: (i, 0)), pl.BlockSpec(kv.shape, lambda i: (0, 0))],
        out_shape=[jax.ShapeDtypeStruct((s, MEM_WIDTH), BF16), jax.ShapeDtypeStruct(kv.shape, F32)],
        compiler_params=_params("arbitrary"), name=name,
    )(proj, kv, dcat)


_SQRT_HALF = math.sqrt(0.5)
_INV_SQRT_2PI = 1.0 / math.sqrt(2.0 * math.pi)


def _gelu(x):
    return 0.5 * x * (1.0 + lax.erf(x * _SQRT_HALF))


def _gelu_grad(x):
    return 0.5 * (1.0 + lax.erf(x * _SQRT_HALF)) + x * (_INV_SQRT_2PI * jnp.exp(-0.5 * x * x))


def _gmlp_specs(s, wd, groups, tb):
    half = lambda c: pl.BlockSpec((tb, wd), lambda i: (i, c))
    ws_spec = pl.BlockSpec((groups, 128, 128), lambda i: (0, 0, 0))
    bs_spec = pl.BlockSpec((groups, 128, 1), lambda i: (0, 0, 0))
    vec = pl.BlockSpec((1, wd), lambda i: (0, 0))
    return half, ws_spec, bs_spec, vec


def _vnorm(zv, g, b):
    mu = jnp.mean(zv, axis=-1, keepdims=True)
    var = jnp.mean(jnp.square(zv - mu), axis=-1, keepdims=True)
    return (zv - mu) * lax.rsqrt(var + LN_EPS) * g + b


def gmlp_forward(proj, ws, bs, vg, vb, name):
    s = proj.shape[0]
    groups = ws.shape[0]
    wd = groups * 128
    tb = _tile(s, 512)
    half, ws_spec, bs_spec, vec = _gmlp_specs(s, wd, groups, tb)

    def body(pu_ref, pv_ref, ws_ref, bs_ref, vg_ref, vb_ref, o_ref, vn_ref):
        causal, _ = _causal_masks()
        vn_ref[...] = _vnorm(_gelu(pv_ref[...]), vg_ref[...], vb_ref[...]).astype(BF16)
        for g in range(groups):
            cols = slice(g * 128, (g + 1) * 128)
            wm = jnp.where(causal, ws_ref[g], 0.0).astype(BF16)
            for c in range(tb // 128):
                rws = slice(c * 128, (c + 1) * 128)
                sg = _dot(wm, vn_ref[rws, cols], NN) + bs_ref[g]
                o_ref[rws, cols] = (_gelu(pu_ref[rws, cols]) * sg).astype(BF16)

    return pl.pallas_call(
        body, grid=(s // tb,),
        in_specs=[half(0), half(1), ws_spec, bs_spec, vec, vec], out_specs=pl.BlockSpec((tb, wd), lambda i: (i, 0)),
        out_shape=jax.ShapeDtypeStruct((s, wd + MEM_WIDTH), BF16),
        scratch_shapes=[pltpu.VMEM((tb, wd), BF16)],
        compiler_params=_params("parallel"), name=name,
    )(proj, proj, ws, bs.reshape(groups, 128, 1), vg.reshape(1, wd), vb.reshape(1, wd))


def gmlp_backward(proj, ws, bs, vg, vb, dcat, name):
    s = proj.shape[0]
    groups = ws.shape[0]
    wd = groups * 128
    tb = _tile(s, 512)
    half, ws_spec, bs_spec, vec = _gmlp_specs(s, wd, groups, tb)

    def body(pu_ref, pv_ref, ws_ref, bs_ref, vg_ref, vb_ref, do_ref, dpu_ref, dpv_ref, dws_ref, dbs_ref, dvg_ref, dvb_ref,
             vn_ref, dvn_ref):
        @pl.when(pl.program_id(0) == 0)
        def _():
            dws_ref[...] = jnp.zeros_like(dws_ref)
            dbs_ref[...] = jnp.zeros_like(dbs_ref)
            dvg_ref[...] = jnp.zeros_like(dvg_ref)
            dvb_ref[...] = jnp.zeros_like(dvb_ref)

        causal, _ = _causal_masks()
        pv = pv_ref[...]
        zv = _gelu(pv)
        vn_ref[...] = _vnorm(zv, vg_ref[...], vb_ref[...]).astype(BF16)
        for g in range(groups):
            cols = slice(g * 128, (g + 1) * 128)
            wm = jnp.where(causal, ws_ref[g], 0.0).astype(BF16)
            dws_g = jnp.zeros((128, 128), F32)
            dbs_g = jnp.zeros((128, 1), F32)
            for c in range(tb // 128):
                rws = slice(c * 128, (c + 1) * 128)
                vn = vn_ref[rws, cols]
                pu = pu_ref[rws, cols]
                do = do_ref[rws, cols]
                sg = _dot(wm, vn, NN) + bs_ref[g]
                dpu_ref[rws, cols] = (do * sg * _gelu_grad(pu)).astype(BF16)
                dsg = do * _gelu(pu)
                dsg_b = dsg.astype(BF16)
                dws_g += _dot(dsg_b, vn, NT)
                dbs_g += jnp.sum(dsg, axis=-1, keepdims=True)
                dvn_ref[rws, cols] = _dot(wm, dsg_b, TN)
            dws_ref[g] += jnp.where(causal, dws_g, 0.0)
            dbs_ref[g] += dbs_g
        dvn = dvn_ref[...]
        dzv, dvg = _ln_bwd_tile(zv, vg_ref[...], dvn)
        dvg_ref[...] += dvg
        dvb_ref[...] += jnp.sum(dvn, axis=0, keepdims=True)
        dpv_ref[...] = (dzv * _gelu_grad(pv)).astype(BF16)

    row = pl.BlockSpec((tb, wd), lambda i: (i, 0))
    dpu, dpv, dws, dbs, dvg, dvb = pl.pallas_call(
        body, grid=(s // tb,),
        in_specs=[half(0), half(1), ws_spec, bs_spec, vec, vec, row],
        out_specs=[row, row, ws_spec, bs_spec, vec, vec],
        out_shape=[jax.ShapeDtypeStruct((s, wd), BF16), jax.ShapeDtypeStruct((s, wd), BF16),
                   jax.ShapeDtypeStruct((groups, 128, 128), F32), jax.ShapeDtypeStruct((groups, 128, 1), F32),
                   jax.ShapeDtypeStruct((1, wd), F32), jax.ShapeDtypeStruct((1, wd), F32)],
        scratch_shapes=[pltpu.VMEM((tb, wd), BF16), pltpu.VMEM((tb, wd), F32)],
        compiler_params=_params("arbitrary"), name=name,
    )(proj, proj, ws, bs.reshape(groups, 128, 1), vg.reshape(1, wd), vb.reshape(1, wd), dcat)
    return dpu, dpv, dws, dbs.reshape(groups, 128), dvg, dvb


def loss_head(y, target, name):
    s, w = y.shape
    tb = _tile(s, 256)
    row = pl.BlockSpec((tb, w), lambda i: (i, 0))
    nsteps = s // tb

    def body(y_ref, t_ref, loss_ref, dy_ref, acc_ref):
        i = pl.program_id(0)

        @pl.when(i == 0)
        def _():
            acc_ref[...] = jnp.zeros_like(acc_ref)

        err = y_ref[...] - t_ref[...]
        dy_ref[...] = err / w
        acc_ref[...] += jnp.sum(jnp.mean(jnp.square(err), axis=-1, keepdims=True), axis=0, keepdims=True)

        @pl.when(i == nsteps - 1)
        def _():
            loss_ref[...] = jnp.broadcast_to(0.5 * acc_ref[...], loss_ref.shape)

    return pl.pallas_call(
        body, grid=(nsteps,), in_specs=[row, row],
        out_specs=[pl.BlockSpec((8, 128), lambda i: (0, 0)), row],
        out_shape=[jax.ShapeDtypeStruct((8, 128), F32), jax.ShapeDtypeStruct((s, w), F32)],
        scratch_shapes=[pltpu.VMEM((1, 1), F32)],
        compiler_params=_params("arbitrary"), name=name,
    )(y, target)


PARTS_WINDOW_ELEMS = 1024 * 1024


def _row_tile(r, c, budget):
    if r * c <= budget or r % 16:
        return r
    fits = [t for t in range(16, r, 16) if r % t == 0 and t * c <= budget]
    return max(fits) if fits else 16


def _sum_in_device_order(p_ref):
    g = p_ref[0].astype(F32)
    for j in range(1, p_ref.shape[0]):
        g = g + p_ref[j].astype(F32)
    return g


def _adamw_update(g, w, m, v):
    nm = ADAM_B1 * m + (1.0 - ADAM_B1) * g
    nv = ADAM_B2 * v + (1.0 - ADAM_B2) * jnp.square(g)
    m_hat = nm / (1.0 - ADAM_B1 ** ADAM_STEP)
    v_hat = nv / (1.0 - ADAM_B2 ** ADAM_STEP)
    return -ADAM_LR * (m_hat / (jnp.sqrt(v_hat) + ADAM_EPS) + ADAM_WD * w), nm, nv


def sum_parts(chunks, name):
    p, r, c = chunks[0].shape
    tr = _row_tile(r, c, PARTS_WINDOW_ELEMS // len(chunks))
    nb = r // tr

    def chunk_spec(q):
        return pl.BlockSpec((p, tr, c), lambda ch, i: (0, jnp.where(ch == q, i, 0), 0))

    def body(*refs):
        for q in range(len(chunks)):
            @pl.when(pl.program_id(0) == q)
            def _():
                refs[-1][...] = _sum_in_device_order(refs[q])

    return pl.pallas_call(
        body, grid=(len(chunks), nb), in_specs=[chunk_spec(q) for q in range(len(chunks))],
        out_specs=pl.BlockSpec((tr, c), lambda ch, i: (ch * nb + i, 0)),
        out_shape=jax.ShapeDtypeStruct((len(chunks) * r, c), F32),
        compiler_params=_params("arbitrary", "arbitrary"), name=name,
    )(*chunks)


ELEMENTWISE_BYTES_PER_US = 2.5e6


def adamw(parts, w, m, v, name, emit_grad=True, comm=None):
    p, r, c = parts.shape
    tr = _row_tile(r, c, 160 * 1024)
    n_out = 4 if emit_grad else 3

    def body(p_ref, w_ref, m_ref, v_ref, *out_refs):
        g = _sum_in_device_order(p_ref)
        vals = _adamw_update(g, w_ref[...], m_ref[...], v_ref[...])
        for o_ref, val in zip(out_refs, ((g,) + vals) if emit_grad else vals):
            o_ref[...] = val

    row = pl.BlockSpec((tr, c), lambda i: (i, 0))
    host_us = (parts.size * parts.dtype.itemsize + (3 + n_out) * 4 * r * c) / ELEMENTWISE_BYTES_PER_US
    return _call(
        body, grid=(r // tr,), in_specs=[pl.BlockSpec((p, tr, c), lambda i: (0, i, 0)), row, row, row],
        out_specs=[row] * n_out, out_shape=[jax.ShapeDtypeStruct((r, c), F32)] * n_out, scratch_shapes=[],
        args=[parts, w, m, v], name=name, semantics=("parallel",), jobs=comm.take(host_us) if comm is not None else (),
    )


def adamw_layers(chunks, w, m, v, name, comm=None):
    n_layers, r, c = w.shape
    n_chunks = len(chunks[0])
    p, rc, _ = chunks[0][0].shape
    tr = _row_tile(rc, c, PARTS_WINDOW_ELEMS // (n_layers * n_chunks))
    nb = rc // tr
    flat = [ch for layer in chunks for ch in layer]

    def chunk_spec(q):
        return pl.BlockSpec((p, tr, c), lambda l, ch, i: (0, jnp.where(l * n_chunks + ch == q, i, 0), 0))

    def body(*refs):
        p_refs, (w_ref, m_ref, v_ref), out_refs = refs[:len(flat)], refs[len(flat):len(flat) + 3], refs[len(flat) + 3:]
        for q in range(len(flat)):
            @pl.when(pl.program_id(0) * n_chunks + pl.program_id(1) == q)
            def _():
                g = _sum_in_device_order(p_refs[q])
                vals = _adamw_update(g, w_ref[...], m_ref[...], v_ref[...])
                for o_ref, val in zip(out_refs, (g,) + vals):
                    o_ref[...] = val

    state = pl.BlockSpec((None, tr, c), lambda l, ch, i: (l, ch * nb + i, 0))
    host_us = (len(flat) * flat[0].size * flat[0].dtype.itemsize + 7 * 4 * w.size) / ELEMENTWISE_BYTES_PER_US
    return _call(
        body, grid=(n_layers, n_chunks, nb), in_specs=[chunk_spec(q) for q in range(len(flat))] + [state] * 3,
        out_specs=[state] * 4, out_shape=[jax.ShapeDtypeStruct(w.shape, F32)] * 4, scratch_shapes=[],
        args=[*flat, w, m, v], name=name, semantics=("arbitrary", "arbitrary", "arbitrary"),
        jobs=comm.take(host_us) if comm is not None else (),
    )


def kernel(x, mem, w_in_a, w_in_b, w_s, b_s, vnorm_g, vnorm_b, w_mem_kv, w_out, ln1_g, ln1_b, w_ff1, w_ff2, ln2_g, ln2_b, loss_target, m_w_in_a, m_w_in_b, m_w_s, m_b_s, m_vnorm_g, m_vnorm_b, m_w_mem_kv, m_w_out, m_ln1_g, m_ln1_b, m_w_ff1, m_w_ff2, m_ln2_g, m_ln2_b, v_w_in_a, v_w_in_b, v_w_s, v_b_s, v_vnorm_g, v_vnorm_b, v_w_mem_kv, v_w_out, v_ln1_g, v_ln1_b, v_w_ff1, v_w_ff2, v_ln2_g, v_ln2_b):
    depth = w_ff1.shape[0]
    groups = w_s.shape[1]
    mix_w = groups * HEAD_DIM
    n_b = vnorm_g.shape[0]
    alpha = (2.0 * depth) ** 0.25
    me = 4 * lax.axis_index("x") + 2 * lax.axis_index("y") + lax.axis_index("c")
    x0 = x[0]
    mem_b = mem[0].astype(BF16)
    target = loss_target[0]

    comm = Exchanges()
    weight_jobs = []
    for i in range(depth):
        w_in = (w_in_a if i % 2 == 0 else w_in_b)[i // 2]
        groups_of_shards = ([w_in.T, w_mem_kv[i]], [w_out[i].T], [w_ff1[i].T], [w_ff2[i]])
        weight_jobs.append([GatherJob([t.astype(BF16) for t in shards], f"g{i}{tag}")
                            for shards, tag in zip(groups_of_shards, ("in", "out", "ff1", "ff2"))])
    weight_jobs[0][0] = GatherJob(weight_jobs[0][0].arrays + [jnp.concatenate([vnorm_g, vnorm_b], axis=0)], "g0in")
    for jobs in weight_jobs:
        for job in jobs:
            comm.submit(job)

    def gathered(job):
        return [g.reshape(N_DEV * g.shape[1], g.shape[2]) for g in comm.require(job)]

    vnorm = comm.require(weight_jobs[0][0])[2]
    vnorm = jnp.transpose(vnorm, (1, 0, 2)).reshape(2 * n_b, mix_w)
    vg_full, vb_full = vnorm[:n_b], vnorm[n_b:]

    saved = []
    weights = []
    xf, xb = x0, x0.astype(BF16)
    for i in range(depth):
        j = i // 2
        w_in_t, w_kv = gathered(weight_jobs[i][0])[:2]
        proj = matmul(xb, w_in_t, "nt", name=f"in_proj_{i % 2}", comm=comm)
        kv = matmul(mem_b, w_kv, "nn", name="mem_kv")
        if i % 2 == 0:
            os_, ls_ = [], []
            for g, d in enumerate(A_DILATIONS):
                o_g, l_g = attention_forward(proj, g, d, groups, name=f"attn_fwd_d{d}", comm=comm)
                os_.append(o_g)
                ls_.append(l_g)
            o, cat, lse = attention_combine(os_, ls_, name="attn_combine")
            qcol = 9 * mix_w
            extra = (o, lse)
        else:
            cat = gmlp_forward(proj, w_s[j], b_s[j], vg_full[j], vb_full[j], name="gmlp_fwd")
            qcol = 2 * mix_w
            extra = ()
        cat = memory_attention(proj, qcol, kv, cat, name=f"mem_attn_{i % 2}")
        (w_o_t,) = gathered(weight_jobs[i][1])
        y1 = matmul(cat, w_o_t, "nt", name="out_proj", comm=comm)
        r1, x1, x1b = ln_residual(xf, y1, ln1_g[i], ln1_b[i], alpha, name="ln1")
        (w_1_t,) = gathered(weight_jobs[i][2])
        h, hid = matmul(x1b, w_1_t, "nt", name="ff1", out_dtypes=(F32, BF16), comm=comm,
                        epilogue=lambda acc: (acc, jnp.square(jnp.maximum(acc, 0.0))))
        (w_2,) = gathered(weight_jobs[i][3])
        y2 = matmul(hid, w_2, "nn", name="ff2", comm=comm)
        r2, x2, x2b = ln_residual(x1, y2, ln2_g[i], ln2_b[i], alpha, name="ln2")
        weights.append((w_in_t, w_o_t, w_1_t, w_2, w_kv))
        saved.append((xb, proj, kv, extra, qcol, cat, r1, x1b, h, hid, r2))
        xf, xb = x2, x2b

    loss_tile, dx = loss_head(xf, target, name="loss_head")
    loss = lax.psum(loss_tile[0, 0], ("x", "y", "c"))

    def exchange(dw, tag):
        rows, cols = dw.shape[0] // N_DEV, dw.shape[1]
        p = dw.reshape(N_DEV, rows, cols)
        n_chunks = 1
        while ExchangeJob.US_PER_MB * dw.size * 2 / 1e6 > 110.0 * n_chunks and rows % (32 * n_chunks) == 0:
            n_chunks *= 2
        step = rows // n_chunks
        return [comm.submit(ExchangeJob(p, q * step, step, f"x{tag}{q}")) for q in range(n_chunks)]

    recv = {n: [None] * depth for n in ("w_in", "w_mem_kv", "w_out", "w_ff1", "w_ff2")}
    small = {n: [None] * depth for n in ("ln1_g", "ln1_b", "ln2_g", "ln2_b")}
    small_b = {n: [None] * n_b for n in ("w_s", "b_s", "vnorm_g", "vnorm_b")}
    for i in reversed(range(depth)):
        w_in_t, w_o_t, w_1_t, w_2, w_kv = weights[i]
        xb, proj, kv, extra, qcol, cat, r1, x1b, h, hid, r2 = saved[i]
        j = i // 2
        dr2, dr2b, small["ln2_g"][i], small["ln2_b"][i] = ln_backward(r2, ln2_g[i], dx, name="ln2_bwd")
        dh = matmul(dr2b, w_2, "nt", name="ff2_dx", extras=(h,), out_dtypes=(BF16,), comm=comm,
                    epilogue=lambda acc, h_t: (acc * (2.0 * jnp.maximum(h_t, 0.0)),))
        recv["w_ff2"][i] = exchange(matmul(hid, dr2b, "tn", name="ff2_dw", out_dtypes=(BF16,), comm=comm), f"ff2{i}")
        dx1 = matmul(dh, w_1_t, "nn", name="ff1_dx", extras=(dr2,), comm=comm, epilogue=lambda acc, res: (acc + alpha * res,))
        recv["w_ff1"][i] = exchange(matmul(dh, x1b, "tn", name="ff1_dw", out_dtypes=(BF16,), comm=comm), f"ff1{i}")
        dr1, dr1b, small["ln1_g"][i], small["ln1_b"][i] = ln_backward(r1, ln1_g[i], dx1, name="ln1_bwd")
        dcat = matmul(dr1b, w_o_t, "nn", name="out_proj_dx", comm=comm)
        recv["w_out"][i] = exchange(matmul(dr1b, cat, "tn", name="out_proj_dw", out_dtypes=(BF16,), comm=comm), f"out{i}")
        dq_mem, dkv = memory_attention_backward(proj, qcol, kv, dcat, mix_w, name=f"mem_attn_bwd_{i % 2}")
        recv["w_mem_kv"][i] = exchange(matmul(mem_b, dkv, "tn", name="mem_kv_dw", out_dtypes=(BF16,)), f"kv{i}")
        if i % 2 == 0:
            o, lse = extra
            pieces = []
            for g, d in enumerate(A_DILATIONS):
                pieces += [t.astype(BF16) for t in attention_backward(proj, o, lse, dcat, g, d, groups, name=f"attn_bwd_d{d}", comm=comm)]
            dproj = jnp.concatenate(pieces + [dq_mem], axis=1)
        else:
            dpu, dpv, small_b["w_s"][j], small_b["b_s"][j], small_b["vnorm_g"][j], small_b["vnorm_b"][j] = gmlp_backward(
                proj, w_s[j], b_s[j], vg_full[j], vb_full[j], dcat, name="gmlp_bwd")
            dproj = jnp.concatenate([dpu, dpv, dq_mem], axis=1)
        recv["w_in"][i] = exchange(matmul(dproj, xb, "tn", name=f"in_proj_dw_{i % 2}", out_dtypes=(BF16,), comm=comm), f"in{i}")
        dx = matmul(dproj, w_in_t, "nn", name=f"in_proj_dx_{i % 2}", extras=(dr1,), comm=comm, epilogue=lambda acc, res: (acc + alpha * res,))
    grad_x = dx[None]

    flat_small = jnp.concatenate(
        [jnp.concatenate(small[n], axis=0).reshape(-1) for n in ("ln1_g", "ln1_b", "ln2_g", "ln2_b")]
        + [jnp.stack(small_b[n]).reshape(-1) for n in ("w_s", "b_s", "vnorm_g", "vnorm_b")])
    pad = (-flat_small.size) % 1024
    small_job = comm.submit(GatherJob([jnp.pad(flat_small, (0, pad)).reshape(-1, 1024)], "gsmall"))

    results = {}

    def received(name, i):
        return [comm.require(job)[0] for job in recv[name][i]]

    def transposed(name, layers, w_, m_, v_):
        grad = jnp.stack([sum_parts(received(name, i), name=f"sum_{name}_{i % 2}").T for i in layers])
        flat = lambda t: t.reshape(-1, t.shape[-1])
        upd = adamw(flat(grad)[None], flat(w_), flat(m_), flat(v_), name=f"adamw_{name}_{layers[0] % 2}", emit_grad=False, comm=comm)
        return [grad] + [t.reshape(w_.shape) for t in upd]

    results["w_ff2"] = adamw_layers([received("w_ff2", i) for i in range(depth)], w_ff2, m_w_ff2, v_w_ff2, name="adamw_w_ff2", comm=comm)
    results["w_ff1"] = transposed("w_ff1", list(range(depth)), w_ff1, m_w_ff1, v_w_ff1)
    results["w_out"] = transposed("w_out", list(range(depth)), w_out, m_w_out, v_w_out)
    results["w_mem_kv"] = adamw_layers([received("w_mem_kv", i) for i in range(depth)], w_mem_kv, m_w_mem_kv, v_w_mem_kv, name="adamw_w_mem_kv")
    results["w_in_b"] = transposed("w_in", list(range(1, depth, 2)), w_in_b, m_w_in_b, v_w_in_b)
    results["w_in_a"] = transposed("w_in", list(range(0, depth, 2)), w_in_a, m_w_in_a, v_w_in_a)

    gathered_small = comm.require(small_job)[0].reshape(N_DEV, -1)
    at = 0

    def take(shape):
        nonlocal at
        size = math.prod(shape)
        at += size
        return gathered_small[:, at - size:at].reshape((N_DEV,) + shape)

    for n, w_, m_, v_ in (("ln1_g", ln1_g, m_ln1_g, v_ln1_g), ("ln1_b", ln1_b, m_ln1_b, v_ln1_b),
                          ("ln2_g", ln2_g, m_ln2_g, v_ln2_g), ("ln2_b", ln2_b, m_ln2_b, v_ln2_b)):
        results[n] = adamw(take(w_.shape), w_, m_, v_, name="adamw_ln")
    lanes = lambda t: t.reshape(-1, 128)
    results["w_s"] = [t.reshape(w_s.shape) for t in adamw(take((w_s.size // 128, 128)), lanes(w_s), lanes(m_w_s), lanes(v_w_s), name="adamw_w_s")]
    results["b_s"] = [t.reshape(b_s.shape) for t in adamw(take((b_s.size // 128, 128)), lanes(b_s), lanes(m_b_s), lanes(v_b_s), name="adamw_b_s")]
    for n, w_, m_, v_ in (("vnorm_g", vnorm_g, m_vnorm_g, v_vnorm_g), ("vnorm_b", vnorm_b, m_vnorm_b, v_vnorm_b)):
        parts = lax.dynamic_slice_in_dim(take((n_b, mix_w)), me * w_.shape[1], w_.shape[1], axis=2)
        results[n] = adamw(parts, w_, m_, v_, name="adamw_vnorm")

    order =("w_in_a", "w_in_b", "w_s", "b_s", "vnorm_g", "vnorm_b", "w_mem_kv", "w_out", "ln1_g", "ln1_b", "w_ff1", "w_ff2", "ln2_g", "ln2_b")
    return (loss, grad_x, *[results[n][0] for n in order], *[results[n][1] for n in order],
            *[results[n][2] for n in order], *[results[n][3] for n in order])
```

```python
import math

import jax
import jax.numpy as jnp
from jax import lax
from jax.experimental import pallas as pl
from jax.experimental.pallas import tpu as pltpu

F32 = jnp.float32
BF16 = jnp.bfloat16
N_DEV = 8
HEAD_DIM = 128
MEM_HEADS = 4
MEM_WIDTH = MEM_HEADS * HEAD_DIM
A_DILATIONS = (1, 4, 16)
LN_EPS = 1e-5
ADAM_LR, ADAM_B1, ADAM_B2, ADAM_EPS, ADAM_WD, ADAM_STEP = 0.001, 0.9, 0.999, 1e-08, 0.01, 10
VMEM_LIMIT_BYTES = 56 * 1024 * 1024
MESH = pl.DeviceIdType.MESH
ANY = pl.BlockSpec(memory_space=pl.ANY)


def _params(*sem):
    return pltpu.CompilerParams(dimension_semantics=sem, vmem_limit_bytes=VMEM_LIMIT_BYTES)


def _tile(dim, pref):
    if dim <= pref:
        return dim
    best = None
    for t in range(128, pref + 1, 128):
        if dim % t == 0:
            best = t
    return best if best is not None else dim


def _dot(a, b, dims):
    return lax.dot_general(a, b, (dims, ((), ())), preferred_element_type=F32)


NN = ((1,), (0,))
NT = ((1,), (1,))
TN = ((0,), (0,))


def _my_position():
    return lax.axis_index("x"), lax.axis_index("y"), lax.axis_index("c")


class GatherJob:
    US_PER_MB = 46.0
    OVERSHOOT = 1.25

    def __init__(self, arrays, tag):
        self.arrays, self.tag, self.outs = list(arrays), tag, None
        self.out_shapes = [jax.ShapeDtypeStruct((N_DEV,) + v.shape, v.dtype) for v in self.arrays]
        self.n_sems, self.n_local = 7 * len(self.arrays), len(self.arrays)
        self.est_us = self.US_PER_MB * sum(v.size * v.dtype.itemsize for v in self.arrays) / 1e6

    def phases(self, v_refs, out_refs, send_sems, recv_sems, local_sems, sem0, loc0):
        n = len(self.arrays)
        x, y, c = _my_position()
        me, sibling = (x, y, c), (x, y, 1 - c)
        chips = [(1 - x, y), (x, 1 - y), (1 - x, 1 - y)]

        def copy(a, k, block, to, from_input=False):
            px, py, pc = block
            slot = out_refs[a].at[4 * px + 2 * py + pc]
            return pltpu.make_async_remote_copy(
                src_ref=v_refs[a] if from_input else slot, dst_ref=slot, send_sem=send_sems.at[sem0 + 7 * a + k],
                recv_sem=recv_sems.at[sem0 + 7 * a + k], device_id=to, device_id_type=MESH)

        def mine(a):
            return pltpu.make_async_copy(v_refs[a], out_refs[a].at[4 * x + 2 * y + c], local_sems.at[loc0 + a])

        def first(a):
            return [copy(a, 0, me, sibling, True)] + [copy(a, 1 + j, me, (*chip, c), True) for j, chip in enumerate(chips)]

        def start():
            for a in range(n):
                mine(a).start()
                for cp in first(a):
                    cp.start()

        def middle():
            for j, chip in enumerate(chips):
                for a in range(n):
                    copy(a, 1 + j, (*chip, c), me).wait_recv()
                    copy(a, 4 + j, (*chip, c), sibling).start()

        def finish():
            for a in range(n):
                copy(a, 0, sibling, me).wait_recv()
                for j, chip in enumerate(chips):
                    copy(a, 4 + j, (*chip, 1 - c), me).wait_recv()
                for cp in first(a) + [copy(a, 4 + j, (*chip, c), sibling) for j, chip in enumerate(chips)]:
                    cp.wait_send()
                mine(a).wait()

        return start, middle, finish


class ExchangeJob:
    US_PER_MB = 11.0
    OVERSHOOT = 1.05
    RELATIONS = [(dx, dy, dc) for dx in (0, 1) for dy in (0, 1) for dc in (0, 1)][1:]

    def __init__(self, p, row0, rows, tag):
        self.arrays, self.tag, self.outs = [p], tag, None
        self.row0, self.rows = row0, rows
        self.out_shapes = [jax.ShapeDtypeStruct((N_DEV, rows, p.shape[2]), p.dtype)]
        self.n_sems, self.n_local = 7, 1
        self.est_us = self.US_PER_MB * N_DEV * rows * p.shape[2] * p.dtype.itemsize / 1e6

    def phases(self, p_refs, out_refs, send_sems, recv_sems, local_sems, sem0, loc0):
        (p_ref,), (out_ref,) = p_refs, out_refs
        x, y, c = _my_position()
        me = 4 * x + 2 * y + c
        chunk = pl.ds(self.row0, self.rows)

        def mine():
            return pltpu.make_async_copy(p_ref.at[me, chunk], out_ref.at[me], local_sems.at[loc0])

        def copies(arriving):
            made = []
            for k, (dx, dy, dc) in enumerate(self.RELATIONS):
                px, py, pc = (x + dx) % 2, (y + dy) % 2, (c + dc) % 2
                peer = 4 * px + 2 * py + pc
                made.append(pltpu.make_async_remote_copy(
                    src_ref=p_ref.at[peer, chunk], dst_ref=out_ref.at[peer if arriving else me],
                    send_sem=send_sems.at[sem0 + k], recv_sem=recv_sems.at[sem0 + k],
                    device_id=(px, py, pc), device_id_type=MESH))
            return made

        def start():
            mine().start()
            for send in copies(False):
                send.start()

        def finish():
            for arrival in copies(True):
                arrival.wait_recv()
            for send in copies(False):
                send.wait_send()
            mine().wait()

        return start, (lambda: None), finish


class Exchanges:
    MIN_HOST_US = 80.0

    def __init__(self):
        self.queue = []

    def submit(self, job):
        self.queue.append(job)
        return job

    def take(self, host_us):
        jobs, used = [], 0.0
        if host_us >= self.MIN_HOST_US:
            for job in list(self.queue):
                if used + job.est_us <= job.OVERSHOOT * host_us:
                    used += job.est_us
                    jobs.append(job)
                    self.queue.remove(job)
        return jobs

    def require(self, job):
        if job.outs is None:
            self.queue.remove(job)
            _call(lambda: None, grid=(), in_specs=[], out_specs=[], out_shape=[], scratch_shapes=[], args=[],
                  name="exchange", semantics=(), jobs=[job])
        return job.outs


def _call(body, *, grid, in_specs, out_specs, out_shape, scratch_shapes, args, name, semantics, jobs=()):
    if not jobs:
        return pl.pallas_call(body, grid=grid, in_specs=in_specs, out_specs=out_specs, out_shape=out_shape,
                              scratch_shapes=scratch_shapes, compiler_params=_params(*semantics), name=name)(*args)
    n_in, n_out, n_scr = len(in_specs), len(out_shape), len(scratch_shapes)
    j_in = [a for job in jobs for a in job.arrays]
    j_out = [s for job in jobs for s in job.out_shapes]
    n_sems, n_local = sum(job.n_sems for job in jobs), sum(job.n_local for job in jobs)
    steps = math.prod(grid)
    middle_step = (7 * steps) // 8

    def wrapped(*refs):
        ins, refs = refs[:n_in], refs[n_in:]
        j_ins, refs = refs[:len(j_in)], refs[len(j_in):]
        outs, refs = refs[:n_out], refs[n_out:]
        j_outs, refs = refs[:len(j_out)], refs[len(j_out):]
        scratch, (send_sems, recv_sems, local_sems) = refs[:n_scr], refs[n_scr:]
        step = 0
        for axis, extent in enumerate(grid):
            step = step * extent + pl.program_id(axis)
        phases, at_in, at_out, sem0, loc0 = [], 0, 0, 0, 0
        for job in jobs:
            k_in, k_out = len(job.arrays), len(job.out_shapes)
            phases.append(job.phases(j_ins[at_in:at_in + k_in], j_outs[at_out:at_out + k_out], send_sems, recv_sems, local_sems, sem0, loc0))
            at_in, at_out, sem0, loc0 = at_in + k_in, at_out + k_out, sem0 + job.n_sems, loc0 + job.n_local

        def run(which):
            for ph in phases:
                ph[which]()

        if steps == 1:
            run(0)
            body(*ins, *outs, *scratch)
            run(1)
            run(2)
        else:
            pl.when(step == 0)(lambda: run(0))
            body(*ins, *outs, *scratch)
            pl.when(step == middle_step)(lambda: run(1))
            pl.when(step == steps - 1)(lambda: run(2))

    res = pl.pallas_call(
        wrapped, grid=grid, in_specs=list(in_specs) + [ANY] * len(j_in), out_specs=list(out_specs) + [ANY] * len(j_out),
        out_shape=list(out_shape) + j_out,
        scratch_shapes=list(scratch_shapes) + [pltpu.SemaphoreType.DMA((n_sems,)), pltpu.SemaphoreType.DMA((n_sems,)),
                                               pltpu.SemaphoreType.DMA((n_local,))],
        compiler_params=_params(*(["arbitrary"] * len(grid))), name=name + "".join("__" + job.tag for job in jobs),
    )(*args, *j_in)
    at = n_out
    for job in jobs:
        job.outs = list(res[at:at + len(job.out_shapes)])
        at += len(job.out_shapes)
    return list(res[:n_out])


MATMUL_TILES = {"nn": (1024, 512, 2560), "nt": (1024, 512, 2560), "tn": (1024, 1024, 512)}
MATMUL_FLOPS_PER_US = {"nn": 7.5e8, "nt": 7.5e8, "tn": 6.2e8}


def matmul(a, b, mode, name, epilogue=None, extras=(), out_dtypes=(F32,), comm=None):
    if mode == "nn":
        (m, k), (k2, n) = a.shape, b.shape
    elif mode == "nt":
        (m, k), (n, k2) = a.shape, b.shape
    else:
        (k, m), (k2, n) = a.shape, b.shape
    assert k == k2, (a.shape, b.shape, mode)
    tm_pref, tn_pref, tk_pref = MATMUL_TILES[mode]
    tm = _tile(m, tm_pref)
    if mode == "tn" and tm < tm_pref:
        tn_pref *= 2
    tn, tk = _tile(n, tn_pref), _tile(k, tk_pref)
    nk = k // tk
    dims = {"nn": NN, "nt": NT, "tn": TN}[mode]
    a_spec = pl.BlockSpec((tk, tm), lambda i, j, kk: (kk, i)) if mode == "tn" else pl.BlockSpec((tm, tk), lambda i, j, kk: (i, kk))
    b_spec = pl.BlockSpec((tn, tk), lambda i, j, kk: (j, kk)) if mode == "nt" else pl.BlockSpec((tk, tn), lambda i, j, kk: (kk, j))
    tile_spec = pl.BlockSpec((tm, tn), lambda i, j, kk: (i, j))
    n_ex, n_out = len(extras), len(out_dtypes)
    if epilogue is None:
        epilogue = lambda acc: (acc,) * n_out

    def body(a_ref, b_ref, *rest):
        ex_refs, out_refs, acc_ref = rest[:n_ex], rest[n_ex:n_ex + n_out], rest[-1]
        kk = pl.program_id(2)

        @pl.when(kk == 0)
        def _():
            acc_ref[...] = jnp.zeros_like(acc_ref)

        acc_ref[...] += _dot(a_ref[...].astype(BF16), b_ref[...].astype(BF16), dims)

        @pl.when(kk == nk - 1)
        def _():
            outs = epilogue(acc_ref[...], *[e[...] for e in ex_refs])
            for o_ref, val in zip(out_refs, outs):
                o_ref[...] = val.astype(o_ref.dtype)

    host_us = 2.0 * m * n * k / MATMUL_FLOPS_PER_US[mode]
    outs = _call(
        body,
        grid=(m // tm, n // tn, nk),
        in_specs=[a_spec, b_spec] + [tile_spec] * n_ex,
        out_specs=[tile_spec] * n_out,
        out_shape=[jax.ShapeDtypeStruct((m, n), dt) for dt in out_dtypes],
        scratch_shapes=[pltpu.VMEM((tm, tn), F32)],
        args=[a, b, *extras], name=name, semantics=("parallel", "parallel", "arbitrary"),
        jobs=comm.take(host_us) if comm is not None else (),
    )
    return outs[0] if n_out == 1 else outs


def ln_residual(x, y, g, b, alpha, name):
    s, w = x.shape
    tb = _tile(s, 256)
    row = pl.BlockSpec((tb, w), lambda i: (i, 0))
    vec = pl.BlockSpec((1, w), lambda i: (0, 0))

    def body(x_ref, y_ref, g_ref, b_ref, r_ref, xn_ref, xnb_ref):
        r = alpha * x_ref[...] + y_ref[...]
        mu = jnp.mean(r, axis=-1, keepdims=True)
        var = jnp.mean(jnp.square(r - mu), axis=-1, keepdims=True)
        xn = (r - mu) * lax.rsqrt(var + LN_EPS) * g_ref[...] + b_ref[...]
        r_ref[...] = r
        xn_ref[...] = xn
        xnb_ref[...] = xn.astype(BF16)

    return pl.pallas_call(
        body, grid=(s // tb,), in_specs=[row, row, vec, vec], out_specs=[row, row, row],
        out_shape=[jax.ShapeDtypeStruct((s, w), F32), jax.ShapeDtypeStruct((s, w), F32), jax.ShapeDtypeStruct((s, w), BF16)],
        compiler_params=_params("parallel"), name=name,
    )(x, y, g.reshape(1, w), b.reshape(1, w))


def _ln_bwd_tile(r, g, dxn):
    mu = jnp.mean(r, axis=-1, keepdims=True)
    cen = r - mu
    rstd = lax.rsqrt(jnp.mean(jnp.square(cen), axis=-1, keepdims=True) + LN_EPS)
    xhat = cen * rstd
    dxh = dxn * g
    dr = rstd * (dxh - jnp.mean(dxh, axis=-1, keepdims=True) - xhat * jnp.mean(dxh * xhat, axis=-1, keepdims=True))
    return dr, jnp.sum(dxn * xhat, axis=0, keepdims=True)


def ln_backward(r, g, dxn, name):
    s, w = r.shape
    tb = _tile(s, 256)
    row = pl.BlockSpec((tb, w), lambda i: (i, 0))
    vec = pl.BlockSpec((1, w), lambda i: (0, 0))

    def body(r_ref, g_ref, d_ref, dr_ref, drb_ref, dg_ref, db_ref):
        @pl.when(pl.program_id(0) == 0)
        def _():
            dg_ref[...] = jnp.zeros_like(dg_ref)
            db_ref[...] = jnp.zeros_like(db_ref)

        dxn = d_ref[...]
        dr, dg = _ln_bwd_tile(r_ref[...], g_ref[...], dxn)
        dr_ref[...] = dr
        drb_ref[...] = dr.astype(BF16)
        dg_ref[...] += dg
        db_ref[...] += jnp.sum(dxn, axis=0, keepdims=True)

    return pl.pallas_call(
        body, grid=(s // tb,), in_specs=[row, vec, row], out_specs=[row, row, vec, vec],
        out_shape=[jax.ShapeDtypeStruct((s, w), F32), jax.ShapeDtypeStruct((s, w), BF16),
                   jax.ShapeDtypeStruct((1, w), F32), jax.ShapeDtypeStruct((1, w), F32)],
        compiler_params=_params("arbitrary"), name=name,
    )(r, g.reshape(1, w), dxn)


ATTENTION_US = {("fwd", 1): 120.0, ("fwd", 4): 95.0, ("fwd", 16): 120.0, ("bwd", 1): 150.0, ("bwd", 4): 130.0, ("bwd", 16): 200.0}
ATTENTION_HEADS_PER_STEP = {("fwd", 1): 8, ("fwd", 4): 1, ("fwd", 16): 1, ("bwd", 1): 8, ("bwd", 4): 1, ("bwd", 16): 1}


def _causal_masks():
    qi = lax.broadcasted_iota(jnp.int32, (128, 128), 0)
    kj = lax.broadcasted_iota(jnp.int32, (128, 128), 1)
    return kj <= qi, kj >= qi


def _window_mask(has_previous):
    qi = lax.broadcasted_iota(jnp.int32, (128, 256), 0)
    kj = lax.broadcasted_iota(jnp.int32, (128, 256), 1)
    return ((kj < 128) & (kj >= qi) & has_previous) | ((kj >= 128) & (kj - 128 <= qi))


def _sub_rows(d, r):
    return pl.ds(r, 128, stride=d) if d > 1 else pl.ds(0, 128)


def _for_each_residue(d, fn):
    if d <= 4:
        for r in range(d):
            fn(r)
    else:
        def step(r, carry):
            fn(r)
            return carry
        lax.fori_loop(0, d, step, 0, unroll=2)


def attention_forward(proj, group, d, heads, name, comm=None):
    s = proj.shape[0]
    rows = 128 * d
    nb = s // rows
    scale = HEAD_DIM ** -0.5
    hp = math.gcd(heads, ATTENTION_HEADS_PER_STEP["fwd", d])
    wide = 128 * hp
    qc, kc, vc = (group * 3) * heads // hp, (group * 3 + 1) * heads // hp, (group * 3 + 2) * heads // hp

    def cur(col):
        return pl.BlockSpec((rows, wide), lambda h, n: (n, col + h))

    def prev(col):
        return pl.BlockSpec((rows, wide), lambda h, n: (jnp.maximum(n - 1, 0), col + h))

    out = pl.BlockSpec((rows, wide), lambda h, n: (n, h))

    def body(q_ref, kc_ref, kp_ref, vc_ref, vp_ref, o_ref, l_ref):
        mask = _window_mask(pl.program_id(1) > 0)

        def one(r):
            rws = _sub_rows(d, r)
            for hh in range(hp):
                sl = (rws, pl.ds(hh * 128, 128))
                k = jnp.concatenate([kp_ref[sl].astype(BF16), kc_ref[sl].astype(BF16)], axis=0)
                v = jnp.concatenate([vp_ref[sl].astype(BF16), vc_ref[sl].astype(BF16)], axis=0)
                sc = jnp.where(mask, _dot(q_ref[sl].astype(BF16), k, NT) * scale, -jnp.inf)
                m = jnp.max(sc, axis=-1, keepdims=True)
                e = jnp.exp(sc - m)
                l = jnp.sum(e, axis=-1, keepdims=True)
                o_ref[sl] = _dot(e.astype(BF16), v, NN) / l
                l_ref[sl] = jnp.broadcast_to(m + jnp.log(l), (128, 128))

        _for_each_residue(d, one)

    return _call(
        body, grid=(heads // hp, nb),
        in_specs=[cur(qc), cur(kc), prev(kc), cur(vc), prev(vc)], out_specs=[out, out],
        out_shape=[jax.ShapeDtypeStruct((s, heads * 128), F32)] * 2, scratch_shapes=[],
        args=[proj] * 5, name=name, semantics=("parallel", "parallel"),
        jobs=comm.take(ATTENTION_US["fwd", d] * s * heads / (4096 * 8)) if comm is not None else (),
    )


def attention_combine(os_, ls_, name):
    s, w = os_[0].shape
    tb = _tile(s, 256)
    row = pl.BlockSpec((tb, w), lambda i: (i, 0))

    def body(o0, o1, o2, l0, l1, l2, o_ref, ob_ref, lse_ref):
        a, b, c = l0[...], l1[...], l2[...]
        m = jnp.maximum(jnp.maximum(a, b), c)
        ea, eb, ec = jnp.exp(a - m), jnp.exp(b - m), jnp.exp(c - m)
        tot = ea + eb + ec
        o = (ea / tot) * o0[...] + (eb / tot) * o1[...] + (ec / tot) * o2[...]
        o_ref[...] = o
        ob_ref[...] = o.astype(BF16)
        lse_ref[...] = m + jnp.log(tot)

    return pl.pallas_call(
        body, grid=(s // tb,), in_specs=[row] * 6, out_specs=[row] * 3,
        out_shape=[jax.ShapeDtypeStruct((s, w), F32), jax.ShapeDtypeStruct((s, w + MEM_WIDTH), BF16), jax.ShapeDtypeStruct((s, w), F32)],
        compiler_params=_params("parallel"), name=name,
    )(*os_, *ls_)


def attention_backward(proj, o, lse, dcat, group, d, heads, name, comm=None):
    s = proj.shape[0]
    rows = 128 * d
    nb = s // rows
    scale = HEAD_DIM ** -0.5
    hp = math.gcd(heads, ATTENTION_HEADS_PER_STEP["bwd", d])
    wide = 128 * hp
    qc, kc, vc = (group * 3) * heads // hp, (group * 3 + 1) * heads // hp, (group * 3 + 2) * heads // hp

    def at(col, shift):
        return pl.BlockSpec((rows, wide), lambda h, n: (jnp.clip(n + shift, 0, nb - 1), col + h))

    def out(col):
        return pl.BlockSpec((rows, wide), lambda h, n: (n, col + h))

    def body(qc_ref, qn_ref, kc_ref, kp_ref, vc_ref, vp_ref, doc_ref, don_ref, oc_ref, on_ref, lc_ref, ln_ref,
             dq_ref, dk_ref, dv_ref):
        n = pl.program_id(1)
        mask_w = _window_mask(n > 0)
        mask_n = _causal_masks()[1] & (n < nb - 1)

        def tile(q, k, v, do, lse_t, dsum, mask):
            p = jnp.where(mask, jnp.exp(_dot(q, k, NT) * scale - lse_t), 0.0)
            ds = (p * (_dot(do, v, NT) - dsum) * scale).astype(BF16)
            return p.astype(BF16), ds

        def one(r):
            rws = _sub_rows(d, r)
            for hh in range(hp):
                sl = (rws, pl.ds(hh * 128, 128))
                q_c, q_n = qc_ref[sl].astype(BF16), qn_ref[sl].astype(BF16)
                k_c, v_c = kc_ref[sl].astype(BF16), vc_ref[sl].astype(BF16)
                k_w = jnp.concatenate([kp_ref[sl].astype(BF16), k_c], axis=0)
                v_w = jnp.concatenate([vp_ref[sl].astype(BF16), v_c], axis=0)
                do_c, do_n = doc_ref[sl], don_ref[sl]
                dsum_c = jnp.sum(do_c * oc_ref[sl], axis=-1, keepdims=True)
                dsum_n = jnp.sum(do_n * on_ref[sl], axis=-1, keepdims=True)
                do_c, do_n = do_c.astype(BF16), do_n.astype(BF16)
                lse_c = lc_ref[sl]
                p_w, ds_w = tile(q_c, k_w, v_w, do_c, jnp.concatenate([lse_c, lse_c], axis=1), dsum_c, mask_w)
                p_n, ds_n = tile(q_n, k_c, v_c, do_n, ln_ref[sl], dsum_n, mask_n)
                dq_ref[sl] = _dot(ds_w, k_w, NN)
                both_q = jnp.concatenate([q_c, q_n], axis=0)
                both_do = jnp.concatenate([do_c, do_n], axis=0)
                dk_ref[sl] = _dot(jnp.concatenate([ds_w[:, 128:], ds_n], axis=0), both_q, TN)
                dv_ref[sl] = _dot(jnp.concatenate([p_w[:, 128:], p_n], axis=0), both_do, TN)

        _for_each_residue(d, one)

    w = heads * 128
    return _call(
        body, grid=(heads // hp, nb),
        in_specs=[at(qc, 0), at(qc, 1), at(kc, 0), at(kc, -1), at(vc, 0), at(vc, -1),
                  at(0, 0), at(0, 1), at(0, 0), at(0, 1), at(0, 0), at(0, 1)],
        out_specs=[out(0)] * 3,
        out_shape=[jax.ShapeDtypeStruct((s, w), F32)] * 3, scratch_shapes=[],
        args=[proj] * 6 + [dcat, dcat, o, o, lse, lse], name=name, semantics=("parallel", "parallel"),
        jobs=comm.take(ATTENTION_US["bwd", d] * s * heads / (4096 * 8)) if comm is not None else (),
    )


def _mem_softmax(q, kv, h, scale):
    k = kv[:, h * 128:(h + 1) * 128].astype(BF16)
    v = kv[:, MEM_WIDTH + h * 128:MEM_WIDTH + (h + 1) * 128].astype(BF16)
    sc = _dot(q, k, NT) * scale
    e = jnp.exp(sc - jnp.max(sc, axis=-1, keepdims=True))
    return e / jnp.sum(e, axis=-1, keepdims=True), k, v


def memory_attention(proj, qcol, kv, cat, name):
    s = proj.shape[0]
    tb = _tile(s, 512)
    scale = HEAD_DIM ** -0.5

    def body(q_ref, kv_ref, cat_ref, o_ref):
        kv_t = kv_ref[...]
        for h in range(MEM_HEADS):
            p, _, v = _mem_softmax(q_ref[:, h * 128:(h + 1) * 128].astype(BF16), kv_t, h, scale)
            o_ref[:, h * 128:(h + 1) * 128] = _dot(p.astype(BF16), v, NN).astype(BF16)

    return pl.pallas_call(
        body, grid=(s // tb,),
        in_specs=[pl.BlockSpec((tb, MEM_WIDTH), lambda i: (i, qcol // MEM_WIDTH)), pl.BlockSpec(kv.shape, lambda i: (0, 0)), ANY],
        out_specs=pl.BlockSpec((tb, MEM_WIDTH), lambda i: (i, cat.shape[1] // MEM_WIDTH - 1)),
        out_shape=jax.ShapeDtypeStruct(cat.shape, BF16), input_output_aliases={2: 0},
        compiler_params=_params("parallel"), name=name,
    )(proj, kv, cat)


def memory_attention_backward(proj, qcol, kv, dcat, dcol, name):
    s = proj.shape[0]
    tb = _tile(s, 512)
    scale = HEAD_DIM ** -0.5

    def body(q_ref, kv_ref, do_ref, dq_ref, dkv_ref):
        @pl.when(pl.program_id(0) == 0)
        def _():
            dkv_ref[...] = jnp.zeros_like(dkv_ref)

        kv_t = kv_ref[...]
        for h in range(MEM_HEADS):
            cols = slice(h * 128, (h + 1) * 128)
            q = q_ref[:, cols].astype(BF16)
            p, k, v = _mem_softmax(q, kv_t, h, scale)
            do = do_ref[:, cols].astype(BF16)
            dp = _dot(do, v, NT)
            ds = (p * (dp - jnp.sum(dp * p, axis=-1, keepdims=True)) * scale).astype(BF16)
            dq_ref[:, cols] = _dot(ds, k, NN).astype(BF16)
            dkv_ref[:, cols] += _dot(ds, q, TN)
            dkv_ref[:, MEM_WIDTH + h * 128:MEM_WIDTH + (h + 1) * 128] += _dot(p.astype(BF16), do, TN)

    return pl.pallas_call(
        body, grid=(s // tb,),
        in_specs=[pl.BlockSpec((tb, MEM_WIDTH), lambda i: (i, qcol // MEM_WIDTH)), pl.BlockSpec(kv.shape, lambda i: (0, 0)),
                  pl.BlockSpec((tb, MEM_WIDTH), lambda i: (i, dcol // MEM_WIDTH))],
        out_specs=[pl.BlockSpec((tb, MEM_WIDTH), lambda i: (i, 0)), pl.BlockSpec(kv.shape, lambda i: (0, 0))],
        out_shape=[jax.ShapeDtypeStruct((s, MEM_WIDTH), BF16), jax.ShapeDtypeStruct(kv.shape, F32)],
        compiler_params=_params("arbitrary"), name=name,
    )(proj, kv, dcat)


_SQRT_HALF = math.sqrt(0.5)
_INV_SQRT_2PI = 1.0 / math.sqrt(2.0 * math.pi)


def _gelu(x):
    return 0.5 * x * (1.0 + lax.erf(x * _SQRT_HALF))


def _gelu_grad(x):
    return 0.5 * (1.0 + lax.erf(x * _SQRT_HALF)) + x * (_INV_SQRT_2PI * jnp.exp(-0.5 * x * x))


def _gmlp_specs(s, wd, groups, tb):
    half = lambda c: pl.BlockSpec((tb, wd), lambda i: (i, c))
    ws_spec = pl.BlockSpec((groups, 128, 128), lambda i: (0, 0, 0))
    bs_spec = pl.BlockSpec((groups, 128, 1), lambda i: (0, 0, 0))
    vec = pl.BlockSpec((1, wd), lambda i: (0, 0))
    return half, ws_spec, bs_spec, vec


def _vnorm(zv, g, b):
    mu = jnp.mean(zv, axis=-1, keepdims=True)
    var = jnp.mean(jnp.square(zv - mu), axis=-1, keepdims=True)
    return (zv - mu) * lax.rsqrt(var + LN_EPS) * g + b


def gmlp_forward(proj, ws, bs, vg, vb, name):
    s = proj.shape[0]
    groups = ws.shape[0]
    wd = groups * 128
    tb = _tile(s, 512)
    half, ws_spec, bs_spec, vec = _gmlp_specs(s, wd, groups, tb)

    def body(pu_ref, pv_ref, ws_ref, bs_ref, vg_ref, vb_ref, o_ref, vn_ref):
        causal, _ = _causal_masks()
        vn_ref[...] = _vnorm(_gelu(pv_ref[...]), vg_ref[...], vb_ref[...]).astype(BF16)
        for g in range(groups):
            cols = slice(g * 128, (g + 1) * 128)
            wm = jnp.where(causal, ws_ref[g], 0.0).astype(BF16)
            for c in range(tb // 128):
                rws = slice(c * 128, (c + 1) * 128)
                sg = _dot(wm, vn_ref[rws, cols], NN) + bs_ref[g]
                o_ref[rws, cols] = (_gelu(pu_ref[rws, cols]) * sg).astype(BF16)

    return pl.pallas_call(
        body, grid=(s // tb,),
        in_specs=[half(0), half(1), ws_spec, bs_spec, vec, vec], out_specs=pl.BlockSpec((tb, wd), lambda i: (i, 0)),
        out_shape=jax.ShapeDtypeStruct((s, wd + MEM_WIDTH), BF16),
        scratch_shapes=[pltpu.VMEM((tb, wd), BF16)],
        compiler_params=_params("parallel"), name=name,
    )(proj, proj, ws, bs.reshape(groups, 128, 1), vg.reshape(1, wd), vb.reshape(1, wd))


def gmlp_backward(proj, ws, bs, vg, vb, dcat, name):
    s = proj.shape[0]
    groups = ws.shape[0]
    wd = groups * 128
    tb = _tile(s, 512)
    half, ws_spec, bs_spec, vec = _gmlp_specs(s, wd, groups, tb)

    def body(pu_ref, pv_ref, ws_ref, bs_ref, vg_ref, vb_ref, do_ref, dpu_ref, dpv_ref, dws_ref, dbs_ref, dvg_ref, dvb_ref,
             vn_ref, dvn_ref):
        @pl.when(pl.program_id(0) == 0)
        def _():
            dws_ref[...] = jnp.zeros_like(dws_ref)
            dbs_ref[...] = jnp.zeros_like(dbs_ref)
            dvg_ref[...] = jnp.zeros_like(dvg_ref)
            dvb_ref[...] = jnp.zeros_like(dvb_ref)

        causal, _ = _causal_masks()
        pv = pv_ref[...]
        zv = _gelu(pv)
        vn_ref[...] = _vnorm(zv, vg_ref[...], vb_ref[...]).astype(BF16)
        for g in range(groups):
            cols = slice(g * 128, (g + 1) * 128)
            wm = jnp.where(causal, ws_ref[g], 0.0).astype(BF16)
            dws_g = jnp.zeros((128, 128), F32)
            dbs_g = jnp.zeros((128, 1), F32)
            for c in range(tb // 128):
                rws = slice(c * 128, (c + 1) * 128)
                vn = vn_ref[rws, cols]
                pu = pu_ref[rws, cols]
                do = do_ref[rws, cols]
                sg = _dot(wm, vn, NN) + bs_ref[g]
                dpu_ref[rws, cols] = (do * sg * _gelu_grad(pu)).astype(BF16)
                dsg = do * _gelu(pu)
                dsg_b = dsg.astype(BF16)
                dws_g += _dot(dsg_b, vn, NT)
                dbs_g += jnp.sum(dsg, axis=-1, keepdims=True)
                dvn_ref[rws, cols] = _dot(wm, dsg_b, TN)
            dws_ref[g] += jnp.where(causal, dws_g, 0.0)
            dbs_ref[g] += dbs_g
        dvn = dvn_ref[...]
        dzv, dvg = _ln_bwd_tile(zv, vg_ref[...], dvn)
        dvg_ref[...] += dvg
        dvb_ref[...] += jnp.sum(dvn, axis=0, keepdims=True)
        dpv_ref[...] = (dzv * _gelu_grad(pv)).astype(BF16)

    row = pl.BlockSpec((tb, wd), lambda i: (i, 0))
    dpu, dpv, dws, dbs, dvg, dvb = pl.pallas_call(
        body, grid=(s // tb,),
        in_specs=[half(0), half(1), ws_spec, bs_spec, vec, vec, row],
        out_specs=[row, row, ws_spec, bs_spec, vec, vec],
        out_shape=[jax.ShapeDtypeStruct((s, wd), BF16), jax.ShapeDtypeStruct((s, wd), BF16),
                   jax.ShapeDtypeStruct((groups, 128, 128), F32), jax.ShapeDtypeStruct((groups, 128, 1), F32),
                   jax.ShapeDtypeStruct((1, wd), F32), jax.ShapeDtypeStruct((1, wd), F32)],
        scratch_shapes=[pltpu.VMEM((tb, wd), BF16), pltpu.VMEM((tb, wd), F32)],
        compiler_params=_params("arbitrary"), name=name,
    )(proj, proj, ws, bs.reshape(groups, 128, 1), vg.reshape(1, wd), vb.reshape(1, wd), dcat)
    return dpu, dpv, dws, dbs.reshape(groups, 128), dvg, dvb


def loss_head(y, target, name):
    s, w = y.shape
    tb = _tile(s, 256)
    row = pl.BlockSpec((tb, w), lambda i: (i, 0))
    nsteps = s // tb

    def body(y_ref, t_ref, loss_ref, dy_ref, acc_ref):
        i = pl.program_id(0)

        @pl.when(i == 0)
        def _():
            acc_ref[...] = jnp.zeros_like(acc_ref)

        err = y_ref[...] - t_ref[...]
        dy_ref[...] = err / w
        acc_ref[...] += jnp.sum(jnp.mean(jnp.square(err), axis=-1, keepdims=True), axis=0, keepdims=True)

        @pl.when(i == nsteps - 1)
        def _():
            loss_ref[...] = jnp.broadcast_to(0.5 * acc_ref[...], loss_ref.shape)

    return pl.pallas_call(
        body, grid=(nsteps,), in_specs=[row, row],
        out_specs=[pl.BlockSpec((8, 128), lambda i: (0, 0)), row],
        out_shape=[jax.ShapeDtypeStruct((8, 128), F32), jax.ShapeDtypeStruct((s, w), F32)],
        scratch_shapes=[pltpu.VMEM((1, 1), F32)],
        compiler_params=_params("arbitrary"), name=name,
    )(y, target)


PARTS_WINDOW_ELEMS = 1024 * 1024


def _row_tile(r, c, budget):
    if r * c <= budget or r % 16:
        return r
    fits = [t for t in range(16, r, 16) if r % t == 0 and t * c <= budget]
    return max(fits) if fits else 16


def _sum_in_device_order(p_ref):
    g = p_ref[0].astype(F32)
    for j in range(1, p_ref.shape[0]):
        g = g + p_ref[j].astype(F32)
    return g


def _adamw_update(g, w, m, v):
    nm = ADAM_B1 * m + (1.0 - ADAM_B1) * g
    nv = ADAM_B2 * v + (1.0 - ADAM_B2) * jnp.square(g)
    m_hat = nm / (1.0 - ADAM_B1 ** ADAM_STEP)
    v_hat = nv / (1.0 - ADAM_B2 ** ADAM_STEP)
    return -ADAM_LR * (m_hat / (jnp.sqrt(v_hat) + ADAM_EPS) + ADAM_WD * w), nm, nv


def sum_parts(chunks, name):
    p, r, c = chunks[0].shape
    tr = _row_tile(r, c, PARTS_WINDOW_ELEMS // len(chunks))
    nb = r // tr

    def chunk_spec(q):
        return pl.BlockSpec((p, tr, c), lambda ch, i: (0, jnp.where(ch == q, i, 0), 0))

    def body(*refs):
        for q in range(len(chunks)):
            @pl.when(pl.program_id(0) == q)
            def _():
                refs[-1][...] = _sum_in_device_order(refs[q])

    return pl.pallas_call(
        body, grid=(len(chunks), nb), in_specs=[chunk_spec(q) for q in range(len(chunks))],
        out_specs=pl.BlockSpec((tr, c), lambda ch, i: (ch * nb + i, 0)),
        out_shape=jax.ShapeDtypeStruct((len(chunks) * r, c), F32),
        compiler_params=_params("arbitrary", "arbitrary"), name=name,
    )(*chunks)


ELEMENTWISE_BYTES_PER_US = 2.5e6


def adamw(parts, w, m, v, name, emit_grad=True, comm=None):
    p, r, c = parts.shape
    tr = _row_tile(r, c, 160 * 1024)
    n_out = 4 if emit_grad else 3

    def body(p_ref, w_ref, m_ref, v_ref, *out_refs):
        g = _sum_in_device_order(p_ref)
        vals = _adamw_update(g, w_ref[...], m_ref[...], v_ref[...])
        for o_ref, val in zip(out_refs, ((g,) + vals) if emit_grad else vals):
            o_ref[...] = val

    row = pl.BlockSpec((tr, c), lambda i: (i, 0))
    host_us = (parts.size * parts.dtype.itemsize + (3 + n_out) * 4 * r * c) / ELEMENTWISE_BYTES_PER_US
    return _call(
        body, grid=(r // tr,), in_specs=[pl.BlockSpec((p, tr, c), lambda i: (0, i, 0)), row, row, row],
        out_specs=[row] * n_out, out_shape=[jax.ShapeDtypeStruct((r, c), F32)] * n_out, scratch_shapes=[],
        args=[parts, w, m, v], name=name, semantics=("parallel",), jobs=comm.take(host_us) if comm is not None else (),
    )


def adamw_layers(chunks, w, m, v, name, comm=None):
    n_layers, r, c = w.shape
    n_chunks = len(chunks[0])
    p, rc, _ = chunks[0][0].shape
    tr = _row_tile(rc, c, PARTS_WINDOW_ELEMS // (n_layers * n_chunks))
    nb = rc // tr
    flat = [ch for layer in chunks for ch in layer]

    def chunk_spec(q):
        return pl.BlockSpec((p, tr, c), lambda l, ch, i: (0, jnp.where(l * n_chunks + ch == q, i, 0), 0))

    def body(*refs):
        p_refs, (w_ref, m_ref, v_ref), out_refs = refs[:len(flat)], refs[len(flat):len(flat) + 3], refs[len(flat) + 3:]
        for q in range(len(flat)):
            @pl.when(pl.program_id(0) * n_chunks + pl.program_id(1) == q)
            def _():
                g = _sum_in_device_order(p_refs[q])
                vals = _adamw_update(g, w_ref[...], m_ref[...], v_ref[...])
                for o_ref, val in zip(out_refs, (g,) + vals):
                    o_ref[...] = val

    state = pl.BlockSpec((None, tr, c), lambda l, ch, i: (l, ch * nb + i, 0))
    host_us = (len(flat) * flat[0].size * flat[0].dtype.itemsize + 7 * 4 * w.size) / ELEMENTWISE_BYTES_PER_US
    return _call(
        body, grid=(n_layers, n_chunks, nb), in_specs=[chunk_spec(q) for q in range(len(flat))] + [state] * 3,
        out_specs=[state] * 4, out_shape=[jax.ShapeDtypeStruct(w.shape, F32)] * 4, scratch_shapes=[],
        args=[*flat, w, m, v], name=name, semantics=("arbitrary", "arbitrary", "arbitrary"),
        jobs=comm.take(host_us) if comm is not None else (),
    )


def kernel(x, mem, w_in_a, w_in_b, w_s, b_s, vnorm_g, vnorm_b, w_mem_kv, w_out, ln1_g, ln1_b, w_ff1, w_ff2, ln2_g, ln2_b, loss_target, m_w_in_a, m_w_in_b, m_w_s, m_b_s, m_vnorm_g, m_vnorm_b, m_w_mem_kv, m_w_out, m_ln1_g, m_ln1_b, m_w_ff1, m_w_ff2, m_ln2_g, m_ln2_b, v_w_in_a, v_w_in_b, v_w_s, v_b_s, v_vnorm_g, v_vnorm_b, v_w_mem_kv, v_w_out, v_ln1_g, v_ln1_b, v_w_ff1, v_w_ff2, v_ln2_g, v_ln2_b):
    depth = w_ff1.shape[0]
    groups = w_s.shape[1]
    mix_w = groups * HEAD_DIM
    n_b = vnorm_g.shape[0]
    alpha = (2.0 * depth) ** 0.25
    me = 4 * lax.axis_index("x") + 2 * lax.axis_index("y") + lax.axis_index("c")
    x0 = x[0]
    mem_b = mem[0].astype(BF16)
    target = loss_target[0]

    comm = Exchanges()
    weight_jobs = []
    for i in range(depth):
        w_in = (w_in_a if i % 2 == 0 else w_in_b)[i // 2]
        groups_of_shards = ([w_in.T, w_mem_kv[i]], [w_out[i].T], [w_ff1[i].T], [w_ff2[i]])
        weight_jobs.append([GatherJob([t.astype(BF16) for t in shards], f"g{i}{tag}")
                            for shards, tag in zip(groups_of_shards, ("in", "out", "ff1", "ff2"))])
    weight_jobs[0][0] = GatherJob(weight_jobs[0][0].arrays + [jnp.concatenate([vnorm_g, vnorm_b], axis=0)], "g0in")
    for jobs in weight_jobs:
        for job in jobs:
            comm.submit(job)

    def gathered(job):
        return [g.reshape(N_DEV * g.shape[1], g.shape[2]) for g in comm.require(job)]

    vnorm = comm.require(weight_jobs[0][0])[2]
    vnorm = jnp.transpose(vnorm, (1, 0, 2)).reshape(2 * n_b, mix_w)
    vg_full, vb_full = vnorm[:n_b], vnorm[n_b:]

    saved = []
    weights = []
    xf, xb = x0, x0.astype(BF16)
    for i in range(depth):
        j = i // 2
        w_in_t, w_kv = gathered(weight_jobs[i][0])[:2]
        proj = matmul(xb, w_in_t, "nt", name=f"in_proj_{i % 2}", comm=comm)
        kv = matmul(mem_b, w_kv, "nn", name="mem_kv")
        if i % 2 == 0:
            os_, ls_ = [], []
            for g, d in enumerate(A_DILATIONS):
                o_g, l_g = attention_forward(proj, g, d, groups, name=f"attn_fwd_d{d}", comm=comm)
                os_.append(o_g)
                ls_.append(l_g)
            o, cat, lse = attention_combine(os_, ls_, name="attn_combine")
            qcol = 9 * mix_w
            extra = (o, lse)
        else:
            cat = gmlp_forward(proj, w_s[j], b_s[j], vg_full[j], vb_full[j], name="gmlp_fwd")
            qcol = 2 * mix_w
            extra = ()
        cat = memory_attention(proj, qcol, kv, cat, name=f"mem_attn_{i % 2}")
        (w_o_t,) = gathered(weight_jobs[i][1])
        y1 = matmul(cat, w_o_t, "nt", name="out_proj", comm=comm)
        r1, x1, x1b = ln_residual(xf, y1, ln1_g[i], ln1_b[i], alpha, name="ln1")
        (w_1_t,) = gathered(weight_jobs[i][2])
        h, hid = matmul(x1b, w_1_t, "nt", name="ff1", out_dtypes=(F32, BF16), comm=comm,
                        epilogue=lambda acc: (acc, jnp.square(jnp.maximum(acc, 0.0))))
        (w_2,) = gathered(weight_jobs[i][3])
        y2 = matmul(hid, w_2, "nn", name="ff2", comm=comm)
        r2, x2, x2b = ln_residual(x1, y2, ln2_g[i], ln2_b[i], alpha, name="ln2")
        weights.append((w_in_t, w_o_t, w_1_t, w_2, w_kv))
        saved.append((xb, proj, kv, extra, qcol, cat, r1, x1b, h, hid, r2))
        xf, xb = x2, x2b

    loss_tile, dx = loss_head(xf, target, name="loss_head")
    loss = lax.psum(loss_tile[0, 0], ("x", "y", "c"))

    def exchange(dw, tag):
        rows, cols = dw.shape[0] // N_DEV, dw.shape[1]
        p = dw.reshape(N_DEV, rows, cols)
        n_chunks = 1
        while ExchangeJob.US_PER_MB * dw.size * 2 / 1e6 > 110.0 * n_chunks and rows % (32 * n_chunks) == 0:
            n_chunks *= 2
        step = rows // n_chunks
        return [comm.submit(ExchangeJob(p, q * step, step, f"x{tag}{q}")) for q in range(n_chunks)]

    recv = {n: [None] * depth for n in ("w_in", "w_mem_kv", "w_out", "w_ff1", "w_ff2")}
    small = {n: [None] * depth for n in ("ln1_g", "ln1_b", "ln2_g", "ln2_b")}
    small_b = {n: [None] * n_b for n in ("w_s", "b_s", "vnorm_g", "vnorm_b")}
    for i in reversed(range(depth)):
        w_in_t, w_o_t, w_1_t, w_2, w_kv = weights[i]
        xb, proj, kv, extra, qcol, cat, r1, x1b, h, hid, r2 = saved[i]
        j = i // 2
        dr2, dr2b, small["ln2_g"][i], small["ln2_b"][i] = ln_backward(r2, ln2_g[i], dx, name="ln2_bwd")
        dh = matmul(dr2b, w_2, "nt", name="ff2_dx", extras=(h,), out_dtypes=(BF16,), comm=comm,
                    epilogue=lambda acc, h_t: (acc * (2.0 * jnp.maximum(h_t, 0.0)),))
        recv["w_ff2"][i] = exchange(matmul(hid, dr2b, "tn", name="ff2_dw", out_dtypes=(BF16,), comm=comm), f"ff2{i}")
        dx1 = matmul(dh, w_1_t, "nn", name="ff1_dx", extras=(dr2,), comm=comm, epilogue=lambda acc, res: (acc + alpha * res,))
        recv["w_ff1"][i] = exchange(matmul(dh, x1b, "tn", name="ff1_dw", out_dtypes=(BF16,), comm=comm), f"ff1{i}")
        dr1, dr1b, small["ln1_g"][i], small["ln1_b"][i] = ln_backward(r1, ln1_g[i], dx1, name="ln1_bwd")
        dcat = matmul(dr1b, w_o_t, "nn", name="out_proj_dx", comm=comm)
        recv["w_out"][i] = exchange(matmul(dr1b, cat, "tn", name="out_proj_dw", out_dtypes=(BF16,), comm=comm), f"out{i}")
        dq_mem, dkv = memory_attention_backward(proj, qcol, kv, dcat, mix_w, name=f"mem_attn_bwd_{i % 2}")
        recv["w_mem_kv"][i] = exchange(matmul(mem_b, dkv, "tn", name="mem_kv_dw", out_dtypes=(BF16,)), f"kv{i}")
        if i % 2 == 0:
            o, lse = extra
            pieces = []
            for g, d in enumerate(A_DILATIONS):
                pieces += [t.astype(BF16) for t in attention_backward(proj, o, lse, dcat, g, d, groups, name=f"attn_bwd_d{d}", comm=comm)]
            dproj = jnp.concatenate(pieces + [dq_mem], axis=1)
        else:
            dpu, dpv, small_b["w_s"][j], small_b["b_s"][j], small_b["vnorm_g"][j], small_b["vnorm_b"][j] = gmlp_backward(
                proj, w_s[j], b_s[j], vg_full[j], vb_full[j], dcat, name="gmlp_bwd")
            dproj = jnp.concatenate([dpu, dpv, dq_mem], axis=1)
        recv["w_in"][i] = exchange(matmul(dproj, xb, "tn", name=f"in_proj_dw_{i % 2}", out_dtypes=(BF16,), comm=comm), f"in{i}")
        dx = matmul(dproj, w_in_t, "nn", name=f"in_proj_dx_{i % 2}", extras=(dr1,), comm=comm, epilogue=lambda acc, res: (acc + alpha * res,))
    grad_x = dx[None]

    flat_small = jnp.concatenate(
        [jnp.concatenate(small[n], axis=0).reshape(-1) for n in ("ln1_g", "ln1_b", "ln2_g", "ln2_b")]
        + [jnp.stack(small_b[n]).reshape(-1) for n in ("w_s", "b_s", "vnorm_g", "vnorm_b")])
    pad = (-flat_small.size) % 1024
    small_job = comm.submit(GatherJob([jnp.pad(flat_small, (0, pad)).reshape(-1, 1024)], "gsmall"))

    results = {}

    def received(name, i):
        return [comm.require(job)[0] for job in recv[name][i]]

    def transposed(name, layers, w_, m_, v_):
        grad = jnp.stack([sum_parts(received(name, i), name=f"sum_{name}_{i % 2}").T for i in layers])
        flat = lambda t: t.reshape(-1, t.shape[-1])
        upd = adamw(flat(grad)[None], flat(w_), flat(m_), flat(v_), name=f"adamw_{name}_{layers[0] % 2}", emit_grad=False, comm=comm)
        return [grad] + [t.reshape(w_.shape) for t in upd]

    results["w_ff2"] = adamw_layers([received("w_ff2", i) for i in range(depth)], w_ff2, m_w_ff2, v_w_ff2, name="adamw_w_ff2", comm=comm)
    results["w_ff1"] = transposed("w_ff1", list(range(depth)), w_ff1, m_w_ff1, v_w_ff1)
    results["w_out"] = transposed("w_out", list(range(depth)), w_out, m_w_out, v_w_out)
    results["w_mem_kv"] = adamw_layers([received("w_mem_kv", i) for i in range(depth)], w_mem_kv, m_w_mem_kv, v_w_mem_kv, name="adamw_w_mem_kv")
    results["w_in_b"] = transposed("w_in", list(range(1, depth, 2)), w_in_b, m_w_in_b, v_w_in_b)
    results["w_in_a"] = transposed("w_in", list(range(0, depth, 2)), w_in_a, m_w_in_a, v_w_in_a)

    gathered_small = comm.require(small_job)[0].reshape(N_DEV, -1)
    at = 0

    def take(shape):
        nonlocal at
        size = math.prod(shape)
        at += size
        return gathered_small[:, at - size:at].reshape((N_DEV,) + shape)

    for n, w_, m_, v_ in (("ln1_g", ln1_g, m_ln1_g, v_ln1_g), ("ln1_b", ln1_b, m_ln1_b, v_ln1_b),
                          ("ln2_g", ln2_g, m_ln2_g, v_ln2_g), ("ln2_b", ln2_b, m_ln2_b, v_ln2_b)):
        results[n] = adamw(take(w_.shape), w_, m_, v_, name="adamw_ln")
    lanes = lambda t: t.reshape(-1, 128)
    results["w_s"] = [t.reshape(w_s.shape) for t in adamw(take((w_s.size // 128, 128)), lanes(w_s), lanes(m_w_s), lanes(v_w_s), name="adamw_w_s")]
    results["b_s"] = [t.reshape(b_s.shape) for t in adamw(take((b_s.size // 128, 128)), lanes(b_s), lanes(m_b_s), lanes(v_b_s), name="adamw_b_s")]
    for n, w_, m_, v_ in (("vnorm_g", vnorm_g, m_vnorm_g, v_vnorm_g), ("vnorm_b", vnorm_b, m_vnorm_b, v_vnorm_b)):
        parts = lax.dynamic_slice_in_dim(take((n_b, mix_w)), me * w_.shape[1], w_.shape[1], axis=2)
        results[n] = adamw(parts, w_, m_, v_, name="adamw_vnorm")

    order =("w_in_a", "w_in_b", "w_s", "b_s", "vnorm_g", "vnorm_b", "w_mem_kv", "w_out", "ln1_g", "ln1_b", "w_ff1", "w_ff2", "ln2_g", "ln2_b")
    return (loss, grad_x, *[results[n][0] for n in order], *[results[n][1] for n in order],
            *[results[n][2] for n in order], *[results[n][3] for n in order])
```

```python
import math

import jax
import jax.numpy as jnp
from jax import lax
from jax.experimental import pallas as pl
from jax.experimental.pallas import tpu as pltpu

F32 = jnp.float32
BF16 = jnp.bfloat16
N_DEV = 8
HEAD_DIM = 128
MEM_HEADS = 4
MEM_WIDTH = MEM_HEADS * HEAD_DIM
A_DILATIONS = (1, 4, 16)
LN_EPS = 1e-5
ADAM_LR, ADAM_B1, ADAM_B2, ADAM_EPS, ADAM_WD, ADAM_STEP = 0.001, 0.9, 0.999, 1e-08, 0.01, 10
VMEM_LIMIT_BYTES = 56 * 1024 * 1024
MESH = pl.DeviceIdType.MESH
ANY = pl.BlockSpec(memory_space=pl.ANY)


def _params(*sem):
    return pltpu.CompilerParams(dimension_semantics=sem, vmem_limit_bytes=VMEM_LIMIT_BYTES)


def _tile(dim, pref):
    if dim <= pref:
        return dim
    best = None
    for t in range(128, pref + 1, 128):
        if dim % t == 0:
            best = t
    return best if best is not None else dim


def _dot(a, b, dims):
    return lax.dot_general(a, b, (dims, ((), ())), preferred_element_type=F32)


NN = ((1,), (0,))
NT = ((1,), (1,))
TN = ((0,), (0,))


def _my_position():
    return lax.axis_index("x"), lax.axis_index("y"), lax.axis_index("c")


class GatherJob:
    US_PER_MB = 46.0
    OVERSHOOT = 1.4

    def __init__(self, arrays, tag):
        self.arrays, self.tag, self.outs = list(arrays), tag, None
        self.out_shapes = [jax.ShapeDtypeStruct((N_DEV,) + v.shape, v.dtype) for v in self.arrays]
        self.n_sems, self.n_local = 7 * len(self.arrays), len(self.arrays)
        self.est_us = self.US_PER_MB * sum(v.size * v.dtype.itemsize for v in self.arrays) / 1e6

    def phases(self, v_refs, out_refs, send_sems, recv_sems, local_sems, sem0, loc0):
        n = len(self.arrays)
        x, y, c = _my_position()
        me, sibling = (x, y, c), (x, y, 1 - c)
        chips = [(1 - x, y), (x, 1 - y), (1 - x, 1 - y)]

        def copy(a, k, block, to, from_input=False):
            px, py, pc = block
            slot = out_refs[a].at[4 * px + 2 * py + pc]
            return pltpu.make_async_remote_copy(
                src_ref=v_refs[a] if from_input else slot, dst_ref=slot, send_sem=send_sems.at[sem0 + 7 * a + k],
                recv_sem=recv_sems.at[sem0 + 7 * a + k], device_id=to, device_id_type=MESH)

        def mine(a):
            return pltpu.make_async_copy(v_refs[a], out_refs[a].at[4 * x + 2 * y + c], local_sems.at[loc0 + a])

        def first(a):
            return [copy(a, 0, me, sibling, True)] + [copy(a, 1 + j, me, (*chip, c), True) for j, chip in enumerate(chips)]

        def start():
            for a in range(n):
                mine(a).start()
                for cp in first(a):
                    cp.start()

        def middle():
            for j, chip in enumerate(chips):
                for a in range(n):
                    copy(a, 1 + j, (*chip, c), me).wait_recv()
                    copy(a, 4 + j, (*chip, c), sibling).start()

        def finish():
            for a in range(n):
                copy(a, 0, sibling, me).wait_recv()
                for j, chip in enumerate(chips):
                    copy(a, 4 + j, (*chip, 1 - c), me).wait_recv()
                for cp in first(a) + [copy(a, 4 + j, (*chip, c), sibling) for j, chip in enumerate(chips)]:
                    cp.wait_send()
                mine(a).wait()

        return start, middle, finish


class ExchangeJob:
    US_PER_MB = 11.0
    OVERSHOOT = 1.05
    RELATIONS = [(dx, dy, dc) for dx in (0, 1) for dy in (0, 1) for dc in (0, 1)][1:]

    def __init__(self, p, row0, rows, tag):
        self.arrays, self.tag, self.outs = [p], tag, None
        self.row0, self.rows = row0, rows
        self.out_shapes = [jax.ShapeDtypeStruct((N_DEV, rows, p.shape[2]), p.dtype)]
        self.n_sems, self.n_local = 7, 1
        self.est_us = self.US_PER_MB * N_DEV * rows * p.shape[2] * p.dtype.itemsize / 1e6

    def phases(self, p_refs, out_refs, send_sems, recv_sems, local_sems, sem0, loc0):
        (p_ref,), (out_ref,) = p_refs, out_refs
        x, y, c = _my_position()
        me = 4 * x + 2 * y + c
        chunk = pl.ds(self.row0, self.rows)

        def mine():
            return pltpu.make_async_copy(p_ref.at[me, chunk], out_ref.at[me], local_sems.at[loc0])

        def copies(arriving):
            made = []
            for k, (dx, dy, dc) in enumerate(self.RELATIONS):
                px, py, pc = (x + dx) % 2, (y + dy) % 2, (c + dc) % 2
                peer = 4 * px + 2 * py + pc
                made.append(pltpu.make_async_remote_copy(
                    src_ref=p_ref.at[peer, chunk], dst_ref=out_ref.at[peer if arriving else me],
                    send_sem=send_sems.at[sem0 + k], recv_sem=recv_sems.at[sem0 + k],
                    device_id=(px, py, pc), device_id_type=MESH))
            return made

        def start():
            mine().start()
            for send in copies(False):
                send.start()

        def finish():
            for arrival in copies(True):
                arrival.wait_recv()
            for send in copies(False):
                send.wait_send()
            mine().wait()

        return start, (lambda: None), finish


class Exchanges:
    MIN_HOST_US = 80.0

    def __init__(self):
        self.queue = []

    def submit(self, job):
        self.queue.append(job)
        return job

    def take(self, host_us):
        jobs, used = [], 0.0
        if host_us >= self.MIN_HOST_US:
            for job in list(self.queue):
                if used + job.est_us <= job.OVERSHOOT * host_us:
                    used += job.est_us
                    jobs.append(job)
                    self.queue.remove(job)
        return jobs

    def require(self, job):
        if job.outs is None:
            self.queue.remove(job)
            _call(lambda: None, grid=(), in_specs=[], out_specs=[], out_shape=[], scratch_shapes=[], args=[],
                  name="exchange", semantics=(), jobs=[job])
        return job.outs


def _call(body, *, grid, in_specs, out_specs, out_shape, scratch_shapes, args, name, semantics, jobs=()):
    if not jobs:
        return pl.pallas_call(body, grid=grid, in_specs=in_specs, out_specs=out_specs, out_shape=out_shape,
                              scratch_shapes=scratch_shapes, compiler_params=_params(*semantics), name=name)(*args)
    n_in, n_out, n_scr = len(in_specs), len(out_shape), len(scratch_shapes)
    j_in = [a for job in jobs for a in job.arrays]
    j_out = [s for job in jobs for s in job.out_shapes]
    n_sems, n_local = sum(job.n_sems for job in jobs), sum(job.n_local for job in jobs)
    steps = math.prod(grid)
    middle_step = (7 * steps) // 8

    def wrapped(*refs):
        ins, refs = refs[:n_in], refs[n_in:]
        j_ins, refs = refs[:len(j_in)], refs[len(j_in):]
        outs, refs = refs[:n_out], refs[n_out:]
        j_outs, refs = refs[:len(j_out)], refs[len(j_out):]
        scratch, (send_sems, recv_sems, local_sems) = refs[:n_scr], refs[n_scr:]
        step = 0
        for axis, extent in enumerate(grid):
            step = step * extent + pl.program_id(axis)
        phases, at_in, at_out, sem0, loc0 = [], 0, 0, 0, 0
        for job in jobs:
            k_in, k_out = len(job.arrays), len(job.out_shapes)
            phases.append(job.phases(j_ins[at_in:at_in + k_in], j_outs[at_out:at_out + k_out], send_sems, recv_sems, local_sems, sem0, loc0))
            at_in, at_out, sem0, loc0 = at_in + k_in, at_out + k_out, sem0 + job.n_sems, loc0 + job.n_local

        def run(which):
            for ph in phases:
                ph[which]()

        if steps == 1:
            run(0)
            body(*ins, *outs, *scratch)
            run(1)
            run(2)
        else:
            pl.when(step == 0)(lambda: run(0))
            body(*ins, *outs, *scratch)
            pl.when(step == middle_step)(lambda: run(1))
            pl.when(step == steps - 1)(lambda: run(2))

    res = pl.pallas_call(
        wrapped, grid=grid, in_specs=list(in_specs) + [ANY] * len(j_in), out_specs=list(out_specs) + [ANY] * len(j_out),
        out_shape=list(out_shape) + j_out,
        scratch_shapes=list(scratch_shapes) + [pltpu.SemaphoreType.DMA((n_sems,)), pltpu.SemaphoreType.DMA((n_sems,)),
                                               pltpu.SemaphoreType.DMA((n_local,))],
        compiler_params=_params(*(["arbitrary"] * len(grid))), name=name + "".join("__" + job.tag for job in jobs),
    )(*args, *j_in)
    at = n_out
    for job in jobs:
        job.outs = list(res[at:at + len(job.out_shapes)])
        at += len(job.out_shapes)
    return list(res[:n_out])


MATMUL_TILES = {"nn": (1024, 512, 2560), "nt": (1024, 512, 2560), "tn": (1024, 1024, 2048)}
MATMUL_FLOPS_PER_US = {"nn": 7.5e8, "nt": 7.5e8, "tn": 6.2e8}


def matmul(a, b, mode, name, epilogue=None, extras=(), out_dtypes=(F32,), comm=None):
    if mode == "nn":
        (m, k), (k2, n) = a.shape, b.shape
    elif mode == "nt":
        (m, k), (n, k2) = a.shape, b.shape
    else:
        (k, m), (k2, n) = a.shape, b.shape
    assert k == k2, (a.shape, b.shape, mode)
    tm_pref, tn_pref, tk_pref = MATMUL_TILES[mode]
    tm = _tile(m, tm_pref)
    if mode == "tn" and tm < tm_pref:
        tn_pref *= 2
    tn, tk = _tile(n, tn_pref), _tile(k, tk_pref)
    nk = k // tk
    dims = {"nn": NN, "nt": NT, "tn": TN}[mode]
    a_spec = pl.BlockSpec((tk, tm), lambda i, j, kk: (kk, i)) if mode == "tn" else pl.BlockSpec((tm, tk), lambda i, j, kk: (i, kk))
    b_spec = pl.BlockSpec((tn, tk), lambda i, j, kk: (j, kk)) if mode == "nt" else pl.BlockSpec((tk, tn), lambda i, j, kk: (kk, j))
    tile_spec = pl.BlockSpec((tm, tn), lambda i, j, kk: (i, j))
    n_ex, n_out = len(extras), len(out_dtypes)
    if epilogue is None:
        epilogue = lambda acc: (acc,) * n_out

    def body(a_ref, b_ref, *rest):
        ex_refs, out_refs, acc_ref = rest[:n_ex], rest[n_ex:n_ex + n_out], rest[-1]
        kk = pl.program_id(2)

        def product():
            return _dot(a_ref[...].astype(BF16), b_ref[...].astype(BF16), dims)

        def finish(acc):
            for o_ref, val in zip(out_refs, epilogue(acc, *[e[...] for e in ex_refs])):
                o_ref[...] = val.astype(o_ref.dtype)

        if nk == 1:
            finish(product())
            return

        @pl.when(kk == 0)
        def _():
            acc_ref[...] = product()

        @pl.when((kk > 0) & (kk < nk - 1))
        def _():
            acc_ref[...] += product()

        @pl.when(kk == nk - 1)
        def _():
            finish(acc_ref[...] + product())

    host_us = 2.0 * m * n * k / MATMUL_FLOPS_PER_US[mode]
    outs = _call(
        body,
        grid=(m // tm, n // tn, nk),
        in_specs=[a_spec, b_spec] + [tile_spec] * n_ex,
        out_specs=[tile_spec] * n_out,
        out_shape=[jax.ShapeDtypeStruct((m, n), dt) for dt in out_dtypes],
        scratch_shapes=[pltpu.VMEM((tm, tn), F32)],
        args=[a, b, *extras], name=name, semantics=("parallel", "parallel", "arbitrary"),
        jobs=comm.take(host_us) if comm is not None else (),
    )
    return outs[0] if n_out == 1 else outs


def ln_residual(x, y, g, b, alpha, name):
    s, w = x.shape
    tb = _tile(s, 256)
    row = pl.BlockSpec((tb, w), lambda i: (i, 0))
    vec = pl.BlockSpec((1, w), lambda i: (0, 0))

    def body(x_ref, y_ref, g_ref, b_ref, r_ref, xn_ref, xnb_ref):
        r = alpha * x_ref[...] + y_ref[...]
        mu = jnp.mean(r, axis=-1, keepdims=True)
        var = jnp.mean(jnp.square(r - mu), axis=-1, keepdims=True)
        xn = (r - mu) * lax.rsqrt(var + LN_EPS) * g_ref[...] + b_ref[...]
        r_ref[...] = r
        xn_ref[...] = xn
        xnb_ref[...] = xn.astype(BF16)

    return pl.pallas_call(
        body, grid=(s // tb,), in_specs=[row, row, vec, vec], out_specs=[row, row, row],
        out_shape=[jax.ShapeDtypeStruct((s, w), F32), jax.ShapeDtypeStruct((s, w), F32), jax.ShapeDtypeStruct((s, w), BF16)],
        compiler_params=_params("parallel"), name=name,
    )(x, y, g.reshape(1, w), b.reshape(1, w))


def _ln_bwd_tile(r, g, dxn):
    mu = jnp.mean(r, axis=-1, keepdims=True)
    cen = r - mu
    rstd = lax.rsqrt(jnp.mean(jnp.square(cen), axis=-1, keepdims=True) + LN_EPS)
    xhat = cen * rstd
    dxh = dxn * g
    dr = rstd * (dxh - jnp.mean(dxh, axis=-1, keepdims=True) - xhat * jnp.mean(dxh * xhat, axis=-1, keepdims=True))
    return dr, jnp.sum(dxn * xhat, axis=0, keepdims=True)


def ln_backward(r, g, dxn, name):
    s, w = r.shape
    tb = _tile(s, 256)
    row = pl.BlockSpec((tb, w), lambda i: (i, 0))
    vec = pl.BlockSpec((1, w), lambda i: (0, 0))

    def body(r_ref, g_ref, d_ref, dr_ref, drb_ref, dg_ref, db_ref):
        @pl.when(pl.program_id(0) == 0)
        def _():
            dg_ref[...] = jnp.zeros_like(dg_ref)
            db_ref[...] = jnp.zeros_like(db_ref)

        dxn = d_ref[...]
        dr, dg = _ln_bwd_tile(r_ref[...], g_ref[...], dxn)
        dr_ref[...] = dr
        drb_ref[...] = dr.astype(BF16)
        dg_ref[...] += dg
        db_ref[...] += jnp.sum(dxn, axis=0, keepdims=True)

    return pl.pallas_call(
        body, grid=(s // tb,), in_specs=[row, vec, row], out_specs=[row, row, vec, vec],
        out_shape=[jax.ShapeDtypeStruct((s, w), F32), jax.ShapeDtypeStruct((s, w), BF16),
                   jax.ShapeDtypeStruct((1, w), F32), jax.ShapeDtypeStruct((1, w), F32)],
        compiler_params=_params("arbitrary"), name=name,
    )(r, g.reshape(1, w), dxn)


ATTENTION_US = {("fwd", 1): 55.0, ("fwd", 4): 80.0, ("fwd", 16): 100.0, ("bwd", 1): 115.0, ("bwd", 4): 130.0, ("bwd", 16): 190.0}
ATTENTION_HEADS_PER_STEP = {("fwd", 1): 8, ("fwd", 4): 1, ("fwd", 16): 1, ("bwd", 1): 8, ("bwd", 4): 1, ("bwd", 16): 1}


def _causal_masks():
    qi = lax.broadcasted_iota(jnp.int32, (128, 128), 0)
    kj = lax.broadcasted_iota(jnp.int32, (128, 128), 1)
    return kj <= qi, kj >= qi


def _window_mask(has_previous):
    qi = lax.broadcasted_iota(jnp.int32, (128, 256), 0)
    kj = lax.broadcasted_iota(jnp.int32, (128, 256), 1)
    return ((kj < 128) & (kj >= qi) & has_previous) | ((kj >= 128) & (kj - 128 <= qi))


def _sub_rows(d, r):
    return pl.ds(r, 128, stride=d) if d > 1 else pl.ds(0, 128)


def _for_each_residue(d, fn):
    if d <= 4:
        for r in range(d):
            fn(r)
    else:
        def step(r, carry):
            fn(r)
            return carry
        lax.fori_loop(0, d, step, 0, unroll=4)


def attention_forward(proj, group, d, heads, name, comm=None):
    s = proj.shape[0]
    rows = 128 * d
    nb = s // rows
    scale = HEAD_DIM ** -0.5
    hp = math.gcd(heads, ATTENTION_HEADS_PER_STEP["fwd", d])
    wide = 128 * hp
    qc, kc, vc = (group * 3) * heads // hp, (group * 3 + 1) * heads // hp, (group * 3 + 2) * heads // hp

    def cur(col):
        return pl.BlockSpec((rows, wide), lambda h, n: (n, col + h))

    def prev(col):
        return pl.BlockSpec((rows, wide), lambda h, n: (jnp.maximum(n - 1, 0), col + h))

    out = pl.BlockSpec((rows, wide), lambda h, n: (n, h))

    def body(q_ref, kc_ref, kp_ref, vc_ref, vp_ref, o_ref, l_ref):
        mask = _window_mask(pl.program_id(1) > 0)

        def one(r):
            rws = _sub_rows(d, r)
            for hh in range(hp):
                sl = (rws, pl.ds(hh * 128, 128))
                k = jnp.concatenate([kp_ref[sl].astype(BF16), kc_ref[sl].astype(BF16)], axis=0)
                v = jnp.concatenate([vp_ref[sl].astype(BF16), vc_ref[sl].astype(BF16)], axis=0)
                sc = jnp.where(mask, _dot(q_ref[sl].astype(BF16), k, NT) * scale, -jnp.inf)
                m = jnp.max(sc, axis=-1, keepdims=True)
                e = jnp.exp(sc - m)
                l = jnp.sum(e, axis=-1, keepdims=True)
                o_ref[sl] = _dot(e.astype(BF16), v, NN) / l
                l_ref[sl] = jnp.broadcast_to(m + jnp.log(l), (128, 128))

        _for_each_residue(d, one)

    return _call(
        body, grid=(heads // hp, nb),
        in_specs=[cur(qc), cur(kc), prev(kc), cur(vc), prev(vc)], out_specs=[out, out],
        out_shape=[jax.ShapeDtypeStruct((s, heads * 128), F32)] * 2, scratch_shapes=[],
        args=[proj] * 5, name=name, semantics=("parallel", "parallel"),
        jobs=comm.take(ATTENTION_US["fwd", d] * s * heads / (4096 * 8)) if comm is not None else (),
    )


def attention_combine(os_, ls_, name):
    s, w = os_[0].shape
    tb = _tile(s, 256)
    row = pl.BlockSpec((tb, w), lambda i: (i, 0))

    def body(o0, o1, o2, l0, l1, l2, o_ref, ob_ref, lse_ref):
        a, b, c = l0[...], l1[...], l2[...]
        m = jnp.maximum(jnp.maximum(a, b), c)
        ea, eb, ec = jnp.exp(a - m), jnp.exp(b - m), jnp.exp(c - m)
        tot = ea + eb + ec
        o = (ea / tot) * o0[...] + (eb / tot) * o1[...] + (ec / tot) * o2[...]
        o_ref[...] = o
        ob_ref[...] = o.astype(BF16)
        lse_ref[...] = m + jnp.log(tot)

    return pl.pallas_call(
        body, grid=(s // tb,), in_specs=[row] * 6, out_specs=[row] * 3,
        out_shape=[jax.ShapeDtypeStruct((s, w), F32), jax.ShapeDtypeStruct((s, w + MEM_WIDTH), BF16), jax.ShapeDtypeStruct((s, w), F32)],
        compiler_params=_params("parallel"), name=name,
    )(*os_, *ls_)


def attention_backward(proj, o, lse, dcat, group, d, heads, name, comm=None):
    s = proj.shape[0]
    rows = 128 * d
    nb = s // rows
    scale = HEAD_DIM ** -0.5
    hp = math.gcd(heads, ATTENTION_HEADS_PER_STEP["bwd", d])
    wide = 128 * hp
    qc, kc, vc = (group * 3) * heads // hp, (group * 3 + 1) * heads // hp, (group * 3 + 2) * heads // hp

    def at(col, shift):
        return pl.BlockSpec((rows, wide), lambda h, n: (jnp.clip(n + shift, 0, nb - 1), col + h))

    def out(col):
        return pl.BlockSpec((rows, wide), lambda h, n: (n, col + h))

    def body(qc_ref, qn_ref, kc_ref, kp_ref, vc_ref, vp_ref, doc_ref, don_ref, oc_ref, on_ref, lc_ref, ln_ref,
             dq_ref, dk_ref, dv_ref):
        n = pl.program_id(1)
        mask_w = _window_mask(n > 0)
        mask_n = _causal_masks()[1] & (n < nb - 1)

        def tile(q, k, v, do, lse_t, dsum, mask):
            p = jnp.where(mask, jnp.exp(_dot(q, k, NT) * scale - lse_t), 0.0)
            ds = (p * (_dot(do, v, NT) - dsum) * scale).astype(BF16)
            return p.astype(BF16), ds

        def one(r):
            rws = _sub_rows(d, r)
            for hh in range(hp):
                sl = (rws, pl.ds(hh * 128, 128))
                q_c, q_n = qc_ref[sl].astype(BF16), qn_ref[sl].astype(BF16)
                k_c, v_c = kc_ref[sl].astype(BF16), vc_ref[sl].astype(BF16)
                k_w = jnp.concatenate([kp_ref[sl].astype(BF16), k_c], axis=0)
                v_w = jnp.concatenate([vp_ref[sl].astype(BF16), v_c], axis=0)
                do_c, do_n = doc_ref[sl], don_ref[sl]
                dsum_c = jnp.sum(do_c * oc_ref[sl], axis=-1, keepdims=True)
                dsum_n = jnp.sum(do_n * on_ref[sl], axis=-1, keepdims=True)
                do_c, do_n = do_c.astype(BF16), do_n.astype(BF16)
                lse_c = lc_ref[sl]
                p_w, ds_w = tile(q_c, k_w, v_w, do_c, jnp.concatenate([lse_c, lse_c], axis=1), dsum_c, mask_w)
                p_n, ds_n = tile(q_n, k_c, v_c, do_n, ln_ref[sl], dsum_n, mask_n)
                dq_ref[sl] = _dot(ds_w, k_w, NN)
                both_q = jnp.concatenate([q_c, q_n], axis=0)
                both_do = jnp.concatenate([do_c, do_n], axis=0)
                dk_ref[sl] = _dot(jnp.concatenate([ds_w[:, 128:], ds_n], axis=0), both_q, TN)
                dv_ref[sl] = _dot(jnp.concatenate([p_w[:, 128:], p_n], axis=0), both_do, TN)

        _for_each_residue(d, one)

    w = heads * 128
    return _call(
        body, grid=(heads // hp, nb),
        in_specs=[at(qc, 0), at(qc, 1), at(kc, 0), at(kc, -1), at(vc, 0), at(vc, -1),
                  at(0, 0), at(0, 1), at(0, 0), at(0, 1), at(0, 0), at(0, 1)],
        out_specs=[out(0)] * 3,
        out_shape=[jax.ShapeDtypeStruct((s, w), F32)] * 3, scratch_shapes=[],
        args=[proj] * 6 + [dcat, dcat, o, o, lse, lse], name=name, semantics=("parallel", "parallel"),
        jobs=comm.take(ATTENTION_US["bwd", d] * s * heads / (4096 * 8)) if comm is not None else (),
    )


def _mem_softmax(q, kv, h, scale):
    k = kv[:, h * 128:(h + 1) * 128].astype(BF16)
    v = kv[:, MEM_WIDTH + h * 128:MEM_WIDTH + (h + 1) * 128].astype(BF16)
    sc = _dot(q, k, NT) * scale
    e = jnp.exp(sc - jnp.max(sc, axis=-1, keepdims=True))
    return e / jnp.sum(e, axis=-1, keepdims=True), k, v


def memory_attention(proj, qcol, kv, cat, name):
    s = proj.shape[0]
    tb = _tile(s, 512)
    scale = HEAD_DIM ** -0.5

    def body(q_ref, kv_ref, cat_ref, o_ref):
        kv_t = kv_ref[...]
        for h in range(MEM_HEADS):
            p, _, v = _mem_softmax(q_ref[:, h * 128:(h + 1) * 128].astype(BF16), kv_t, h, scale)
            o_ref[:, h * 128:(h + 1) * 128] = _dot(p.astype(BF16), v, NN).astype(BF16)

    return pl.pallas_call(
        body, grid=(s // tb,),
        in_specs=[pl.BlockSpec((tb, MEM_WIDTH), lambda i: (i, qcol // MEM_WIDTH)), pl.BlockSpec(kv.shape, lambda i: (0, 0)), ANY],
        out_specs=pl.BlockSpec((tb, MEM_WIDTH), lambda i: (i, cat.shape[1] // MEM_WIDTH - 1)),
        out_shape=jax.ShapeDtypeStruct(cat.shape, BF16), input_output_aliases={2: 0},
        compiler_params=_params("parallel"), name=name,
    )(proj, kv, cat)


def memory_attention_backward(proj, qcol, kv, dcat, dcol, name):
    s = proj.shape[0]
    tb = _tile(s, 512)
    scale = HEAD_DIM ** -0.5

    def body(q_ref, kv_ref, do_ref, dq_ref, dkv_ref):
        @pl.when(pl.program_id(0) == 0)
        def _():
            dkv_ref[...] = jnp.zeros_like(dkv_ref)

        kv_t = kv_ref[...]
        for h in range(MEM_HEADS):
            cols = slice(h * 128, (h + 1) * 128)
            q = q_ref[:, cols].astype(BF16)
            p, k, v = _mem_softmax(q, kv_t, h, scale)
            do = do_ref[:, cols].astype(BF16)
            dp = _dot(do, v, NT)
            ds = (p * (dp - jnp.sum(dp * p, axis=-1, keepdims=True)) * scale).astype(BF16)
            dq_ref[:, cols] = _dot(ds, k, NN).astype(BF16)
            dkv_ref[:, cols] += _dot(ds, q, TN)
            dkv_ref[:, MEM_WIDTH + h * 128:MEM_WIDTH + (h + 1) * 128] += _dot(p.astype(BF16), do, TN)

    return pl.pallas_call(
        body, grid=(s // tb,),
        in_specs=[pl.BlockSpec((tb, MEM_WIDTH), lambda i: (i, qcol // MEM_WIDTH)), pl.BlockSpec(kv.shape, lambda i: (0, 0)),
                  pl.BlockSpec((tb, MEM_WIDTH), lambda i: (i, dcol // MEM_WIDTH))],
        out_specs=[pl.BlockSpec((tb, MEM_WIDTH), lambda i: (i, 0)), pl.BlockSpec(kv.shape, lambda i: (0, 0))],
        out_shape=[jax.ShapeDtypeStruct((s, MEM_WIDTH), BF16), jax.ShapeDtypeStruct(kv.shape, F32)],
        compiler_params=_params("arbitrary"), name=name,
    )(proj, kv, dcat)


_SQRT_HALF = math.sqrt(0.5)
_INV_SQRT_2PI = 1.0 / math.sqrt(2.0 * math.pi)


def _gelu(x):
    return 0.5 * x * (1.0 + lax.erf(x * _SQRT_HALF))


def _gelu_grad(x):
    return 0.5 * (1.0 + lax.erf(x * _SQRT_HALF)) + x * (_INV_SQRT_2PI * jnp.exp(-0.5 * x * x))


def _gmlp_specs(s, wd, groups, tb):
    half = lambda c: pl.BlockSpec((tb, wd), lambda i: (i, c))
    ws_spec = pl.BlockSpec((groups, 128, 128), lambda i: (0, 0, 0))
    bs_spec = pl.BlockSpec((groups, 128, 1), lambda i: (0, 0, 0))
    vec = pl.BlockSpec((1, wd), lambda i: (0, 0))
    return half, ws_spec, bs_spec, vec


def _vnorm(zv, g, b):
    mu = jnp.mean(zv, axis=-1, keepdims=True)
    var = jnp.mean(jnp.square(zv - mu), axis=-1, keepdims=True)
    return (zv - mu) * lax.rsqrt(var + LN_EPS) * g + b


def gmlp_forward(proj, ws, bs, vg, vb, name):
    s = proj.shape[0]
    groups = ws.shape[0]
    wd = groups * 128
    tb = _tile(s, 512)
    half, ws_spec, bs_spec, vec = _gmlp_specs(s, wd, groups, tb)

    def body(pu_ref, pv_ref, ws_ref, bs_ref, vg_ref, vb_ref, o_ref, vn_ref):
        causal, _ = _causal_masks()
        vn_ref[...] = _vnorm(_gelu(pv_ref[...]), vg_ref[...], vb_ref[...]).astype(BF16)
        for g in range(groups):
            cols = slice(g * 128, (g + 1) * 128)
            wm = jnp.where(causal, ws_ref[g], 0.0).astype(BF16)
            for c in range(tb // 128):
                rws = slice(c * 128, (c + 1) * 128)
                sg = _dot(wm, vn_ref[rws, cols], NN) + bs_ref[g]
                o_ref[rws, cols] = (_gelu(pu_ref[rws, cols]) * sg).astype(BF16)

    return pl.pallas_call(
        body, grid=(s // tb,),
        in_specs=[half(0), half(1), ws_spec, bs_spec, vec, vec], out_specs=pl.BlockSpec((tb, wd), lambda i: (i, 0)),
        out_shape=jax.ShapeDtypeStruct((s, wd + MEM_WIDTH), BF16),
        scratch_shapes=[pltpu.VMEM((tb, wd), BF16)],
        compiler_params=_params("parallel"), name=name,
    )(proj, proj, ws, bs.reshape(groups, 128, 1), vg.reshape(1, wd), vb.reshape(1, wd))


def gmlp_backward(proj, ws, bs, vg, vb, dcat, name):
    s = proj.shape[0]
    groups = ws.shape[0]
    wd = groups * 128
    tb = _tile(s, 512)
    half, ws_spec, bs_spec, vec = _gmlp_specs(s, wd, groups, tb)

    def body(pu_ref, pv_ref, ws_ref, bs_ref, vg_ref, vb_ref, do_ref, dpu_ref, dpv_ref, dws_ref, dbs_ref, dvg_ref, dvb_ref,
             vn_ref, dvn_ref):
        @pl.when(pl.program_id(0) == 0)
        def _():
            dws_ref[...] = jnp.zeros_like(dws_ref)
            dbs_ref[...] = jnp.zeros_like(dbs_ref)
            dvg_ref[...] = jnp.zeros_like(dvg_ref)
            dvb_ref[...] = jnp.zeros_like(dvb_ref)

        causal, _ = _causal_masks()
        pv = pv_ref[...]
        zv = _gelu(pv)
        vn_ref[...] = _vnorm(zv, vg_ref[...], vb_ref[...]).astype(BF16)
        for g in range(groups):
            cols = slice(g * 128, (g + 1) * 128)
            wm = jnp.where(causal, ws_ref[g], 0.0).astype(BF16)
            dws_g = jnp.zeros((128, 128), F32)
            dbs_g = jnp.zeros((128, 1), F32)
            for c in range(tb // 128):
                rws = slice(c * 128, (c + 1) * 128)
                vn = vn_ref[rws, cols]
                pu = pu_ref[rws, cols]
                do = do_ref[rws, cols]
                sg = _dot(wm, vn, NN) + bs_ref[g]
                dpu_ref[rws, cols] = (do * sg * _gelu_grad(pu)).astype(BF16)
                dsg = do * _gelu(pu)
                dsg_b = dsg.astype(BF16)
                dws_g += _dot(dsg_b, vn, NT)
                dbs_g += jnp.sum(dsg, axis=-1, keepdims=True)
                dvn_ref[rws, cols] = _dot(wm, dsg_b, TN)
            dws_ref[g] += jnp.where(causal, dws_g, 0.0)
            dbs_ref[g] += dbs_g
        dvn = dvn_ref[...]
        dzv, dvg = _ln_bwd_tile(zv, vg_ref[...], dvn)
        dvg_ref[...] += dvg
        dvb_ref[...] += jnp.sum(dvn, axis=0, keepdims=True)
        dpv_ref[...] = (dzv * _gelu_grad(pv)).astype(BF16)

    row = pl.BlockSpec((tb, wd), lambda i: (i, 0))
    dpu, dpv, dws, dbs, dvg, dvb = pl.pallas_call(
        body, grid=(s // tb,),
        in_specs=[half(0), half(1), ws_spec, bs_spec, vec, vec, row],
        out_specs=[row, row, ws_spec, bs_spec, vec, vec],
        out_shape=[jax.ShapeDtypeStruct((s, wd), BF16), jax.ShapeDtypeStruct((s, wd), BF16),
                   jax.ShapeDtypeStruct((groups, 128, 128), F32), jax.ShapeDtypeStruct((groups, 128, 1), F32),
                   jax.ShapeDtypeStruct((1, wd), F32), jax.ShapeDtypeStruct((1, wd), F32)],
        scratch_shapes=[pltpu.VMEM((tb, wd), BF16), pltpu.VMEM((tb, wd), F32)],
        compiler_params=_params("arbitrary"), name=name,
    )(proj, proj, ws, bs.reshape(groups, 128, 1), vg.reshape(1, wd), vb.reshape(1, wd), dcat)
    return dpu, dpv, dws, dbs.reshape(groups, 128), dvg, dvb


def loss_head(y, target, name):
    s, w = y.shape
    tb = _tile(s, 256)
    row = pl.BlockSpec((tb, w), lambda i: (i, 0))
    nsteps = s // tb

    def body(y_ref, t_ref, loss_ref, dy_ref, acc_ref):
        i = pl.program_id(0)

        @pl.when(i == 0)
        def _():
            acc_ref[...] = jnp.zeros_like(acc_ref)

        err = y_ref[...] - t_ref[...]
        dy_ref[...] = err / w
        acc_ref[...] += jnp.sum(jnp.mean(jnp.square(err), axis=-1, keepdims=True), axis=0, keepdims=True)

        @pl.when(i == nsteps - 1)
        def _():
            loss_ref[...] = jnp.broadcast_to(0.5 * acc_ref[...], loss_ref.shape)

    return pl.pallas_call(
        body, grid=(nsteps,), in_specs=[row, row],
        out_specs=[pl.BlockSpec((8, 128), lambda i: (0, 0)), row],
        out_shape=[jax.ShapeDtypeStruct((8, 128), F32), jax.ShapeDtypeStruct((s, w), F32)],
        scratch_shapes=[pltpu.VMEM((1, 1), F32)],
        compiler_params=_params("arbitrary"), name=name,
    )(y, target)


PARTS_WINDOW_ELEMS = 1024 * 1024


def _row_tile(r, c, budget):
    if r * c <= budget or r % 16:
        return r
    fits = [t for t in range(16, r, 16) if r % t == 0 and t * c <= budget]
    return max(fits) if fits else 16


def _sum_in_device_order(p_ref):
    g = p_ref[0].astype(F32)
    for j in range(1, p_ref.shape[0]):
        g = g + p_ref[j].astype(F32)
    return g


def _adamw_update(g, w, m, v):
    nm = ADAM_B1 * m + (1.0 - ADAM_B1) * g
    nv = ADAM_B2 * v + (1.0 - ADAM_B2) * jnp.square(g)
    m_hat = nm / (1.0 - ADAM_B1 ** ADAM_STEP)
    v_hat = nv / (1.0 - ADAM_B2 ** ADAM_STEP)
    return -ADAM_LR * (m_hat / (jnp.sqrt(v_hat) + ADAM_EPS) + ADAM_WD * w), nm, nv


def sum_parts(chunks, name):
    p, r, c = chunks[0].shape
    tr = _row_tile(r, c, PARTS_WINDOW_ELEMS // len(chunks))
    nb = r // tr

    def chunk_spec(q):
        return pl.BlockSpec((p, tr, c), lambda ch, i: (0, jnp.where(ch == q, i, 0), 0))

    def body(*refs):
        for q in range(len(chunks)):
            @pl.when(pl.program_id(0) == q)
            def _():
                refs[-1][...] = _sum_in_device_order(refs[q])

    return pl.pallas_call(
        body, grid=(len(chunks), nb), in_specs=[chunk_spec(q) for q in range(len(chunks))],
        out_specs=pl.BlockSpec((tr, c), lambda ch, i: (ch * nb + i, 0)),
        out_shape=jax.ShapeDtypeStruct((len(chunks) * r, c), F32),
        compiler_params=_params("arbitrary", "arbitrary"), name=name,
    )(*chunks)


ELEMENTWISE_BYTES_PER_US = 2.0e6


def adamw(parts, w, m, v, name, emit_grad=True, comm=None):
    p, r, c = parts.shape
    tr = _row_tile(r, c, 160 * 1024)
    n_out = 4 if emit_grad else 3

    def body(p_ref, w_ref, m_ref, v_ref, *out_refs):
        g = _sum_in_device_order(p_ref)
        vals = _adamw_update(g, w_ref[...], m_ref[...], v_ref[...])
        for o_ref, val in zip(out_refs, ((g,) + vals) if emit_grad else vals):
            o_ref[...] = val

    row = pl.BlockSpec((tr, c), lambda i: (i, 0))
    host_us = (parts.size * parts.dtype.itemsize + (3 + n_out) * 4 * r * c) / ELEMENTWISE_BYTES_PER_US
    return _call(
        body, grid=(r // tr,), in_specs=[pl.BlockSpec((p, tr, c), lambda i: (0, i, 0)), row, row, row],
        out_specs=[row] * n_out, out_shape=[jax.ShapeDtypeStruct((r, c), F32)] * n_out, scratch_shapes=[],
        args=[parts, w, m, v], name=name, semantics=("parallel",), jobs=comm.take(host_us) if comm is not None else (),
    )


def adamw_layers(chunks, w, m, v, name, comm=None):
    n_layers, r, c = w.shape
    n_chunks = len(chunks[0])
    p, rc, _ = chunks[0][0].shape
    tr = _row_tile(rc, c, PARTS_WINDOW_ELEMS // (n_layers * n_chunks))
    nb = rc // tr
    flat = [ch for layer in chunks for ch in layer]

    def chunk_spec(q):
        return pl.BlockSpec((p, tr, c), lambda l, ch, i: (0, jnp.where(l * n_chunks + ch == q, i, 0), 0))

    def body(*refs):
        p_refs, (w_ref, m_ref, v_ref), out_refs = refs[:len(flat)], refs[len(flat):len(flat) + 3], refs[len(flat) + 3:]
        for q in range(len(flat)):
            @pl.when(pl.program_id(0) * n_chunks + pl.program_id(1) == q)
            def _():
                g = _sum_in_device_order(p_refs[q])
                vals = _adamw_update(g, w_ref[...], m_ref[...], v_ref[...])
                for o_ref, val in zip(out_refs, (g,) + vals):
                    o_ref[...] = val

    state = pl.BlockSpec((None, tr, c), lambda l, ch, i: (l, ch * nb + i, 0))
    host_us = (len(flat) * flat[0].size * flat[0].dtype.itemsize + 7 * 4 * w.size) / ELEMENTWISE_BYTES_PER_US
    return _call(
        body, grid=(n_layers, n_chunks, nb), in_specs=[chunk_spec(q) for q in range(len(flat))] + [state] * 3,
        out_specs=[state] * 4, out_shape=[jax.ShapeDtypeStruct(w.shape, F32)] * 4, scratch_shapes=[],
        args=[*flat, w, m, v], name=name, semantics=("arbitrary", "arbitrary", "arbitrary"),
        jobs=comm.take(host_us) if comm is not None else (),
    )


def kernel(x, mem, w_in_a, w_in_b, w_s, b_s, vnorm_g, vnorm_b, w_mem_kv, w_out, ln1_g, ln1_b, w_ff1, w_ff2, ln2_g, ln2_b, loss_target, m_w_in_a, m_w_in_b, m_w_s, m_b_s, m_vnorm_g, m_vnorm_b, m_w_mem_kv, m_w_out, m_ln1_g, m_ln1_b, m_w_ff1, m_w_ff2, m_ln2_g, m_ln2_b, v_w_in_a, v_w_in_b, v_w_s, v_b_s, v_vnorm_g, v_vnorm_b, v_w_mem_kv, v_w_out, v_ln1_g, v_ln1_b, v_w_ff1, v_w_ff2, v_ln2_g, v_ln2_b):
    depth = w_ff1.shape[0]
    groups = w_s.shape[1]
    mix_w = groups * HEAD_DIM
    n_b = vnorm_g.shape[0]
    alpha = (2.0 * depth) ** 0.25
    me = 4 * lax.axis_index("x") + 2 * lax.axis_index("y") + lax.axis_index("c")
    x0 = x[0]
    mem_b = mem[0].astype(BF16)
    target = loss_target[0]

    comm = Exchanges()
    weight_jobs = []
    for i in range(depth):
        w_in = (w_in_a if i % 2 == 0 else w_in_b)[i // 2]
        groups_of_shards = ([w_in.T, w_mem_kv[i]], [w_out[i].T], [w_ff1[i].T], [w_ff2[i]])
        weight_jobs.append([GatherJob([t.astype(BF16) for t in shards], f"g{i}{tag}")
                            for shards, tag in zip(groups_of_shards, ("in", "out", "ff1", "ff2"))])
    weight_jobs[0][0] = GatherJob(weight_jobs[0][0].arrays + [jnp.concatenate([vnorm_g, vnorm_b], axis=0)], "g0in")
    for jobs in weight_jobs:
        for job in jobs:
            comm.submit(job)

    def gathered(job):
        return [g.reshape(N_DEV * g.shape[1], g.shape[2]) for g in comm.require(job)]

    vnorm = comm.require(weight_jobs[0][0])[2]
    vnorm = jnp.transpose(vnorm, (1, 0, 2)).reshape(2 * n_b, mix_w)
    vg_full, vb_full = vnorm[:n_b], vnorm[n_b:]

    saved = []
    weights = []
    xf, xb = x0, x0.astype(BF16)
    for i in range(depth):
        j = i // 2
        w_in_t, w_kv = gathered(weight_jobs[i][0])[:2]
        proj = matmul(xb, w_in_t, "nt", name=f"in_proj_{i % 2}", comm=comm)
        kv = matmul(mem_b, w_kv, "nn", name="mem_kv")
        if i % 2 == 0:
            os_, ls_ = [], []
            for g, d in enumerate(A_DILATIONS):
                o_g, l_g = attention_forward(proj, g, d, groups, name=f"attn_fwd_d{d}", comm=comm)
                os_.append(o_g)
                ls_.append(l_g)
            o, cat, lse = attention_combine(os_, ls_, name="attn_combine")
            qcol = 9 * mix_w
            extra = (o, lse)
        else:
            cat = gmlp_forward(proj, w_s[j], b_s[j], vg_full[j], vb_full[j], name="gmlp_fwd")
            qcol = 2 * mix_w
            extra = ()
        cat = memory_attention(proj, qcol, kv, cat, name=f"mem_attn_{i % 2}")
        (w_o_t,) = gathered(weight_jobs[i][1])
        y1 = matmul(cat, w_o_t, "nt", name="out_proj", comm=comm)
        r1, x1, x1b = ln_residual(xf, y1, ln1_g[i], ln1_b[i], alpha, name="ln1")
        (w_1_t,) = gathered(weight_jobs[i][2])
        h, hid = matmul(x1b, w_1_t, "nt", name="ff1", out_dtypes=(F32, BF16), comm=comm,
                        epilogue=lambda acc: (acc, jnp.square(jnp.maximum(acc, 0.0))))
        (w_2,) = gathered(weight_jobs[i][3])
        y2 = matmul(hid, w_2, "nn", name="ff2", comm=comm)
        r2, x2, x2b = ln_residual(x1, y2, ln2_g[i], ln2_b[i], alpha, name="ln2")
        weights.append((w_in_t, w_o_t, w_1_t, w_2, w_kv))
        saved.append((xb, proj, kv, extra, qcol, cat, r1, x1b, h, hid, r2))
        xf, xb = x2, x2b

    loss_tile, dx = loss_head(xf, target, name="loss_head")
    loss = lax.psum(loss_tile[0, 0], ("x", "y", "c"))

    def exchange(dw, tag):
        rows, cols = dw.shape[0] // N_DEV, dw.shape[1]
        p = dw.reshape(N_DEV, rows, cols)
        n_chunks = 1
        while ExchangeJob.US_PER_MB * dw.size * 2 / 1e6 > 110.0 * n_chunks and rows % (32 * n_chunks) == 0:
            n_chunks *= 2
        step = rows // n_chunks
        return [comm.submit(ExchangeJob(p, q * step, step, f"x{tag}{q}")) for q in range(n_chunks)]

    recv = {n: [None] * depth for n in ("w_in", "w_mem_kv", "w_out", "w_ff1", "w_ff2")}
    small = {n: [None] * depth for n in ("ln1_g", "ln1_b", "ln2_g", "ln2_b")}
    small_b = {n: [None] * n_b for n in ("w_s", "b_s", "vnorm_g", "vnorm_b")}
    for i in reversed(range(depth)):
        w_in_t, w_o_t, w_1_t, w_2, w_kv = weights[i]
        xb, proj, kv, extra, qcol, cat, r1, x1b, h, hid, r2 = saved[i]
        j = i // 2
        dr2, dr2b, small["ln2_g"][i], small["ln2_b"][i] = ln_backward(r2, ln2_g[i], dx, name="ln2_bwd")
        dh = matmul(dr2b, w_2, "nt", name="ff2_dx", extras=(h,), out_dtypes=(BF16,), comm=comm,
                    epilogue=lambda acc, h_t: (acc * (2.0 * jnp.maximum(h_t, 0.0)),))
        recv["w_ff2"][i] = exchange(matmul(hid, dr2b, "tn", name="ff2_dw", out_dtypes=(BF16,), comm=comm), f"ff2{i}")
        dx1 = matmul(dh, w_1_t, "nn", name="ff1_dx", extras=(dr2,), comm=comm, epilogue=lambda acc, res: (acc + alpha * res,))
        recv["w_ff1"][i] = exchange(matmul(dh, x1b, "tn", name="ff1_dw", out_dtypes=(BF16,), comm=comm), f"ff1{i}")
        dr1, dr1b, small["ln1_g"][i], small["ln1_b"][i] = ln_backward(r1, ln1_g[i], dx1, name="ln1_bwd")
        dcat = matmul(dr1b, w_o_t, "nn", name="out_proj_dx", comm=comm)
        recv["w_out"][i] = exchange(matmul(dr1b, cat, "tn", name="out_proj_dw", out_dtypes=(BF16,), comm=comm), f"out{i}")
        dq_mem, dkv = memory_attention_backward(proj, qcol, kv, dcat, mix_w, name=f"mem_attn_bwd_{i % 2}")
        recv["w_mem_kv"][i] = exchange(matmul(mem_b, dkv, "tn", name="mem_kv_dw", out_dtypes=(BF16,)), f"kv{i}")
        if i % 2 == 0:
            o, lse = extra
            pieces = []
            for g, d in enumerate(A_DILATIONS):
                pieces += [t.astype(BF16) for t in attention_backward(proj, o, lse, dcat, g, d, groups, name=f"attn_bwd_d{d}", comm=comm)]
            dproj = jnp.concatenate(pieces + [dq_mem], axis=1)
        else:
            dpu, dpv, small_b["w_s"][j], small_b["b_s"][j], small_b["vnorm_g"][j], small_b["vnorm_b"][j] = gmlp_backward(
                proj, w_s[j], b_s[j], vg_full[j], vb_full[j], dcat, name="gmlp_bwd")
            dproj = jnp.concatenate([dpu, dpv, dq_mem], axis=1)
        recv["w_in"][i] = exchange(matmul(dproj, xb, "tn", name=f"in_proj_dw_{i % 2}", out_dtypes=(BF16,), comm=comm), f"in{i}")
        dx = matmul(dproj, w_in_t, "nn", name=f"in_proj_dx_{i % 2}", extras=(dr1,), comm=comm, epilogue=lambda acc, res: (acc + alpha * res,))
    grad_x = dx[None]

    flat_small = jnp.concatenate(
        [jnp.concatenate(small[n], axis=0).reshape(-1) for n in ("ln1_g", "ln1_b", "ln2_g", "ln2_b")]
        + [jnp.stack(small_b[n]).reshape(-1) for n in ("w_s", "b_s", "vnorm_g", "vnorm_b")])
    pad = (-flat_small.size) % 1024
    small_job = comm.submit(GatherJob([jnp.pad(flat_small, (0, pad)).reshape(-1, 1024)], "gsmall"))

    results = {}

    def received(name, i):
        return [comm.require(job)[0] for job in recv[name][i]]

    def transposed(name, layers, w_, m_, v_):
        grad = jnp.stack([sum_parts(received(name, i), name=f"sum_{name}_{i % 2}").T for i in layers])
        flat = lambda t: t.reshape(-1, t.shape[-1])
        upd = adamw(flat(grad)[None], flat(w_), flat(m_), flat(v_), name=f"adamw_{name}_{layers[0] % 2}", emit_grad=False, comm=comm)
        return [grad] + [t.reshape(w_.shape) for t in upd]

    results["w_ff2"] = adamw_layers([received("w_ff2", i) for i in range(depth)], w_ff2, m_w_ff2, v_w_ff2, name="adamw_w_ff2", comm=comm)
    results["w_ff1"] = transposed("w_ff1", list(range(depth)), w_ff1, m_w_ff1, v_w_ff1)
    results["w_out"] = transposed("w_out", list(range(depth)), w_out, m_w_out, v_w_out)
    results["w_mem_kv"] = adamw_layers([received("w_mem_kv", i) for i in range(depth)], w_mem_kv, m_w_mem_kv, v_w_mem_kv, name="adamw_w_mem_kv")
    results["w_in_b"] = transposed("w_in", list(range(1, depth, 2)), w_in_b, m_w_in_b, v_w_in_b)
    results["w_in_a"] = transposed("w_in", list(range(0, depth, 2)), w_in_a, m_w_in_a, v_w_in_a)

    gathered_small = comm.require(small_job)[0].reshape(N_DEV, -1)
    at = 0

    def take(shape):
        nonlocal at
        size = math.prod(shape)
        at += size
        return gathered_small[:, at - size:at].reshape((N_DEV,) + shape)

    for n, w_, m_, v_ in (("ln1_g", ln1_g, m_ln1_g, v_ln1_g), ("ln1_b", ln1_b, m_ln1_b, v_ln1_b),
                          ("ln2_g", ln2_g, m_ln2_g, v_ln2_g), ("ln2_b", ln2_b, m_ln2_b, v_ln2_b)):
        results[n] = adamw(take(w_.shape), w_, m_, v_, name="adamw_ln")
    lanes = lambda t: t.reshape(-1, 128)
    results["w_s"] = [t.reshape(w_s.shape) for t in adamw(take((w_s.size // 128, 128)), lanes(w_s), lanes(m_w_s), lanes(v_w_s), name="adamw_w_s")]
    results["b_s"] = [t.reshape(b_s.shape) for t in adamw(take((b_s.size // 128, 128)), lanes(b_s), lanes(m_b_s), lanes(v_b_s), name="adamw_b_s")]
    for n, w_, m_, v_ in (("vnorm_g", vnorm_g, m_vnorm_g, v_vnorm_g), ("vnorm_b", vnorm_b, m_vnorm_b, v_vnorm_b)):
        parts = lax.dynamic_slice_in_dim(take((n_b, mix_w)), me * w_.shape[1], w_.shape[1], axis=2)
        results[n] = adamw(parts, w_, m_, v_, name="adamw_vnorm")

    order =("w_in_a", "w_in_b", "w_s", "b_s", "vnorm_g", "vnorm_b", "w_mem_kv", "w_out", "ln1_g", "ln1_b", "w_ff1", "w_ff2", "ln2_g", "ln2_b")
    return (loss, grad_x, *[results[n][0] for n in order], *[results[n][1] for n in order],
            *[results[n][2] for n in order], *[results[n][3] for n in order])
```

```python
import math

import jax
import jax.numpy as jnp
from jax import lax
from jax.experimental import pallas as pl
from jax.experimental.pallas import tpu as pltpu

F32 = jnp.float32
BF16 = jnp.bfloat16
N_DEV = 8
HEAD_DIM = 128
MEM_HEADS = 4
MEM_WIDTH = MEM_HEADS * HEAD_DIM
A_DILATIONS = (1, 4, 16)
LN_EPS = 1e-5
ADAM_LR, ADAM_B1, ADAM_B2, ADAM_EPS, ADAM_WD, ADAM_STEP = 0.001, 0.9, 0.999, 1e-08, 0.01, 10
VMEM_LIMIT_BYTES = 56 * 1024 * 1024
MESH = pl.DeviceIdType.MESH
ANY = pl.BlockSpec(memory_space=pl.ANY)


def _params(*sem):
    return pltpu.CompilerParams(dimension_semantics=sem, vmem_limit_bytes=VMEM_LIMIT_BYTES)


def _tile(dim, pref):
    if dim <= pref:
        return dim
    best = None
    for t in range(128, pref + 1, 128):
        if dim % t == 0:
            best = t
    return best if best is not None else dim


def _dot(a, b, dims):
    return lax.dot_general(a, b, (dims, ((), ())), preferred_element_type=F32)


NN = ((1,), (0,))
NT = ((1,), (1,))
TN = ((0,), (0,))


def _my_position():
    return lax.axis_index("x"), lax.axis_index("y"), lax.axis_index("c")


class GatherJob:
    US_PER_MB = 46.0
    OVERSHOOT = 1.4

    def __init__(self, arrays, tag):
        self.arrays, self.tag, self.outs = list(arrays), tag, None
        self.out_shapes = [jax.ShapeDtypeStruct((N_DEV,) + v.shape, v.dtype) for v in self.arrays]
        self.n_sems, self.n_local = 7 * len(self.arrays), len(self.arrays)
        self.est_us = self.US_PER_MB * sum(v.size * v.dtype.itemsize for v in self.arrays) / 1e6

    def phases(self, v_refs, out_refs, send_sems, recv_sems, local_sems, sem0, loc0):
        n = len(self.arrays)
        x, y, c = _my_position()
        me, sibling = (x, y, c), (x, y, 1 - c)
        chips = [(1 - x, y), (x, 1 - y), (1 - x, 1 - y)]

        def copy(a, k, block, to, from_input=False):
            px, py, pc = block
            slot = out_refs[a].at[4 * px + 2 * py + pc]
            return pltpu.make_async_remote_copy(
                src_ref=v_refs[a] if from_input else slot, dst_ref=slot, send_sem=send_sems.at[sem0 + 7 * a + k],
                recv_sem=recv_sems.at[sem0 + 7 * a + k], device_id=to, device_id_type=MESH)

        def mine(a):
            return pltpu.make_async_copy(v_refs[a], out_refs[a].at[4 * x + 2 * y + c], local_sems.at[loc0 + a])

        def first(a):
            return [copy(a, 0, me, sibling, True)] + [copy(a, 1 + j, me, (*chip, c), True) for j, chip in enumerate(chips)]

        def start():
            for a in range(n):
                mine(a).start()
                for cp in first(a):
                    cp.start()

        def middle():
            for j, chip in enumerate(chips):
                for a in range(n):
                    copy(a, 1 + j, (*chip, c), me).wait_recv()
                    copy(a, 4 + j, (*chip, c), sibling).start()

        def finish():
            for a in range(n):
                copy(a, 0, sibling, me).wait_recv()
                for j, chip in enumerate(chips):
                    copy(a, 4 + j, (*chip, 1 - c), me).wait_recv()
                for cp in first(a) + [copy(a, 4 + j, (*chip, c), sibling) for j, chip in enumerate(chips)]:
                    cp.wait_send()
                mine(a).wait()

        return start, middle, finish


class ExchangeJob:
    US_PER_MB = 11.0
    OVERSHOOT = 1.05
    CHUNK_US = 50.0
    RELATIONS = [(dx, dy, dc) for dx in (0, 1) for dy in (0, 1) for dc in (0, 1)][1:]

    def __init__(self, p, row0, rows, tag):
        self.arrays, self.tag, self.outs = [p], tag, None
        self.row0, self.rows = row0, rows
        self.out_shapes = [jax.ShapeDtypeStruct((N_DEV, rows, p.shape[2]), p.dtype)]
        self.n_sems, self.n_local = 7, 1
        self.est_us = self.US_PER_MB * N_DEV * rows * p.shape[2] * p.dtype.itemsize / 1e6

    def phases(self, p_refs, out_refs, send_sems, recv_sems, local_sems, sem0, loc0):
        (p_ref,), (out_ref,) = p_refs, out_refs
        x, y, c = _my_position()
        me = 4 * x + 2 * y + c
        chunk = pl.ds(self.row0, self.rows)

        def mine():
            return pltpu.make_async_copy(p_ref.at[me, chunk], out_ref.at[me], local_sems.at[loc0])

        def copies(arriving):
            made = []
            for k, (dx, dy, dc) in enumerate(self.RELATIONS):
                px, py, pc = (x + dx) % 2, (y + dy) % 2, (c + dc) % 2
                peer = 4 * px + 2 * py + pc
                made.append(pltpu.make_async_remote_copy(
                    src_ref=p_ref.at[peer, chunk], dst_ref=out_ref.at[peer if arriving else me],
                    send_sem=send_sems.at[sem0 + k], recv_sem=recv_sems.at[sem0 + k],
                    device_id=(px, py, pc), device_id_type=MESH))
            return made

        def start():
            mine().start()
            for send in copies(False):
                send.start()

        def finish():
            for arrival in copies(True):
                arrival.wait_recv()
            for send in copies(False):
                send.wait_send()
            mine().wait()

        return start, (lambda: None), finish


class Exchanges:
    MIN_HOST_US = 35.0
    MAX_JOBS = 6

    def __init__(self):
        self.queue = []

    def submit(self, job):
        self.queue.append(job)
        return job

    def take(self, host_us):
        jobs, used = [], 0.0
        if host_us >= self.MIN_HOST_US:
            for job in list(self.queue):
                if used + job.est_us <= job.OVERSHOOT * host_us and len(jobs) < self.MAX_JOBS:
                    used += job.est_us
                    jobs.append(job)
                    self.queue.remove(job)
        return jobs

    def require(self, job):
        if job.outs is None:
            self.queue.remove(job)
            _call(lambda: None, grid=(), in_specs=[], out_specs=[], out_shape=[], scratch_shapes=[], args=[],
                  name="exchange", semantics=(), jobs=[job])
        return job.outs


def _call(body, *, grid, in_specs, out_specs, out_shape, scratch_shapes, args, name, semantics, jobs=()):
    if not jobs:
        return pl.pallas_call(body, grid=grid, in_specs=in_specs, out_specs=out_specs, out_shape=out_shape,
                              scratch_shapes=scratch_shapes, compiler_params=_params(*semantics), name=name)(*args)
    n_in, n_out, n_scr = len(in_specs), len(out_shape), len(scratch_shapes)
    j_in = [a for job in jobs for a in job.arrays]
    j_out = [s for job in jobs for s in job.out_shapes]
    n_sems, n_local = sum(job.n_sems for job in jobs), sum(job.n_local for job in jobs)
    steps = math.prod(grid)
    middle_step = (7 * steps) // 8

    def wrapped(*refs):
        ins, refs = refs[:n_in], refs[n_in:]
        j_ins, refs = refs[:len(j_in)], refs[len(j_in):]
        outs, refs = refs[:n_out], refs[n_out:]
        j_outs, refs = refs[:len(j_out)], refs[len(j_out):]
        scratch, (send_sems, recv_sems, local_sems) = refs[:n_scr], refs[n_scr:]
        step = 0
        for axis, extent in enumerate(grid):
            step = step * extent + pl.program_id(axis)
        phases, at_in, at_out, sem0, loc0 = [], 0, 0, 0, 0
        for job in jobs:
            k_in, k_out = len(job.arrays), len(job.out_shapes)
            phases.append(job.phases(j_ins[at_in:at_in + k_in], j_outs[at_out:at_out + k_out], send_sems, recv_sems, local_sems, sem0, loc0))
            at_in, at_out, sem0, loc0 = at_in + k_in, at_out + k_out, sem0 + job.n_sems, loc0 + job.n_local

        def run(which):
            for ph in phases:
                ph[which]()

        if steps == 1:
            run(0)
            body(*ins, *outs, *scratch)
            run(1)
            run(2)
        else:
            pl.when(step == 0)(lambda: run(0))
            body(*ins, *outs, *scratch)
            pl.when(step == middle_step)(lambda: run(1))
            pl.when(step == steps - 1)(lambda: run(2))

    res = pl.pallas_call(
        wrapped, grid=grid, in_specs=list(in_specs) + [ANY] * len(j_in), out_specs=list(out_specs) + [ANY] * len(j_out),
        out_shape=list(out_shape) + j_out,
        scratch_shapes=list(scratch_shapes) + [pltpu.SemaphoreType.DMA((n_sems,)), pltpu.SemaphoreType.DMA((n_sems,)),
                                               pltpu.SemaphoreType.DMA((n_local,))],
        compiler_params=_params(*(["arbitrary"] * len(grid))), name=name + "".join("__" + job.tag for job in jobs),
    )(*args, *j_in)
    at = n_out
    for job in jobs:
        job.outs = list(res[at:at + len(job.out_shapes)])
        at += len(job.out_shapes)
    return list(res[:n_out])


MATMUL_TILES = {"nn": (1024, 512, 2560), "nt": (1024, 512, 2560), "tn": (1024, 1024, 2048)}
MATMUL_FLOPS_PER_US = {"nn": 7.5e8, "nt": 7.5e8, "tn": 8.5e8}


def matmul(a, b, mode, name, epilogue=None, extras=(), out_dtypes=(F32,), comm=None):
    if mode == "nn":
        (m, k), (k2, n) = a.shape, b.shape
    elif mode == "nt":
        (m, k), (n, k2) = a.shape, b.shape
    else:
        (k, m), (k2, n) = a.shape, b.shape
    assert k == k2, (a.shape, b.shape, mode)
    tm_pref, tn_pref, tk_pref = MATMUL_TILES[mode]
    tm = _tile(m, tm_pref)
    if mode == "tn" and tm < tm_pref:
        tn_pref *= 2
    tn, tk = _tile(n, tn_pref), _tile(k, tk_pref)
    nk = k // tk
    dims = {"nn": NN, "nt": NT, "tn": TN}[mode]
    a_spec = pl.BlockSpec((tk, tm), lambda i, j, kk: (kk, i)) if mode == "tn" else pl.BlockSpec((tm, tk), lambda i, j, kk: (i, kk))
    b_spec = pl.BlockSpec((tn, tk), lambda i, j, kk: (j, kk)) if mode == "nt" else pl.BlockSpec((tk, tn), lambda i, j, kk: (kk, j))
    tile_spec = pl.BlockSpec((tm, tn), lambda i, j, kk: (i, j))
    n_ex, n_out = len(extras), len(out_dtypes)
    if epilogue is None:
        epilogue = lambda acc: (acc,) * n_out

    def body(a_ref, b_ref, *rest):
        ex_refs, out_refs, acc_ref = rest[:n_ex], rest[n_ex:n_ex + n_out], rest[-1]
        kk = pl.program_id(2)

        def product():
            return _dot(a_ref[...].astype(BF16), b_ref[...].astype(BF16), dims)

        def finish(acc):
            for o_ref, val in zip(out_refs, epilogue(acc, *[e[...] for e in ex_refs])):
                o_ref[...] = val.astype(o_ref.dtype)

        if nk == 1:
            finish(product())
            return

        @pl.when(kk == 0)
        def _():
            acc_ref[...] = product()

        @pl.when((kk > 0) & (kk < nk - 1))
        def _():
            acc_ref[...] += product()

        @pl.when(kk == nk - 1)
        def _():
            finish(acc_ref[...] + product())

    host_us = 2.0 * m * n * k / MATMUL_FLOPS_PER_US[mode]
    outs = _call(
        body,
        grid=(m // tm, n // tn, nk),
        in_specs=[a_spec, b_spec] + [tile_spec] * n_ex,
        out_specs=[tile_spec] * n_out,
        out_shape=[jax.ShapeDtypeStruct((m, n), dt) for dt in out_dtypes],
        scratch_shapes=[pltpu.VMEM((tm, tn), F32)],
        args=[a, b, *extras], name=name, semantics=("parallel", "parallel", "arbitrary"),
        jobs=comm.take(host_us) if comm is not None else (),
    )
    return outs[0] if n_out == 1 else outs


LN_BYTES_PER_US = 3.0e6


def ln_residual(x, y, g, b, alpha, name, comm=None):
    s, w = x.shape
    tb = _tile(s, 256)
    row = pl.BlockSpec((tb, w), lambda i: (i, 0))
    vec = pl.BlockSpec((1, w), lambda i: (0, 0))

    def body(x_ref, y_ref, g_ref, b_ref, r_ref, xn_ref, xnb_ref):
        r = alpha * x_ref[...] + y_ref[...]
        mu = jnp.mean(r, axis=-1, keepdims=True)
        var = jnp.mean(jnp.square(r - mu), axis=-1, keepdims=True)
        xn = (r - mu) * lax.rsqrt(var + LN_EPS) * g_ref[...] + b_ref[...]
        r_ref[...] = r
        xn_ref[...] = xn
        xnb_ref[...] = xn.astype(BF16)

    return _call(
        body, grid=(s // tb,), in_specs=[row, row, vec, vec], out_specs=[row, row, row],
        out_shape=[jax.ShapeDtypeStruct((s, w), F32), jax.ShapeDtypeStruct((s, w), F32), jax.ShapeDtypeStruct((s, w), BF16)],
        scratch_shapes=[], args=[x, y, g.reshape(1, w), b.reshape(1, w)], name=name, semantics=("parallel",),
        jobs=comm.take(18.0 * s * w / LN_BYTES_PER_US) if comm is not None else (),
    )


def _ln_bwd_tile(r, g, dxn):
    mu = jnp.mean(r, axis=-1, keepdims=True)
    cen = r - mu
    rstd = lax.rsqrt(jnp.mean(jnp.square(cen), axis=-1, keepdims=True) + LN_EPS)
    xhat = cen * rstd
    dxh = dxn * g
    dr = rstd * (dxh - jnp.mean(dxh, axis=-1, keepdims=True) - xhat * jnp.mean(dxh * xhat, axis=-1, keepdims=True))
    return dr, jnp.sum(dxn * xhat, axis=0, keepdims=True)


def ln_backward(r, g, dxn, name, comm=None):
    s, w = r.shape
    tb = _tile(s, 256)
    row = pl.BlockSpec((tb, w), lambda i: (i, 0))
    vec = pl.BlockSpec((1, w), lambda i: (0, 0))

    def body(r_ref, g_ref, d_ref, dr_ref, drb_ref, dg_ref, db_ref):
        @pl.when(pl.program_id(0) == 0)
        def _():
            dg_ref[...] = jnp.zeros_like(dg_ref)
            db_ref[...] = jnp.zeros_like(db_ref)

        dxn = d_ref[...]
        dr, dg = _ln_bwd_tile(r_ref[...], g_ref[...], dxn)
        dr_ref[...] = dr
        drb_ref[...] = dr.astype(BF16)
        dg_ref[...] += dg
        db_ref[...] += jnp.sum(dxn, axis=0, keepdims=True)

    return _call(
        body, grid=(s // tb,), in_specs=[row, vec, row], out_specs=[row, row, vec, vec],
        out_shape=[jax.ShapeDtypeStruct((s, w), F32), jax.ShapeDtypeStruct((s, w), BF16),
                   jax.ShapeDtypeStruct((1, w), F32), jax.ShapeDtypeStruct((1, w), F32)],
        scratch_shapes=[], args=[r, g.reshape(1, w), dxn], name=name, semantics=("arbitrary",),
        jobs=comm.take(14.0 * s * w / LN_BYTES_PER_US) if comm is not None else (),
    )


ATTENTION_US = {("fwd", 1): 55.0, ("fwd", 4): 80.0, ("fwd", 16): 100.0, ("bwd", 1): 115.0, ("bwd", 4): 130.0, ("bwd", 16): 190.0}
ATTENTION_HEADS_PER_STEP = {("fwd", 1): 8, ("fwd", 4): 1, ("fwd", 16): 1, ("bwd", 1): 8, ("bwd", 4): 1, ("bwd", 16): 1}


def _causal_masks():
    qi = lax.broadcasted_iota(jnp.int32, (128, 128), 0)
    kj = lax.broadcasted_iota(jnp.int32, (128, 128), 1)
    return kj <= qi, kj >= qi


def _window_mask(has_previous):
    qi = lax.broadcasted_iota(jnp.int32, (128, 256), 0)
    kj = lax.broadcasted_iota(jnp.int32, (128, 256), 1)
    return ((kj < 128) & (kj >= qi) & has_previous) | ((kj >= 128) & (kj - 128 <= qi))


def _sub_rows(d, r):
    return pl.ds(r, 128, stride=d) if d > 1 else pl.ds(0, 128)


def _for_each_residue(d, fn):
    if d <= 4:
        for r in range(d):
            fn(r)
    else:
        def step(r, carry):
            fn(r)
            return carry
        lax.fori_loop(0, d, step, 0, unroll=4)


def attention_forward(proj, group, d, heads, name, comm=None):
    s = proj.shape[0]
    rows = 128 * d
    nb = s // rows
    scale = HEAD_DIM ** -0.5
    hp = math.gcd(heads, ATTENTION_HEADS_PER_STEP["fwd", d])
    wide = 128 * hp
    qc, kc, vc = (group * 3) * heads // hp, (group * 3 + 1) * heads // hp, (group * 3 + 2) * heads // hp

    def cur(col):
        return pl.BlockSpec((rows, wide), lambda h, n: (n, col + h))

    def prev(col):
        return pl.BlockSpec((rows, wide), lambda h, n: (jnp.maximum(n - 1, 0), col + h))

    out = pl.BlockSpec((rows, wide), lambda h, n: (n, h))

    def body(q_ref, kc_ref, kp_ref, vc_ref, vp_ref, o_ref, l_ref):
        mask = _window_mask(pl.program_id(1) > 0)

        def one(r):
            rws = _sub_rows(d, r)
            for hh in range(hp):
                sl = (rws, pl.ds(hh * 128, 128))
                k = jnp.concatenate([kp_ref[sl].astype(BF16), kc_ref[sl].astype(BF16)], axis=0)
                v = jnp.concatenate([vp_ref[sl].astype(BF16), vc_ref[sl].astype(BF16)], axis=0)
                sc = jnp.where(mask, _dot(q_ref[sl].astype(BF16), k, NT) * scale, -jnp.inf)
                m = jnp.max(sc, axis=-1, keepdims=True)
                e = jnp.exp(sc - m)
                l = jnp.sum(e, axis=-1, keepdims=True)
                o_ref[sl] = _dot(e.astype(BF16), v, NN) / l
                l_ref[sl] = jnp.broadcast_to(m + jnp.log(l), (128, 128))

        _for_each_residue(d, one)

    return _call(
        body, grid=(heads // hp, nb),
        in_specs=[cur(qc), cur(kc), prev(kc), cur(vc), prev(vc)], out_specs=[out, out],
        out_shape=[jax.ShapeDtypeStruct((s, heads * 128), F32)] * 2, scratch_shapes=[],
        args=[proj] * 5, name=name, semantics=("parallel", "parallel"),
        jobs=comm.take(ATTENTION_US["fwd", d] * s * heads / (4096 * 8)) if comm is not None else (),
    )


def attention_combine(os_, ls_, name):
    s, w = os_[0].shape
    tb = _tile(s, 256)
    row = pl.BlockSpec((tb, w), lambda i: (i, 0))

    def body(o0, o1, o2, l0, l1, l2, o_ref, ob_ref, lse_ref):
        a, b, c = l0[...], l1[...], l2[...]
        m = jnp.maximum(jnp.maximum(a, b), c)
        ea, eb, ec = jnp.exp(a - m), jnp.exp(b - m), jnp.exp(c - m)
        tot = ea + eb + ec
        o = (ea / tot) * o0[...] + (eb / tot) * o1[...] + (ec / tot) * o2[...]
        o_ref[...] = o
        ob_ref[...] = o.astype(BF16)
        lse_ref[...] = m + jnp.log(tot)

    return pl.pallas_call(
        body, grid=(s // tb,), in_specs=[row] * 6, out_specs=[row] * 3,
        out_shape=[jax.ShapeDtypeStruct((s, w), F32), jax.ShapeDtypeStruct((s, w + MEM_WIDTH), BF16), jax.ShapeDtypeStruct((s, w), F32)],
        compiler_params=_params("parallel"), name=name,
    )(*os_, *ls_)


def attention_backward(proj, o, lse, dcat, group, d, heads, name, comm=None):
    s = proj.shape[0]
    rows = 128 * d
    nb = s // rows
    scale = HEAD_DIM ** -0.5
    hp = math.gcd(heads, ATTENTION_HEADS_PER_STEP["bwd", d])
    wide = 128 * hp
    qc, kc, vc = (group * 3) * heads // hp, (group * 3 + 1) * heads // hp, (group * 3 + 2) * heads // hp

    def at(col, shift):
        return pl.BlockSpec((rows, wide), lambda h, n: (jnp.clip(n + shift, 0, nb - 1), col + h))

    def out(col):
        return pl.BlockSpec((rows, wide), lambda h, n: (n, col + h))

    def body(qc_ref, qn_ref, kc_ref, kp_ref, vc_ref, vp_ref, doc_ref, don_ref, oc_ref, on_ref, lc_ref, ln_ref,
             dq_ref, dk_ref, dv_ref):
        n = pl.program_id(1)
        mask_w = _window_mask(n > 0)
        mask_n = _causal_masks()[1] & (n < nb - 1)

        def tile(q, k, v, do, lse_t, dsum, mask):
            p = jnp.where(mask, jnp.exp(_dot(q, k, NT) * scale - lse_t), 0.0)
            ds = (p * (_dot(do, v, NT) - dsum) * scale).astype(BF16)
            return p.astype(BF16), ds

        def one(r):
            rws = _sub_rows(d, r)
            for hh in range(hp):
                sl = (rws, pl.ds(hh * 128, 128))
                q_c, q_n = qc_ref[sl].astype(BF16), qn_ref[sl].astype(BF16)
                k_c, v_c = kc_ref[sl].astype(BF16), vc_ref[sl].astype(BF16)
                k_w = jnp.concatenate([kp_ref[sl].astype(BF16), k_c], axis=0)
                v_w = jnp.concatenate([vp_ref[sl].astype(BF16), v_c], axis=0)
                do_c, do_n = doc_ref[sl], don_ref[sl]
                dsum_c = jnp.sum(do_c * oc_ref[sl], axis=-1, keepdims=True)
                dsum_n = jnp.sum(do_n * on_ref[sl], axis=-1, keepdims=True)
                do_c, do_n = do_c.astype(BF16), do_n.astype(BF16)
                lse_c = lc_ref[sl]
                p_w, ds_w = tile(q_c, k_w, v_w, do_c, jnp.concatenate([lse_c, lse_c], axis=1), dsum_c, mask_w)
                p_n, ds_n = tile(q_n, k_c, v_c, do_n, ln_ref[sl], dsum_n, mask_n)
                dq_ref[sl] = _dot(ds_w, k_w, NN)
                both_q = jnp.concatenate([q_c, q_n], axis=0)
                both_do = jnp.concatenate([do_c, do_n], axis=0)
                dk_ref[sl] = _dot(jnp.concatenate([ds_w[:, 128:], ds_n], axis=0), both_q, TN)
                dv_ref[sl] = _dot(jnp.concatenate([p_w[:, 128:], p_n], axis=0), both_do, TN)

        _for_each_residue(d, one)

    w = heads * 128
    return _call(
        body, grid=(heads // hp, nb),
        in_specs=[at(qc, 0), at(qc, 1), at(kc, 0), at(kc, -1), at(vc, 0), at(vc, -1),
                  at(0, 0), at(0, 1), at(0, 0), at(0, 1), at(0, 0), at(0, 1)],
        out_specs=[out(0)] * 3,
        out_shape=[jax.ShapeDtypeStruct((s, w), F32)] * 3, scratch_shapes=[],
        args=[proj] * 6 + [dcat, dcat, o, o, lse, lse], name=name, semantics=("parallel", "parallel"),
        jobs=comm.take(ATTENTION_US["bwd", d] * s * heads / (4096 * 8)) if comm is not None else (),
    )


def _mem_softmax(q, kv, h, scale):
    k = kv[:, h * 128:(h + 1) * 128].astype(BF16)
    v = kv[:, MEM_WIDTH + h * 128:MEM_WIDTH + (h + 1) * 128].astype(BF16)
    sc = _dot(q, k, NT) * scale
    e = jnp.exp(sc - jnp.max(sc, axis=-1, keepdims=True))
    return e / jnp.sum(e, axis=-1, keepdims=True), k, v


def memory_attention(proj, qcol, kv, cat, name):
    s = proj.shape[0]
    tb = _tile(s, 512)
    scale = HEAD_DIM ** -0.5

    def body(q_ref, kv_ref, cat_ref, o_ref):
        kv_t = kv_ref[...]
        for h in range(MEM_HEADS):
            p, _, v = _mem_softmax(q_ref[:, h * 128:(h + 1) * 128].astype(BF16), kv_t, h, scale)
            o_ref[:, h * 128:(h + 1) * 128] = _dot(p.astype(BF16), v, NN).astype(BF16)

    return pl.pallas_call(
        body, grid=(s // tb,),
        in_specs=[pl.BlockSpec((tb, MEM_WIDTH), lambda i: (i, qcol // MEM_WIDTH)), pl.BlockSpec(kv.shape, lambda i: (0, 0)), ANY],
        out_specs=pl.BlockSpec((tb, MEM_WIDTH), lambda i: (i, cat.shape[1] // MEM_WIDTH - 1)),
        out_shape=jax.ShapeDtypeStruct(cat.shape, BF16), input_output_aliases={2: 0},
        compiler_params=_params("parallel"), name=name,
    )(proj, kv, cat)


def memory_attention_backward(proj, qcol, kv, dcat, dcol, name):
    s = proj.shape[0]
    tb = _tile(s, 512)
    scale = HEAD_DIM ** -0.5

    def body(q_ref, kv_ref, do_ref, dq_ref, dkv_ref):
        @pl.when(pl.program_id(0) == 0)
        def _():
            dkv_ref[...] = jnp.zeros_like(dkv_ref)

        kv_t = kv_ref[...]
        for h in range(MEM_HEADS):
            cols = slice(h * 128, (h + 1) * 128)
            q = q_ref[:, cols].astype(BF16)
            p, k, v = _mem_softmax(q, kv_t, h, scale)
            do = do_ref[:, cols].astype(BF16)
            dp = _dot(do, v, NT)
            ds = (p * (dp - jnp.sum(dp * p, axis=-1, keepdims=True)) * scale).astype(BF16)
            dq_ref[:, cols] = _dot(ds, k, NN).astype(BF16)
            dkv_ref[:, cols] += _dot(ds, q, TN)
            dkv_ref[:, MEM_WIDTH + h * 128:MEM_WIDTH + (h + 1) * 128] += _dot(p.astype(BF16), do, TN)

    return pl.pallas_call(
        body, grid=(s // tb,),
        in_specs=[pl.BlockSpec((tb, MEM_WIDTH), lambda i: (i, qcol // MEM_WIDTH)), pl.BlockSpec(kv.shape, lambda i: (0, 0)),
                  pl.BlockSpec((tb, MEM_WIDTH), lambda i: (i, dcol // MEM_WIDTH))],
        out_specs=[pl.BlockSpec((tb, MEM_WIDTH), lambda i: (i, 0)), pl.BlockSpec(kv.shape, lambda i: (0, 0))],
        out_shape=[jax.ShapeDtypeStruct((s, MEM_WIDTH), BF16), jax.ShapeDtypeStruct(kv.shape, F32)],
        compiler_params=_params("arbitrary"), name=name,
    )(proj, kv, dcat)


_SQRT_HALF = math.sqrt(0.5)
_INV_SQRT_2PI = 1.0 / math.sqrt(2.0 * math.pi)


def _gelu(x):
    return 0.5 * x * (1.0 + lax.erf(x * _SQRT_HALF))


def _gelu_grad(x):
    return 0.5 * (1.0 + lax.erf(x * _SQRT_HALF)) + x * (_INV_SQRT_2PI * jnp.exp(-0.5 * x * x))


def _gmlp_specs(s, wd, groups, tb):
    half = lambda c: pl.BlockSpec((tb, wd), lambda i: (i, c))
    ws_spec = pl.BlockSpec((groups, 128, 128), lambda i: (0, 0, 0))
    bs_spec = pl.BlockSpec((groups, 128, 1), lambda i: (0, 0, 0))
    vec = pl.BlockSpec((1, wd), lambda i: (0, 0))
    return half, ws_spec, bs_spec, vec


def _vnorm(zv, g, b):
    mu = jnp.mean(zv, axis=-1, keepdims=True)
    var = jnp.mean(jnp.square(zv - mu), axis=-1, keepdims=True)
    return (zv - mu) * lax.rsqrt(var + LN_EPS) * g + b


def gmlp_forward(proj, ws, bs, vg, vb, name):
    s = proj.shape[0]
    groups = ws.shape[0]
    wd = groups * 128
    tb = _tile(s, 512)
    half, ws_spec, bs_spec, vec = _gmlp_specs(s, wd, groups, tb)

    def body(pu_ref, pv_ref, ws_ref, bs_ref, vg_ref, vb_ref, o_ref, vn_ref):
        causal, _ = _causal_masks()
        vn_ref[...] = _vnorm(_gelu(pv_ref[...]), vg_ref[...], vb_ref[...]).astype(BF16)
        for g in range(groups):
            cols = slice(g * 128, (g + 1) * 128)
            wm = jnp.where(causal, ws_ref[g], 0.0).astype(BF16)
            for c in range(tb // 128):
                rws = slice(c * 128, (c + 1) * 128)
                sg = _dot(wm, vn_ref[rws, cols], NN) + bs_ref[g]
                o_ref[rws, cols] = (_gelu(pu_ref[rws, cols]) * sg).astype(BF16)

    return pl.pallas_call(
        body, grid=(s // tb,),
        in_specs=[half(0), half(1), ws_spec, bs_spec, vec, vec], out_specs=pl.BlockSpec((tb, wd), lambda i: (i, 0)),
        out_shape=jax.ShapeDtypeStruct((s, wd + MEM_WIDTH), BF16),
        scratch_shapes=[pltpu.VMEM((tb, wd), BF16)],
        compiler_params=_params("parallel"), name=name,
    )(proj, proj, ws, bs.reshape(groups, 128, 1), vg.reshape(1, wd), vb.reshape(1, wd))


def gmlp_backward(proj, ws, bs, vg, vb, dcat, name):
    s = proj.shape[0]
    groups = ws.shape[0]
    wd = groups * 128
    tb = _tile(s, 512)
    half, ws_spec, bs_spec, vec = _gmlp_specs(s, wd, groups, tb)

    def body(pu_ref, pv_ref, ws_ref, bs_ref, vg_ref, vb_ref, do_ref, dpu_ref, dpv_ref, dws_ref, dbs_ref, dvg_ref, dvb_ref,
             vn_ref, dvn_ref):
        @pl.when(pl.program_id(0) == 0)
        def _():
            dws_ref[...] = jnp.zeros_like(dws_ref)
            dbs_ref[...] = jnp.zeros_like(dbs_ref)
            dvg_ref[...] = jnp.zeros_like(dvg_ref)
            dvb_ref[...] = jnp.zeros_like(dvb_ref)

        causal, _ = _causal_masks()
        pv = pv_ref[...]
        zv = _gelu(pv)
        vn_ref[...] = _vnorm(zv, vg_ref[...], vb_ref[...]).astype(BF16)
        for g in range(groups):
            cols = slice(g * 128, (g + 1) * 128)
            wm = jnp.where(causal, ws_ref[g], 0.0).astype(BF16)
            dws_g = jnp.zeros((128, 128), F32)
            dbs_g = jnp.zeros((128, 1), F32)
            for c in range(tb // 128):
                rws = slice(c * 128, (c + 1) * 128)
                vn = vn_ref[rws, cols]
                pu = pu_ref[rws, cols]
                do = do_ref[rws, cols]
                sg = _dot(wm, vn, NN) + bs_ref[g]
                dpu_ref[rws, cols] = (do * sg * _gelu_grad(pu)).astype(BF16)
                dsg = do * _gelu(pu)
                dsg_b = dsg.astype(BF16)
                dws_g += _dot(dsg_b, vn, NT)
                dbs_g += jnp.sum(dsg, axis=-1, keepdims=True)
                dvn_ref[rws, cols] = _dot(wm, dsg_b, TN)
            dws_ref[g] += jnp.where(causal, dws_g, 0.0)
            dbs_ref[g] += dbs_g
        dvn = dvn_ref[...]
        dzv, dvg = _ln_bwd_tile(zv, vg_ref[...], dvn)
        dvg_ref[...] += dvg
        dvb_ref[...] += jnp.sum(dvn, axis=0, keepdims=True)
        dpv_ref[...] = (dzv * _gelu_grad(pv)).astype(BF16)

    row = pl.BlockSpec((tb, wd), lambda i: (i, 0))
    dpu, dpv, dws, dbs, dvg, dvb = pl.pallas_call(
        body, grid=(s // tb,),
        in_specs=[half(0), half(1), ws_spec, bs_spec, vec, vec, row],
        out_specs=[row, row, ws_spec, bs_spec, vec, vec],
        out_shape=[jax.ShapeDtypeStruct((s, wd), BF16), jax.ShapeDtypeStruct((s, wd), BF16),
                   jax.ShapeDtypeStruct((groups, 128, 128), F32), jax.ShapeDtypeStruct((groups, 128, 1), F32),
                   jax.ShapeDtypeStruct((1, wd), F32), jax.ShapeDtypeStruct((1, wd), F32)],
        scratch_shapes=[pltpu.VMEM((tb, wd), BF16), pltpu.VMEM((tb, wd), F32)],
        compiler_params=_params("arbitrary"), name=name,
    )(proj, proj, ws, bs.reshape(groups, 128, 1), vg.reshape(1, wd), vb.reshape(1, wd), dcat)
    return dpu, dpv, dws, dbs.reshape(groups, 128), dvg, dvb


def loss_head(y, target, name):
    s, w = y.shape
    tb = _tile(s, 256)
    row = pl.BlockSpec((tb, w), lambda i: (i, 0))
    nsteps = s // tb

    def body(y_ref, t_ref, loss_ref, dy_ref, acc_ref):
        i = pl.program_id(0)

        @pl.when(i == 0)
        def _():
            acc_ref[...] = jnp.zeros_like(acc_ref)

        err = y_ref[...] - t_ref[...]
        dy_ref[...] = err / w
        acc_ref[...] += jnp.sum(jnp.mean(jnp.square(err), axis=-1, keepdims=True), axis=0, keepdims=True)

        @pl.when(i == nsteps - 1)
        def _():
            loss_ref[...] = jnp.broadcast_to(0.5 * acc_ref[...], loss_ref.shape)

    return pl.pallas_call(
        body, grid=(nsteps,), in_specs=[row, row],
        out_specs=[pl.BlockSpec((8, 128), lambda i: (0, 0)), row],
        out_shape=[jax.ShapeDtypeStruct((8, 128), F32), jax.ShapeDtypeStruct((s, w), F32)],
        scratch_shapes=[pltpu.VMEM((1, 1), F32)],
        compiler_params=_params("arbitrary"), name=name,
    )(y, target)


PARTS_WINDOW_ELEMS = 1024 * 1024


def _row_tile(r, c, budget):
    if r * c <= budget or r % 16:
        return r
    fits = [t for t in range(16, r, 16) if r % t == 0 and t * c <= budget]
    return max(fits) if fits else 16


def _sum_in_device_order(p_ref):
    g = p_ref[0].astype(F32)
    for j in range(1, p_ref.shape[0]):
        g = g + p_ref[j].astype(F32)
    return g


def _adamw_update(g, w, m, v):
    nm = ADAM_B1 * m + (1.0 - ADAM_B1) * g
    nv = ADAM_B2 * v + (1.0 - ADAM_B2) * jnp.square(g)
    m_hat = nm / (1.0 - ADAM_B1 ** ADAM_STEP)
    v_hat = nv / (1.0 - ADAM_B2 ** ADAM_STEP)
    return -ADAM_LR * (m_hat / (jnp.sqrt(v_hat) + ADAM_EPS) + ADAM_WD * w), nm, nv


def sum_parts(chunks, name):
    p, r, c = chunks[0].shape
    tr = _row_tile(r, c, PARTS_WINDOW_ELEMS // len(chunks))
    nb = r // tr

    def chunk_spec(q):
        return pl.BlockSpec((p, tr, c), lambda ch, i: (0, jnp.where(ch == q, i, 0), 0))

    def body(*refs):
        for q in range(len(chunks)):
            @pl.when(pl.program_id(0) == q)
            def _():
                refs[-1][...] = _sum_in_device_order(refs[q])

    return pl.pallas_call(
        body, grid=(len(chunks), nb), in_specs=[chunk_spec(q) for q in range(len(chunks))],
        out_specs=pl.BlockSpec((tr, c), lambda ch, i: (ch * nb + i, 0)),
        out_shape=jax.ShapeDtypeStruct((len(chunks) * r, c), F32),
        compiler_params=_params("arbitrary", "arbitrary"), name=name,
    )(*chunks)


ELEMENTWISE_BYTES_PER_US = 2.0e6


def adamw(parts, w, m, v, name, emit_grad=True, comm=None):
    p, r, c = parts.shape
    tr = _row_tile(r, c, 160 * 1024)
    n_out = 4 if emit_grad else 3

    def body(p_ref, w_ref, m_ref, v_ref, *out_refs):
        g = _sum_in_device_order(p_ref)
        vals = _adamw_update(g, w_ref[...], m_ref[...], v_ref[...])
        for o_ref, val in zip(out_refs, ((g,) + vals) if emit_grad else vals):
            o_ref[...] = val

    row = pl.BlockSpec((tr, c), lambda i: (i, 0))
    host_us = (parts.size * parts.dtype.itemsize + (3 + n_out) * 4 * r * c) / ELEMENTWISE_BYTES_PER_US
    return _call(
        body, grid=(r // tr,), in_specs=[pl.BlockSpec((p, tr, c), lambda i: (0, i, 0)), row, row, row],
        out_specs=[row] * n_out, out_shape=[jax.ShapeDtypeStruct((r, c), F32)] * n_out, scratch_shapes=[],
        args=[parts, w, m, v], name=name, semantics=("parallel",), jobs=comm.take(host_us) if comm is not None else (),
    )


def kernel(x, mem, w_in_a, w_in_b, w_s, b_s, vnorm_g, vnorm_b, w_mem_kv, w_out, ln1_g, ln1_b, w_ff1, w_ff2, ln2_g, ln2_b, loss_target, m_w_in_a, m_w_in_b, m_w_s, m_b_s, m_vnorm_g, m_vnorm_b, m_w_mem_kv, m_w_out, m_ln1_g, m_ln1_b, m_w_ff1, m_w_ff2, m_ln2_g, m_ln2_b, v_w_in_a, v_w_in_b, v_w_s, v_b_s, v_vnorm_g, v_vnorm_b, v_w_mem_kv, v_w_out, v_ln1_g, v_ln1_b, v_w_ff1, v_w_ff2, v_ln2_g, v_ln2_b):
    depth = w_ff1.shape[0]
    groups = w_s.shape[1]
    mix_w = groups * HEAD_DIM
    n_b = vnorm_g.shape[0]
    alpha = (2.0 * depth) ** 0.25
    me = 4 * lax.axis_index("x") + 2 * lax.axis_index("y") + lax.axis_index("c")
    x0 = x[0]
    mem_b = mem[0].astype(BF16)
    target = loss_target[0]

    comm = Exchanges()
    weight_jobs = []
    for i in range(depth):
        w_in = (w_in_a if i % 2 == 0 else w_in_b)[i // 2]
        groups_of_shards = ([w_in.T, w_mem_kv[i]], [w_out[i].T], [w_ff1[i].T], [w_ff2[i]])
        weight_jobs.append([GatherJob([t.astype(BF16) for t in shards], f"g{i}{tag}")
                            for shards, tag in zip(groups_of_shards, ("in", "out", "ff1", "ff2"))])
    weight_jobs[0][0] = GatherJob(weight_jobs[0][0].arrays + [jnp.concatenate([vnorm_g, vnorm_b], axis=0)], "g0in")
    for jobs in weight_jobs:
        for job in jobs:
            comm.submit(job)

    def gathered(job):
        return [g.reshape(N_DEV * g.shape[1], g.shape[2]) for g in comm.require(job)]

    vnorm = comm.require(weight_jobs[0][0])[2]
    vnorm = jnp.transpose(vnorm, (1, 0, 2)).reshape(2 * n_b, mix_w)
    vg_full, vb_full = vnorm[:n_b], vnorm[n_b:]

    saved = []
    weights = []
    xf, xb = x0, x0.astype(BF16)
    for i in range(depth):
        j = i // 2
        w_in_t, w_kv = gathered(weight_jobs[i][0])[:2]
        proj = matmul(xb, w_in_t, "nt", name=f"in_proj_{i % 2}", comm=comm)
        kv = matmul(mem_b, w_kv, "nn", name="mem_kv")
        if i % 2 == 0:
            os_, ls_ = [], []
            for g, d in enumerate(A_DILATIONS):
                o_g, l_g = attention_forward(proj, g, d, groups, name=f"attn_fwd_d{d}", comm=comm)
                os_.append(o_g)
                ls_.append(l_g)
            o, cat, lse = attention_combine(os_, ls_, name="attn_combine")
            qcol = 9 * mix_w
            extra = (o, lse)
        else:
            cat = gmlp_forward(proj, w_s[j], b_s[j], vg_full[j], vb_full[j], name="gmlp_fwd")
            qcol = 2 * mix_w
            extra = ()
        cat = memory_attention(proj, qcol, kv, cat, name=f"mem_attn_{i % 2}")
        (w_o_t,) = gathered(weight_jobs[i][1])
        y1 = matmul(cat, w_o_t, "nt", name="out_proj", comm=comm)
        r1, x1, x1b = ln_residual(xf, y1, ln1_g[i], ln1_b[i], alpha, name="ln1", comm=comm)
        (w_1_t,) = gathered(weight_jobs[i][2])
        h, hid = matmul(x1b, w_1_t, "nt", name="ff1", out_dtypes=(F32, BF16), comm=comm,
                        epilogue=lambda acc: (acc, jnp.square(jnp.maximum(acc, 0.0))))
        (w_2,) = gathered(weight_jobs[i][3])
        y2 = matmul(hid, w_2, "nn", name="ff2", comm=comm)
        r2, x2, x2b = ln_residual(x1, y2, ln2_g[i], ln2_b[i], alpha, name="ln2", comm=comm)
        weights.append((w_in_t, w_o_t, w_1_t, w_2, w_kv))
        saved.append((xb, proj, kv, extra, qcol, cat, r1, x1b, h, hid, r2))
        xf, xb = x2, x2b

    loss_tile, dx = loss_head(xf, target, name="loss_head")
    loss = lax.psum(loss_tile[0, 0], ("x", "y", "c"))

    def exchange(dw, tag):
        rows, cols = dw.shape[0] // N_DEV, dw.shape[1]
        p = dw.reshape(N_DEV, rows, cols)
        whole_us = ExchangeJob.US_PER_MB * dw.size * dw.dtype.itemsize / 1e6
        counts = [q for q in range(1, rows // 16 + 1) if (rows // 16) % q == 0]
        n_chunks = next((q for q in counts if whole_us <= ExchangeJob.CHUNK_US * q), counts[-1])
        step = rows // n_chunks
        return [comm.submit(ExchangeJob(p, q * step, step, f"x{tag}{q}")) for q in range(n_chunks)]

    recv = {n: [None] * depth for n in ("w_in", "w_mem_kv", "w_out", "w_ff1", "w_ff2")}
    small = {n: [None] * depth for n in ("ln1_g", "ln1_b", "ln2_g", "ln2_b")}
    small_b = {n: [None] * n_b for n in ("w_s", "b_s", "vnorm_g", "vnorm_b")}
    for i in reversed(range(depth)):
        w_in_t, w_o_t, w_1_t, w_2, w_kv = weights[i]
        xb, proj, kv, extra, qcol, cat, r1, x1b, h, hid, r2 = saved[i]
        j = i // 2
        dr2, dr2b, small["ln2_g"][i], small["ln2_b"][i] = ln_backward(r2, ln2_g[i], dx, name="ln2_bwd", comm=comm)
        dh = matmul(dr2b, w_2, "nt", name="ff2_dx", extras=(h,), out_dtypes=(BF16,), comm=comm,
                    epilogue=lambda acc, h_t: (acc * (2.0 * jnp.maximum(h_t, 0.0)),))
        recv["w_ff2"][i] = exchange(matmul(hid, dr2b, "tn", name="ff2_dw", out_dtypes=(BF16,), comm=comm), f"ff2{i}")
        dx1 = matmul(dh, w_1_t, "nn", name="ff1_dx", extras=(dr2,), comm=comm, epilogue=lambda acc, res: (acc + alpha * res,))
        recv["w_ff1"][i] = exchange(matmul(dh, x1b, "tn", name="ff1_dw", out_dtypes=(BF16,), comm=comm), f"ff1{i}")
        dr1, dr1b, small["ln1_g"][i], small["ln1_b"][i] = ln_backward(r1, ln1_g[i], dx1, name="ln1_bwd", comm=comm)
        dcat = matmul(dr1b, w_o_t, "nn", name="out_proj_dx", comm=comm)
        recv["w_out"][i] = exchange(matmul(dr1b, cat, "tn", name="out_proj_dw", out_dtypes=(BF16,), comm=comm), f"out{i}")
        dq_mem, dkv = memory_attention_backward(proj, qcol, kv, dcat, mix_w, name=f"mem_attn_bwd_{i % 2}")
        recv["w_mem_kv"][i] = exchange(matmul(mem_b, dkv, "tn", name="mem_kv_dw", out_dtypes=(BF16,)), f"kv{i}")
        if i % 2 == 0:
            o, lse = extra
            pieces = []
            for g, d in enumerate(A_DILATIONS):
                pieces += [t.astype(BF16) for t in attention_backward(proj, o, lse, dcat, g, d, groups, name=f"attn_bwd_d{d}", comm=comm)]
            dproj = jnp.concatenate(pieces + [dq_mem], axis=1)
        else:
            dpu, dpv, small_b["w_s"][j], small_b["b_s"][j], small_b["vnorm_g"][j], small_b["vnorm_b"][j] = gmlp_backward(
                proj, w_s[j], b_s[j], vg_full[j], vb_full[j], dcat, name="gmlp_bwd")
            dproj = jnp.concatenate([dpu, dpv, dq_mem], axis=1)
        recv["w_in"][i] = exchange(matmul(dproj, xb, "tn", name=f"in_proj_dw_{i % 2}", out_dtypes=(BF16,), comm=comm), f"in{i}")
        dx = matmul(dproj, w_in_t, "nn", name=f"in_proj_dx_{i % 2}", extras=(dr1,), comm=comm, epilogue=lambda acc, res: (acc + alpha * res,))
    grad_x = dx[None]

    flat_small = jnp.concatenate(
        [jnp.concatenate(small[n], axis=0).reshape(-1) for n in ("ln1_g", "ln1_b", "ln2_g", "ln2_b")]
        + [jnp.stack(small_b[n]).reshape(-1) for n in ("w_s", "b_s", "vnorm_g", "vnorm_b")])
    pad = (-flat_small.size) % 1024
    small_job = comm.submit(GatherJob([jnp.pad(flat_small, (0, pad)).reshape(-1, 1024)], "gsmall"))

    results = {}

    def received(name, i):
        return [comm.require(job)[0] for job in recv[name][i]]

    def update(name, layers, w_, m_, v_, transposed):
        sums = [sum_parts(received(name, i), name=f"sum_{name}_{i % 2}") for i in layers]
        grad = jnp.stack([t.T if transposed else t for t in sums])
        flat = lambda t: t.reshape(-1, t.shape[-1])
        upd = adamw(flat(grad)[None], flat(w_), flat(m_), flat(v_), name=f"adamw_{name}_{layers[0] % 2}", emit_grad=False, comm=comm)
        return [grad] + [t.reshape(w_.shape) for t in upd]

    results["w_ff2"] = update("w_ff2", list(range(depth)), w_ff2, m_w_ff2, v_w_ff2, False)
    results["w_ff1"] = update("w_ff1", list(range(depth)), w_ff1, m_w_ff1, v_w_ff1, True)
    results["w_out"] = update("w_out", list(range(depth)), w_out, m_w_out, v_w_out, True)
    results["w_mem_kv"] = update("w_mem_kv", list(range(depth)), w_mem_kv, m_w_mem_kv, v_w_mem_kv, False)
    results["w_in_b"] = update("w_in", list(range(1, depth, 2)), w_in_b, m_w_in_b, v_w_in_b, True)
    results["w_in_a"] = update("w_in", list(range(0, depth, 2)), w_in_a, m_w_in_a, v_w_in_a, True)

    gathered_small = comm.require(small_job)[0].reshape(N_DEV, -1)
    at = 0

    def take(shape):
        nonlocal at
        size = math.prod(shape)
        at += size
        return gathered_small[:, at - size:at].reshape((N_DEV,) + shape)

    for n, w_, m_, v_ in (("ln1_g", ln1_g, m_ln1_g, v_ln1_g), ("ln1_b", ln1_b, m_ln1_b, v_ln1_b),
                          ("ln2_g", ln2_g, m_ln2_g, v_ln2_g), ("ln2_b", ln2_b, m_ln2_b, v_ln2_b)):
        results[n] = adamw(take(w_.shape), w_, m_, v_, name="adamw_ln")
    lanes = lambda t: t.reshape(-1, 128)
    results["w_s"] = [t.reshape(w_s.shape) for t in adamw(take((w_s.size // 128, 128)), lanes(w_s), lanes(m_w_s), lanes(v_w_s), name="adamw_w_s")]
    results["b_s"] = [t.reshape(b_s.shape) for t in adamw(take((b_s.size // 128, 128)), lanes(b_s), lanes(m_b_s), lanes(v_b_s), name="adamw_b_s")]
    for n, w_, m_, v_ in (("vnorm_g", vnorm_g, m_vnorm_g, v_vnorm_g), ("vnorm_b", vnorm_b, m_vnorm_b, v_vnorm_b)):
        parts = lax.dynamic_slice_in_dim(take((n_b, mix_w)), me * w_.shape[1], w_.shape[1], axis=2)
        results[n] = adamw(parts, w_, m_, v_, name="adamw_vnorm")

    order =("w_in_a", "w_in_b", "w_s", "b_s", "vnorm_g", "vnorm_b", "w_mem_kv", "w_out", "ln1_g", "ln1_b", "w_ff1", "w_ff2", "ln2_g", "ln2_b")
    return (loss, grad_x, *[results[n][0] for n in order], *[results[n][1] for n in order],
            *[results[n][2] for n in order], *[results[n][3] for n in order])
```

```python
import math

import jax
import jax.numpy as jnp
from jax import lax
from jax.experimental import pallas as pl
from jax.experimental.pallas import tpu as pltpu

F32 = jnp.float32
BF16 = jnp.bfloat16
N_DEV = 8
HEAD_DIM = 128
MEM_HEADS = 4
MEM_WIDTH = MEM_HEADS * HEAD_DIM
A_DILATIONS = (1, 4, 16)
LN_EPS = 1e-5
ADAM_LR, ADAM_B1, ADAM_B2, ADAM_EPS, ADAM_WD, ADAM_STEP = 0.001, 0.9, 0.999, 1e-08, 0.01, 10
VMEM_LIMIT_BYTES = 56 * 1024 * 1024
MESH = pl.DeviceIdType.MESH
ANY = pl.BlockSpec(memory_space=pl.ANY)


def _params(*sem):
    return pltpu.CompilerParams(dimension_semantics=sem, vmem_limit_bytes=VMEM_LIMIT_BYTES)


def _tile(dim, pref):
    if dim <= pref:
        return dim
    best = None
    for t in range(128, pref + 1, 128):
        if dim % t == 0:
            best = t
    return best if best is not None else dim


def _dot(a, b, dims):
    return lax.dot_general(a, b, (dims, ((), ())), preferred_element_type=F32)


NN = ((1,), (0,))
NT = ((1,), (1,))
TN = ((0,), (0,))


def _my_position():
    return lax.axis_index("x"), lax.axis_index("y"), lax.axis_index("c")


class GatherJob:
    US_PER_MB = 46.0
    OVERSHOOT = 1.26

    def __init__(self, arrays, tag):
        self.arrays, self.tag, self.outs = list(arrays), tag, None
        self.out_shapes = [jax.ShapeDtypeStruct((N_DEV,) + v.shape, v.dtype) for v in self.arrays]
        self.n_sems, self.n_local = 7 * len(self.arrays), len(self.arrays)
        self.est_us = self.US_PER_MB * sum(v.size * v.dtype.itemsize for v in self.arrays) / 1e6

    def phases(self, v_refs, out_refs, send_sems, recv_sems, local_sems, sem0, loc0):
        n = len(self.arrays)
        x, y, c = _my_position()
        me, sibling = (x, y, c), (x, y, 1 - c)
        chips = [(1 - x, y), (x, 1 - y), (1 - x, 1 - y)]

        def copy(a, k, block, to, from_input=False):
            px, py, pc = block
            slot = out_refs[a].at[4 * px + 2 * py + pc]
            return pltpu.make_async_remote_copy(
                src_ref=v_refs[a] if from_input else slot, dst_ref=slot, send_sem=send_sems.at[sem0 + 7 * a + k],
                recv_sem=recv_sems.at[sem0 + 7 * a + k], device_id=to, device_id_type=MESH)

        def mine(a):
            return pltpu.make_async_copy(v_refs[a], out_refs[a].at[4 * x + 2 * y + c], local_sems.at[loc0 + a])

        def first(a):
            return [copy(a, 0, me, sibling, True)] + [copy(a, 1 + j, me, (*chip, c), True) for j, chip in enumerate(chips)]

        def start():
            for a in range(n):
                mine(a).start()
                for cp in first(a):
                    cp.start()

        def middle():
            for j, chip in enumerate(chips):
                for a in range(n):
                    copy(a, 1 + j, (*chip, c), me).wait_recv()
                    copy(a, 4 + j, (*chip, c), sibling).start()

        def finish():
            for a in range(n):
                copy(a, 0, sibling, me).wait_recv()
                for j, chip in enumerate(chips):
                    copy(a, 4 + j, (*chip, 1 - c), me).wait_recv()
                for cp in first(a) + [copy(a, 4 + j, (*chip, c), sibling) for j, chip in enumerate(chips)]:
                    cp.wait_send()
                mine(a).wait()

        return start, middle, finish


class ExchangeJob:
    US_PER_MB = 11.0
    OVERSHOOT = 1.05
    CHUNK_US = 50.0
    RELATIONS = [(dx, dy, dc) for dx in (0, 1) for dy in (0, 1) for dc in (0, 1)][1:]

    def __init__(self, p, row0, rows, tag):
        self.arrays, self.tag, self.outs = [p], tag, None
        self.row0, self.rows = row0, rows
        self.out_shapes = [jax.ShapeDtypeStruct((N_DEV, rows, p.shape[2]), p.dtype)]
        self.n_sems, self.n_local = 7, 1
        self.est_us = self.US_PER_MB * N_DEV * rows * p.shape[2] * p.dtype.itemsize / 1e6

    def phases(self, p_refs, out_refs, send_sems, recv_sems, local_sems, sem0, loc0):
        (p_ref,), (out_ref,) = p_refs, out_refs
        x, y, c = _my_position()
        me = 4 * x + 2 * y + c
        chunk = pl.ds(self.row0, self.rows)

        def mine():
            return pltpu.make_async_copy(p_ref.at[me, chunk], out_ref.at[me], local_sems.at[loc0])

        def copies(arriving):
            made = []
            for k, (dx, dy, dc) in enumerate(self.RELATIONS):
                px, py, pc = (x + dx) % 2, (y + dy) % 2, (c + dc) % 2
                peer = 4 * px + 2 * py + pc
                made.append(pltpu.make_async_remote_copy(
                    src_ref=p_ref.at[peer, chunk], dst_ref=out_ref.at[peer if arriving else me],
                    send_sem=send_sems.at[sem0 + k], recv_sem=recv_sems.at[sem0 + k],
                    device_id=(px, py, pc), device_id_type=MESH))
            return made

        def start():
            mine().start()
            for send in copies(False):
                send.start()

        def finish():
            for arrival in copies(True):
                arrival.wait_recv()
            for send in copies(False):
                send.wait_send()
            mine().wait()

        return start, (lambda: None), finish


class Exchanges:
    MIN_HOST_US = 20.0
    MAX_JOBS = 6

    def __init__(self):
        self.queue = []

    def submit(self, job):
        self.queue.append(job)
        return job

    def take(self, host_us):
        jobs, used = [], 0.0
        if host_us >= self.MIN_HOST_US:
            for job in list(self.queue):
                if used + job.est_us <= job.OVERSHOOT * host_us and len(jobs) < self.MAX_JOBS:
                    used += job.est_us
                    jobs.append(job)
                    self.queue.remove(job)
        return jobs

    def require(self, *jobs):
        waiting = [job for job in jobs if job.outs is None]
        if waiting:
            for job in waiting:
                self.queue.remove(job)
            _call(lambda: None, grid=(), in_specs=[], out_specs=[], out_shape=[], scratch_shapes=[], args=[],
                  name="exchange", semantics=(), jobs=waiting)
        return [job.outs[0] for job in jobs]


def _call(body, *, grid, in_specs, out_specs, out_shape, scratch_shapes, args, name, semantics, jobs=()):
    if not jobs:
        return pl.pallas_call(body, grid=grid, in_specs=in_specs, out_specs=out_specs, out_shape=out_shape,
                              scratch_shapes=scratch_shapes, compiler_params=_params(*semantics), name=name)(*args)
    n_in, n_out, n_scr = len(in_specs), len(out_shape), len(scratch_shapes)
    j_in = [a for job in jobs for a in job.arrays]
    j_out = [s for job in jobs for s in job.out_shapes]
    n_sems, n_local = sum(job.n_sems for job in jobs), sum(job.n_local for job in jobs)
    steps = math.prod(grid)
    middle_step = (7 * steps) // 8

    def wrapped(*refs):
        ins, refs = refs[:n_in], refs[n_in:]
        j_ins, refs = refs[:len(j_in)], refs[len(j_in):]
        outs, refs = refs[:n_out], refs[n_out:]
        j_outs, refs = refs[:len(j_out)], refs[len(j_out):]
        scratch, (send_sems, recv_sems, local_sems) = refs[:n_scr], refs[n_scr:]
        step = 0
        for axis, extent in enumerate(grid):
            step = step * extent + pl.program_id(axis)
        phases, at_in, at_out, sem0, loc0 = [], 0, 0, 0, 0
        for job in jobs:
            k_in, k_out = len(job.arrays), len(job.out_shapes)
            phases.append(job.phases(j_ins[at_in:at_in + k_in], j_outs[at_out:at_out + k_out], send_sems, recv_sems, local_sems, sem0, loc0))
            at_in, at_out, sem0, loc0 = at_in + k_in, at_out + k_out, sem0 + job.n_sems, loc0 + job.n_local

        def run(which):
            for ph in phases:
                ph[which]()

        if steps == 1:
            run(0)
            body(*ins, *outs, *scratch)
            run(1)
            run(2)
        else:
            pl.when(step == 0)(lambda: run(0))
            body(*ins, *outs, *scratch)
            pl.when(step == middle_step)(lambda: run(1))
            pl.when(step == steps - 1)(lambda: run(2))

    res = pl.pallas_call(
        wrapped, grid=grid, in_specs=list(in_specs) + [ANY] * len(j_in), out_specs=list(out_specs) + [ANY] * len(j_out),
        out_shape=list(out_shape) + j_out,
        scratch_shapes=list(scratch_shapes) + [pltpu.SemaphoreType.DMA((n_sems,)), pltpu.SemaphoreType.DMA((n_sems,)),
                                               pltpu.SemaphoreType.DMA((n_local,))],
        compiler_params=_params(*(["arbitrary"] * len(grid))), name=name + "".join("__" + job.tag for job in jobs),
    )(*args, *j_in)
    at = n_out
    for job in jobs:
        job.outs = list(res[at:at + len(job.out_shapes)])
        at += len(job.out_shapes)
    return list(res[:n_out])


MATMUL_TILES = {"nn": (1024, 512, 2560), "nt": (1024, 512, 2560), "tn": (1024, 1024, 2048)}
MATMUL_FLOPS_PER_US = {"nn": 7.5e8, "nt": 7.5e8, "tn": 8.5e8}


def matmul(a, b, mode, name, epilogue=None, extras=(), out_dtypes=(F32,), comm=None):
    if mode == "nn":
        (m, k), (k2, n) = a.shape, b.shape
    elif mode == "nt":
        (m, k), (n, k2) = a.shape, b.shape
    else:
        (k, m), (k2, n) = a.shape, b.shape
    assert k == k2, (a.shape, b.shape, mode)
    tm_pref, tn_pref, tk_pref = MATMUL_TILES[mode]
    tm = _tile(m, tm_pref)
    if mode == "tn" and tm < tm_pref:
        tn_pref *= 2
    tn, tk = _tile(n, tn_pref), _tile(k, tk_pref)
    nk = k // tk
    dims = {"nn": NN, "nt": NT, "tn": TN}[mode]
    a_spec = pl.BlockSpec((tk, tm), lambda i, j, kk: (kk, i)) if mode == "tn" else pl.BlockSpec((tm, tk), lambda i, j, kk: (i, kk))
    b_spec = pl.BlockSpec((tn, tk), lambda i, j, kk: (j, kk)) if mode == "nt" else pl.BlockSpec((tk, tn), lambda i, j, kk: (kk, j))
    tile_spec = pl.BlockSpec((tm, tn), lambda i, j, kk: (i, j))
    n_ex, n_out = len(extras), len(out_dtypes)
    if epilogue is None:
        epilogue = lambda acc: (acc,) * n_out

    def body(a_ref, b_ref, *rest):
        ex_refs, out_refs, acc_ref = rest[:n_ex], rest[n_ex:n_ex + n_out], rest[-1]
        kk = pl.program_id(2)

        def product():
            return _dot(a_ref[...].astype(BF16), b_ref[...].astype(BF16), dims)

        def finish(acc):
            for o_ref, val in zip(out_refs, epilogue(acc, *[e[...] for e in ex_refs])):
                o_ref[...] = val.astype(o_ref.dtype)

        if nk == 1:
            finish(product())
            return

        @pl.when(kk == 0)
        def _():
            acc_ref[...] = product()

        @pl.when((kk > 0) & (kk < nk - 1))
        def _():
            acc_ref[...] += product()

        @pl.when(kk == nk - 1)
        def _():
            finish(acc_ref[...] + product())

    host_us = 2.0 * m * n * k / MATMUL_FLOPS_PER_US[mode]
    outs = _call(
        body,
        grid=(m // tm, n // tn, nk),
        in_specs=[a_spec, b_spec] + [tile_spec] * n_ex,
        out_specs=[tile_spec] * n_out,
        out_shape=[jax.ShapeDtypeStruct((m, n), dt) for dt in out_dtypes],
        scratch_shapes=[pltpu.VMEM((tm, tn), F32)],
        args=[a, b, *extras], name=name, semantics=("parallel", "parallel", "arbitrary"),
        jobs=comm.take(host_us) if comm is not None else (),
    )
    return outs[0] if n_out == 1 else outs


LN_BYTES_PER_US = 3.0e6


def ln_residual(x, y, g, b, alpha, name, comm=None):
    s, w = x.shape
    tb = _tile(s, 256)
    row = pl.BlockSpec((tb, w), lambda i: (i, 0))
    vec = pl.BlockSpec((1, w), lambda i: (0, 0))

    def body(x_ref, y_ref, g_ref, b_ref, r_ref, xn_ref, xnb_ref):
        r = alpha * x_ref[...] + y_ref[...]
        mu = jnp.mean(r, axis=-1, keepdims=True)
        var = jnp.mean(jnp.square(r - mu), axis=-1, keepdims=True)
        xn = (r - mu) * lax.rsqrt(var + LN_EPS) * g_ref[...] + b_ref[...]
        r_ref[...] = r
        xn_ref[...] = xn
        xnb_ref[...] = xn.astype(BF16)

    return _call(
        body, grid=(s // tb,), in_specs=[row, row, vec, vec], out_specs=[row, row, row],
        out_shape=[jax.ShapeDtypeStruct((s, w), F32), jax.ShapeDtypeStruct((s, w), F32), jax.ShapeDtypeStruct((s, w), BF16)],
        scratch_shapes=[], args=[x, y, g.reshape(1, w), b.reshape(1, w)], name=name, semantics=("parallel",),
        jobs=comm.take(18.0 * s * w / LN_BYTES_PER_US) if comm is not None else (),
    )


def _ln_bwd_tile(r, g, dxn):
    mu = jnp.mean(r, axis=-1, keepdims=True)
    cen = r - mu
    rstd = lax.rsqrt(jnp.mean(jnp.square(cen), axis=-1, keepdims=True) + LN_EPS)
    xhat = cen * rstd
    dxh = dxn * g
    dr = rstd * (dxh - jnp.mean(dxh, axis=-1, keepdims=True) - xhat * jnp.mean(dxh * xhat, axis=-1, keepdims=True))
    return dr, jnp.sum(dxn * xhat, axis=0, keepdims=True)


def ln_backward(r, g, dxn, name, comm=None):
    s, w = r.shape
    tb = _tile(s, 256)
    row = pl.BlockSpec((tb, w), lambda i: (i, 0))
    vec = pl.BlockSpec((1, w), lambda i: (0, 0))

    def body(r_ref, g_ref, d_ref, dr_ref, drb_ref, dg_ref, db_ref):
        @pl.when(pl.program_id(0) == 0)
        def _():
            dg_ref[...] = jnp.zeros_like(dg_ref)
            db_ref[...] = jnp.zeros_like(db_ref)

        dxn = d_ref[...]
        dr, dg = _ln_bwd_tile(r_ref[...], g_ref[...], dxn)
        dr_ref[...] = dr
        drb_ref[...] = dr.astype(BF16)
        dg_ref[...] += dg
        db_ref[...] += jnp.sum(dxn, axis=0, keepdims=True)

    return _call(
        body, grid=(s // tb,), in_specs=[row, vec, row], out_specs=[row, row, vec, vec],
        out_shape=[jax.ShapeDtypeStruct((s, w), F32), jax.ShapeDtypeStruct((s, w), BF16),
                   jax.ShapeDtypeStruct((1, w), F32), jax.ShapeDtypeStruct((1, w), F32)],
        scratch_shapes=[], args=[r, g.reshape(1, w), dxn], name=name, semantics=("arbitrary",),
        jobs=comm.take(14.0 * s * w / LN_BYTES_PER_US) if comm is not None else (),
    )


ATTENTION_US = {("fwd", 1): 55.0, ("fwd", 4): 80.0, ("fwd", 16): 100.0, ("bwd", 1): 115.0, ("bwd", 4): 130.0, ("bwd", 16): 190.0}
ATTENTION_HEADS_PER_STEP = {("fwd", 1): 8, ("fwd", 4): 1, ("fwd", 16): 1, ("bwd", 1): 8, ("bwd", 4): 1, ("bwd", 16): 1}


def _causal_masks():
    qi = lax.broadcasted_iota(jnp.int32, (128, 128), 0)
    kj = lax.broadcasted_iota(jnp.int32, (128, 128), 1)
    return kj <= qi, kj >= qi


def _window_mask(has_previous):
    qi = lax.broadcasted_iota(jnp.int32, (128, 256), 0)
    kj = lax.broadcasted_iota(jnp.int32, (128, 256), 1)
    return ((kj < 128) & (kj >= qi) & has_previous) | ((kj >= 128) & (kj - 128 <= qi))


def _sub_rows(d, r):
    return pl.ds(r, 128, stride=d) if d > 1 else pl.ds(0, 128)


def _for_each_residue(d, fn):
    if d <= 4:
        for r in range(d):
            fn(r)
    else:
        def step(r, carry):
            fn(r)
            return carry
        lax.fori_loop(0, d, step, 0, unroll=8)


def attention_forward(proj, group, d, heads, name, comm=None):
    s = proj.shape[0]
    rows = 128 * d
    nb = s // rows
    scale = HEAD_DIM ** -0.5
    hp = math.gcd(heads, ATTENTION_HEADS_PER_STEP["fwd", d])
    wide = 128 * hp
    qc, kc, vc = (group * 3) * heads // hp, (group * 3 + 1) * heads // hp, (group * 3 + 2) * heads // hp

    def cur(col):
        return pl.BlockSpec((rows, wide), lambda h, n: (n, col + h))

    def prev(col):
        return pl.BlockSpec((rows, wide), lambda h, n: (jnp.maximum(n - 1, 0), col + h))

    out = pl.BlockSpec((rows, wide), lambda h, n: (n, h))

    def body(q_ref, kc_ref, kp_ref, vc_ref, vp_ref, o_ref, l_ref):
        mask = _window_mask(pl.program_id(1) > 0)

        def one(r):
            rws = _sub_rows(d, r)
            for hh in range(hp):
                sl = (rws, pl.ds(hh * 128, 128))
                k = jnp.concatenate([kp_ref[sl].astype(BF16), kc_ref[sl].astype(BF16)], axis=0)
                v = jnp.concatenate([vp_ref[sl].astype(BF16), vc_ref[sl].astype(BF16)], axis=0)
                sc = jnp.where(mask, _dot(q_ref[sl].astype(BF16), k, NT) * scale, -jnp.inf)
                m = jnp.max(sc, axis=-1, keepdims=True)
                e = jnp.exp(sc - m)
                l = jnp.sum(e, axis=-1, keepdims=True)
                o_ref[sl] = _dot(e.astype(BF16), v, NN) / l
                l_ref[sl] = jnp.broadcast_to(m + jnp.log(l), (128, 128))

        _for_each_residue(d, one)

    return _call(
        body, grid=(heads // hp, nb),
        in_specs=[cur(qc), cur(kc), prev(kc), cur(vc), prev(vc)], out_specs=[out, out],
        out_shape=[jax.ShapeDtypeStruct((s, heads * 128), F32)] * 2, scratch_shapes=[],
        args=[proj] * 5, name=name, semantics=("parallel", "parallel"),
        jobs=comm.take(ATTENTION_US["fwd", d] * s * heads / (4096 * 8)) if comm is not None else (),
    )


def attention_combine(os_, ls_, name):
    s, w = os_[0].shape
    tb = _tile(s, 256)
    row = pl.BlockSpec((tb, w), lambda i: (i, 0))

    def body(o0, o1, o2, l0, l1, l2, o_ref, ob_ref, lse_ref):
        a, b, c = l0[...], l1[...], l2[...]
        m = jnp.maximum(jnp.maximum(a, b), c)
        ea, eb, ec = jnp.exp(a - m), jnp.exp(b - m), jnp.exp(c - m)
        tot = ea + eb + ec
        o = (ea / tot) * o0[...] + (eb / tot) * o1[...] + (ec / tot) * o2[...]
        o_ref[...] = o
        ob_ref[...] = o.astype(BF16)
        lse_ref[...] = m + jnp.log(tot)

    return pl.pallas_call(
        body, grid=(s // tb,), in_specs=[row] * 6, out_specs=[row] * 3,
        out_shape=[jax.ShapeDtypeStruct((s, w), F32), jax.ShapeDtypeStruct((s, w + MEM_WIDTH), BF16), jax.ShapeDtypeStruct((s, w), F32)],
        compiler_params=_params("parallel"), name=name,
    )(*os_, *ls_)


def attention_backward(proj, o, lse, dcat, group, d, heads, name, comm=None):
    s = proj.shape[0]
    rows = 128 * d
    nb = s // rows
    scale = HEAD_DIM ** -0.5
    hp = math.gcd(heads, ATTENTION_HEADS_PER_STEP["bwd", d])
    wide = 128 * hp
    qc, kc, vc = (group * 3) * heads // hp, (group * 3 + 1) * heads // hp, (group * 3 + 2) * heads // hp

    def at(col, shift):
        return pl.BlockSpec((rows, wide), lambda h, n: (jnp.clip(n + shift, 0, nb - 1), col + h))

    def out(col):
        return pl.BlockSpec((rows, wide), lambda h, n: (n, col + h))

    def body(qc_ref, qn_ref, kc_ref, kp_ref, vc_ref, vp_ref, doc_ref, don_ref, oc_ref, on_ref, lc_ref, ln_ref,
             dq_ref, dk_ref, dv_ref):
        n = pl.program_id(1)
        mask_w = _window_mask(n > 0)
        mask_n = _causal_masks()[1] & (n < nb - 1)

        def tile(q, k, v, do, lse_t, dsum, mask):
            p = jnp.where(mask, jnp.exp(_dot(q, k, NT) * scale - lse_t), 0.0)
            ds = (p * (_dot(do, v, NT) - dsum) * scale).astype(BF16)
            return p.astype(BF16), ds

        def one(r):
            rws = _sub_rows(d, r)
            for hh in range(hp):
                sl = (rws, pl.ds(hh * 128, 128))
                q_c, q_n = qc_ref[sl].astype(BF16), qn_ref[sl].astype(BF16)
                k_c, v_c = kc_ref[sl].astype(BF16), vc_ref[sl].astype(BF16)
                k_w = jnp.concatenate([kp_ref[sl].astype(BF16), k_c], axis=0)
                v_w = jnp.concatenate([vp_ref[sl].astype(BF16), v_c], axis=0)
                do_c, do_n = doc_ref[sl], don_ref[sl]
                dsum_c = jnp.sum(do_c * oc_ref[sl], axis=-1, keepdims=True)
                dsum_n = jnp.sum(do_n * on_ref[sl], axis=-1, keepdims=True)
                do_c, do_n = do_c.astype(BF16), do_n.astype(BF16)
                lse_c = lc_ref[sl]
                p_w, ds_w = tile(q_c, k_w, v_w, do_c, jnp.concatenate([lse_c, lse_c], axis=1), dsum_c, mask_w)
                p_n, ds_n = tile(q_n, k_c, v_c, do_n, ln_ref[sl], dsum_n, mask_n)
                dq_ref[sl] = _dot(ds_w, k_w, NN)
                both_q = jnp.concatenate([q_c, q_n], axis=0)
                both_do = jnp.concatenate([do_c, do_n], axis=0)
                dk_ref[sl] = _dot(jnp.concatenate([ds_w[:, 128:], ds_n], axis=0), both_q, TN)
                dv_ref[sl] = _dot(jnp.concatenate([p_w[:, 128:], p_n], axis=0), both_do, TN)

        _for_each_residue(d, one)

    w = heads * 128
    return _call(
        body, grid=(heads // hp, nb),
        in_specs=[at(qc, 0), at(qc, 1), at(kc, 0), at(kc, -1), at(vc, 0), at(vc, -1),
                  at(0, 0), at(0, 1), at(0, 0), at(0, 1), at(0, 0), at(0, 1)],
        out_specs=[out(0)] * 3,
        out_shape=[jax.ShapeDtypeStruct((s, w), F32)] * 3, scratch_shapes=[],
        args=[proj] * 6 + [dcat, dcat, o, o, lse, lse], name=name, semantics=("parallel", "parallel"),
        jobs=comm.take(ATTENTION_US["bwd", d] * s * heads / (4096 * 8)) if comm is not None else (),
    )


def _mem_softmax(q, kv, h, scale):
    k = kv[:, h * 128:(h + 1) * 128].astype(BF16)
    v = kv[:, MEM_WIDTH + h * 128:MEM_WIDTH + (h + 1) * 128].astype(BF16)
    sc = _dot(q, k, NT) * scale
    e = jnp.exp(sc - jnp.max(sc, axis=-1, keepdims=True))
    return e / jnp.sum(e, axis=-1, keepdims=True), k, v


def memory_attention(proj, qcol, kv, cat, name):
    s = proj.shape[0]
    tb = _tile(s, 512)
    scale = HEAD_DIM ** -0.5

    def body(q_ref, kv_ref, cat_ref, o_ref):
        kv_t = kv_ref[...]
        for h in range(MEM_HEADS):
            p, _, v = _mem_softmax(q_ref[:, h * 128:(h + 1) * 128].astype(BF16), kv_t, h, scale)
            o_ref[:, h * 128:(h + 1) * 128] = _dot(p.astype(BF16), v, NN).astype(BF16)

    return pl.pallas_call(
        body, grid=(s // tb,),
        in_specs=[pl.BlockSpec((tb, MEM_WIDTH), lambda i: (i, qcol // MEM_WIDTH)), pl.BlockSpec(kv.shape, lambda i: (0, 0)), ANY],
        out_specs=pl.BlockSpec((tb, MEM_WIDTH), lambda i: (i, cat.shape[1] // MEM_WIDTH - 1)),
        out_shape=jax.ShapeDtypeStruct(cat.shape, BF16), input_output_aliases={2: 0},
        compiler_params=_params("parallel"), name=name,
    )(proj, kv, cat)


def memory_attention_backward(proj, qcol, kv, dcat, dcol, name):
    s = proj.shape[0]
    tb = _tile(s, 512)
    scale = HEAD_DIM ** -0.5

    def body(q_ref, kv_ref, do_ref, dq_ref, dkv_ref):
        @pl.when(pl.program_id(0) == 0)
        def _():
            dkv_ref[...] = jnp.zeros_like(dkv_ref)

        kv_t = kv_ref[...]
        for h in range(MEM_HEADS):
            cols = slice(h * 128, (h + 1) * 128)
            q = q_ref[:, cols].astype(BF16)
            p, k, v = _mem_softmax(q, kv_t, h, scale)
            do = do_ref[:, cols].astype(BF16)
            dp = _dot(do, v, NT)
            ds = (p * (dp - jnp.sum(dp * p, axis=-1, keepdims=True)) * scale).astype(BF16)
            dq_ref[:, cols] = _dot(ds, k, NN).astype(BF16)
            dkv_ref[:, cols] += _dot(ds, q, TN)
            dkv_ref[:, MEM_WIDTH + h * 128:MEM_WIDTH + (h + 1) * 128] += _dot(p.astype(BF16), do, TN)

    return pl.pallas_call(
        body, grid=(s // tb,),
        in_specs=[pl.BlockSpec((tb, MEM_WIDTH), lambda i: (i, qcol // MEM_WIDTH)), pl.BlockSpec(kv.shape, lambda i: (0, 0)),
                  pl.BlockSpec((tb, MEM_WIDTH), lambda i: (i, dcol // MEM_WIDTH))],
        out_specs=[pl.BlockSpec((tb, MEM_WIDTH), lambda i: (i, 0)), pl.BlockSpec(kv.shape, lambda i: (0, 0))],
        out_shape=[jax.ShapeDtypeStruct((s, MEM_WIDTH), BF16), jax.ShapeDtypeStruct(kv.shape, F32)],
        compiler_params=_params("arbitrary"), name=name,
    )(proj, kv, dcat)


_SQRT_HALF = math.sqrt(0.5)
_INV_SQRT_2PI = 1.0 / math.sqrt(2.0 * math.pi)


def _gelu(x):
    return 0.5 * x * (1.0 + lax.erf(x * _SQRT_HALF))


def _gelu_grad(x):
    return 0.5 * (1.0 + lax.erf(x * _SQRT_HALF)) + x * (_INV_SQRT_2PI * jnp.exp(-0.5 * x * x))


def _gmlp_specs(s, wd, groups, tb):
    half = lambda c: pl.BlockSpec((tb, wd), lambda i: (i, c))
    ws_spec = pl.BlockSpec((groups, 128, 128), lambda i: (0, 0, 0))
    bs_spec = pl.BlockSpec((groups, 128, 1), lambda i: (0, 0, 0))
    vec = pl.BlockSpec((1, wd), lambda i: (0, 0))
    return half, ws_spec, bs_spec, vec


def _vnorm(zv, g, b):
    mu = jnp.mean(zv, axis=-1, keepdims=True)
    var = jnp.mean(jnp.square(zv - mu), axis=-1, keepdims=True)
    return (zv - mu) * lax.rsqrt(var + LN_EPS) * g + b


def gmlp_forward(proj, ws, bs, vg, vb, name):
    s = proj.shape[0]
    groups = ws.shape[0]
    wd = groups * 128
    tb = _tile(s, 512)
    half, ws_spec, bs_spec, vec = _gmlp_specs(s, wd, groups, tb)

    def body(pu_ref, pv_ref, ws_ref, bs_ref, vg_ref, vb_ref, o_ref, vn_ref):
        causal, _ = _causal_masks()
        vn_ref[...] = _vnorm(_gelu(pv_ref[...]), vg_ref[...], vb_ref[...]).astype(BF16)
        for g in range(groups):
            cols = slice(g * 128, (g + 1) * 128)
            wm = jnp.where(causal, ws_ref[g], 0.0).astype(BF16)
            for c in range(tb // 128):
                rws = slice(c * 128, (c + 1) * 128)
                sg = _dot(wm, vn_ref[rws, cols], NN) + bs_ref[g]
                o_ref[rws, cols] = (_gelu(pu_ref[rws, cols]) * sg).astype(BF16)

    return pl.pallas_call(
        body, grid=(s // tb,),
        in_specs=[half(0), half(1), ws_spec, bs_spec, vec, vec], out_specs=pl.BlockSpec((tb, wd), lambda i: (i, 0)),
        out_shape=jax.ShapeDtypeStruct((s, wd + MEM_WIDTH), BF16),
        scratch_shapes=[pltpu.VMEM((tb, wd), BF16)],
        compiler_params=_params("parallel"), name=name,
    )(proj, proj, ws, bs.reshape(groups, 128, 1), vg.reshape(1, wd), vb.reshape(1, wd))


def gmlp_backward(proj, ws, bs, vg, vb, dcat, name):
    s = proj.shape[0]
    groups = ws.shape[0]
    wd = groups * 128
    tb = _tile(s, 512)
    half, ws_spec, bs_spec, vec = _gmlp_specs(s, wd, groups, tb)

    def body(pu_ref, pv_ref, ws_ref, bs_ref, vg_ref, vb_ref, do_ref, dpu_ref, dpv_ref, dws_ref, dbs_ref, dvg_ref, dvb_ref,
             vn_ref, dvn_ref):
        @pl.when(pl.program_id(0) == 0)
        def _():
            dws_ref[...] = jnp.zeros_like(dws_ref)
            dbs_ref[...] = jnp.zeros_like(dbs_ref)
            dvg_ref[...] = jnp.zeros_like(dvg_ref)
            dvb_ref[...] = jnp.zeros_like(dvb_ref)

        causal, _ = _causal_masks()
        pv = pv_ref[...]
        zv = _gelu(pv)
        vn_ref[...] = _vnorm(zv, vg_ref[...], vb_ref[...]).astype(BF16)
        for g in range(groups):
            cols = slice(g * 128, (g + 1) * 128)
            wm = jnp.where(causal, ws_ref[g], 0.0).astype(BF16)
            dws_g = jnp.zeros((128, 128), F32)
            dbs_g = jnp.zeros((128, 1), F32)
            for c in range(tb // 128):
                rws = slice(c * 128, (c + 1) * 128)
                vn = vn_ref[rws, cols]
                pu = pu_ref[rws, cols]
                do = do_ref[rws, cols]
                sg = _dot(wm, vn, NN) + bs_ref[g]
                dpu_ref[rws, cols] = (do * sg * _gelu_grad(pu)).astype(BF16)
                dsg = do * _gelu(pu)
                dsg_b = dsg.astype(BF16)
                dws_g += _dot(dsg_b, vn, NT)
                dbs_g += jnp.sum(dsg, axis=-1, keepdims=True)
                dvn_ref[rws, cols] = _dot(wm, dsg_b, TN)
            dws_ref[g] += jnp.where(causal, dws_g, 0.0)
            dbs_ref[g] += dbs_g
        dvn = dvn_ref[...]
        dzv, dvg = _ln_bwd_tile(zv, vg_ref[...], dvn)
        dvg_ref[...] += dvg
        dvb_ref[...] += jnp.sum(dvn, axis=0, keepdims=True)
        dpv_ref[...] = (dzv * _gelu_grad(pv)).astype(BF16)

    row = pl.BlockSpec((tb, wd), lambda i: (i, 0))
    dpu, dpv, dws, dbs, dvg, dvb = pl.pallas_call(
        body, grid=(s // tb,),
        in_specs=[half(0), half(1), ws_spec, bs_spec, vec, vec, row],
        out_specs=[row, row, ws_spec, bs_spec, vec, vec],
        out_shape=[jax.ShapeDtypeStruct((s, wd), BF16), jax.ShapeDtypeStruct((s, wd), BF16),
                   jax.ShapeDtypeStruct((groups, 128, 128), F32), jax.ShapeDtypeStruct((groups, 128, 1), F32),
                   jax.ShapeDtypeStruct((1, wd), F32), jax.ShapeDtypeStruct((1, wd), F32)],
        scratch_shapes=[pltpu.VMEM((tb, wd), BF16), pltpu.VMEM((tb, wd), F32)],
        compiler_params=_params("arbitrary"), name=name,
    )(proj, proj, ws, bs.reshape(groups, 128, 1), vg.reshape(1, wd), vb.reshape(1, wd), dcat)
    return dpu, dpv, dws, dbs.reshape(groups, 128), dvg, dvb


def loss_head(y, target, name):
    s, w = y.shape
    tb = _tile(s, 256)
    row = pl.BlockSpec((tb, w), lambda i: (i, 0))
    nsteps = s // tb

    def body(y_ref, t_ref, loss_ref, dy_ref, acc_ref):
        i = pl.program_id(0)

        @pl.when(i == 0)
        def _():
            acc_ref[...] = jnp.zeros_like(acc_ref)

        err = y_ref[...] - t_ref[...]
        dy_ref[...] = err / w
        acc_ref[...] += jnp.sum(jnp.mean(jnp.square(err), axis=-1, keepdims=True), axis=0, keepdims=True)

        @pl.when(i == nsteps - 1)
        def _():
            loss_ref[...] = jnp.broadcast_to(0.5 * acc_ref[...], loss_ref.shape)

    return pl.pallas_call(
        body, grid=(nsteps,), in_specs=[row, row],
        out_specs=[pl.BlockSpec((8, 128), lambda i: (0, 0)), row],
        out_shape=[jax.ShapeDtypeStruct((8, 128), F32), jax.ShapeDtypeStruct((s, w), F32)],
        scratch_shapes=[pltpu.VMEM((1, 1), F32)],
        compiler_params=_params("arbitrary"), name=name,
    )(y, target)


PARTS_WINDOW_ELEMS = 1024 * 1024


def _row_tile(r, c, budget):
    if r * c <= budget or r % 16:
        return r
    fits = [t for t in range(16, r, 16) if r % t == 0 and t * c <= budget]
    return max(fits) if fits else 16


def _sum_in_device_order(p_ref):
    g = p_ref[0].astype(F32)
    for j in range(1, p_ref.shape[0]):
        g = g + p_ref[j].astype(F32)
    return g


def _adamw_update(g, w, m, v):
    nm = ADAM_B1 * m + (1.0 - ADAM_B1) * g
    nv = ADAM_B2 * v + (1.0 - ADAM_B2) * jnp.square(g)
    m_hat = nm / (1.0 - ADAM_B1 ** ADAM_STEP)
    v_hat = nv / (1.0 - ADAM_B2 ** ADAM_STEP)
    return -ADAM_LR * (m_hat / (jnp.sqrt(v_hat) + ADAM_EPS) + ADAM_WD * w), nm, nv


SUM_BYTES_PER_US = 1.7e6


def sum_parts(chunks, name, comm=None):
    p, r, c = chunks[0].shape
    tr = _row_tile(r, c, PARTS_WINDOW_ELEMS // len(chunks))
    nb = r // tr

    def chunk_spec(q):
        return pl.BlockSpec((p, tr, c), lambda ch, i: (0, jnp.where(ch == q, i, 0), 0))

    def body(*refs):
        for q in range(len(chunks)):
            @pl.when(pl.program_id(0) == q)
            def _():
                refs[-1][...] = _sum_in_device_order(refs[q])

    host_us = len(chunks) * r * c * (p * chunks[0].dtype.itemsize + 4) / SUM_BYTES_PER_US
    return _call(
        body, grid=(len(chunks), nb), in_specs=[chunk_spec(q) for q in range(len(chunks))],
        out_specs=[pl.BlockSpec((tr, c), lambda ch, i: (ch * nb + i, 0))],
        out_shape=[jax.ShapeDtypeStruct((len(chunks) * r, c), F32)], scratch_shapes=[],
        args=list(chunks), name=name, semantics=("arbitrary", "arbitrary"),
        jobs=comm.take(host_us) if comm is not None else (),
    )[0]


ELEMENTWISE_BYTES_PER_US = 2.0e6


def adamw(parts, w, m, v, name, emit_grad=True, comm=None):
    p, r, c = parts.shape
    tr = _row_tile(r, c, 160 * 1024)
    n_out = 4 if emit_grad else 3

    def body(p_ref, w_ref, m_ref, v_ref, *out_refs):
        g = _sum_in_device_order(p_ref)
        vals = _adamw_update(g, w_ref[...], m_ref[...], v_ref[...])
        for o_ref, val in zip(out_refs, ((g,) + vals) if emit_grad else vals):
            o_ref[...] = val

    row = pl.BlockSpec((tr, c), lambda i: (i, 0))
    host_us = (parts.size * parts.dtype.itemsize + (3 + n_out) * 4 * r * c) / ELEMENTWISE_BYTES_PER_US
    return _call(
        body, grid=(r // tr,), in_specs=[pl.BlockSpec((p, tr, c), lambda i: (0, i, 0)), row, row, row],
        out_specs=[row] * n_out, out_shape=[jax.ShapeDtypeStruct((r, c), F32)] * n_out, scratch_shapes=[],
        args=[parts, w, m, v], name=name, semantics=("parallel",), jobs=comm.take(host_us) if comm is not None else (),
    )


def kernel(x, mem, w_in_a, w_in_b, w_s, b_s, vnorm_g, vnorm_b, w_mem_kv, w_out, ln1_g, ln1_b, w_ff1, w_ff2, ln2_g, ln2_b, loss_target, m_w_in_a, m_w_in_b, m_w_s, m_b_s, m_vnorm_g, m_vnorm_b, m_w_mem_kv, m_w_out, m_ln1_g, m_ln1_b, m_w_ff1, m_w_ff2, m_ln2_g, m_ln2_b, v_w_in_a, v_w_in_b, v_w_s, v_b_s, v_vnorm_g, v_vnorm_b, v_w_mem_kv, v_w_out, v_ln1_g, v_ln1_b, v_w_ff1, v_w_ff2, v_ln2_g, v_ln2_b):
    depth = w_ff1.shape[0]
    groups = w_s.shape[1]
    mix_w = groups * HEAD_DIM
    n_b = vnorm_g.shape[0]
    alpha = (2.0 * depth) ** 0.25
    me = 4 * lax.axis_index("x") + 2 * lax.axis_index("y") + lax.axis_index("c")
    x0 = x[0]
    mem_b = mem[0].astype(BF16)
    target = loss_target[0]

    comm = Exchanges()
    weight_jobs = []
    for i in range(depth):
        w_in = (w_in_a if i % 2 == 0 else w_in_b)[i // 2]
        groups_of_shards = ([w_in.T], [w_mem_kv[i], w_out[i].T], [w_ff1[i].T], [w_ff2[i]])
        weight_jobs.append([GatherJob([t.astype(BF16) for t in shards], f"g{i}{tag}")
                            for shards, tag in zip(groups_of_shards, ("in", "out", "ff1", "ff2"))])
    weight_jobs[0][0] = GatherJob(weight_jobs[0][0].arrays + [jnp.concatenate([vnorm_g, vnorm_b], axis=0)], "g0in")
    for jobs in weight_jobs:
        for job in jobs:
            comm.submit(job)

    def gathered(job):
        comm.require(job)
        return [g.reshape(N_DEV * g.shape[1], g.shape[2]) for g in job.outs]

    vnorm = gathered(weight_jobs[0][0])[1].reshape(N_DEV, 2 * n_b, mix_w // N_DEV)
    vnorm = jnp.transpose(vnorm, (1, 0, 2)).reshape(2 * n_b, mix_w)
    vg_full, vb_full = vnorm[:n_b], vnorm[n_b:]

    saved = []
    weights = []
    xf, xb = x0, x0.astype(BF16)
    for i in range(depth):
        j = i // 2
        w_in_t = gathered(weight_jobs[i][0])[0]
        proj = matmul(xb, w_in_t, "nt", name=f"in_proj_{i % 2}", comm=comm)
        if i % 2 == 0:
            os_, ls_ = [], []
            for g, d in enumerate(A_DILATIONS):
                o_g, l_g = attention_forward(proj, g, d, groups, name=f"attn_fwd_d{d}", comm=comm)
                os_.append(o_g)
                ls_.append(l_g)
            o, cat, lse = attention_combine(os_, ls_, name="attn_combine")
            qcol = 9 * mix_w
            extra = (o, lse)
        else:
            cat = gmlp_forward(proj, w_s[j], b_s[j], vg_full[j], vb_full[j], name="gmlp_fwd")
            qcol = 2 * mix_w
            extra = ()
        w_kv, w_o_t = gathered(weight_jobs[i][1])
        kv = matmul(mem_b, w_kv, "nn", name="mem_kv")
        cat = memory_attention(proj, qcol, kv, cat, name=f"mem_attn_{i % 2}")
        y1 = matmul(cat, w_o_t, "nt", name="out_proj", comm=comm)
        r1, x1, x1b = ln_residual(xf, y1, ln1_g[i], ln1_b[i], alpha, name="ln1", comm=comm)
        (w_1_t,) = gathered(weight_jobs[i][2])
        h, hid = matmul(x1b, w_1_t, "nt", name="ff1", out_dtypes=(F32, BF16), comm=comm,
                        epilogue=lambda acc: (acc, jnp.square(jnp.maximum(acc, 0.0))))
        (w_2,) = gathered(weight_jobs[i][3])
        y2 = matmul(hid, w_2, "nn", name="ff2", comm=comm)
        r2, x2, x2b = ln_residual(x1, y2, ln2_g[i], ln2_b[i], alpha, name="ln2", comm=comm)
        weights.append((w_in_t, w_o_t, w_1_t, w_2, w_kv))
        saved.append((xb, proj, kv, extra, qcol, cat, r1, x1b, h, hid, r2))
        xf, xb = x2, x2b

    loss_tile, dx = loss_head(xf, target, name="loss_head")
    loss = lax.psum(loss_tile[0, 0], ("x", "y", "c"))

    def exchange(dw, tag):
        rows, cols = dw.shape[0] // N_DEV, dw.shape[1]
        p = dw.reshape(N_DEV, rows, cols)
        whole_us = ExchangeJob.US_PER_MB * dw.size * dw.dtype.itemsize / 1e6
        counts = [q for q in range(1, rows // 16 + 1) if (rows // 16) % q == 0]
        n_chunks = next((q for q in counts if whole_us <= ExchangeJob.CHUNK_US * q), counts[-1])
        step = rows // n_chunks
        return [comm.submit(ExchangeJob(p, q * step, step, f"x{tag}{q}")) for q in range(n_chunks)]

    recv = {n: [None] * depth for n in ("w_in", "w_mem_kv", "w_out", "w_ff1", "w_ff2")}
    small = {n: [None] * depth for n in ("ln1_g", "ln1_b", "ln2_g", "ln2_b")}
    small_b = {n: [None] * n_b for n in ("w_s", "b_s", "vnorm_g", "vnorm_b")}
    for i in reversed(range(depth)):
        w_in_t, w_o_t, w_1_t, w_2, w_kv = weights[i]
        xb, proj, kv, extra, qcol, cat, r1, x1b, h, hid, r2 = saved[i]
        j = i // 2
        dr2, dr2b, small["ln2_g"][i], small["ln2_b"][i] = ln_backward(r2, ln2_g[i], dx, name="ln2_bwd", comm=comm)
        dh = matmul(dr2b, w_2, "nt", name="ff2_dx", extras=(h,), out_dtypes=(BF16,), comm=comm,
                    epilogue=lambda acc, h_t: (acc * (2.0 * jnp.maximum(h_t, 0.0)),))
        recv["w_ff2"][i] = exchange(matmul(hid, dr2b, "tn", name="ff2_dw", out_dtypes=(BF16,), comm=comm), f"ff2{i}")
        dx1 = matmul(dh, w_1_t, "nn", name="ff1_dx", extras=(dr2,), comm=comm, epilogue=lambda acc, res: (acc + alpha * res,))
        recv["w_ff1"][i] = exchange(matmul(dh, x1b, "tn", name="ff1_dw", out_dtypes=(BF16,), comm=comm), f"ff1{i}")
        dr1, dr1b, small["ln1_g"][i], small["ln1_b"][i] = ln_backward(r1, ln1_g[i], dx1, name="ln1_bwd", comm=comm)
        dcat = matmul(dr1b, w_o_t, "nn", name="out_proj_dx", comm=comm)
        recv["w_out"][i] = exchange(matmul(dr1b, cat, "tn", name="out_proj_dw", out_dtypes=(BF16,), comm=comm), f"out{i}")
        dq_mem, dkv = memory_attention_backward(proj, qcol, kv, dcat, mix_w, name=f"mem_attn_bwd_{i % 2}")
        recv["w_mem_kv"][i] = exchange(matmul(mem_b, dkv, "tn", name="mem_kv_dw", out_dtypes=(BF16,)), f"kv{i}")
        if i % 2 == 0:
            o, lse = extra
            pieces = []
            for g, d in enumerate(A_DILATIONS):
                pieces += [t.astype(BF16) for t in attention_backward(proj, o, lse, dcat, g, d, groups, name=f"attn_bwd_d{d}", comm=comm)]
            dproj = jnp.concatenate(pieces + [dq_mem], axis=1)
        else:
            dpu, dpv, small_b["w_s"][j], small_b["b_s"][j], small_b["vnorm_g"][j], small_b["vnorm_b"][j] = gmlp_backward(
                proj, w_s[j], b_s[j], vg_full[j], vb_full[j], dcat, name="gmlp_bwd")
            dproj = jnp.concatenate([dpu, dpv, dq_mem], axis=1)
        recv["w_in"][i] = exchange(matmul(dproj, xb, "tn", name=f"in_proj_dw_{i % 2}", out_dtypes=(BF16,), comm=comm), f"in{i}")
        dx = matmul(dproj, w_in_t, "nn", name=f"in_proj_dx_{i % 2}", extras=(dr1,), comm=comm, epilogue=lambda acc, res: (acc + alpha * res,))
    grad_x = dx[None]

    flat_small = jnp.concatenate(
        [jnp.concatenate(small[n], axis=0).reshape(-1) for n in ("ln1_g", "ln1_b", "ln2_g", "ln2_b")]
        + [jnp.stack(small_b[n]).reshape(-1) for n in ("w_s", "b_s", "vnorm_g", "vnorm_b")])
    pad = (-flat_small.size) % 1024
    small_job = comm.submit(GatherJob([jnp.pad(flat_small, (0, pad)).reshape(-1, 1024)], "gsmall"))

    results = {}

    def received(name, i):
        return comm.require(*recv[name][i])

    def update(name, layers, w_, m_, v_, transposed):
        sums = [sum_parts(received(name, i), name=f"sum_{name}_{i % 2}", comm=comm) for i in layers]
        grad = jnp.stack([t.T if transposed else t for t in sums])
        flat = lambda t: t.reshape(-1, t.shape[-1])
        upd = adamw(flat(grad)[None], flat(w_), flat(m_), flat(v_), name=f"adamw_{name}_{layers[0] % 2}", emit_grad=False)
        return [grad] + [t.reshape(w_.shape) for t in upd]

    results["w_ff2"] = update("w_ff2", list(range(depth)), w_ff2, m_w_ff2, v_w_ff2, False)
    results["w_ff1"] = update("w_ff1", list(range(depth)), w_ff1, m_w_ff1, v_w_ff1, True)
    results["w_out"] = update("w_out", list(range(depth)), w_out, m_w_out, v_w_out, True)
    results["w_mem_kv"] = update("w_mem_kv", list(range(depth)), w_mem_kv, m_w_mem_kv, v_w_mem_kv, False)
    results["w_in_b"] = update("w_in", list(range(1, depth, 2)), w_in_b, m_w_in_b, v_w_in_b, True)
    results["w_in_a"] = update("w_in", list(range(0, depth, 2)), w_in_a, m_w_in_a, v_w_in_a, True)

    gathered_small = comm.require(small_job)[0].reshape(N_DEV, -1)
    at = 0

    def take(shape):
        nonlocal at
        size = math.prod(shape)
        at += size
        return gathered_small[:, at - size:at].reshape((N_DEV,) + shape)

    for n, w_, m_, v_ in (("ln1_g", ln1_g, m_ln1_g, v_ln1_g), ("ln1_b", ln1_b, m_ln1_b, v_ln1_b),
                          ("ln2_g", ln2_g, m_ln2_g, v_ln2_g), ("ln2_b", ln2_b, m_ln2_b, v_ln2_b)):
        results[n] = adamw(take(w_.shape), w_, m_, v_, name="adamw_ln")
    lanes = lambda t: t.reshape(-1, 128)
    results["w_s"] = [t.reshape(w_s.shape) for t in adamw(take((w_s.size // 128, 128)), lanes(w_s), lanes(m_w_s), lanes(v_w_s), name="adamw_w_s")]
    results["b_s"] = [t.reshape(b_s.shape) for t in adamw(take((b_s.size // 128, 128)), lanes(b_s), lanes(m_b_s), lanes(v_b_s), name="adamw_b_s")]
    for n, w_, m_, v_ in (("vnorm_g", vnorm_g, m_vnorm_g, v_vnorm_g), ("vnorm_b", vnorm_b, m_vnorm_b, v_vnorm_b)):
        parts = lax.dynamic_slice_in_dim(take((n_b, mix_w)), me * w_.shape[1], w_.shape[1], axis=2)
        results[n] = adamw(parts, w_, m_, v_, name="adamw_vnorm")

    order =("w_in_a", "w_in_b", "w_s", "b_s", "vnorm_g", "vnorm_b", "w_mem_kv", "w_out", "ln1_g", "ln1_b", "w_ff1", "w_ff2", "ln2_g", "ln2_b")
    return (loss, grad_x, *[results[n][0] for n in order], *[results[n][1] for n in order],
            *[results[n][2] for n in order], *[results[n][3] for n in order])
```

```python
import math

import jax
import jax.numpy as jnp
from jax import lax
from jax.experimental import pallas as pl
from jax.experimental.pallas import tpu as pltpu

F32 = jnp.float32
BF16 = jnp.bfloat16
N_DEV = 8
HEAD_DIM = 128
MEM_HEADS = 4
MEM_WIDTH = MEM_HEADS * HEAD_DIM
A_DILATIONS = (1, 4, 16)
LN_EPS = 1e-5
ADAM_LR, ADAM_B1, ADAM_B2, ADAM_EPS, ADAM_WD, ADAM_STEP = 0.001, 0.9, 0.999, 1e-08, 0.01, 10
VMEM_LIMIT_BYTES = 56 * 1024 * 1024
MESH = pl.DeviceIdType.MESH
ANY = pl.BlockSpec(memory_space=pl.ANY)


def _params(*sem):
    return pltpu.CompilerParams(dimension_semantics=sem, vmem_limit_bytes=VMEM_LIMIT_BYTES)


def _tile(dim, pref):
    if dim <= pref:
        return dim
    best = None
    for t in range(128, pref + 1, 128):
        if dim % t == 0:
            best = t
    return best if best is not None else dim


def _dot(a, b, dims):
    return lax.dot_general(a, b, (dims, ((), ())), preferred_element_type=F32)


NN = ((1,), (0,))
NT = ((1,), (1,))
TN = ((0,), (0,))


def _my_position():
    return lax.axis_index("x"), lax.axis_index("y"), lax.axis_index("c")


class GatherJob:
    US_PER_MB = 46.0
    OVERSHOOT = 1.1

    def __init__(self, arrays, tag):
        self.arrays, self.tag, self.outs = list(arrays), tag, None
        self.out_shapes = [jax.ShapeDtypeStruct((N_DEV,) + v.shape, v.dtype) for v in self.arrays]
        self.n_sems, self.n_local = 7 * len(self.arrays), len(self.arrays)
        self.est_us = self.US_PER_MB * sum(v.size * v.dtype.itemsize for v in self.arrays) / 1e6

    def phases(self, v_refs, out_refs, send_sems, recv_sems, local_sems, sem0, loc0):
        n = len(self.arrays)
        x, y, c = _my_position()
        me, sibling = (x, y, c), (x, y, 1 - c)
        chips = [(1 - x, y), (x, 1 - y), (1 - x, 1 - y)]

        def copy(a, k, block, to, from_input=False):
            px, py, pc = block
            slot = out_refs[a].at[4 * px + 2 * py + pc]
            return pltpu.make_async_remote_copy(
                src_ref=v_refs[a] if from_input else slot, dst_ref=slot, send_sem=send_sems.at[sem0 + 7 * a + k],
                recv_sem=recv_sems.at[sem0 + 7 * a + k], device_id=to, device_id_type=MESH)

        def mine(a):
            return pltpu.make_async_copy(v_refs[a], out_refs[a].at[4 * x + 2 * y + c], local_sems.at[loc0 + a])

        def first(a):
            return [copy(a, 0, me, sibling, True)] + [copy(a, 1 + j, me, (*chip, c), True) for j, chip in enumerate(chips)]

        def start():
            for a in range(n):
                mine(a).start()
                for cp in first(a):
                    cp.start()

        def middle():
            for j, chip in enumerate(chips):
                for a in range(n):
                    copy(a, 1 + j, (*chip, c), me).wait_recv()
                    copy(a, 4 + j, (*chip, c), sibling).start()

        def finish():
            for a in range(n):
                copy(a, 0, sibling, me).wait_recv()
                for j, chip in enumerate(chips):
                    copy(a, 4 + j, (*chip, 1 - c), me).wait_recv()
                for cp in first(a) + [copy(a, 4 + j, (*chip, c), sibling) for j, chip in enumerate(chips)]:
                    cp.wait_send()
                mine(a).wait()

        return start, middle, finish


class ExchangeJob:
    US_PER_MB = 11.0
    OVERSHOOT = 1.05
    CHUNK_US = 50.0
    RELATIONS = [(dx, dy, dc) for dx in (0, 1) for dy in (0, 1) for dc in (0, 1)][1:]

    def __init__(self, p, row0, rows, tag):
        self.arrays, self.tag, self.outs = [p], tag, None
        self.row0, self.rows = row0, rows
        self.out_shapes = [jax.ShapeDtypeStruct((N_DEV, rows, p.shape[2]), p.dtype)]
        self.n_sems, self.n_local = 7, 1
        self.est_us = self.US_PER_MB * N_DEV * rows * p.shape[2] * p.dtype.itemsize / 1e6

    def phases(self, p_refs, out_refs, send_sems, recv_sems, local_sems, sem0, loc0):
        (p_ref,), (out_ref,) = p_refs, out_refs
        x, y, c = _my_position()
        me = 4 * x + 2 * y + c
        chunk = pl.ds(self.row0, self.rows)

        def mine():
            return pltpu.make_async_copy(p_ref.at[me, chunk], out_ref.at[me], local_sems.at[loc0])

        def copies(arriving):
            made = []
            for k, (dx, dy, dc) in enumerate(self.RELATIONS):
                px, py, pc = (x + dx) % 2, (y + dy) % 2, (c + dc) % 2
                peer = 4 * px + 2 * py + pc
                made.append(pltpu.make_async_remote_copy(
                    src_ref=p_ref.at[peer, chunk], dst_ref=out_ref.at[peer if arriving else me],
                    send_sem=send_sems.at[sem0 + k], recv_sem=recv_sems.at[sem0 + k],
                    device_id=(px, py, pc), device_id_type=MESH))
            return made

        def start():
            mine().start()
            for send in copies(False):
                send.start()

        def finish():
            for arrival in copies(True):
                arrival.wait_recv()
            for send in copies(False):
                send.wait_send()
            mine().wait()

        return start, (lambda: None), finish


class Exchanges:
    MIN_HOST_US = 20.0
    MAX_JOBS = 8

    def __init__(self):
        self.queue = []

    def submit(self, job):
        self.queue.append(job)
        return job

    def take(self, host_us):
        jobs, used = [], 0.0
        if host_us >= self.MIN_HOST_US:
            for job in list(self.queue):
                if used + job.est_us <= job.OVERSHOOT * host_us and len(jobs) < self.MAX_JOBS:
                    used += job.est_us
                    jobs.append(job)
                    self.queue.remove(job)
        return jobs

    def require(self, *jobs):
        waiting = [job for job in jobs if job.outs is None]
        if waiting:
            for job in waiting:
                self.queue.remove(job)
            _call(lambda: None, grid=(), in_specs=[], out_specs=[], out_shape=[], scratch_shapes=[], args=[],
                  name="exchange", semantics=(), jobs=waiting)
        return [job.outs[0] for job in jobs]


def _call(body, *, grid, in_specs, out_specs, out_shape, scratch_shapes, args, name, semantics, jobs=()):
    if not jobs:
        return pl.pallas_call(body, grid=grid, in_specs=in_specs, out_specs=out_specs, out_shape=out_shape,
                              scratch_shapes=scratch_shapes, compiler_params=_params(*semantics), name=name)(*args)
    n_in, n_out, n_scr = len(in_specs), len(out_shape), len(scratch_shapes)
    j_in = [a for job in jobs for a in job.arrays]
    j_out = [s for job in jobs for s in job.out_shapes]
    n_sems, n_local = sum(job.n_sems for job in jobs), sum(job.n_local for job in jobs)
    steps = math.prod(grid)
    middle_step = (7 * steps) // 8

    def wrapped(*refs):
        ins, refs = refs[:n_in], refs[n_in:]
        j_ins, refs = refs[:len(j_in)], refs[len(j_in):]
        outs, refs = refs[:n_out], refs[n_out:]
        j_outs, refs = refs[:len(j_out)], refs[len(j_out):]
        scratch, (send_sems, recv_sems, local_sems) = refs[:n_scr], refs[n_scr:]
        step = 0
        for axis, extent in enumerate(grid):
            step = step * extent + pl.program_id(axis)
        phases, at_in, at_out, sem0, loc0 = [], 0, 0, 0, 0
        for job in jobs:
            k_in, k_out = len(job.arrays), len(job.out_shapes)
            phases.append(job.phases(j_ins[at_in:at_in + k_in], j_outs[at_out:at_out + k_out], send_sems, recv_sems, local_sems, sem0, loc0))
            at_in, at_out, sem0, loc0 = at_in + k_in, at_out + k_out, sem0 + job.n_sems, loc0 + job.n_local

        def run(which):
            for ph in phases:
                ph[which]()

        if steps == 1:
            run(0)
            body(*ins, *outs, *scratch)
            run(1)
            run(2)
        else:
            pl.when(step == 0)(lambda: run(0))
            body(*ins, *outs, *scratch)
            pl.when(step == middle_step)(lambda: run(1))
            pl.when(step == steps - 1)(lambda: run(2))

    res = pl.pallas_call(
        wrapped, grid=grid, in_specs=list(in_specs) + [ANY] * len(j_in), out_specs=list(out_specs) + [ANY] * len(j_out),
        out_shape=list(out_shape) + j_out,
        scratch_shapes=list(scratch_shapes) + [pltpu.SemaphoreType.DMA((n_sems,)), pltpu.SemaphoreType.DMA((n_sems,)),
                                               pltpu.SemaphoreType.DMA((n_local,))],
        compiler_params=_params(*(["arbitrary"] * len(grid))), name=name + "".join("__" + job.tag for job in jobs),
    )(*args, *j_in)
    at = n_out
    for job in jobs:
        job.outs = list(res[at:at + len(job.out_shapes)])
        at += len(job.out_shapes)
    return list(res[:n_out])


MATMUL_TILES = {"nn": (1024, 512, 2560), "nt": (1024, 512, 2560), "tn": (1024, 1024, 2048)}
MATMUL_FLOPS_PER_US = {"nn": 7.5e8, "nt": 7.5e8, "tn": 8.5e8}


def matmul(a, b, mode, name, epilogue=None, extras=(), out_dtypes=(F32,), comm=None):
    pieces = list(b) if isinstance(b, (list, tuple)) else [b]
    n_pc = len(pieces)
    b_rows, b_cols = pieces[0].shape[0], n_pc * pieces[0].shape[1]
    assert mode != "tn" or n_pc == 1
    if mode == "nn":
        (m, k), (k2, n) = a.shape, (b_rows, b_cols)
    elif mode == "nt":
        (m, k), (n, k2) = a.shape, (b_rows, b_cols)
    else:
        (k, m), (k2, n) = a.shape, (b_rows, b_cols)
    assert k == k2, (a.shape, (b_rows, b_cols), mode)
    tm_pref, tn_pref, tk_pref = MATMUL_TILES[mode]
    tm = _tile(m, tm_pref)
    if mode == "tn" and tm < tm_pref:
        tn_pref *= 2
    tn = _tile(n // n_pc if mode == "nn" else n, tn_pref)
    tk = _tile(k // n_pc if mode == "nt" else k, tk_pref)
    nj, nk = n // tn, k // tk
    per_piece = (nj if mode == "nn" else nk) // n_pc
    dims = {"nn": NN, "nt": NT, "tn": TN}[mode]
    a_spec = pl.BlockSpec((tk, tm), lambda i, j, kk: (kk, i)) if mode == "tn" else pl.BlockSpec((tm, tk), lambda i, j, kk: (i, kk))

    def b_spec(p):
        if mode == "nt":
            return pl.BlockSpec((tn, tk), lambda i, j, kk: (j, jnp.clip(kk - p * per_piece, 0, per_piece - 1)))
        if n_pc == 1:
            return pl.BlockSpec((tk, tn), lambda i, j, kk: (kk, j))
        return pl.BlockSpec((tk, tn), lambda i, j, kk: (jnp.where(j // per_piece == p, kk, 0),
                                                       jnp.where(j // per_piece == p, j - p * per_piece, 0)))

    tile_spec = pl.BlockSpec((tm, tn), lambda i, j, kk: (i, j))
    n_ex, n_out = len(extras), len(out_dtypes)
    if epilogue is None:
        epilogue = lambda acc: (acc,) * n_out

    def body(a_ref, *rest):
        b_refs, rest = rest[:n_pc], rest[n_pc:]
        ex_refs, out_refs, acc_ref = rest[:n_ex], rest[n_ex:n_ex + n_out], rest[-1]
        j, kk = pl.program_id(1), pl.program_id(2)

        def finish(acc):
            for o_ref, val in zip(out_refs, epilogue(acc, *[e[...] for e in ex_refs])):
                o_ref[...] = val.astype(o_ref.dtype)

        for p, b_ref in enumerate(b_refs):
            in_use = True if n_pc == 1 else ((j if mode == "nn" else kk) // per_piece == p)

            def product():
                return _dot(a_ref[...].astype(BF16), b_ref[...].astype(BF16), dims)

            if nk == 1:
                if n_pc == 1:
                    finish(product())
                else:
                    pl.when(in_use)(lambda: finish(product()))
                continue

            @pl.when(in_use & (kk == 0))
            def _():
                acc_ref[...] = product()

            @pl.when(in_use & (kk > 0) & (kk < nk - 1))
            def _():
                acc_ref[...] += product()

            @pl.when(in_use & (kk == nk - 1))
            def _():
                finish(acc_ref[...] + product())

    host_us = 2.0 * m * n * k / MATMUL_FLOPS_PER_US[mode]
    outs = _call(
        body,
        grid=(m // tm, nj, nk),
        in_specs=[a_spec] + [b_spec(p) for p in range(n_pc)] + [tile_spec] * n_ex,
        out_specs=[tile_spec] * n_out,
        out_shape=[jax.ShapeDtypeStruct((m, n), dt) for dt in out_dtypes],
        scratch_shapes=[pltpu.VMEM((tm, tn), F32)],
        args=[a, *pieces, *extras], name=name, semantics=("parallel", "parallel", "arbitrary"),
        jobs=comm.take(host_us) if comm is not None else (),
    )
    return outs[0] if n_out == 1 else outs


LN_BYTES_PER_US = 3.0e6


def ln_residual(x, y, g, b, alpha, name, comm=None):
    s, w = x.shape
    tb = _tile(s, 256)
    row = pl.BlockSpec((tb, w), lambda i: (i, 0))
    vec = pl.BlockSpec((1, w), lambda i: (0, 0))

    def body(x_ref, y_ref, g_ref, b_ref, r_ref, xn_ref, xnb_ref):
        r = alpha * x_ref[...] + y_ref[...]
        mu = jnp.mean(r, axis=-1, keepdims=True)
        var = jnp.mean(jnp.square(r - mu), axis=-1, keepdims=True)
        xn = (r - mu) * lax.rsqrt(var + LN_EPS) * g_ref[...] + b_ref[...]
        r_ref[...] = r
        xn_ref[...] = xn
        xnb_ref[...] = xn.astype(BF16)

    return _call(
        body, grid=(s // tb,), in_specs=[row, row, vec, vec], out_specs=[row, row, row],
        out_shape=[jax.ShapeDtypeStruct((s, w), F32), jax.ShapeDtypeStruct((s, w), F32), jax.ShapeDtypeStruct((s, w), BF16)],
        scratch_shapes=[], args=[x, y, g.reshape(1, w), b.reshape(1, w)], name=name, semantics=("parallel",),
        jobs=comm.take(18.0 * s * w / LN_BYTES_PER_US) if comm is not None else (),
    )


def _ln_bwd_tile(r, g, dxn):
    mu = jnp.mean(r, axis=-1, keepdims=True)
    cen = r - mu
    rstd = lax.rsqrt(jnp.mean(jnp.square(cen), axis=-1, keepdims=True) + LN_EPS)
    xhat = cen * rstd
    dxh = dxn * g
    dr = rstd * (dxh - jnp.mean(dxh, axis=-1, keepdims=True) - xhat * jnp.mean(dxh * xhat, axis=-1, keepdims=True))
    return dr, jnp.sum(dxn * xhat, axis=0, keepdims=True)


def ln_backward(r, g, dxn, name, comm=None):
    s, w = r.shape
    tb = _tile(s, 256)
    row = pl.BlockSpec((tb, w), lambda i: (i, 0))
    vec = pl.BlockSpec((1, w), lambda i: (0, 0))

    def body(r_ref, g_ref, d_ref, dr_ref, drb_ref, dg_ref, db_ref):
        @pl.when(pl.program_id(0) == 0)
        def _():
            dg_ref[...] = jnp.zeros_like(dg_ref)
            db_ref[...] = jnp.zeros_like(db_ref)

        dxn = d_ref[...]
        dr, dg = _ln_bwd_tile(r_ref[...], g_ref[...], dxn)
        dr_ref[...] = dr
        drb_ref[...] = dr.astype(BF16)
        dg_ref[...] += dg
        db_ref[...] += jnp.sum(dxn, axis=0, keepdims=True)

    return _call(
        body, grid=(s // tb,), in_specs=[row, vec, row], out_specs=[row, row, vec, vec],
        out_shape=[jax.ShapeDtypeStruct((s, w), F32), jax.ShapeDtypeStruct((s, w), BF16),
                   jax.ShapeDtypeStruct((1, w), F32), jax.ShapeDtypeStruct((1, w), F32)],
        scratch_shapes=[], args=[r, g.reshape(1, w), dxn], name=name, semantics=("arbitrary",),
        jobs=comm.take(14.0 * s * w / LN_BYTES_PER_US) if comm is not None else (),
    )


ATTENTION_US = {("fwd", 1): 55.0, ("fwd", 4): 80.0, ("fwd", 16): 100.0, ("bwd", 1): 115.0, ("bwd", 4): 130.0, ("bwd", 16): 190.0}
ATTENTION_HEADS_PER_STEP = {("fwd", 1): 8, ("fwd", 4): 1, ("fwd", 16): 1, ("bwd", 1): 8, ("bwd", 4): 1, ("bwd", 16): 1}


def _causal_masks():
    qi = lax.broadcasted_iota(jnp.int32, (128, 128), 0)
    kj = lax.broadcasted_iota(jnp.int32, (128, 128), 1)
    return kj <= qi, kj >= qi


def _window_mask(has_previous):
    qi = lax.broadcasted_iota(jnp.int32, (128, 256), 0)
    kj = lax.broadcasted_iota(jnp.int32, (128, 256), 1)
    return ((kj < 128) & (kj >= qi) & has_previous) | ((kj >= 128) & (kj - 128 <= qi))


def _sub_rows(d, r):
    return pl.ds(r, 128, stride=d) if d > 1 else pl.ds(0, 128)


def _for_each_residue(d, fn):
    if d <= 4:
        for r in range(d):
            fn(r)
    else:
        def step(r, carry):
            fn(r)
            return carry
        lax.fori_loop(0, d, step, 0, unroll=8)


def attention_forward(proj, group, d, heads, name, comm=None):
    s = proj.shape[0]
    rows = 128 * d
    nb = s // rows
    scale = HEAD_DIM ** -0.5
    hp = math.gcd(heads, ATTENTION_HEADS_PER_STEP["fwd", d])
    wide = 128 * hp
    qc, kc, vc = (group * 3) * heads // hp, (group * 3 + 1) * heads // hp, (group * 3 + 2) * heads // hp

    def cur(col):
        return pl.BlockSpec((rows, wide), lambda h, n: (n, col + h))

    def prev(col):
        return pl.BlockSpec((rows, wide), lambda h, n: (jnp.maximum(n - 1, 0), col + h))

    out = pl.BlockSpec((rows, wide), lambda h, n: (n, h))

    def body(q_ref, kc_ref, kp_ref, vc_ref, vp_ref, o_ref, l_ref):
        mask = _window_mask(pl.program_id(1) > 0)

        def one(r):
            rws = _sub_rows(d, r)
            for hh in range(hp):
                sl = (rws, pl.ds(hh * 128, 128))
                k = jnp.concatenate([kp_ref[sl].astype(BF16), kc_ref[sl].astype(BF16)], axis=0)
                v = jnp.concatenate([vp_ref[sl].astype(BF16), vc_ref[sl].astype(BF16)], axis=0)
                sc = jnp.where(mask, _dot(q_ref[sl].astype(BF16), k, NT) * scale, -jnp.inf)
                m = jnp.max(sc, axis=-1, keepdims=True)
                e = jnp.exp(sc - m)
                l = jnp.sum(e, axis=-1, keepdims=True)
                o_ref[sl] = _dot(e.astype(BF16), v, NN) / l
                l_ref[sl] = jnp.broadcast_to(m + jnp.log(l), (128, 128))

        _for_each_residue(d, one)

    return _call(
        body, grid=(heads // hp, nb),
        in_specs=[cur(qc), cur(kc), prev(kc), cur(vc), prev(vc)], out_specs=[out, out],
        out_shape=[jax.ShapeDtypeStruct((s, heads * 128), F32)] * 2, scratch_shapes=[],
        args=[proj] * 5, name=name, semantics=("parallel", "parallel"),
        jobs=comm.take(ATTENTION_US["fwd", d] * s * heads / (4096 * 8)) if comm is not None else (),
    )


def attention_combine(os_, ls_, name):
    s, w = os_[0].shape
    tb = _tile(s, 256)
    row = pl.BlockSpec((tb, w), lambda i: (i, 0))

    def body(o0, o1, o2, l0, l1, l2, o_ref, ob_ref, lse_ref):
        a, b, c = l0[...], l1[...], l2[...]
        m = jnp.maximum(jnp.maximum(a, b), c)
        ea, eb, ec = jnp.exp(a - m), jnp.exp(b - m), jnp.exp(c - m)
        tot = ea + eb + ec
        o = (ea / tot) * o0[...] + (eb / tot) * o1[...] + (ec / tot) * o2[...]
        o_ref[...] = o
        ob_ref[...] = o.astype(BF16)
        lse_ref[...] = m + jnp.log(tot)

    return pl.pallas_call(
        body, grid=(s // tb,), in_specs=[row] * 6, out_specs=[row] * 3,
        out_shape=[jax.ShapeDtypeStruct((s, w), F32), jax.ShapeDtypeStruct((s, w + MEM_WIDTH), BF16), jax.ShapeDtypeStruct((s, w), F32)],
        compiler_params=_params("parallel"), name=name,
    )(*os_, *ls_)


def attention_backward(proj, o, lse, dcat, group, d, heads, name, comm=None):
    s = proj.shape[0]
    rows = 128 * d
    nb = s // rows
    scale = HEAD_DIM ** -0.5
    hp = math.gcd(heads, ATTENTION_HEADS_PER_STEP["bwd", d])
    wide = 128 * hp
    qc, kc, vc = (group * 3) * heads // hp, (group * 3 + 1) * heads // hp, (group * 3 + 2) * heads // hp

    def at(col, shift):
        return pl.BlockSpec((rows, wide), lambda h, n: (jnp.clip(n + shift, 0, nb - 1), col + h))

    def out(col):
        return pl.BlockSpec((rows, wide), lambda h, n: (n, col + h))

    def body(qc_ref, qn_ref, kc_ref, kp_ref, vc_ref, vp_ref, doc_ref, don_ref, oc_ref, on_ref, lc_ref, ln_ref,
             dq_ref, dk_ref, dv_ref):
        n = pl.program_id(1)
        mask_w = _window_mask(n > 0)
        mask_n = _causal_masks()[1] & (n < nb - 1)

        def tile(q, k, v, do, lse_t, dsum, mask):
            p = jnp.where(mask, jnp.exp(_dot(q, k, NT) * scale - lse_t), 0.0)
            ds = (p * (_dot(do, v, NT) - dsum) * scale).astype(BF16)
            return p.astype(BF16), ds

        def one(r):
            rws = _sub_rows(d, r)
            for hh in range(hp):
                sl = (rws, pl.ds(hh * 128, 128))
                q_c, q_n = qc_ref[sl].astype(BF16), qn_ref[sl].astype(BF16)
                k_c, v_c = kc_ref[sl].astype(BF16), vc_ref[sl].astype(BF16)
                k_w = jnp.concatenate([kp_ref[sl].astype(BF16), k_c], axis=0)
                v_w = jnp.concatenate([vp_ref[sl].astype(BF16), v_c], axis=0)
                do_c, do_n = doc_ref[sl], don_ref[sl]
                dsum_c = jnp.sum(do_c * oc_ref[sl], axis=-1, keepdims=True)
                dsum_n = jnp.sum(do_n * on_ref[sl], axis=-1, keepdims=True)
                do_c, do_n = do_c.astype(BF16), do_n.astype(BF16)
                lse_c = lc_ref[sl]
                p_w, ds_w = tile(q_c, k_w, v_w, do_c, jnp.concatenate([lse_c, lse_c], axis=1), dsum_c, mask_w)
                p_n, ds_n = tile(q_n, k_c, v_c, do_n, ln_ref[sl], dsum_n, mask_n)
                dq_ref[sl] = _dot(ds_w, k_w, NN)
                both_q = jnp.concatenate([q_c, q_n], axis=0)
                both_do = jnp.concatenate([do_c, do_n], axis=0)
                dk_ref[sl] = _dot(jnp.concatenate([ds_w[:, 128:], ds_n], axis=0), both_q, TN)
                dv_ref[sl] = _dot(jnp.concatenate([p_w[:, 128:], p_n], axis=0), both_do, TN)

        _for_each_residue(d, one)

    w = heads * 128
    return _call(
        body, grid=(heads // hp, nb),
        in_specs=[at(qc, 0), at(qc, 1), at(kc, 0), at(kc, -1), at(vc, 0), at(vc, -1),
                  at(0, 0), at(0, 1), at(0, 0), at(0, 1), at(0, 0), at(0, 1)],
        out_specs=[out(0)] * 3,
        out_shape=[jax.ShapeDtypeStruct((s, w), F32)] * 3, scratch_shapes=[],
        args=[proj] * 6 + [dcat, dcat, o, o, lse, lse], name=name, semantics=("parallel", "parallel"),
        jobs=comm.take(ATTENTION_US["bwd", d] * s * heads / (4096 * 8)) if comm is not None else (),
    )


def _mem_softmax(q, kv, h, scale):
    k = kv[:, h * 128:(h + 1) * 128].astype(BF16)
    v = kv[:, MEM_WIDTH + h * 128:MEM_WIDTH + (h + 1) * 128].astype(BF16)
    sc = _dot(q, k, NT) * scale
    e = jnp.exp(sc - jnp.max(sc, axis=-1, keepdims=True))
    return e / jnp.sum(e, axis=-1, keepdims=True), k, v


def memory_attention(proj, qcol, kv, cat, name):
    s = proj.shape[0]
    tb = _tile(s, 512)
    scale = HEAD_DIM ** -0.5

    def body(q_ref, kv_ref, cat_ref, o_ref):
        kv_t = kv_ref[...]
        for h in range(MEM_HEADS):
            p, _, v = _mem_softmax(q_ref[:, h * 128:(h + 1) * 128].astype(BF16), kv_t, h, scale)
            o_ref[:, h * 128:(h + 1) * 128] = _dot(p.astype(BF16), v, NN).astype(BF16)

    return pl.pallas_call(
        body, grid=(s // tb,),
        in_specs=[pl.BlockSpec((tb, MEM_WIDTH), lambda i: (i, qcol // MEM_WIDTH)), pl.BlockSpec(kv.shape, lambda i: (0, 0)), ANY],
        out_specs=pl.BlockSpec((tb, MEM_WIDTH), lambda i: (i, cat.shape[1] // MEM_WIDTH - 1)),
        out_shape=jax.ShapeDtypeStruct(cat.shape, BF16), input_output_aliases={2: 0},
        compiler_params=_params("parallel"), name=name,
    )(proj, kv, cat)


def memory_attention_backward(proj, qcol, kv, dcat, dcol, name):
    s = proj.shape[0]
    tb = _tile(s, 512)
    scale = HEAD_DIM ** -0.5

    def body(q_ref, kv_ref, do_ref, dq_ref, dkv_ref):
        @pl.when(pl.program_id(0) == 0)
        def _():
            dkv_ref[...] = jnp.zeros_like(dkv_ref)

        kv_t = kv_ref[...]
        for h in range(MEM_HEADS):
            cols = slice(h * 128, (h + 1) * 128)
            q = q_ref[:, cols].astype(BF16)
            p, k, v = _mem_softmax(q, kv_t, h, scale)
            do = do_ref[:, cols].astype(BF16)
            dp = _dot(do, v, NT)
            ds = (p * (dp - jnp.sum(dp * p, axis=-1, keepdims=True)) * scale).astype(BF16)
            dq_ref[:, cols] = _dot(ds, k, NN).astype(BF16)
            dkv_ref[:, cols] += _dot(ds, q, TN)
            dkv_ref[:, MEM_WIDTH + h * 128:MEM_WIDTH + (h + 1) * 128] += _dot(p.astype(BF16), do, TN)

    return pl.pallas_call(
        body, grid=(s // tb,),
        in_specs=[pl.BlockSpec((tb, MEM_WIDTH), lambda i: (i, qcol // MEM_WIDTH)), pl.BlockSpec(kv.shape, lambda i: (0, 0)),
                  pl.BlockSpec((tb, MEM_WIDTH), lambda i: (i, dcol // MEM_WIDTH))],
        out_specs=[pl.BlockSpec((tb, MEM_WIDTH), lambda i: (i, 0)), pl.BlockSpec(kv.shape, lambda i: (0, 0))],
        out_shape=[jax.ShapeDtypeStruct((s, MEM_WIDTH), BF16), jax.ShapeDtypeStruct(kv.shape, F32)],
        compiler_params=_params("arbitrary"), name=name,
    )(proj, kv, dcat)


_SQRT_HALF = math.sqrt(0.5)
_INV_SQRT_2PI = 1.0 / math.sqrt(2.0 * math.pi)


def _gelu(x):
    return 0.5 * x * (1.0 + lax.erf(x * _SQRT_HALF))


def _gelu_grad(x):
    return 0.5 * (1.0 + lax.erf(x * _SQRT_HALF)) + x * (_INV_SQRT_2PI * jnp.exp(-0.5 * x * x))


def _gmlp_specs(s, wd, groups, tb):
    half = lambda c: pl.BlockSpec((tb, wd), lambda i: (i, c))
    ws_spec = pl.BlockSpec((groups, 128, 128), lambda i: (0, 0, 0))
    bs_spec = pl.BlockSpec((groups, 128, 1), lambda i: (0, 0, 0))
    vec = pl.BlockSpec((1, wd), lambda i: (0, 0))
    return half, ws_spec, bs_spec, vec


def _vnorm(zv, g, b):
    mu = jnp.mean(zv, axis=-1, keepdims=True)
    var = jnp.mean(jnp.square(zv - mu), axis=-1, keepdims=True)
    return (zv - mu) * lax.rsqrt(var + LN_EPS) * g + b


def gmlp_forward(proj, ws, bs, vg, vb, name):
    s = proj.shape[0]
    groups = ws.shape[0]
    wd = groups * 128
    tb = _tile(s, 512)
    half, ws_spec, bs_spec, vec = _gmlp_specs(s, wd, groups, tb)

    def body(pu_ref, pv_ref, ws_ref, bs_ref, vg_ref, vb_ref, o_ref, vn_ref):
        causal, _ = _causal_masks()
        vn_ref[...] = _vnorm(_gelu(pv_ref[...]), vg_ref[...], vb_ref[...]).astype(BF16)
        for g in range(groups):
            cols = slice(g * 128, (g + 1) * 128)
            wm = jnp.where(causal, ws_ref[g], 0.0).astype(BF16)
            for c in range(tb // 128):
                rws = slice(c * 128, (c + 1) * 128)
                sg = _dot(wm, vn_ref[rws, cols], NN) + bs_ref[g]
                o_ref[rws, cols] = (_gelu(pu_ref[rws, cols]) * sg).astype(BF16)

    return pl.pallas_call(
        body, grid=(s // tb,),
        in_specs=[half(0), half(1), ws_spec, bs_spec, vec, vec], out_specs=pl.BlockSpec((tb, wd), lambda i: (i, 0)),
        out_shape=jax.ShapeDtypeStruct((s, wd + MEM_WIDTH), BF16),
        scratch_shapes=[pltpu.VMEM((tb, wd), BF16)],
        compiler_params=_params("parallel"), name=name,
    )(proj, proj, ws, bs.reshape(groups, 128, 1), vg.reshape(1, wd), vb.reshape(1, wd))


def gmlp_backward(proj, ws, bs, vg, vb, dcat, name):
    s = proj.shape[0]
    groups = ws.shape[0]
    wd = groups * 128
    tb = _tile(s, 512)
    half, ws_spec, bs_spec, vec = _gmlp_specs(s, wd, groups, tb)

    def body(pu_ref, pv_ref, ws_ref, bs_ref, vg_ref, vb_ref, do_ref, dpu_ref, dpv_ref, dws_ref, dbs_ref, dvg_ref, dvb_ref,
             vn_ref, dvn_ref):
        @pl.when(pl.program_id(0) == 0)
        def _():
            dws_ref[...] = jnp.zeros_like(dws_ref)
            dbs_ref[...] = jnp.zeros_like(dbs_ref)
            dvg_ref[...] = jnp.zeros_like(dvg_ref)
            dvb_ref[...] = jnp.zeros_like(dvb_ref)

        causal, _ = _causal_masks()
        pv = pv_ref[...]
        zv = _gelu(pv)
        vn_ref[...] = _vnorm(zv, vg_ref[...], vb_ref[...]).astype(BF16)
        for g in range(groups):
            cols = slice(g * 128, (g + 1) * 128)
            wm = jnp.where(causal, ws_ref[g], 0.0).astype(BF16)
            dws_g = jnp.zeros((128, 128), F32)
            dbs_g = jnp.zeros((128, 1), F32)
            for c in range(tb // 128):
                rws = slice(c * 128, (c + 1) * 128)
                vn = vn_ref[rws, cols]
                pu = pu_ref[rws, cols]
                do = do_ref[rws, cols]
                sg = _dot(wm, vn, NN) + bs_ref[g]
                dpu_ref[rws, cols] = (do * sg * _gelu_grad(pu)).astype(BF16)
                dsg = do * _gelu(pu)
                dsg_b = dsg.astype(BF16)
                dws_g += _dot(dsg_b, vn, NT)
                dbs_g += jnp.sum(dsg, axis=-1, keepdims=True)
                dvn_ref[rws, cols] = _dot(wm, dsg_b, TN)
            dws_ref[g] += jnp.where(causal, dws_g, 0.0)
            dbs_ref[g] += dbs_g
        dvn = dvn_ref[...]
        dzv, dvg = _ln_bwd_tile(zv, vg_ref[...], dvn)
        dvg_ref[...] += dvg
        dvb_ref[...] += jnp.sum(dvn, axis=0, keepdims=True)
        dpv_ref[...] = (dzv * _gelu_grad(pv)).astype(BF16)

    row = pl.BlockSpec((tb, wd), lambda i: (i, 0))
    dpu, dpv, dws, dbs, dvg, dvb = pl.pallas_call(
        body, grid=(s // tb,),
        in_specs=[half(0), half(1), ws_spec, bs_spec, vec, vec, row],
        out_specs=[row, row, ws_spec, bs_spec, vec, vec],
        out_shape=[jax.ShapeDtypeStruct((s, wd), BF16), jax.ShapeDtypeStruct((s, wd), BF16),
                   jax.ShapeDtypeStruct((groups, 128, 128), F32), jax.ShapeDtypeStruct((groups, 128, 1), F32),
                   jax.ShapeDtypeStruct((1, wd), F32), jax.ShapeDtypeStruct((1, wd), F32)],
        scratch_shapes=[pltpu.VMEM((tb, wd), BF16), pltpu.VMEM((tb, wd), F32)],
        compiler_params=_params("arbitrary"), name=name,
    )(proj, proj, ws, bs.reshape(groups, 128, 1), vg.reshape(1, wd), vb.reshape(1, wd), dcat)
    return dpu, dpv, dws, dbs.reshape(groups, 128), dvg, dvb


def loss_head(y, target, name):
    s, w = y.shape
    tb = _tile(s, 256)
    row = pl.BlockSpec((tb, w), lambda i: (i, 0))
    nsteps = s // tb

    def body(y_ref, t_ref, loss_ref, dy_ref, acc_ref):
        i = pl.program_id(0)

        @pl.when(i == 0)
        def _():
            acc_ref[...] = jnp.zeros_like(acc_ref)

        err = y_ref[...] - t_ref[...]
        dy_ref[...] = err / w
        acc_ref[...] += jnp.sum(jnp.mean(jnp.square(err), axis=-1, keepdims=True), axis=0, keepdims=True)

        @pl.when(i == nsteps - 1)
        def _():
            loss_ref[...] = jnp.broadcast_to(0.5 * acc_ref[...], loss_ref.shape)

    return pl.pallas_call(
        body, grid=(nsteps,), in_specs=[row, row],
        out_specs=[pl.BlockSpec((8, 128), lambda i: (0, 0)), row],
        out_shape=[jax.ShapeDtypeStruct((8, 128), F32), jax.ShapeDtypeStruct((s, w), F32)],
        scratch_shapes=[pltpu.VMEM((1, 1), F32)],
        compiler_params=_params("arbitrary"), name=name,
    )(y, target)


PARTS_WINDOW_ELEMS = 1024 * 1024


def _row_tile(r, c, budget):
    if r * c <= budget or r % 16:
        return r
    fits = [t for t in range(16, r, 16) if r % t == 0 and t * c <= budget]
    return max(fits) if fits else 16


def _sum_in_device_order(p_ref):
    g = p_ref[0].astype(F32)
    for j in range(1, p_ref.shape[0]):
        g = g + p_ref[j].astype(F32)
    return g


def _adamw_update(g, w, m, v):
    nm = ADAM_B1 * m + (1.0 - ADAM_B1) * g
    nv = ADAM_B2 * v + (1.0 - ADAM_B2) * jnp.square(g)
    m_hat = nm / (1.0 - ADAM_B1 ** ADAM_STEP)
    v_hat = nv / (1.0 - ADAM_B2 ** ADAM_STEP)
    return -ADAM_LR * (m_hat / (jnp.sqrt(v_hat) + ADAM_EPS) + ADAM_WD * w), nm, nv


SUM_BYTES_PER_US = 1.7e6


def sum_parts(chunks, name, comm=None):
    p, r, c = chunks[0].shape
    tr = _row_tile(r, c, PARTS_WINDOW_ELEMS // len(chunks))
    nb = r // tr

    def chunk_spec(q):
        return pl.BlockSpec((p, tr, c), lambda ch, i: (0, jnp.where(ch == q, i, 0), 0))

    def body(*refs):
        for q in range(len(chunks)):
            @pl.when(pl.program_id(0) == q)
            def _():
                refs[-1][...] = _sum_in_device_order(refs[q])

    host_us = len(chunks) * r * c * (p * chunks[0].dtype.itemsize + 4) / SUM_BYTES_PER_US
    return _call(
        body, grid=(len(chunks), nb), in_specs=[chunk_spec(q) for q in range(len(chunks))],
        out_specs=[pl.BlockSpec((tr, c), lambda ch, i: (ch * nb + i, 0))],
        out_shape=[jax.ShapeDtypeStruct((len(chunks) * r, c), F32)], scratch_shapes=[],
        args=list(chunks), name=name, semantics=("arbitrary", "arbitrary"),
        jobs=comm.take(host_us) if comm is not None else (),
    )[0]


ELEMENTWISE_BYTES_PER_US = 2.0e6


def adamw(parts, w, m, v, name, emit_grad=True, comm=None):
    p, r, c = parts.shape
    tr = _row_tile(r, c, 160 * 1024)
    n_out = 4 if emit_grad else 3

    def body(p_ref, w_ref, m_ref, v_ref, *out_refs):
        g = _sum_in_device_order(p_ref)
        vals = _adamw_update(g, w_ref[...], m_ref[...], v_ref[...])
        for o_ref, val in zip(out_refs, ((g,) + vals) if emit_grad else vals):
            o_ref[...] = val

    row = pl.BlockSpec((tr, c), lambda i: (i, 0))
    host_us = (parts.size * parts.dtype.itemsize + (3 + n_out) * 4 * r * c) / ELEMENTWISE_BYTES_PER_US
    return _call(
        body, grid=(r // tr,), in_specs=[pl.BlockSpec((p, tr, c), lambda i: (0, i, 0)), row, row, row],
        out_specs=[row] * n_out, out_shape=[jax.ShapeDtypeStruct((r, c), F32)] * n_out, scratch_shapes=[],
        args=[parts, w, m, v], name=name, semantics=("parallel",), jobs=comm.take(host_us) if comm is not None else (),
    )


def kernel(x, mem, w_in_a, w_in_b, w_s, b_s, vnorm_g, vnorm_b, w_mem_kv, w_out, ln1_g, ln1_b, w_ff1, w_ff2, ln2_g, ln2_b, loss_target, m_w_in_a, m_w_in_b, m_w_s, m_b_s, m_vnorm_g, m_vnorm_b, m_w_mem_kv, m_w_out, m_ln1_g, m_ln1_b, m_w_ff1, m_w_ff2, m_ln2_g, m_ln2_b, v_w_in_a, v_w_in_b, v_w_s, v_b_s, v_vnorm_g, v_vnorm_b, v_w_mem_kv, v_w_out, v_ln1_g, v_ln1_b, v_w_ff1, v_w_ff2, v_ln2_g, v_ln2_b):
    depth = w_ff1.shape[0]
    groups = w_s.shape[1]
    mix_w = groups * HEAD_DIM
    n_b = vnorm_g.shape[0]
    alpha = (2.0 * depth) ** 0.25
    me = 4 * lax.axis_index("x") + 2 * lax.axis_index("y") + lax.axis_index("c")
    x0 = x[0]
    mem_b = mem[0].astype(BF16)
    target = loss_target[0]

    comm = Exchanges()
    weight_jobs = []
    for i in range(depth):
        w_in = (w_in_a if i % 2 == 0 else w_in_b)[i // 2]
        halves = lambda t: [t[:, :t.shape[1] // 2], t[:, t.shape[1] // 2:]]
        groups_of_shards = [[h] for h in halves(w_in.T)] + [[w_mem_kv[i], w_out[i].T]] + [[h] for h in halves(w_ff1[i].T)] + [[h] for h in halves(w_ff2[i])]
        weight_jobs.append([GatherJob([t.astype(BF16) for t in shards], f"g{i}{tag}")
                            for shards, tag in zip(groups_of_shards, ("ina", "inb", "out", "ff1a", "ff1b", "ff2a", "ff2b"))])
    weight_jobs[0][0] = GatherJob(weight_jobs[0][0].arrays + [jnp.concatenate([vnorm_g, vnorm_b], axis=0)], "g0ina")
    for jobs in weight_jobs:
        for job in jobs:
            comm.submit(job)

    def gathered(*jobs):
        comm.require(*jobs)
        return [g.reshape(N_DEV * g.shape[1], g.shape[2]) for job in jobs for g in job.outs]

    vnorm = gathered(weight_jobs[0][0], weight_jobs[0][1])[1].reshape(N_DEV, 2 * n_b, mix_w // N_DEV)
    vnorm = jnp.transpose(vnorm, (1, 0, 2)).reshape(2 * n_b, mix_w)
    vg_full, vb_full = vnorm[:n_b], vnorm[n_b:]

    saved = []
    weights = []
    xf, xb = x0, x0.astype(BF16)
    for i in range(depth):
        j = i // 2
        w_in_t = [gathered(weight_jobs[i][0])[0], gathered(weight_jobs[i][1])[0]]
        proj = matmul(xb, w_in_t, "nt", name=f"in_proj_{i % 2}", comm=comm)
        if i % 2 == 0:
            os_, ls_ = [], []
            for g, d in enumerate(A_DILATIONS):
                o_g, l_g = attention_forward(proj, g, d, groups, name=f"attn_fwd_d{d}", comm=comm)
                os_.append(o_g)
                ls_.append(l_g)
            o, cat, lse = attention_combine(os_, ls_, name="attn_combine")
            qcol = 9 * mix_w
            extra = (o, lse)
        else:
            cat = gmlp_forward(proj, w_s[j], b_s[j], vg_full[j], vb_full[j], name="gmlp_fwd")
            qcol = 2 * mix_w
            extra = ()
        w_kv, w_o_t = gathered(weight_jobs[i][2])
        kv = matmul(mem_b, w_kv, "nn", name="mem_kv")
        cat = memory_attention(proj, qcol, kv, cat, name=f"mem_attn_{i % 2}")
        y1 = matmul(cat, w_o_t, "nt", name="out_proj", comm=comm)
        r1, x1, x1b = ln_residual(xf, y1, ln1_g[i], ln1_b[i], alpha, name="ln1", comm=comm)
        w_1_t = gathered(weight_jobs[i][3], weight_jobs[i][4])
        h, hid = matmul(x1b, w_1_t, "nt", name="ff1", out_dtypes=(F32, BF16), comm=comm,
                        epilogue=lambda acc: (acc, jnp.square(jnp.maximum(acc, 0.0))))
        w_2 = gathered(weight_jobs[i][5], weight_jobs[i][6])
        y2 = matmul(hid, w_2, "nn", name="ff2", comm=comm)
        r2, x2, x2b = ln_residual(x1, y2, ln2_g[i], ln2_b[i], alpha, name="ln2", comm=comm)
        weights.append((w_in_t, w_o_t, w_1_t, w_2, w_kv))
        saved.append((xb, proj, kv, extra, qcol, cat, r1, x1b, h, hid, r2))
        xf, xb = x2, x2b

    loss_tile, dx = loss_head(xf, target, name="loss_head")
    loss = lax.psum(loss_tile[0, 0], ("x", "y", "c"))

    def exchange(dw, tag):
        rows, cols = dw.shape[0] // N_DEV, dw.shape[1]
        p = dw.reshape(N_DEV, rows, cols)
        whole_us = ExchangeJob.US_PER_MB * dw.size * dw.dtype.itemsize / 1e6
        counts = [q for q in range(1, rows // 16 + 1) if (rows // 16) % q == 0]
        n_chunks = next((q for q in counts if whole_us <= ExchangeJob.CHUNK_US * q), counts[-1])
        step = rows // n_chunks
        return [comm.submit(ExchangeJob(p, q * step, step, f"x{tag}{q}")) for q in range(n_chunks)]

    recv = {n: [None] * depth for n in ("w_in", "w_mem_kv", "w_out", "w_ff1", "w_ff2")}
    small = {n: [None] * depth for n in ("ln1_g", "ln1_b", "ln2_g", "ln2_b")}
    small_b = {n: [None] * n_b for n in ("w_s", "b_s", "vnorm_g", "vnorm_b")}
    for i in reversed(range(depth)):
        w_in_t, w_o_t, w_1_t, w_2, w_kv = weights[i]
        xb, proj, kv, extra, qcol, cat, r1, x1b, h, hid, r2 = saved[i]
        j = i // 2
        dr2, dr2b, small["ln2_g"][i], small["ln2_b"][i] = ln_backward(r2, ln2_g[i], dx, name="ln2_bwd", comm=comm)
        dh = matmul(dr2b, w_2, "nt", name="ff2_dx", extras=(h,), out_dtypes=(BF16,), comm=comm,
                    epilogue=lambda acc, h_t: (acc * (2.0 * jnp.maximum(h_t, 0.0)),))
        recv["w_ff2"][i] = exchange(matmul(hid, dr2b, "tn", name="ff2_dw", out_dtypes=(BF16,), comm=comm), f"ff2{i}")
        dx1 = matmul(dh, w_1_t, "nn", name="ff1_dx", extras=(dr2,), comm=comm, epilogue=lambda acc, res: (acc + alpha * res,))
        recv["w_ff1"][i] = exchange(matmul(dh, x1b, "tn", name="ff1_dw", out_dtypes=(BF16,), comm=comm), f"ff1{i}")
        dr1, dr1b, small["ln1_g"][i], small["ln1_b"][i] = ln_backward(r1, ln1_g[i], dx1, name="ln1_bwd", comm=comm)
        dcat = matmul(dr1b, w_o_t, "nn", name="out_proj_dx", comm=comm)
        recv["w_out"][i] = exchange(matmul(dr1b, cat, "tn", name="out_proj_dw", out_dtypes=(BF16,), comm=comm), f"out{i}")
        dq_mem, dkv = memory_attention_backward(proj, qcol, kv, dcat, mix_w, name=f"mem_attn_bwd_{i % 2}")
        recv["w_mem_kv"][i] = exchange(matmul(mem_b, dkv, "tn", name="mem_kv_dw", out_dtypes=(BF16,)), f"kv{i}")
        if i % 2 == 0:
            o, lse = extra
            pieces = []
            for g, d in enumerate(A_DILATIONS):
                pieces += [t.astype(BF16) for t in attention_backward(proj, o, lse, dcat, g, d, groups, name=f"attn_bwd_d{d}", comm=comm)]
            dproj = jnp.concatenate(pieces + [dq_mem], axis=1)
        else:
            dpu, dpv, small_b["w_s"][j], small_b["b_s"][j], small_b["vnorm_g"][j], small_b["vnorm_b"][j] = gmlp_backward(
                proj, w_s[j], b_s[j], vg_full[j], vb_full[j], dcat, name="gmlp_bwd")
            dproj = jnp.concatenate([dpu, dpv, dq_mem], axis=1)
        recv["w_in"][i] = exchange(matmul(dproj, xb, "tn", name=f"in_proj_dw_{i % 2}", out_dtypes=(BF16,), comm=comm), f"in{i}")
        dx = matmul(dproj, w_in_t, "nn", name=f"in_proj_dx_{i % 2}", extras=(dr1,), comm=comm, epilogue=lambda acc, res: (acc + alpha * res,))
    grad_x = dx[None]

    flat_small = jnp.concatenate(
        [jnp.concatenate(small[n], axis=0).reshape(-1) for n in ("ln1_g", "ln1_b", "ln2_g", "ln2_b")]
        + [jnp.stack(small_b[n]).reshape(-1) for n in ("w_s", "b_s", "vnorm_g", "vnorm_b")])
    pad = (-flat_small.size) % 1024
    small_job = comm.submit(GatherJob([jnp.pad(flat_small, (0, pad)).reshape(-1, 1024)], "gsmall"))

    results = {}

    def received(name, i):
        return comm.require(*recv[name][i])

    def update(name, layers, w_, m_, v_, transposed):
        sums = [sum_parts(received(name, i), name=f"sum_{name}_{i % 2}", comm=comm) for i in layers]
        grad = jnp.stack([t.T if transposed else t for t in sums])
        flat = lambda t: t.reshape(-1, t.shape[-1])
        upd = adamw(flat(grad)[None], flat(w_), flat(m_), flat(v_), name=f"adamw_{name}_{layers[0] % 2}", emit_grad=False)
        return [grad] + [t.reshape(w_.shape) for t in upd]

    results["w_ff2"] = update("w_ff2", list(range(depth)), w_ff2, m_w_ff2, v_w_ff2, False)
    results["w_ff1"] = update("w_ff1", list(range(depth)), w_ff1, m_w_ff1, v_w_ff1, True)
    results["w_out"] = update("w_out", list(range(depth)), w_out, m_w_out, v_w_out, True)
    results["w_mem_kv"] = update("w_mem_kv", list(range(depth)), w_mem_kv, m_w_mem_kv, v_w_mem_kv, False)
    results["w_in_b"] = update("w_in", list(range(1, depth, 2)), w_in_b, m_w_in_b, v_w_in_b, True)
    results["w_in_a"] = update("w_in", list(range(0, depth, 2)), w_in_a, m_w_in_a, v_w_in_a, True)

    gathered_small = comm.require(small_job)[0].reshape(N_DEV, -1)
    at = 0

    def take(shape):
        nonlocal at
        size = math.prod(shape)
        at += size
        return gathered_small[:, at - size:at].reshape((N_DEV,) + shape)

    for n, w_, m_, v_ in (("ln1_g", ln1_g, m_ln1_g, v_ln1_g), ("ln1_b", ln1_b, m_ln1_b, v_ln1_b),
                          ("ln2_g", ln2_g, m_ln2_g, v_ln2_g), ("ln2_b", ln2_b, m_ln2_b, v_ln2_b)):
        results[n] = adamw(take(w_.shape), w_, m_, v_, name="adamw_ln")
    lanes = lambda t: t.reshape(-1, 128)
    results["w_s"] = [t.reshape(w_s.shape) for t in adamw(take((w_s.size // 128, 128)), lanes(w_s), lanes(m_w_s), lanes(v_w_s), name="adamw_w_s")]
    results["b_s"] = [t.reshape(b_s.shape) for t in adamw(take((b_s.size // 128, 128)), lanes(b_s), lanes(m_b_s), lanes(v_b_s), name="adamw_b_s")]
    for n, w_, m_, v_ in (("vnorm_g", vnorm_g, m_vnorm_g, v_vnorm_g), ("vnorm_b", vnorm_b, m_vnorm_b, v_vnorm_b)):
        parts = lax.dynamic_slice_in_dim(take((n_b, mix_w)), me * w_.shape[1], w_.shape[1], axis=2)
        results[n] = adamw(parts, w_, m_, v_, name="adamw_vnorm")

    order =("w_in_a", "w_in_b", "w_s", "b_s", "vnorm_g", "vnorm_b", "w_mem_kv", "w_out", "ln1_g", "ln1_b", "w_ff1", "w_ff2", "ln2_g", "ln2_b")
    return (loss, grad_x, *[results[n][0] for n in order], *[results[n][1] for n in order],
            *[results[n][2] for n in order], *[results[n][3] for n in order])
```

```python
import math

import jax
import jax.numpy as jnp
from jax import lax
from jax.experimental import pallas as pl
from jax.experimental.pallas import tpu as pltpu

F32 = jnp.float32
BF16 = jnp.bfloat16
N_DEV = 8
HEAD_DIM = 128
MEM_HEADS = 4
MEM_WIDTH = MEM_HEADS * HEAD_DIM
A_DILATIONS = (1, 4, 16)
LN_EPS = 1e-5
ADAM_LR, ADAM_B1, ADAM_B2, ADAM_EPS, ADAM_WD, ADAM_STEP = 0.001, 0.9, 0.999, 1e-08, 0.01, 10
VMEM_LIMIT_BYTES = 56 * 1024 * 1024
MESH = pl.DeviceIdType.MESH
ANY = pl.BlockSpec(memory_space=pl.ANY)


def _params(*sem):
    return pltpu.CompilerParams(dimension_semantics=sem, vmem_limit_bytes=VMEM_LIMIT_BYTES)


def _tile(dim, pref):
    if dim <= pref:
        return dim
    best = None
    for t in range(128, pref + 1, 128):
        if dim % t == 0:
            best = t
    return best if best is not None else dim


def _dot(a, b, dims):
    return lax.dot_general(a, b, (dims, ((), ())), preferred_element_type=F32)


NN = ((1,), (0,))
NT = ((1,), (1,))
TN = ((0,), (0,))


def _my_position():
    return lax.axis_index("x"), lax.axis_index("y"), lax.axis_index("c")


class GatherJob:
    US_PER_MB = 46.0
    OVERSHOOT = 1.1

    def __init__(self, arrays, tag):
        self.arrays, self.tag, self.outs = list(arrays), tag, None
        self.out_shapes = [jax.ShapeDtypeStruct((N_DEV,) + v.shape, v.dtype) for v in self.arrays]
        self.n_sems, self.n_local = 7 * len(self.arrays), len(self.arrays)
        self.est_us = self.US_PER_MB * sum(v.size * v.dtype.itemsize for v in self.arrays) / 1e6

    def phases(self, v_refs, out_refs, send_sems, recv_sems, local_sems, sem0, loc0):
        n = len(self.arrays)
        x, y, c = _my_position()
        me, sibling = (x, y, c), (x, y, 1 - c)
        chips = [(1 - x, y), (x, 1 - y), (1 - x, 1 - y)]

        def copy(a, k, block, to, from_input=False):
            px, py, pc = block
            slot = out_refs[a].at[4 * px + 2 * py + pc]
            return pltpu.make_async_remote_copy(
                src_ref=v_refs[a] if from_input else slot, dst_ref=slot, send_sem=send_sems.at[sem0 + 7 * a + k],
                recv_sem=recv_sems.at[sem0 + 7 * a + k], device_id=to, device_id_type=MESH)

        def mine(a):
            return pltpu.make_async_copy(v_refs[a], out_refs[a].at[4 * x + 2 * y + c], local_sems.at[loc0 + a])

        def first(a):
            return [copy(a, 0, me, sibling, True)] + [copy(a, 1 + j, me, (*chip, c), True) for j, chip in enumerate(chips)]

        def start():
            for a in range(n):
                mine(a).start()
                for cp in first(a):
                    cp.start()

        def middle():
            for j, chip in enumerate(chips):
                for a in range(n):
                    copy(a, 1 + j, (*chip, c), me).wait_recv()
                    copy(a, 4 + j, (*chip, c), sibling).start()

        def finish():
            for a in range(n):
                copy(a, 0, sibling, me).wait_recv()
                for j, chip in enumerate(chips):
                    copy(a, 4 + j, (*chip, 1 - c), me).wait_recv()
                for cp in first(a) + [copy(a, 4 + j, (*chip, c), sibling) for j, chip in enumerate(chips)]:
                    cp.wait_send()
                mine(a).wait()

        return start, middle, finish


class ExchangeJob:
    US_PER_MB = 11.0
    OVERSHOOT = 1.05
    CHUNK_US = 50.0
    RELATIONS = [(dx, dy, dc) for dx in (0, 1) for dy in (0, 1) for dc in (0, 1)][1:]

    def __init__(self, p, row0, rows, tag):
        self.arrays, self.tag, self.outs = [p], tag, None
        self.row0, self.rows = row0, rows
        self.out_shapes = [jax.ShapeDtypeStruct((N_DEV, rows, p.shape[2]), p.dtype)]
        self.n_sems, self.n_local = 7, 1
        self.est_us = self.US_PER_MB * N_DEV * rows * p.shape[2] * p.dtype.itemsize / 1e6

    def phases(self, p_refs, out_refs, send_sems, recv_sems, local_sems, sem0, loc0):
        (p_ref,), (out_ref,) = p_refs, out_refs
        x, y, c = _my_position()
        me = 4 * x + 2 * y + c
        chunk = pl.ds(self.row0, self.rows)

        def mine():
            return pltpu.make_async_copy(p_ref.at[me, chunk], out_ref.at[me], local_sems.at[loc0])

        def copies(arriving):
            made = []
            for k, (dx, dy, dc) in enumerate(self.RELATIONS):
                px, py, pc = (x + dx) % 2, (y + dy) % 2, (c + dc) % 2
                peer = 4 * px + 2 * py + pc
                made.append(pltpu.make_async_remote_copy(
                    src_ref=p_ref.at[peer, chunk], dst_ref=out_ref.at[peer if arriving else me],
                    send_sem=send_sems.at[sem0 + k], recv_sem=recv_sems.at[sem0 + k],
                    device_id=(px, py, pc), device_id_type=MESH))
            return made

        def start():
            mine().start()
            for send in copies(False):
                send.start()

        def finish():
            for arrival in copies(True):
                arrival.wait_recv()
            for send in copies(False):
                send.wait_send()
            mine().wait()

        return start, (lambda: None), finish


class Exchanges:
    MIN_HOST_US = 20.0
    MAX_JOBS = 8

    def __init__(self):
        self.queue = []

    def submit(self, job):
        self.queue.append(job)
        return job

    def take(self, host_us):
        jobs, used = [], 0.0
        if host_us >= self.MIN_HOST_US:
            for job in list(self.queue):
                if used + job.est_us <= job.OVERSHOOT * host_us and len(jobs) < self.MAX_JOBS:
                    used += job.est_us
                    jobs.append(job)
                    self.queue.remove(job)
        return jobs

    def require(self, *jobs):
        waiting = [job for job in jobs if job.outs is None]
        if waiting:
            for job in waiting:
                self.queue.remove(job)
            _call(lambda: None, grid=(), in_specs=[], out_specs=[], out_shape=[], scratch_shapes=[], args=[],
                  name="exchange", semantics=(), jobs=waiting)
        return [job.outs[0] for job in jobs]


def _call(body, *, grid, in_specs, out_specs, out_shape, scratch_shapes, args, name, semantics, jobs=()):
    if not jobs:
        return pl.pallas_call(body, grid=grid, in_specs=in_specs, out_specs=out_specs, out_shape=out_shape,
                              scratch_shapes=scratch_shapes, compiler_params=_params(*semantics), name=name)(*args)
    n_in, n_out, n_scr = len(in_specs), len(out_shape), len(scratch_shapes)
    j_in = [a for job in jobs for a in job.arrays]
    j_out = [s for job in jobs for s in job.out_shapes]
    n_sems, n_local = sum(job.n_sems for job in jobs), sum(job.n_local for job in jobs)
    steps = math.prod(grid)
    middle_step = (7 * steps) // 8

    def wrapped(*refs):
        ins, refs = refs[:n_in], refs[n_in:]
        j_ins, refs = refs[:len(j_in)], refs[len(j_in):]
        outs, refs = refs[:n_out], refs[n_out:]
        j_outs, refs = refs[:len(j_out)], refs[len(j_out):]
        scratch, (send_sems, recv_sems, local_sems) = refs[:n_scr], refs[n_scr:]
        step = 0
        for axis, extent in enumerate(grid):
            step = step * extent + pl.program_id(axis)
        phases, at_in, at_out, sem0, loc0 = [], 0, 0, 0, 0
        for job in jobs:
            k_in, k_out = len(job.arrays), len(job.out_shapes)
            phases.append(job.phases(j_ins[at_in:at_in + k_in], j_outs[at_out:at_out + k_out], send_sems, recv_sems, local_sems, sem0, loc0))
            at_in, at_out, sem0, loc0 = at_in + k_in, at_out + k_out, sem0 + job.n_sems, loc0 + job.n_local

        def run(which):
            for ph in phases:
                ph[which]()

        if steps == 1:
            run(0)
            body(*ins, *outs, *scratch)
            run(1)
            run(2)
        else:
            pl.when(step == 0)(lambda: run(0))
            body(*ins, *outs, *scratch)
            pl.when(step == middle_step)(lambda: run(1))
            pl.when(step == steps - 1)(lambda: run(2))

    res = pl.pallas_call(
        wrapped, grid=grid, in_specs=list(in_specs) + [ANY] * len(j_in), out_specs=list(out_specs) + [ANY] * len(j_out),
        out_shape=list(out_shape) + j_out,
        scratch_shapes=list(scratch_shapes) + [pltpu.SemaphoreType.DMA((n_sems,)), pltpu.SemaphoreType.DMA((n_sems,)),
                                               pltpu.SemaphoreType.DMA((n_local,))],
        compiler_params=_params(*(["arbitrary"] * len(grid))), name=name + "".join("__" + job.tag for job in jobs),
    )(*args, *j_in)
    at = n_out
    for job in jobs:
        job.outs = list(res[at:at + len(job.out_shapes)])
        at += len(job.out_shapes)
    return list(res[:n_out])


MATMUL_TILES = {"nn": (1024, 512, 2560), "nt": (1024, 512, 2560), "tn": (1024, 1024, 2048)}
MATMUL_FLOPS_PER_US = {"nn": 7.5e8, "nt": 7.5e8, "tn": 8.5e8}


def matmul(a, b, mode, name, epilogue=None, extras=(), out_dtypes=(F32,), comm=None):
    pieces = list(b) if isinstance(b, (list, tuple)) else [b]
    n_pc = len(pieces)
    b_rows, b_cols = pieces[0].shape[0], n_pc * pieces[0].shape[1]
    assert mode != "tn" or n_pc == 1
    if mode == "nn":
        (m, k), (k2, n) = a.shape, (b_rows, b_cols)
    elif mode == "nt":
        (m, k), (n, k2) = a.shape, (b_rows, b_cols)
    else:
        (k, m), (k2, n) = a.shape, (b_rows, b_cols)
    assert k == k2, (a.shape, (b_rows, b_cols), mode)
    tm_pref, tn_pref, tk_pref = MATMUL_TILES[mode]
    tm = _tile(m, tm_pref)
    if mode == "tn" and tm < tm_pref:
        tn_pref *= 2
    tn = _tile(n // n_pc if mode == "nn" else n, tn_pref)
    tk = _tile(k, tk_pref)
    nj, nk = n // tn, k // tk
    assert mode != "nt" or n_pc == 1 or nk == 1, (k, tk)
    per_piece = nj // n_pc
    k_piece = k // n_pc
    dims = {"nn": NN, "nt": NT, "tn": TN}[mode]
    a_spec = pl.BlockSpec((tk, tm), lambda i, j, kk: (kk, i)) if mode == "tn" else pl.BlockSpec((tm, tk), lambda i, j, kk: (i, kk))

    def b_spec(p):
        if mode == "nt" and n_pc > 1:
            return pl.BlockSpec((tn, k_piece), lambda i, j, kk: (j, 0))
        if mode == "nt":
            return pl.BlockSpec((tn, tk), lambda i, j, kk: (j, kk))
        if n_pc == 1:
            return pl.BlockSpec((tk, tn), lambda i, j, kk: (kk, j))
        return pl.BlockSpec((tk, tn), lambda i, j, kk: (jnp.where(j // per_piece == p, kk, 0),
                                                       jnp.where(j // per_piece == p, j - p * per_piece, 0)))

    tile_spec = pl.BlockSpec((tm, tn), lambda i, j, kk: (i, j))
    n_ex, n_out = len(extras), len(out_dtypes)
    if epilogue is None:
        epilogue = lambda acc: (acc,) * n_out

    def body(a_ref, *rest):
        b_refs, rest = rest[:n_pc], rest[n_pc:]
        ex_refs, out_refs, acc_ref = rest[:n_ex], rest[n_ex:n_ex + n_out], rest[-1]
        j, kk = pl.program_id(1), pl.program_id(2)

        def finish(acc):
            for o_ref, val in zip(out_refs, epilogue(acc, *[e[...] for e in ex_refs])):
                o_ref[...] = val.astype(o_ref.dtype)

        if mode == "nt" and n_pc > 1:
            finish(sum(_dot(a_ref[:, p * k_piece:(p + 1) * k_piece].astype(BF16), b_ref[...].astype(BF16), NT)
                       for p, b_ref in enumerate(b_refs)))
            return

        for p, b_ref in enumerate(b_refs):
            in_use = True if n_pc == 1 else (j // per_piece == p)

            def product():
                return _dot(a_ref[...].astype(BF16), b_ref[...].astype(BF16), dims)

            if nk == 1:
                if n_pc == 1:
                    finish(product())
                else:
                    pl.when(in_use)(lambda: finish(product()))
                continue

            @pl.when(in_use & (kk == 0))
            def _():
                acc_ref[...] = product()

            @pl.when(in_use & (kk > 0) & (kk < nk - 1))
            def _():
                acc_ref[...] += product()

            @pl.when(in_use & (kk == nk - 1))
            def _():
                finish(acc_ref[...] + product())

    host_us = 2.0 * m * n * k / MATMUL_FLOPS_PER_US[mode]
    outs = _call(
        body,
        grid=(m // tm, nj, nk),
        in_specs=[a_spec] + [b_spec(p) for p in range(n_pc)] + [tile_spec] * n_ex,
        out_specs=[tile_spec] * n_out,
        out_shape=[jax.ShapeDtypeStruct((m, n), dt) for dt in out_dtypes],
        scratch_shapes=[pltpu.VMEM((tm, tn), F32)],
        args=[a, *pieces, *extras], name=name, semantics=("parallel", "parallel", "arbitrary"),
        jobs=comm.take(host_us) if comm is not None else (),
    )
    return outs[0] if n_out == 1 else outs


LN_BYTES_PER_US = 3.0e6


def ln_residual(x, y, g, b, alpha, name, comm=None):
    s, w = x.shape
    tb = _tile(s, 256)
    row = pl.BlockSpec((tb, w), lambda i: (i, 0))
    vec = pl.BlockSpec((1, w), lambda i: (0, 0))

    def body(x_ref, y_ref, g_ref, b_ref, r_ref, xn_ref, xnb_ref):
        r = alpha * x_ref[...] + y_ref[...]
        mu = jnp.mean(r, axis=-1, keepdims=True)
        var = jnp.mean(jnp.square(r - mu), axis=-1, keepdims=True)
        xn = (r - mu) * lax.rsqrt(var + LN_EPS) * g_ref[...] + b_ref[...]
        r_ref[...] = r
        xn_ref[...] = xn
        xnb_ref[...] = xn.astype(BF16)

    return _call(
        body, grid=(s // tb,), in_specs=[row, row, vec, vec], out_specs=[row, row, row],
        out_shape=[jax.ShapeDtypeStruct((s, w), F32), jax.ShapeDtypeStruct((s, w), F32), jax.ShapeDtypeStruct((s, w), BF16)],
        scratch_shapes=[], args=[x, y, g.reshape(1, w), b.reshape(1, w)], name=name, semantics=("parallel",),
        jobs=comm.take(18.0 * s * w / LN_BYTES_PER_US) if comm is not None else (),
    )


def _ln_bwd_tile(r, g, dxn):
    mu = jnp.mean(r, axis=-1, keepdims=True)
    cen = r - mu
    rstd = lax.rsqrt(jnp.mean(jnp.square(cen), axis=-1, keepdims=True) + LN_EPS)
    xhat = cen * rstd
    dxh = dxn * g
    dr = rstd * (dxh - jnp.mean(dxh, axis=-1, keepdims=True) - xhat * jnp.mean(dxh * xhat, axis=-1, keepdims=True))
    return dr, jnp.sum(dxn * xhat, axis=0, keepdims=True)


def ln_backward(r, g, dxn, name, comm=None):
    s, w = r.shape
    tb = _tile(s, 256)
    row = pl.BlockSpec((tb, w), lambda i: (i, 0))
    vec = pl.BlockSpec((1, w), lambda i: (0, 0))

    def body(r_ref, g_ref, d_ref, dr_ref, drb_ref, dg_ref, db_ref):
        @pl.when(pl.program_id(0) == 0)
        def _():
            dg_ref[...] = jnp.zeros_like(dg_ref)
            db_ref[...] = jnp.zeros_like(db_ref)

        dxn = d_ref[...]
        dr, dg = _ln_bwd_tile(r_ref[...], g_ref[...], dxn)
        dr_ref[...] = dr
        drb_ref[...] = dr.astype(BF16)
        dg_ref[...] += dg
        db_ref[...] += jnp.sum(dxn, axis=0, keepdims=True)

    return _call(
        body, grid=(s // tb,), in_specs=[row, vec, row], out_specs=[row, row, vec, vec],
        out_shape=[jax.ShapeDtypeStruct((s, w), F32), jax.ShapeDtypeStruct((s, w), BF16),
                   jax.ShapeDtypeStruct((1, w), F32), jax.ShapeDtypeStruct((1, w), F32)],
        scratch_shapes=[], args=[r, g.reshape(1, w), dxn], name=name, semantics=("arbitrary",),
        jobs=comm.take(14.0 * s * w / LN_BYTES_PER_US) if comm is not None else (),
    )


ATTENTION_US = {("fwd", 1): 55.0, ("fwd", 4): 80.0, ("fwd", 16): 100.0, ("bwd", 1): 115.0, ("bwd", 4): 130.0, ("bwd", 16): 190.0}
ATTENTION_HEADS_PER_STEP = {("fwd", 1): 8, ("fwd", 4): 1, ("fwd", 16): 1, ("bwd", 1): 8, ("bwd", 4): 1, ("bwd", 16): 1}


def _causal_masks():
    qi = lax.broadcasted_iota(jnp.int32, (128, 128), 0)
    kj = lax.broadcasted_iota(jnp.int32, (128, 128), 1)
    return kj <= qi, kj >= qi


def _window_mask(has_previous):
    qi = lax.broadcasted_iota(jnp.int32, (128, 256), 0)
    kj = lax.broadcasted_iota(jnp.int32, (128, 256), 1)
    return ((kj < 128) & (kj >= qi) & has_previous) | ((kj >= 128) & (kj - 128 <= qi))


def _sub_rows(d, r):
    return pl.ds(r, 128, stride=d) if d > 1 else pl.ds(0, 128)


def _for_each_residue(d, fn):
    if d <= 4:
        for r in range(d):
            fn(r)
    else:
        def step(r, carry):
            fn(r)
            return carry
        lax.fori_loop(0, d, step, 0, unroll=8)


def attention_forward(proj, group, d, heads, name, comm=None):
    s = proj.shape[0]
    rows = 128 * d
    nb = s // rows
    scale = HEAD_DIM ** -0.5
    hp = math.gcd(heads, ATTENTION_HEADS_PER_STEP["fwd", d])
    wide = 128 * hp
    qc, kc, vc = (group * 3) * heads // hp, (group * 3 + 1) * heads // hp, (group * 3 + 2) * heads // hp

    def cur(col):
        return pl.BlockSpec((rows, wide), lambda h, n: (n, col + h))

    def prev(col):
        return pl.BlockSpec((rows, wide), lambda h, n: (jnp.maximum(n - 1, 0), col + h))

    out = pl.BlockSpec((rows, wide), lambda h, n: (n, h))

    def body(q_ref, kc_ref, kp_ref, vc_ref, vp_ref, o_ref, l_ref):
        mask = _window_mask(pl.program_id(1) > 0)

        def one(r):
            rws = _sub_rows(d, r)
            for hh in range(hp):
                sl = (rws, pl.ds(hh * 128, 128))
                k = jnp.concatenate([kp_ref[sl].astype(BF16), kc_ref[sl].astype(BF16)], axis=0)
                v = jnp.concatenate([vp_ref[sl].astype(BF16), vc_ref[sl].astype(BF16)], axis=0)
                sc = jnp.where(mask, _dot(q_ref[sl].astype(BF16), k, NT) * scale, -jnp.inf)
                m = jnp.max(sc, axis=-1, keepdims=True)
                e = jnp.exp(sc - m)
                l = jnp.sum(e, axis=-1, keepdims=True)
                o_ref[sl] = _dot(e.astype(BF16), v, NN) / l
                l_ref[sl] = jnp.broadcast_to(m + jnp.log(l), (128, 128))

        _for_each_residue(d, one)

    return _call(
        body, grid=(heads // hp, nb),
        in_specs=[cur(qc), cur(kc), prev(kc), cur(vc), prev(vc)], out_specs=[out, out],
        out_shape=[jax.ShapeDtypeStruct((s, heads * 128), F32)] * 2, scratch_shapes=[],
        args=[proj] * 5, name=name, semantics=("parallel", "parallel"),
        jobs=comm.take(ATTENTION_US["fwd", d] * s * heads / (4096 * 8)) if comm is not None else (),
    )


def attention_combine(os_, ls_, name):
    s, w = os_[0].shape
    tb = _tile(s, 256)
    row = pl.BlockSpec((tb, w), lambda i: (i, 0))

    def body(o0, o1, o2, l0, l1, l2, o_ref, ob_ref, lse_ref):
        a, b, c = l0[...], l1[...], l2[...]
        m = jnp.maximum(jnp.maximum(a, b), c)
        ea, eb, ec = jnp.exp(a - m), jnp.exp(b - m), jnp.exp(c - m)
        tot = ea + eb + ec
        o = (ea / tot) * o0[...] + (eb / tot) * o1[...] + (ec / tot) * o2[...]
        o_ref[...] = o
        ob_ref[...] = o.astype(BF16)
        lse_ref[...] = m + jnp.log(tot)

    return pl.pallas_call(
        body, grid=(s // tb,), in_specs=[row] * 6, out_specs=[row] * 3,
        out_shape=[jax.ShapeDtypeStruct((s, w), F32), jax.ShapeDtypeStruct((s, w + MEM_WIDTH), BF16), jax.ShapeDtypeStruct((s, w), F32)],
        compiler_params=_params("parallel"), name=name,
    )(*os_, *ls_)


def attention_backward(proj, o, lse, dcat, group, d, heads, name, comm=None):
    s = proj.shape[0]
    rows = 128 * d
    nb = s // rows
    scale = HEAD_DIM ** -0.5
    hp = math.gcd(heads, ATTENTION_HEADS_PER_STEP["bwd", d])
    wide = 128 * hp
    qc, kc, vc = (group * 3) * heads // hp, (group * 3 + 1) * heads // hp, (group * 3 + 2) * heads // hp

    def at(col, shift):
        return pl.BlockSpec((rows, wide), lambda h, n: (jnp.clip(n + shift, 0, nb - 1), col + h))

    def out(col):
        return pl.BlockSpec((rows, wide), lambda h, n: (n, col + h))

    def body(qc_ref, qn_ref, kc_ref, kp_ref, vc_ref, vp_ref, doc_ref, don_ref, oc_ref, on_ref, lc_ref, ln_ref,
             dq_ref, dk_ref, dv_ref):
        n = pl.program_id(1)
        mask_w = _window_mask(n > 0)
        mask_n = _causal_masks()[1] & (n < nb - 1)

        def tile(q, k, v, do, lse_t, dsum, mask):
            p = jnp.where(mask, jnp.exp(_dot(q, k, NT) * scale - lse_t), 0.0)
            ds = (p * (_dot(do, v, NT) - dsum) * scale).astype(BF16)
            return p.astype(BF16), ds

        def one(r):
            rws = _sub_rows(d, r)
            for hh in range(hp):
                sl = (rws, pl.ds(hh * 128, 128))
                q_c, q_n = qc_ref[sl].astype(BF16), qn_ref[sl].astype(BF16)
                k_c, v_c = kc_ref[sl].astype(BF16), vc_ref[sl].astype(BF16)
                k_w = jnp.concatenate([kp_ref[sl].astype(BF16), k_c], axis=0)
                v_w = jnp.concatenate([vp_ref[sl].astype(BF16), v_c], axis=0)
                do_c, do_n = doc_ref[sl], don_ref[sl]
                dsum_c = jnp.sum(do_c * oc_ref[sl], axis=-1, keepdims=True)
                dsum_n = jnp.sum(do_n * on_ref[sl], axis=-1, keepdims=True)
                do_c, do_n = do_c.astype(BF16), do_n.astype(BF16)
                lse_c = lc_ref[sl]
                p_w, ds_w = tile(q_c, k_w, v_w, do_c, jnp.concatenate([lse_c, lse_c], axis=1), dsum_c, mask_w)
                p_n, ds_n = tile(q_n, k_c, v_c, do_n, ln_ref[sl], dsum_n, mask_n)
                dq_ref[sl] = _dot(ds_w, k_w, NN)
                both_q = jnp.concatenate([q_c, q_n], axis=0)
                both_do = jnp.concatenate([do_c, do_n], axis=0)
                dk_ref[sl] = _dot(jnp.concatenate([ds_w[:, 128:], ds_n], axis=0), both_q, TN)
                dv_ref[sl] = _dot(jnp.concatenate([p_w[:, 128:], p_n], axis=0), both_do, TN)

        _for_each_residue(d, one)

    w = heads * 128
    return _call(
        body, grid=(heads // hp, nb),
        in_specs=[at(qc, 0), at(qc, 1), at(kc, 0), at(kc, -1), at(vc, 0), at(vc, -1),
                  at(0, 0), at(0, 1), at(0, 0), at(0, 1), at(0, 0), at(0, 1)],
        out_specs=[out(0)] * 3,
        out_shape=[jax.ShapeDtypeStruct((s, w), F32)] * 3, scratch_shapes=[],
        args=[proj] * 6 + [dcat, dcat, o, o, lse, lse], name=name, semantics=("parallel", "parallel"),
        jobs=comm.take(ATTENTION_US["bwd", d] * s * heads / (4096 * 8)) if comm is not None else (),
    )


def _mem_softmax(q, kv, h, scale):
    k = kv[:, h * 128:(h + 1) * 128].astype(BF16)
    v = kv[:, MEM_WIDTH + h * 128:MEM_WIDTH + (h + 1) * 128].astype(BF16)
    sc = _dot(q, k, NT) * scale
    e = jnp.exp(sc - jnp.max(sc, axis=-1, keepdims=True))
    return e / jnp.sum(e, axis=-1, keepdims=True), k, v


def memory_attention(proj, qcol, kv, cat, name):
    s = proj.shape[0]
    tb = _tile(s, 512)
    scale = HEAD_DIM ** -0.5

    def body(q_ref, kv_ref, cat_ref, o_ref):
        kv_t = kv_ref[...]
        for h in range(MEM_HEADS):
            p, _, v = _mem_softmax(q_ref[:, h * 128:(h + 1) * 128].astype(BF16), kv_t, h, scale)
            o_ref[:, h * 128:(h + 1) * 128] = _dot(p.astype(BF16), v, NN).astype(BF16)

    return pl.pallas_call(
        body, grid=(s // tb,),
        in_specs=[pl.BlockSpec((tb, MEM_WIDTH), lambda i: (i, qcol // MEM_WIDTH)), pl.BlockSpec(kv.shape, lambda i: (0, 0)), ANY],
        out_specs=pl.BlockSpec((tb, MEM_WIDTH), lambda i: (i, cat.shape[1] // MEM_WIDTH - 1)),
        out_shape=jax.ShapeDtypeStruct(cat.shape, BF16), input_output_aliases={2: 0},
        compiler_params=_params("parallel"), name=name,
    )(proj, kv, cat)


def memory_attention_backward(proj, qcol, kv, dcat, dcol, name):
    s = proj.shape[0]
    tb = _tile(s, 512)
    scale = HEAD_DIM ** -0.5

    def body(q_ref, kv_ref, do_ref, dq_ref, dkv_ref):
        @pl.when(pl.program_id(0) == 0)
        def _():
            dkv_ref[...] = jnp.zeros_like(dkv_ref)

        kv_t = kv_ref[...]
        for h in range(MEM_HEADS):
            cols = slice(h * 128, (h + 1) * 128)
            q = q_ref[:, cols].astype(BF16)
            p, k, v = _mem_softmax(q, kv_t, h, scale)
            do = do_ref[:, cols].astype(BF16)
            dp = _dot(do, v, NT)
            ds = (p * (dp - jnp.sum(dp * p, axis=-1, keepdims=True)) * scale).astype(BF16)
            dq_ref[:, cols] = _dot(ds, k, NN).astype(BF16)
            dkv_ref[:, cols] += _dot(ds, q, TN)
            dkv_ref[:, MEM_WIDTH + h * 128:MEM_WIDTH + (h + 1) * 128] += _dot(p.astype(BF16), do, TN)

    return pl.pallas_call(
        body, grid=(s // tb,),
        in_specs=[pl.BlockSpec((tb, MEM_WIDTH), lambda i: (i, qcol // MEM_WIDTH)), pl.BlockSpec(kv.shape, lambda i: (0, 0)),
                  pl.BlockSpec((tb, MEM_WIDTH), lambda i: (i, dcol // MEM_WIDTH))],
        out_specs=[pl.BlockSpec((tb, MEM_WIDTH), lambda i: (i, 0)), pl.BlockSpec(kv.shape, lambda i: (0, 0))],
        out_shape=[jax.ShapeDtypeStruct((s, MEM_WIDTH), BF16), jax.ShapeDtypeStruct(kv.shape, F32)],
        compiler_params=_params("arbitrary"), name=name,
    )(proj, kv, dcat)


_SQRT_HALF = math.sqrt(0.5)
_INV_SQRT_2PI = 1.0 / math.sqrt(2.0 * math.pi)


def _gelu(x):
    return 0.5 * x * (1.0 + lax.erf(x * _SQRT_HALF))


def _gelu_grad(x):
    return 0.5 * (1.0 + lax.erf(x * _SQRT_HALF)) + x * (_INV_SQRT_2PI * jnp.exp(-0.5 * x * x))


def _gmlp_specs(s, wd, groups, tb):
    half = lambda c: pl.BlockSpec((tb, wd), lambda i: (i, c))
    ws_spec = pl.BlockSpec((groups, 128, 128), lambda i: (0, 0, 0))
    bs_spec = pl.BlockSpec((groups, 128, 1), lambda i: (0, 0, 0))
    vec = pl.BlockSpec((1, wd), lambda i: (0, 0))
    return half, ws_spec, bs_spec, vec


def _vnorm(zv, g, b):
    mu = jnp.mean(zv, axis=-1, keepdims=True)
    var = jnp.mean(jnp.square(zv - mu), axis=-1, keepdims=True)
    return (zv - mu) * lax.rsqrt(var + LN_EPS) * g + b


def gmlp_forward(proj, ws, bs, vg, vb, name):
    s = proj.shape[0]
    groups = ws.shape[0]
    wd = groups * 128
    tb = _tile(s, 512)
    half, ws_spec, bs_spec, vec = _gmlp_specs(s, wd, groups, tb)

    def body(pu_ref, pv_ref, ws_ref, bs_ref, vg_ref, vb_ref, o_ref, vn_ref):
        causal, _ = _causal_masks()
        vn_ref[...] = _vnorm(_gelu(pv_ref[...]), vg_ref[...], vb_ref[...]).astype(BF16)
        for g in range(groups):
            cols = slice(g * 128, (g + 1) * 128)
            wm = jnp.where(causal, ws_ref[g], 0.0).astype(BF16)
            for c in range(tb // 128):
                rws = slice(c * 128, (c + 1) * 128)
                sg = _dot(wm, vn_ref[rws, cols], NN) + bs_ref[g]
                o_ref[rws, cols] = (_gelu(pu_ref[rws, cols]) * sg).astype(BF16)

    return pl.pallas_call(
        body, grid=(s // tb,),
        in_specs=[half(0), half(1), ws_spec, bs_spec, vec, vec], out_specs=pl.BlockSpec((tb, wd), lambda i: (i, 0)),
        out_shape=jax.ShapeDtypeStruct((s, wd + MEM_WIDTH), BF16),
        scratch_shapes=[pltpu.VMEM((tb, wd), BF16)],
        compiler_params=_params("parallel"), name=name,
    )(proj, proj, ws, bs.reshape(groups, 128, 1), vg.reshape(1, wd), vb.reshape(1, wd))


def gmlp_backward(proj, ws, bs, vg, vb, dcat, name):
    s = proj.shape[0]
    groups = ws.shape[0]
    wd = groups * 128
    tb = _tile(s, 512)
    half, ws_spec, bs_spec, vec = _gmlp_specs(s, wd, groups, tb)

    def body(pu_ref, pv_ref, ws_ref, bs_ref, vg_ref, vb_ref, do_ref, dpu_ref, dpv_ref, dws_ref, dbs_ref, dvg_ref, dvb_ref,
             vn_ref, dvn_ref):
        @pl.when(pl.program_id(0) == 0)
        def _():
            dws_ref[...] = jnp.zeros_like(dws_ref)
            dbs_ref[...] = jnp.zeros_like(dbs_ref)
            dvg_ref[...] = jnp.zeros_like(dvg_ref)
            dvb_ref[...] = jnp.zeros_like(dvb_ref)

        causal, _ = _causal_masks()
        pv = pv_ref[...]
        zv = _gelu(pv)
        vn_ref[...] = _vnorm(zv, vg_ref[...], vb_ref[...]).astype(BF16)
        for g in range(groups):
            cols = slice(g * 128, (g + 1) * 128)
            wm = jnp.where(causal, ws_ref[g], 0.0).astype(BF16)
            dws_g = jnp.zeros((128, 128), F32)
            dbs_g = jnp.zeros((128, 1), F32)
            for c in range(tb // 128):
                rws = slice(c * 128, (c + 1) * 128)
                vn = vn_ref[rws, cols]
                pu = pu_ref[rws, cols]
                do = do_ref[rws, cols]
                sg = _dot(wm, vn, NN) + bs_ref[g]
                dpu_ref[rws, cols] = (do * sg * _gelu_grad(pu)).astype(BF16)
                dsg = do * _gelu(pu)
                dsg_b = dsg.astype(BF16)
                dws_g += _dot(dsg_b, vn, NT)
                dbs_g += jnp.sum(dsg, axis=-1, keepdims=True)
                dvn_ref[rws, cols] = _dot(wm, dsg_b, TN)
            dws_ref[g] += jnp.where(causal, dws_g, 0.0)
            dbs_ref[g] += dbs_g
        dvn = dvn_ref[...]
        dzv, dvg = _ln_bwd_tile(zv, vg_ref[...], dvn)
        dvg_ref[...] += dvg
        dvb_ref[...] += jnp.sum(dvn, axis=0, keepdims=True)
        dpv_ref[...] = (dzv * _gelu_grad(pv)).astype(BF16)

    row = pl.BlockSpec((tb, wd), lambda i: (i, 0))
    dpu, dpv, dws, dbs, dvg, dvb = pl.pallas_call(
        body, grid=(s // tb,),
        in_specs=[half(0), half(1), ws_spec, bs_spec, vec, vec, row],
        out_specs=[row, row, ws_spec, bs_spec, vec, vec],
        out_shape=[jax.ShapeDtypeStruct((s, wd), BF16), jax.ShapeDtypeStruct((s, wd), BF16),
                   jax.ShapeDtypeStruct((groups, 128, 128), F32), jax.ShapeDtypeStruct((groups, 128, 1), F32),
                   jax.ShapeDtypeStruct((1, wd), F32), jax.ShapeDtypeStruct((1, wd), F32)],
        scratch_shapes=[pltpu.VMEM((tb, wd), BF16), pltpu.VMEM((tb, wd), F32)],
        compiler_params=_params("arbitrary"), name=name,
    )(proj, proj, ws, bs.reshape(groups, 128, 1), vg.reshape(1, wd), vb.reshape(1, wd), dcat)
    return dpu, dpv, dws, dbs.reshape(groups, 128), dvg, dvb


def loss_head(y, target, name):
    s, w = y.shape
    tb = _tile(s, 256)
    row = pl.BlockSpec((tb, w), lambda i: (i, 0))
    nsteps = s // tb

    def body(y_ref, t_ref, loss_ref, dy_ref, acc_ref):
        i = pl.program_id(0)

        @pl.when(i == 0)
        def _():
            acc_ref[...] = jnp.zeros_like(acc_ref)

        err = y_ref[...] - t_ref[...]
        dy_ref[...] = err / w
        acc_ref[...] += jnp.sum(jnp.mean(jnp.square(err), axis=-1, keepdims=True), axis=0, keepdims=True)

        @pl.when(i == nsteps - 1)
        def _():
            loss_ref[...] = jnp.broadcast_to(0.5 * acc_ref[...], loss_ref.shape)

    return pl.pallas_call(
        body, grid=(nsteps,), in_specs=[row, row],
        out_specs=[pl.BlockSpec((8, 128), lambda i: (0, 0)), row],
        out_shape=[jax.ShapeDtypeStruct((8, 128), F32), jax.ShapeDtypeStruct((s, w), F32)],
        scratch_shapes=[pltpu.VMEM((1, 1), F32)],
        compiler_params=_params("arbitrary"), name=name,
    )(y, target)


PARTS_WINDOW_ELEMS = 1024 * 1024


def _row_tile(r, c, budget):
    if r * c <= budget or r % 16:
        return r
    fits = [t for t in range(16, r, 16) if r % t == 0 and t * c <= budget]
    return max(fits) if fits else 16


def _sum_in_device_order(p_ref):
    g = p_ref[0].astype(F32)
    for j in range(1, p_ref.shape[0]):
        g = g + p_ref[j].astype(F32)
    return g


def _adamw_update(g, w, m, v):
    nm = ADAM_B1 * m + (1.0 - ADAM_B1) * g
    nv = ADAM_B2 * v + (1.0 - ADAM_B2) * jnp.square(g)
    m_hat = nm / (1.0 - ADAM_B1 ** ADAM_STEP)
    v_hat = nv / (1.0 - ADAM_B2 ** ADAM_STEP)
    return -ADAM_LR * (m_hat / (jnp.sqrt(v_hat) + ADAM_EPS) + ADAM_WD * w), nm, nv


SUM_BYTES_PER_US = 1.7e6


def sum_parts(chunks, name, comm=None):
    p, r, c = chunks[0].shape
    tr = _row_tile(r, c, PARTS_WINDOW_ELEMS // len(chunks))
    nb = r // tr

    def chunk_spec(q):
        return pl.BlockSpec((p, tr, c), lambda ch, i: (0, jnp.where(ch == q, i, 0), 0))

    def body(*refs):
        for q in range(len(chunks)):
            @pl.when(pl.program_id(0) == q)
            def _():
                refs[-1][...] = _sum_in_device_order(refs[q])

    host_us = len(chunks) * r * c * (p * chunks[0].dtype.itemsize + 4) / SUM_BYTES_PER_US
    return _call(
        body, grid=(len(chunks), nb), in_specs=[chunk_spec(q) for q in range(len(chunks))],
        out_specs=[pl.BlockSpec((tr, c), lambda ch, i: (ch * nb + i, 0))],
        out_shape=[jax.ShapeDtypeStruct((len(chunks) * r, c), F32)], scratch_shapes=[],
        args=list(chunks), name=name, semantics=("arbitrary", "arbitrary"),
        jobs=comm.take(host_us) if comm is not None else (),
    )[0]


ELEMENTWISE_BYTES_PER_US = 2.0e6


def adamw(parts, w, m, v, name, emit_grad=True, comm=None):
    p, r, c = parts.shape
    tr = _row_tile(r, c, 160 * 1024)
    n_out = 4 if emit_grad else 3

    def body(p_ref, w_ref, m_ref, v_ref, *out_refs):
        g = _sum_in_device_order(p_ref)
        vals = _adamw_update(g, w_ref[...], m_ref[...], v_ref[...])
        for o_ref, val in zip(out_refs, ((g,) + vals) if emit_grad else vals):
            o_ref[...] = val

    row = pl.BlockSpec((tr, c), lambda i: (i, 0))
    host_us = (parts.size * parts.dtype.itemsize + (3 + n_out) * 4 * r * c) / ELEMENTWISE_BYTES_PER_US
    return _call(
        body, grid=(r // tr,), in_specs=[pl.BlockSpec((p, tr, c), lambda i: (0, i, 0)), row, row, row],
        out_specs=[row] * n_out, out_shape=[jax.ShapeDtypeStruct((r, c), F32)] * n_out, scratch_shapes=[],
        args=[parts, w, m, v], name=name, semantics=("parallel",), jobs=comm.take(host_us) if comm is not None else (),
    )


def kernel(x, mem, w_in_a, w_in_b, w_s, b_s, vnorm_g, vnorm_b, w_mem_kv, w_out, ln1_g, ln1_b, w_ff1, w_ff2, ln2_g, ln2_b, loss_target, m_w_in_a, m_w_in_b, m_w_s, m_b_s, m_vnorm_g, m_vnorm_b, m_w_mem_kv, m_w_out, m_ln1_g, m_ln1_b, m_w_ff1, m_w_ff2, m_ln2_g, m_ln2_b, v_w_in_a, v_w_in_b, v_w_s, v_b_s, v_vnorm_g, v_vnorm_b, v_w_mem_kv, v_w_out, v_ln1_g, v_ln1_b, v_w_ff1, v_w_ff2, v_ln2_g, v_ln2_b):
    depth = w_ff1.shape[0]
    groups = w_s.shape[1]
    mix_w = groups * HEAD_DIM
    n_b = vnorm_g.shape[0]
    alpha = (2.0 * depth) ** 0.25
    me = 4 * lax.axis_index("x") + 2 * lax.axis_index("y") + lax.axis_index("c")
    x0 = x[0]
    mem_b = mem[0].astype(BF16)
    target = loss_target[0]

    comm = Exchanges()
    weight_jobs = []
    for i in range(depth):
        w_in = (w_in_a if i % 2 == 0 else w_in_b)[i // 2]
        halves = lambda t: [t[:, :t.shape[1] // 2], t[:, t.shape[1] // 2:]]
        groups_of_shards = [[h] for h in halves(w_in.T)] + [[w_mem_kv[i], w_out[i].T]] + [[h] for h in halves(w_ff1[i].T)] + [[h] for h in halves(w_ff2[i])]
        weight_jobs.append([GatherJob([t.astype(BF16) for t in shards], f"g{i}{tag}")
                            for shards, tag in zip(groups_of_shards, ("ina", "inb", "out", "ff1a", "ff1b", "ff2a", "ff2b"))])
    weight_jobs[0][0] = GatherJob(weight_jobs[0][0].arrays + [jnp.concatenate([vnorm_g, vnorm_b], axis=0)], "g0ina")
    for jobs in weight_jobs:
        for job in jobs:
            comm.submit(job)

    def gathered(*jobs):
        comm.require(*jobs)
        return [g.reshape(N_DEV * g.shape[1], g.shape[2]) for job in jobs for g in job.outs]

    vnorm = gathered(weight_jobs[0][0], weight_jobs[0][1])[1].reshape(N_DEV, 2 * n_b, mix_w // N_DEV)
    vnorm = jnp.transpose(vnorm, (1, 0, 2)).reshape(2 * n_b, mix_w)
    vg_full, vb_full = vnorm[:n_b], vnorm[n_b:]

    saved = []
    weights = []
    xf, xb = x0, x0.astype(BF16)
    for i in range(depth):
        j = i // 2
        w_in_t = [gathered(weight_jobs[i][0])[0], gathered(weight_jobs[i][1])[0]]
        proj = matmul(xb, w_in_t, "nt", name=f"in_proj_{i % 2}", comm=comm)
        if i % 2 == 0:
            os_, ls_ = [], []
            for g, d in enumerate(A_DILATIONS):
                o_g, l_g = attention_forward(proj, g, d, groups, name=f"attn_fwd_d{d}", comm=comm)
                os_.append(o_g)
                ls_.append(l_g)
            o, cat, lse = attention_combine(os_, ls_, name="attn_combine")
            qcol = 9 * mix_w
            extra = (o, lse)
        else:
            cat = gmlp_forward(proj, w_s[j], b_s[j], vg_full[j], vb_full[j], name="gmlp_fwd")
            qcol = 2 * mix_w
            extra = ()
        w_kv, w_o_t = gathered(weight_jobs[i][2])
        kv = matmul(mem_b, w_kv, "nn", name="mem_kv")
        cat = memory_attention(proj, qcol, kv, cat, name=f"mem_attn_{i % 2}")
        y1 = matmul(cat, w_o_t, "nt", name="out_proj", comm=comm)
        r1, x1, x1b = ln_residual(xf, y1, ln1_g[i], ln1_b[i], alpha, name="ln1", comm=comm)
        w_1_t = gathered(weight_jobs[i][3], weight_jobs[i][4])
        h, hid = matmul(x1b, w_1_t, "nt", name="ff1", out_dtypes=(F32, BF16), comm=comm,
                        epilogue=lambda acc: (acc, jnp.square(jnp.maximum(acc, 0.0))))
        w_2 = gathered(weight_jobs[i][5], weight_jobs[i][6])
        y2 = matmul(hid, w_2, "nn", name="ff2", comm=comm)
        r2, x2, x2b = ln_residual(x1, y2, ln2_g[i], ln2_b[i], alpha, name="ln2", comm=comm)
        weights.append((w_in_t, w_o_t, w_1_t, w_2, w_kv))
        saved.append((xb, proj, kv, extra, qcol, cat, r1, x1b, h, hid, r2))
        xf, xb = x2, x2b

    loss_tile, dx = loss_head(xf, target, name="loss_head")
    loss = lax.psum(loss_tile[0, 0], ("x", "y", "c"))

    def exchange(dw, tag):
        rows, cols = dw.shape[0] // N_DEV, dw.shape[1]
        p = dw.reshape(N_DEV, rows, cols)
        whole_us = ExchangeJob.US_PER_MB * dw.size * dw.dtype.itemsize / 1e6
        counts = [q for q in range(1, rows // 16 + 1) if (rows // 16) % q == 0]
        n_chunks = next((q for q in counts if whole_us <= ExchangeJob.CHUNK_US * q), counts[-1])
        step = rows // n_chunks
        return [comm.submit(ExchangeJob(p, q * step, step, f"x{tag}{q}")) for q in range(n_chunks)]

    recv = {n: [None] * depth for n in ("w_in", "w_mem_kv", "w_out", "w_ff1", "w_ff2")}
    small = {n: [None] * depth for n in ("ln1_g", "ln1_b", "ln2_g", "ln2_b")}
    small_b = {n: [None] * n_b for n in ("w_s", "b_s", "vnorm_g", "vnorm_b")}
    for i in reversed(range(depth)):
        w_in_t, w_o_t, w_1_t, w_2, w_kv = weights[i]
        xb, proj, kv, extra, qcol, cat, r1, x1b, h, hid, r2 = saved[i]
        j = i // 2
        dr2, dr2b, small["ln2_g"][i], small["ln2_b"][i] = ln_backward(r2, ln2_g[i], dx, name="ln2_bwd", comm=comm)
        dh = matmul(dr2b, w_2, "nt", name="ff2_dx", extras=(h,), out_dtypes=(BF16,), comm=comm,
                    epilogue=lambda acc, h_t: (acc * (2.0 * jnp.maximum(h_t, 0.0)),))
        recv["w_ff2"][i] = exchange(matmul(hid, dr2b, "tn", name="ff2_dw", out_dtypes=(BF16,), comm=comm), f"ff2{i}")
        dx1 = matmul(dh, w_1_t, "nn", name="ff1_dx", extras=(dr2,), comm=comm, epilogue=lambda acc, res: (acc + alpha * res,))
        recv["w_ff1"][i] = exchange(matmul(dh, x1b, "tn", name="ff1_dw", out_dtypes=(BF16,), comm=comm), f"ff1{i}")
        dr1, dr1b, small["ln1_g"][i], small["ln1_b"][i] = ln_backward(r1, ln1_g[i], dx1, name="ln1_bwd", comm=comm)
        dcat = matmul(dr1b, w_o_t, "nn", name="out_proj_dx", comm=comm)
        recv["w_out"][i] = exchange(matmul(dr1b, cat, "tn", name="out_proj_dw", out_dtypes=(BF16,), comm=comm), f"out{i}")
        dq_mem, dkv = memory_attention_backward(proj, qcol, kv, dcat, mix_w, name=f"mem_attn_bwd_{i % 2}")
        recv["w_mem_kv"][i] = exchange(matmul(mem_b, dkv, "tn", name="mem_kv_dw", out_dtypes=(BF16,)), f"kv{i}")
        if i % 2 == 0:
            o, lse = extra
            pieces = []
            for g, d in enumerate(A_DILATIONS):
                pieces += [t.astype(BF16) for t in attention_backward(proj, o, lse, dcat, g, d, groups, name=f"attn_bwd_d{d}", comm=comm)]
            dproj = jnp.concatenate(pieces + [dq_mem], axis=1)
        else:
            dpu, dpv, small_b["w_s"][j], small_b["b_s"][j], small_b["vnorm_g"][j], small_b["vnorm_b"][j] = gmlp_backward(
                proj, w_s[j], b_s[j], vg_full[j], vb_full[j], dcat, name="gmlp_bwd")
            dproj = jnp.concatenate([dpu, dpv, dq_mem], axis=1)
        recv["w_in"][i] = exchange(matmul(dproj, xb, "tn", name=f"in_proj_dw_{i % 2}", out_dtypes=(BF16,), comm=comm), f"in{i}")
        dx = matmul(dproj, w_in_t, "nn", name=f"in_proj_dx_{i % 2}", extras=(dr1,), comm=comm, epilogue=lambda acc, res: (acc + alpha * res,))
    grad_x = dx[None]

    flat_small = jnp.concatenate(
        [jnp.concatenate(small[n], axis=0).reshape(-1) for n in ("ln1_g", "ln1_b", "ln2_g", "ln2_b")]
        + [jnp.stack(small_b[n]).reshape(-1) for n in ("w_s", "b_s", "vnorm_g", "vnorm_b")])
    pad = (-flat_small.size) % 1024
    small_job = comm.submit(GatherJob([jnp.pad(flat_small, (0, pad)).reshape(-1, 1024)], "gsmall"))

    results = {}

    def received(name, i):
        return comm.require(*recv[name][i])

    def update(name, layers, w_, m_, v_, transposed):
        sums = [sum_parts(received(name, i), name=f"sum_{name}_{i % 2}", comm=comm) for i in layers]
        grad = jnp.stack([t.T if transposed else t for t in sums])
        flat = lambda t: t.reshape(-1, t.shape[-1])
        upd = adamw(flat(grad)[None], flat(w_), flat(m_), flat(v_), name=f"adamw_{name}_{layers[0] % 2}", emit_grad=False)
        return [grad] + [t.reshape(w_.shape) for t in upd]

    results["w_ff2"] = update("w_ff2", list(range(depth)), w_ff2, m_w_ff2, v_w_ff2, False)
    results["w_ff1"] = update("w_ff1", list(range(depth)), w_ff1, m_w_ff1, v_w_ff1, True)
    results["w_out"] = update("w_out", list(range(depth)), w_out, m_w_out, v_w_out, True)
    results["w_mem_kv"] = update("w_mem_kv", list(range(depth)), w_mem_kv, m_w_mem_kv, v_w_mem_kv, False)
    results["w_in_b"] = update("w_in", list(range(1, depth, 2)), w_in_b, m_w_in_b, v_w_in_b, True)
    results["w_in_a"] = update("w_in", list(range(0, depth, 2)), w_in_a, m_w_in_a, v_w_in_a, True)

    gathered_small = comm.require(small_job)[0].reshape(N_DEV, -1)
    at = 0

    def take(shape):
        nonlocal at
        size = math.prod(shape)
        at += size
        return gathered_small[:, at - size:at].reshape((N_DEV,) + shape)

    for n, w_, m_, v_ in (("ln1_g", ln1_g, m_ln1_g, v_ln1_g), ("ln1_b", ln1_b, m_ln1_b, v_ln1_b),
                          ("ln2_g", ln2_g, m_ln2_g, v_ln2_g), ("ln2_b", ln2_b, m_ln2_b, v_ln2_b)):
        results[n] = adamw(take(w_.shape), w_, m_, v_, name="adamw_ln")
    lanes = lambda t: t.reshape(-1, 128)
    results["w_s"] = [t.reshape(w_s.shape) for t in adamw(take((w_s.size // 128, 128)), lanes(w_s), lanes(m_w_s), lanes(v_w_s), name="adamw_w_s")]
    results["b_s"] = [t.reshape(b_s.shape) for t in adamw(take((b_s.size // 128, 128)), lanes(b_s), lanes(m_b_s), lanes(v_b_s), name="adamw_b_s")]
    for n, w_, m_, v_ in (("vnorm_g", vnorm_g, m_vnorm_g, v_vnorm_g), ("vnorm_b", vnorm_b, m_vnorm_b, v_vnorm_b)):
        parts = lax.dynamic_slice_in_dim(take((n_b, mix_w)), me * w_.shape[1], w_.shape[1], axis=2)
        results[n] = adamw(parts, w_, m_, v_, name="adamw_vnorm")

    order =("w_in_a", "w_in_b", "w_s", "b_s", "vnorm_g", "vnorm_b", "w_mem_kv", "w_out", "ln1_g", "ln1_b", "w_ff1", "w_ff2", "ln2_g", "ln2_b")
    return (loss, grad_x, *[results[n][0] for n in order], *[results[n][1] for n in order],
            *[results[n][2] for n in order], *[results[n][3] for n in order])
```

```python
import math

import jax
import jax.numpy as jnp
from jax import lax
from jax.experimental import pallas as pl
from jax.experimental.pallas import tpu as pltpu

F32 = jnp.float32
BF16 = jnp.bfloat16
N_DEV = 8
HEAD_DIM = 128
MEM_HEADS = 4
MEM_WIDTH = MEM_HEADS * HEAD_DIM
A_DILATIONS = (1, 4, 16)
LN_EPS = 1e-5
ADAM_LR, ADAM_B1, ADAM_B2, ADAM_EPS, ADAM_WD, ADAM_STEP = 0.001, 0.9, 0.999, 1e-08, 0.01, 10
VMEM_LIMIT_BYTES = 56 * 1024 * 1024
MESH = pl.DeviceIdType.MESH
ANY = pl.BlockSpec(memory_space=pl.ANY)


def _params(*sem):
    return pltpu.CompilerParams(dimension_semantics=sem, vmem_limit_bytes=VMEM_LIMIT_BYTES)


def _tile(dim, pref):
    if dim <= pref:
        return dim
    best = None
    for t in range(128, pref + 1, 128):
        if dim % t == 0:
            best = t
    return best if best is not None else dim


def _dot(a, b, dims):
    return lax.dot_general(a, b, (dims, ((), ())), preferred_element_type=F32)


NN = ((1,), (0,))
NT = ((1,), (1,))
TN = ((0,), (0,))


def _my_position():
    return lax.axis_index("x"), lax.axis_index("y"), lax.axis_index("c")


class GatherJob:
    US_PER_MB = 46.0
    OVERSHOOT = 1.3

    def __init__(self, arrays, tag):
        self.arrays, self.tag, self.outs = list(arrays), tag, None
        self.out_shapes = [jax.ShapeDtypeStruct((N_DEV,) + v.shape, v.dtype) for v in self.arrays]
        self.n_sems, self.n_local = 7 * len(self.arrays), len(self.arrays)
        self.est_us = self.US_PER_MB * sum(v.size * v.dtype.itemsize for v in self.arrays) / 1e6

    def phases(self, v_refs, out_refs, send_sems, recv_sems, local_sems, sem0, loc0):
        n = len(self.arrays)
        x, y, c = _my_position()
        me, sibling = (x, y, c), (x, y, 1 - c)
        chips = [(1 - x, y), (x, 1 - y), (1 - x, 1 - y)]

        def copy(a, k, block, to, from_input=False):
            px, py, pc = block
            slot = out_refs[a].at[4 * px + 2 * py + pc]
            return pltpu.make_async_remote_copy(
                src_ref=v_refs[a] if from_input else slot, dst_ref=slot, send_sem=send_sems.at[sem0 + 7 * a + k],
                recv_sem=recv_sems.at[sem0 + 7 * a + k], device_id=to, device_id_type=MESH)

        def mine(a):
            return pltpu.make_async_copy(v_refs[a], out_refs[a].at[4 * x + 2 * y + c], local_sems.at[loc0 + a])

        def first(a):
            return [copy(a, 0, me, sibling, True)] + [copy(a, 1 + j, me, (*chip, c), True) for j, chip in enumerate(chips)]

        def start():
            for a in range(n):
                mine(a).start()
                for cp in first(a):
                    cp.start()

        def middle():
            for j, chip in enumerate(chips):
                for a in range(n):
                    copy(a, 1 + j, (*chip, c), me).wait_recv()
                    copy(a, 4 + j, (*chip, c), sibling).start()

        def finish():
            for a in range(n):
                copy(a, 0, sibling, me).wait_recv()
                for j, chip in enumerate(chips):
                    copy(a, 4 + j, (*chip, 1 - c), me).wait_recv()
                for cp in first(a) + [copy(a, 4 + j, (*chip, c), sibling) for j, chip in enumerate(chips)]:
                    cp.wait_send()
                mine(a).wait()

        return start, middle, finish


class ExchangeJob:
    US_PER_MB = 11.0
    OVERSHOOT = 1.05
    CHUNK_US = 50.0
    RELATIONS = [(dx, dy, dc) for dx in (0, 1) for dy in (0, 1) for dc in (0, 1)][1:]

    def __init__(self, p, row0, rows, tag):
        self.arrays, self.tag, self.outs = [p], tag, None
        self.row0, self.rows = row0, rows
        self.out_shapes = [jax.ShapeDtypeStruct((N_DEV, rows, p.shape[2]), p.dtype)]
        self.n_sems, self.n_local = 7, 1
        self.est_us = self.US_PER_MB * N_DEV * rows * p.shape[2] * p.dtype.itemsize / 1e6

    def phases(self, p_refs, out_refs, send_sems, recv_sems, local_sems, sem0, loc0):
        (p_ref,), (out_ref,) = p_refs, out_refs
        x, y, c = _my_position()
        me = 4 * x + 2 * y + c
        chunk = pl.ds(self.row0, self.rows)

        def mine():
            return pltpu.make_async_copy(p_ref.at[me, chunk], out_ref.at[me], local_sems.at[loc0])

        def copies(arriving):
            made = []
            for k, (dx, dy, dc) in enumerate(self.RELATIONS):
                px, py, pc = (x + dx) % 2, (y + dy) % 2, (c + dc) % 2
                peer = 4 * px + 2 * py + pc
                made.append(pltpu.make_async_remote_copy(
                    src_ref=p_ref.at[peer, chunk], dst_ref=out_ref.at[peer if arriving else me],
                    send_sem=send_sems.at[sem0 + k], recv_sem=recv_sems.at[sem0 + k],
                    device_id=(px, py, pc), device_id_type=MESH))
            return made

        def start():
            mine().start()
            for send in copies(False):
                send.start()

        def finish():
            for arrival in copies(True):
                arrival.wait_recv()
            for send in copies(False):
                send.wait_send()
            mine().wait()

        return start, (lambda: None), finish


class Exchanges:
    MIN_HOST_US = 20.0
    MAX_JOBS = 8

    def __init__(self):
        self.queue = []

    def submit(self, job):
        self.queue.append(job)
        return job

    def take(self, host_us):
        jobs, used = [], 0.0
        if host_us >= self.MIN_HOST_US:
            for job in list(self.queue):
                if used + job.est_us <= job.OVERSHOOT * host_us and len(jobs) < self.MAX_JOBS:
                    used += job.est_us
                    jobs.append(job)
                    self.queue.remove(job)
        return jobs

    def require(self, *jobs):
        waiting = [job for job in jobs if job.outs is None]
        if waiting:
            for job in waiting:
                self.queue.remove(job)
            _call(lambda: None, grid=(), in_specs=[], out_specs=[], out_shape=[], scratch_shapes=[], args=[],
                  name="exchange", semantics=(), jobs=waiting)
        return [job.outs[0] for job in jobs]


def _call(body, *, grid, in_specs, out_specs, out_shape, scratch_shapes, args, name, semantics, jobs=()):
    if not jobs:
        return pl.pallas_call(body, grid=grid, in_specs=in_specs, out_specs=out_specs, out_shape=out_shape,
                              scratch_shapes=scratch_shapes, compiler_params=_params(*semantics), name=name)(*args)
    n_in, n_out, n_scr = len(in_specs), len(out_shape), len(scratch_shapes)
    j_in = [a for job in jobs for a in job.arrays]
    j_out = [s for job in jobs for s in job.out_shapes]
    n_sems, n_local = sum(job.n_sems for job in jobs), sum(job.n_local for job in jobs)
    steps = math.prod(grid)
    middle_step = (7 * steps) // 8

    def wrapped(*refs):
        ins, refs = refs[:n_in], refs[n_in:]
        j_ins, refs = refs[:len(j_in)], refs[len(j_in):]
        outs, refs = refs[:n_out], refs[n_out:]
        j_outs, refs = refs[:len(j_out)], refs[len(j_out):]
        scratch, (send_sems, recv_sems, local_sems) = refs[:n_scr], refs[n_scr:]
        step = 0
        for axis, extent in enumerate(grid):
            step = step * extent + pl.program_id(axis)
        phases, at_in, at_out, sem0, loc0 = [], 0, 0, 0, 0
        for job in jobs:
            k_in, k_out = len(job.arrays), len(job.out_shapes)
            phases.append(job.phases(j_ins[at_in:at_in + k_in], j_outs[at_out:at_out + k_out], send_sems, recv_sems, local_sems, sem0, loc0))
            at_in, at_out, sem0, loc0 = at_in + k_in, at_out + k_out, sem0 + job.n_sems, loc0 + job.n_local

        def run(which):
            for ph in phases:
                ph[which]()

        if steps == 1:
            run(0)
            body(*ins, *outs, *scratch)
            run(1)
            run(2)
        else:
            pl.when(step == 0)(lambda: run(0))
            body(*ins, *outs, *scratch)
            pl.when(step == middle_step)(lambda: run(1))
            pl.when(step == steps - 1)(lambda: run(2))

    res = pl.pallas_call(
        wrapped, grid=grid, in_specs=list(in_specs) + [ANY] * len(j_in), out_specs=list(out_specs) + [ANY] * len(j_out),
        out_shape=list(out_shape) + j_out,
        scratch_shapes=list(scratch_shapes) + [pltpu.SemaphoreType.DMA((n_sems,)), pltpu.SemaphoreType.DMA((n_sems,)),
                                               pltpu.SemaphoreType.DMA((n_local,))],
        compiler_params=_params(*(["arbitrary"] * len(grid))), name=name + "".join("__" + job.tag for job in jobs),
    )(*args, *j_in)
    at = n_out
    for job in jobs:
        job.outs = list(res[at:at + len(job.out_shapes)])
        at += len(job.out_shapes)
    return list(res[:n_out])


MATMUL_TILES = {"nn": (1024, 512, 2560), "nt": (1024, 512, 2560), "tn": (1024, 1024, 2048)}
MATMUL_FLOPS_PER_US = {"nn": 7.5e8, "nt": 7.5e8, "tn": 8.5e8}


def matmul(a, b, mode, name, epilogue=None, extras=(), out_dtypes=(F32,), comm=None, tiles=None):
    pieces = list(b) if isinstance(b, (list, tuple)) else [b]
    n_pc = len(pieces)
    b_rows, b_cols = pieces[0].shape[0], n_pc * pieces[0].shape[1]
    assert mode != "tn" or n_pc == 1
    if mode == "nn":
        (m, k), (k2, n) = a.shape, (b_rows, b_cols)
    elif mode == "nt":
        (m, k), (n, k2) = a.shape, (b_rows, b_cols)
    else:
        (k, m), (k2, n) = a.shape, (b_rows, b_cols)
    assert k == k2, (a.shape, (b_rows, b_cols), mode)
    tm_pref, tn_pref, tk_pref = tiles or MATMUL_TILES[mode]
    tm = _tile(m, tm_pref)
    if mode == "tn" and tm < tm_pref:
        tn_pref *= 2
    tn = _tile(n // n_pc if mode == "nn" else n, tn_pref)
    tk = _tile(k, tk_pref)
    nj, nk = n // tn, k // tk
    assert mode != "nt" or n_pc == 1 or nk == 1, (k, tk)
    per_piece = nj // n_pc
    k_piece = k // n_pc
    dims = {"nn": NN, "nt": NT, "tn": TN}[mode]
    a_spec = pl.BlockSpec((tk, tm), lambda i, j, kk: (kk, i)) if mode == "tn" else pl.BlockSpec((tm, tk), lambda i, j, kk: (i, kk))

    def b_spec(p):
        if mode == "nt" and n_pc > 1:
            return pl.BlockSpec((tn, k_piece), lambda i, j, kk: (j, 0))
        if mode == "nt":
            return pl.BlockSpec((tn, tk), lambda i, j, kk: (j, kk))
        if n_pc == 1:
            return pl.BlockSpec((tk, tn), lambda i, j, kk: (kk, j))
        return pl.BlockSpec((tk, tn), lambda i, j, kk: (jnp.where(j // per_piece == p, kk, 0),
                                                       jnp.where(j // per_piece == p, j - p * per_piece, 0)))

    tile_spec = pl.BlockSpec((tm, tn), lambda i, j, kk: (i, j))
    row_spec = pl.BlockSpec((1, tn), lambda i, j, kk: (0, j))
    n_ex, n_out = len(extras), len(out_dtypes)
    if epilogue is None:
        epilogue = lambda acc: (acc,) * n_out

    def body(a_ref, *rest):
        b_refs, rest = rest[:n_pc], rest[n_pc:]
        ex_refs, out_refs, acc_ref = rest[:n_ex], rest[n_ex:n_ex + n_out], rest[-1]
        j, kk = pl.program_id(1), pl.program_id(2)

        def finish(acc):
            for o_ref, val in zip(out_refs, epilogue(acc, *[e[...] for e in ex_refs])):
                o_ref[...] = val.astype(o_ref.dtype)

        if mode == "nt" and n_pc > 1:
            finish(sum(_dot(a_ref[:, p * k_piece:(p + 1) * k_piece].astype(BF16), b_ref[...].astype(BF16), NT)
                       for p, b_ref in enumerate(b_refs)))
            return

        for p, b_ref in enumerate(b_refs):
            in_use = True if n_pc == 1 else (j // per_piece == p)

            def product():
                return _dot(a_ref[...].astype(BF16), b_ref[...].astype(BF16), dims)

            if nk == 1:
                if n_pc == 1:
                    finish(product())
                else:
                    pl.when(in_use)(lambda: finish(product()))
                continue

            @pl.when(in_use & (kk == 0))
            def _():
                acc_ref[...] = product()

            @pl.when(in_use & (kk > 0) & (kk < nk - 1))
            def _():
                acc_ref[...] += product()

            @pl.when(in_use & (kk == nk - 1))
            def _():
                finish(acc_ref[...] + product())

    host_us = 2.0 * m * n * k / MATMUL_FLOPS_PER_US[mode]
    outs = _call(
        body,
        grid=(m // tm, nj, nk),
        in_specs=[a_spec] + [b_spec(p) for p in range(n_pc)] + [row_spec if e.shape[0] == 1 else tile_spec for e in extras],
        out_specs=[tile_spec] * n_out,
        out_shape=[jax.ShapeDtypeStruct((m, n), dt) for dt in out_dtypes],
        scratch_shapes=[pltpu.VMEM((tm, tn), F32)],
        args=[a, *pieces, *extras], name=name, semantics=("parallel", "parallel", "arbitrary"),
        jobs=comm.take(host_us) if comm is not None else (),
    )
    return outs[0] if n_out == 1 else outs


LN_BYTES_PER_US = 3.0e6


def _residual_ln(r, g, b):
    mu = jnp.mean(r, axis=-1, keepdims=True)
    var = jnp.mean(jnp.square(r - mu), axis=-1, keepdims=True)
    xn = (r - mu) * lax.rsqrt(var + LN_EPS) * g + b
    return r, xn, xn


def ln_residual(x, y, g, b, alpha, name, comm=None):
    s, w = x.shape
    tb = _tile(s, 256)
    row = pl.BlockSpec((tb, w), lambda i: (i, 0))
    vec = pl.BlockSpec((1, w), lambda i: (0, 0))

    def body(x_ref, y_ref, g_ref, b_ref, r_ref, xn_ref, xnb_ref):
        r, xn, _ = _residual_ln(alpha * x_ref[...] + y_ref[...], g_ref[...], b_ref[...])
        r_ref[...] = r
        xn_ref[...] = xn
        xnb_ref[...] = xn.astype(BF16)

    return _call(
        body, grid=(s // tb,), in_specs=[row, row, vec, vec], out_specs=[row, row, row],
        out_shape=[jax.ShapeDtypeStruct((s, w), F32), jax.ShapeDtypeStruct((s, w), F32), jax.ShapeDtypeStruct((s, w), BF16)],
        scratch_shapes=[], args=[x, y, g.reshape(1, w), b.reshape(1, w)], name=name, semantics=("parallel",),
        jobs=comm.take(18.0 * s * w / LN_BYTES_PER_US) if comm is not None else (),
    )


def _ln_bwd_tile(r, g, dxn):
    mu = jnp.mean(r, axis=-1, keepdims=True)
    cen = r - mu
    rstd = lax.rsqrt(jnp.mean(jnp.square(cen), axis=-1, keepdims=True) + LN_EPS)
    xhat = cen * rstd
    dxh = dxn * g
    dr = rstd * (dxh - jnp.mean(dxh, axis=-1, keepdims=True) - xhat * jnp.mean(dxh * xhat, axis=-1, keepdims=True))
    return dr, jnp.sum(dxn * xhat, axis=0, keepdims=True)


def ln_backward(r, g, dxn, name, comm=None):
    s, w = r.shape
    tb = _tile(s, 256)
    row = pl.BlockSpec((tb, w), lambda i: (i, 0))
    vec = pl.BlockSpec((1, w), lambda i: (0, 0))

    def body(r_ref, g_ref, d_ref, dr_ref, drb_ref, dg_ref, db_ref):
        @pl.when(pl.program_id(0) == 0)
        def _():
            dg_ref[...] = jnp.zeros_like(dg_ref)
            db_ref[...] = jnp.zeros_like(db_ref)

        dxn = d_ref[...]
        dr, dg = _ln_bwd_tile(r_ref[...], g_ref[...], dxn)
        dr_ref[...] = dr
        drb_ref[...] = dr.astype(BF16)
        dg_ref[...] += dg
        db_ref[...] += jnp.sum(dxn, axis=0, keepdims=True)

    return _call(
        body, grid=(s // tb,), in_specs=[row, vec, row], out_specs=[row, row, vec, vec],
        out_shape=[jax.ShapeDtypeStruct((s, w), F32), jax.ShapeDtypeStruct((s, w), BF16),
                   jax.ShapeDtypeStruct((1, w), F32), jax.ShapeDtypeStruct((1, w), F32)],
        scratch_shapes=[], args=[r, g.reshape(1, w), dxn], name=name, semantics=("arbitrary",),
        jobs=comm.take(14.0 * s * w / LN_BYTES_PER_US) if comm is not None else (),
    )


ATTENTION_US = {("fwd", 1): 55.0, ("fwd", 4): 80.0, ("fwd", 16): 100.0, ("bwd", 1): 115.0, ("bwd", 4): 130.0, ("bwd", 16): 190.0}
ATTENTION_HEADS_PER_STEP = {("fwd", 1): 8, ("fwd", 4): 1, ("fwd", 16): 1, ("bwd", 1): 8, ("bwd", 4): 1, ("bwd", 16): 1}


def _causal_masks():
    qi = lax.broadcasted_iota(jnp.int32, (128, 128), 0)
    kj = lax.broadcasted_iota(jnp.int32, (128, 128), 1)
    return kj <= qi, kj >= qi


def _window_mask(has_previous):
    qi = lax.broadcasted_iota(jnp.int32, (128, 256), 0)
    kj = lax.broadcasted_iota(jnp.int32, (128, 256), 1)
    return ((kj < 128) & (kj >= qi) & has_previous) | ((kj >= 128) & (kj - 128 <= qi))


def _sub_rows(d, r):
    return pl.ds(r, 128, stride=d) if d > 1 else pl.ds(0, 128)


def _for_each_residue(d, fn):
    if d <= 4:
        for r in range(d):
            fn(r)
    else:
        def step(r, carry):
            fn(r)
            return carry
        lax.fori_loop(0, d, step, 0, unroll=8)


def attention_forward(proj, group, d, heads, name, comm=None):
    s = proj.shape[0]
    rows = 128 * d
    nb = s // rows
    scale = HEAD_DIM ** -0.5
    hp = math.gcd(heads, ATTENTION_HEADS_PER_STEP["fwd", d])
    wide = 128 * hp
    qc, kc, vc = (group * 3) * heads // hp, (group * 3 + 1) * heads // hp, (group * 3 + 2) * heads // hp

    def cur(col):
        return pl.BlockSpec((rows, wide), lambda h, n: (n, col + h))

    def prev(col):
        return pl.BlockSpec((rows, wide), lambda h, n: (jnp.maximum(n - 1, 0), col + h))

    out = pl.BlockSpec((rows, wide), lambda h, n: (n, h))

    def body(q_ref, kc_ref, kp_ref, vc_ref, vp_ref, o_ref, l_ref):
        mask = _window_mask(pl.program_id(1) > 0)

        def one(r):
            rws = _sub_rows(d, r)
            for hh in range(hp):
                sl = (rws, pl.ds(hh * 128, 128))
                k = jnp.concatenate([kp_ref[sl].astype(BF16), kc_ref[sl].astype(BF16)], axis=0)
                v = jnp.concatenate([vp_ref[sl].astype(BF16), vc_ref[sl].astype(BF16)], axis=0)
                sc = jnp.where(mask, _dot(q_ref[sl].astype(BF16), k, NT) * scale, -jnp.inf)
                m = jnp.max(sc, axis=-1, keepdims=True)
                e = jnp.exp(sc - m)
                l = jnp.sum(e, axis=-1, keepdims=True)
                o_ref[sl] = _dot(e.astype(BF16), v, NN) / l
                l_ref[sl] = jnp.broadcast_to(m + jnp.log(l), (128, 128))

        _for_each_residue(d, one)

    return _call(
        body, grid=(heads // hp, nb),
        in_specs=[cur(qc), cur(kc), prev(kc), cur(vc), prev(vc)], out_specs=[out, out],
        out_shape=[jax.ShapeDtypeStruct((s, heads * 128), F32)] * 2, scratch_shapes=[],
        args=[proj] * 5, name=name, semantics=("parallel", "parallel"),
        jobs=comm.take(ATTENTION_US["fwd", d] * s * heads / (4096 * 8)) if comm is not None else (),
    )


def attention_combine(os_, ls_, name):
    s, w = os_[0].shape
    tb = _tile(s, 256)
    row = pl.BlockSpec((tb, w), lambda i: (i, 0))

    def body(o0, o1, o2, l0, l1, l2, o_ref, ob_ref, lse_ref):
        a, b, c = l0[...], l1[...], l2[...]
        m = jnp.maximum(jnp.maximum(a, b), c)
        ea, eb, ec = jnp.exp(a - m), jnp.exp(b - m), jnp.exp(c - m)
        tot = ea + eb + ec
        o = (ea / tot) * o0[...] + (eb / tot) * o1[...] + (ec / tot) * o2[...]
        o_ref[...] = o
        ob_ref[...] = o.astype(BF16)
        lse_ref[...] = m + jnp.log(tot)

    return pl.pallas_call(
        body, grid=(s // tb,), in_specs=[row] * 6, out_specs=[row] * 3,
        out_shape=[jax.ShapeDtypeStruct((s, w), F32), jax.ShapeDtypeStruct((s, w + MEM_WIDTH), BF16), jax.ShapeDtypeStruct((s, w), F32)],
        compiler_params=_params("parallel"), name=name,
    )(*os_, *ls_)


def attention_backward(proj, o, lse, dcat, group, d, heads, name, comm=None):
    s = proj.shape[0]
    rows = 128 * d
    nb = s // rows
    scale = HEAD_DIM ** -0.5
    hp = math.gcd(heads, ATTENTION_HEADS_PER_STEP["bwd", d])
    wide = 128 * hp
    qc, kc, vc = (group * 3) * heads // hp, (group * 3 + 1) * heads // hp, (group * 3 + 2) * heads // hp

    def at(col, shift):
        return pl.BlockSpec((rows, wide), lambda h, n: (jnp.clip(n + shift, 0, nb - 1), col + h))

    def out(col):
        return pl.BlockSpec((rows, wide), lambda h, n: (n, col + h))

    def body(qc_ref, qn_ref, kc_ref, kp_ref, vc_ref, vp_ref, doc_ref, don_ref, oc_ref, on_ref, lc_ref, ln_ref,
             dq_ref, dk_ref, dv_ref):
        n = pl.program_id(1)
        mask_w = _window_mask(n > 0)
        mask_n = _causal_masks()[1] & (n < nb - 1)

        def tile(q, k, v, do, lse_t, dsum, mask):
            p = jnp.where(mask, jnp.exp(_dot(q, k, NT) * scale - lse_t), 0.0)
            ds = (p * (_dot(do, v, NT) - dsum) * scale).astype(BF16)
            return p.astype(BF16), ds

        def one(r):
            rws = _sub_rows(d, r)
            for hh in range(hp):
                sl = (rws, pl.ds(hh * 128, 128))
                q_c, q_n = qc_ref[sl].astype(BF16), qn_ref[sl].astype(BF16)
                k_c, v_c = kc_ref[sl].astype(BF16), vc_ref[sl].astype(BF16)
                k_w = jnp.concatenate([kp_ref[sl].astype(BF16), k_c], axis=0)
                v_w = jnp.concatenate([vp_ref[sl].astype(BF16), v_c], axis=0)
                do_c, do_n = doc_ref[sl], don_ref[sl]
                dsum_c = jnp.sum(do_c * oc_ref[sl], axis=-1, keepdims=True)
                dsum_n = jnp.sum(do_n * on_ref[sl], axis=-1, keepdims=True)
                do_c, do_n = do_c.astype(BF16), do_n.astype(BF16)
                lse_c = lc_ref[sl]
                p_w, ds_w = tile(q_c, k_w, v_w, do_c, jnp.concatenate([lse_c, lse_c], axis=1), dsum_c, mask_w)
                p_n, ds_n = tile(q_n, k_c, v_c, do_n, ln_ref[sl], dsum_n, mask_n)
                dq_ref[sl] = _dot(ds_w, k_w, NN)
                both_q = jnp.concatenate([q_c, q_n], axis=0)
                both_do = jnp.concatenate([do_c, do_n], axis=0)
                dk_ref[sl] = _dot(jnp.concatenate([ds_w[:, 128:], ds_n], axis=0), both_q, TN)
                dv_ref[sl] = _dot(jnp.concatenate([p_w[:, 128:], p_n], axis=0), both_do, TN)

        _for_each_residue(d, one)

    w = heads * 128
    return _call(
        body, grid=(heads // hp, nb),
        in_specs=[at(qc, 0), at(qc, 1), at(kc, 0), at(kc, -1), at(vc, 0), at(vc, -1),
                  at(0, 0), at(0, 1), at(0, 0), at(0, 1), at(0, 0), at(0, 1)],
        out_specs=[out(0)] * 3,
        out_shape=[jax.ShapeDtypeStruct((s, w), F32)] * 3, scratch_shapes=[],
        args=[proj] * 6 + [dcat, dcat, o, o, lse, lse], name=name, semantics=("parallel", "parallel"),
        jobs=comm.take(ATTENTION_US["bwd", d] * s * heads / (4096 * 8)) if comm is not None else (),
    )


def _mem_softmax(q, kv, h, scale):
    k = kv[:, h * 128:(h + 1) * 128].astype(BF16)
    v = kv[:, MEM_WIDTH + h * 128:MEM_WIDTH + (h + 1) * 128].astype(BF16)
    sc = _dot(q, k, NT) * scale
    e = jnp.exp(sc - jnp.max(sc, axis=-1, keepdims=True))
    return e / jnp.sum(e, axis=-1, keepdims=True), k, v


def memory_attention(proj, qcol, kv, cat, name):
    s = proj.shape[0]
    tb = _tile(s, 512)
    scale = HEAD_DIM ** -0.5

    def body(q_ref, kv_ref, cat_ref, o_ref):
        kv_t = kv_ref[...]
        for h in range(MEM_HEADS):
            p, _, v = _mem_softmax(q_ref[:, h * 128:(h + 1) * 128].astype(BF16), kv_t, h, scale)
            o_ref[:, h * 128:(h + 1) * 128] = _dot(p.astype(BF16), v, NN).astype(BF16)

    return pl.pallas_call(
        body, grid=(s // tb,),
        in_specs=[pl.BlockSpec((tb, MEM_WIDTH), lambda i: (i, qcol // MEM_WIDTH)), pl.BlockSpec(kv.shape, lambda i: (0, 0)), ANY],
        out_specs=pl.BlockSpec((tb, MEM_WIDTH), lambda i: (i, cat.shape[1] // MEM_WIDTH - 1)),
        out_shape=jax.ShapeDtypeStruct(cat.shape, BF16), input_output_aliases={2: 0},
        compiler_params=_params("parallel"), name=name,
    )(proj, kv, cat)


def memory_attention_backward(proj, qcol, kv, dcat, dcol, name):
    s = proj.shape[0]
    tb = _tile(s, 512)
    scale = HEAD_DIM ** -0.5

    def body(q_ref, kv_ref, do_ref, dq_ref, dkv_ref):
        @pl.when(pl.program_id(0) == 0)
        def _():
            dkv_ref[...] = jnp.zeros_like(dkv_ref)

        kv_t = kv_ref[...]
        for h in range(MEM_HEADS):
            cols = slice(h * 128, (h + 1) * 128)
            q = q_ref[:, cols].astype(BF16)
            p, k, v = _mem_softmax(q, kv_t, h, scale)
            do = do_ref[:, cols].astype(BF16)
            dp = _dot(do, v, NT)
            ds = (p * (dp - jnp.sum(dp * p, axis=-1, keepdims=True)) * scale).astype(BF16)
            dq_ref[:, cols] = _dot(ds, k, NN).astype(BF16)
            dkv_ref[:, cols] += _dot(ds, q, TN)
            dkv_ref[:, MEM_WIDTH + h * 128:MEM_WIDTH + (h + 1) * 128] += _dot(p.astype(BF16), do, TN)

    return pl.pallas_call(
        body, grid=(s // tb,),
        in_specs=[pl.BlockSpec((tb, MEM_WIDTH), lambda i: (i, qcol // MEM_WIDTH)), pl.BlockSpec(kv.shape, lambda i: (0, 0)),
                  pl.BlockSpec((tb, MEM_WIDTH), lambda i: (i, dcol // MEM_WIDTH))],
        out_specs=[pl.BlockSpec((tb, MEM_WIDTH), lambda i: (i, 0)), pl.BlockSpec(kv.shape, lambda i: (0, 0))],
        out_shape=[jax.ShapeDtypeStruct((s, MEM_WIDTH), BF16), jax.ShapeDtypeStruct(kv.shape, F32)],
        compiler_params=_params("arbitrary"), name=name,
    )(proj, kv, dcat)


_SQRT_HALF = math.sqrt(0.5)
_INV_SQRT_2PI = 1.0 / math.sqrt(2.0 * math.pi)


def _gelu(x):
    return 0.5 * x * (1.0 + lax.erf(x * _SQRT_HALF))


def _gelu_grad(x):
    return 0.5 * (1.0 + lax.erf(x * _SQRT_HALF)) + x * (_INV_SQRT_2PI * jnp.exp(-0.5 * x * x))


def _gmlp_specs(s, wd, groups, tb):
    half = lambda c: pl.BlockSpec((tb, wd), lambda i: (i, c))
    ws_spec = pl.BlockSpec((groups, 128, 128), lambda i: (0, 0, 0))
    bs_spec = pl.BlockSpec((groups, 128, 1), lambda i: (0, 0, 0))
    vec = pl.BlockSpec((1, wd), lambda i: (0, 0))
    return half, ws_spec, bs_spec, vec


def _vnorm(zv, g, b):
    mu = jnp.mean(zv, axis=-1, keepdims=True)
    var = jnp.mean(jnp.square(zv - mu), axis=-1, keepdims=True)
    return (zv - mu) * lax.rsqrt(var + LN_EPS) * g + b


def gmlp_forward(proj, ws, bs, vg, vb, name):
    s = proj.shape[0]
    groups = ws.shape[0]
    wd = groups * 128
    tb = _tile(s, 512)
    half, ws_spec, bs_spec, vec = _gmlp_specs(s, wd, groups, tb)

    def body(pu_ref, pv_ref, ws_ref, bs_ref, vg_ref, vb_ref, o_ref, vn_ref):
        causal, _ = _causal_masks()
        vn_ref[...] = _vnorm(_gelu(pv_ref[...]), vg_ref[...], vb_ref[...]).astype(BF16)
        for g in range(groups):
            cols = slice(g * 128, (g + 1) * 128)
            wm = jnp.where(causal, ws_ref[g], 0.0).astype(BF16)
            for c in range(tb // 128):
                rws = slice(c * 128, (c + 1) * 128)
                sg = _dot(wm, vn_ref[rws, cols], NN) + bs_ref[g]
                o_ref[rws, cols] = (_gelu(pu_ref[rws, cols]) * sg).astype(BF16)

    return pl.pallas_call(
        body, grid=(s // tb,),
        in_specs=[half(0), half(1), ws_spec, bs_spec, vec, vec], out_specs=pl.BlockSpec((tb, wd), lambda i: (i, 0)),
        out_shape=jax.ShapeDtypeStruct((s, wd + MEM_WIDTH), BF16),
        scratch_shapes=[pltpu.VMEM((tb, wd), BF16)],
        compiler_params=_params("parallel"), name=name,
    )(proj, proj, ws, bs.reshape(groups, 128, 1), vg.reshape(1, wd), vb.reshape(1, wd))


def gmlp_backward(proj, ws, bs, vg, vb, dcat, name):
    s = proj.shape[0]
    groups = ws.shape[0]
    wd = groups * 128
    tb = _tile(s, 512)
    half, ws_spec, bs_spec, vec = _gmlp_specs(s, wd, groups, tb)

    def body(pu_ref, pv_ref, ws_ref, bs_ref, vg_ref, vb_ref, do_ref, dpu_ref, dpv_ref, dws_ref, dbs_ref, dvg_ref, dvb_ref,
             vn_ref, dvn_ref):
        @pl.when(pl.program_id(0) == 0)
        def _():
            dws_ref[...] = jnp.zeros_like(dws_ref)
            dbs_ref[...] = jnp.zeros_like(dbs_ref)
            dvg_ref[...] = jnp.zeros_like(dvg_ref)
            dvb_ref[...] = jnp.zeros_like(dvb_ref)

        causal, _ = _causal_masks()
        pv = pv_ref[...]
        zv = _gelu(pv)
        vn_ref[...] = _vnorm(zv, vg_ref[...], vb_ref[...]).astype(BF16)
        for g in range(groups):
            cols = slice(g * 128, (g + 1) * 128)
            wm = jnp.where(causal, ws_ref[g], 0.0).astype(BF16)
            dws_g = jnp.zeros((128, 128), F32)
            dbs_g = jnp.zeros((128, 1), F32)
            for c in range(tb // 128):
                rws = slice(c * 128, (c + 1) * 128)
                vn = vn_ref[rws, cols]
                pu = pu_ref[rws, cols]
                do = do_ref[rws, cols]
                sg = _dot(wm, vn, NN) + bs_ref[g]
                dpu_ref[rws, cols] = (do * sg * _gelu_grad(pu)).astype(BF16)
                dsg = do * _gelu(pu)
                dsg_b = dsg.astype(BF16)
                dws_g += _dot(dsg_b, vn, NT)
                dbs_g += jnp.sum(dsg, axis=-1, keepdims=True)
                dvn_ref[rws, cols] = _dot(wm, dsg_b, TN)
            dws_ref[g] += jnp.where(causal, dws_g, 0.0)
            dbs_ref[g] += dbs_g
        dvn = dvn_ref[...]
        dzv, dvg = _ln_bwd_tile(zv, vg_ref[...], dvn)
        dvg_ref[...] += dvg
        dvb_ref[...] += jnp.sum(dvn, axis=0, keepdims=True)
        dpv_ref[...] = (dzv * _gelu_grad(pv)).astype(BF16)

    row = pl.BlockSpec((tb, wd), lambda i: (i, 0))
    dpu, dpv, dws, dbs, dvg, dvb = pl.pallas_call(
        body, grid=(s // tb,),
        in_specs=[half(0), half(1), ws_spec, bs_spec, vec, vec, row],
        out_specs=[row, row, ws_spec, bs_spec, vec, vec],
        out_shape=[jax.ShapeDtypeStruct((s, wd), BF16), jax.ShapeDtypeStruct((s, wd), BF16),
                   jax.ShapeDtypeStruct((groups, 128, 128), F32), jax.ShapeDtypeStruct((groups, 128, 1), F32),
                   jax.ShapeDtypeStruct((1, wd), F32), jax.ShapeDtypeStruct((1, wd), F32)],
        scratch_shapes=[pltpu.VMEM((tb, wd), BF16), pltpu.VMEM((tb, wd), F32)],
        compiler_params=_params("arbitrary"), name=name,
    )(proj, proj, ws, bs.reshape(groups, 128, 1), vg.reshape(1, wd), vb.reshape(1, wd), dcat)
    return dpu, dpv, dws, dbs.reshape(groups, 128), dvg, dvb


def loss_head(y, target, name):
    s, w = y.shape
    tb = _tile(s, 256)
    row = pl.BlockSpec((tb, w), lambda i: (i, 0))
    nsteps = s // tb

    def body(y_ref, t_ref, loss_ref, dy_ref, acc_ref):
        i = pl.program_id(0)

        @pl.when(i == 0)
        def _():
            acc_ref[...] = jnp.zeros_like(acc_ref)

        err = y_ref[...] - t_ref[...]
        dy_ref[...] = err / w
        acc_ref[...] += jnp.sum(jnp.mean(jnp.square(err), axis=-1, keepdims=True), axis=0, keepdims=True)

        @pl.when(i == nsteps - 1)
        def _():
            loss_ref[...] = jnp.broadcast_to(0.5 * acc_ref[...], loss_ref.shape)

    return pl.pallas_call(
        body, grid=(nsteps,), in_specs=[row, row],
        out_specs=[pl.BlockSpec((8, 128), lambda i: (0, 0)), row],
        out_shape=[jax.ShapeDtypeStruct((8, 128), F32), jax.ShapeDtypeStruct((s, w), F32)],
        scratch_shapes=[pltpu.VMEM((1, 1), F32)],
        compiler_params=_params("arbitrary"), name=name,
    )(y, target)


PARTS_WINDOW_ELEMS = 1024 * 1024


def _row_tile(r, c, budget):
    if r * c <= budget or r % 16:
        return r
    fits = [t for t in range(16, r, 16) if r % t == 0 and t * c <= budget]
    return max(fits) if fits else 16


def _sum_in_device_order(p_ref):
    g = p_ref[0].astype(F32)
    for j in range(1, p_ref.shape[0]):
        g = g + p_ref[j].astype(F32)
    return g


def _adamw_update(g, w, m, v):
    nm = ADAM_B1 * m + (1.0 - ADAM_B1) * g
    nv = ADAM_B2 * v + (1.0 - ADAM_B2) * jnp.square(g)
    m_hat = nm / (1.0 - ADAM_B1 ** ADAM_STEP)
    v_hat = nv / (1.0 - ADAM_B2 ** ADAM_STEP)
    return -ADAM_LR * (m_hat / (jnp.sqrt(v_hat) + ADAM_EPS) + ADAM_WD * w), nm, nv


SUM_BYTES_PER_US = 1.7e6


def sum_parts(chunks, name, comm=None):
    p, r, c = chunks[0].shape
    tr = _row_tile(r, c, PARTS_WINDOW_ELEMS // len(chunks))
    nb = r // tr

    def chunk_spec(q):
        return pl.BlockSpec((p, tr, c), lambda ch, i: (0, jnp.where(ch == q, i, 0), 0))

    def body(*refs):
        for q in range(len(chunks)):
            @pl.when(pl.program_id(0) == q)
            def _():
                refs[-1][...] = _sum_in_device_order(refs[q])

    host_us = len(chunks) * r * c * (p * chunks[0].dtype.itemsize + 4) / SUM_BYTES_PER_US
    return _call(
        body, grid=(len(chunks), nb), in_specs=[chunk_spec(q) for q in range(len(chunks))],
        out_specs=[pl.BlockSpec((tr, c), lambda ch, i: (ch * nb + i, 0))],
        out_shape=[jax.ShapeDtypeStruct((len(chunks) * r, c), F32)], scratch_shapes=[],
        args=list(chunks), name=name, semantics=("arbitrary", "arbitrary"),
        jobs=comm.take(host_us) if comm is not None else (),
    )[0]


ELEMENTWISE_BYTES_PER_US = 2.0e6


def adamw(parts, w, m, v, name, emit_grad=True, comm=None):
    p, r, c = parts.shape
    tr = _row_tile(r, c, 160 * 1024)
    n_out = 4 if emit_grad else 3

    def body(p_ref, w_ref, m_ref, v_ref, *out_refs):
        g = _sum_in_device_order(p_ref)
        vals = _adamw_update(g, w_ref[...], m_ref[...], v_ref[...])
        for o_ref, val in zip(out_refs, ((g,) + vals) if emit_grad else vals):
            o_ref[...] = val

    row = pl.BlockSpec((tr, c), lambda i: (i, 0))
    host_us = (parts.size * parts.dtype.itemsize + (3 + n_out) * 4 * r * c) / ELEMENTWISE_BYTES_PER_US
    return _call(
        body, grid=(r // tr,), in_specs=[pl.BlockSpec((p, tr, c), lambda i: (0, i, 0)), row, row, row],
        out_specs=[row] * n_out, out_shape=[jax.ShapeDtypeStruct((r, c), F32)] * n_out, scratch_shapes=[],
        args=[parts, w, m, v], name=name, semantics=("parallel",), jobs=comm.take(host_us) if comm is not None else (),
    )


def kernel(x, mem, w_in_a, w_in_b, w_s, b_s, vnorm_g, vnorm_b, w_mem_kv, w_out, ln1_g, ln1_b, w_ff1, w_ff2, ln2_g, ln2_b, loss_target, m_w_in_a, m_w_in_b, m_w_s, m_b_s, m_vnorm_g, m_vnorm_b, m_w_mem_kv, m_w_out, m_ln1_g, m_ln1_b, m_w_ff1, m_w_ff2, m_ln2_g, m_ln2_b, v_w_in_a, v_w_in_b, v_w_s, v_b_s, v_vnorm_g, v_vnorm_b, v_w_mem_kv, v_w_out, v_ln1_g, v_ln1_b, v_w_ff1, v_w_ff2, v_ln2_g, v_ln2_b):
    depth = w_ff1.shape[0]
    groups = w_s.shape[1]
    mix_w = groups * HEAD_DIM
    n_b = vnorm_g.shape[0]
    alpha = (2.0 * depth) ** 0.25
    me = 4 * lax.axis_index("x") + 2 * lax.axis_index("y") + lax.axis_index("c")
    x0 = x[0]
    mem_b = mem[0].astype(BF16)
    target = loss_target[0]

    comm = Exchanges()
    weight_jobs = []
    for i in range(depth):
        w_in = (w_in_a if i % 2 == 0 else w_in_b)[i // 2]
        halves = lambda t: [t[:, :t.shape[1] // 2], t[:, t.shape[1] // 2:]]
        groups_of_shards = [[h] for h in halves(w_in.T)] + [[w_mem_kv[i], w_out[i].T]] + [[h] for h in halves(w_ff1[i].T)] + [[h] for h in halves(w_ff2[i])]
        weight_jobs.append([GatherJob([t.astype(BF16) for t in shards], f"g{i}{tag}")
                            for shards, tag in zip(groups_of_shards, ("ina", "inb", "out", "ff1a", "ff1b", "ff2a", "ff2b"))])
    weight_jobs[0][0] = GatherJob(weight_jobs[0][0].arrays + [jnp.concatenate([vnorm_g, vnorm_b], axis=0)], "g0ina")
    for jobs in weight_jobs:
        for job in jobs:
            comm.submit(job)

    def gathered(*jobs):
        comm.require(*jobs)
        return [g.reshape(N_DEV * g.shape[1], g.shape[2]) for job in jobs for g in job.outs]

    vnorm = gathered(weight_jobs[0][0], weight_jobs[0][1])[1].reshape(N_DEV, 2 * n_b, mix_w // N_DEV)
    vnorm = jnp.transpose(vnorm, (1, 0, 2)).reshape(2 * n_b, mix_w)
    vg_full, vb_full = vnorm[:n_b], vnorm[n_b:]

    saved = []
    weights = []
    xf, xb = x0, x0.astype(BF16)
    for i in range(depth):
        j = i // 2
        w_in_t = [gathered(weight_jobs[i][0])[0], gathered(weight_jobs[i][1])[0]]
        proj = matmul(xb, w_in_t, "nt", name=f"in_proj_{i % 2}", comm=comm)
        if i % 2 == 0:
            os_, ls_ = [], []
            for g, d in enumerate(A_DILATIONS):
                o_g, l_g = attention_forward(proj, g, d, groups, name=f"attn_fwd_d{d}", comm=comm)
                os_.append(o_g)
                ls_.append(l_g)
            o, cat, lse = attention_combine(os_, ls_, name="attn_combine")
            qcol = 9 * mix_w
            extra = (o, lse)
        else:
            cat = gmlp_forward(proj, w_s[j], b_s[j], vg_full[j], vb_full[j], name="gmlp_fwd")
            qcol = 2 * mix_w
            extra = ()
        w_kv, w_o_t = gathered(weight_jobs[i][2])
        kv = matmul(mem_b, w_kv, "nn", name="mem_kv")
        cat = memory_attention(proj, qcol, kv, cat, name=f"mem_attn_{i % 2}")
        r1, x1, x1b = matmul(cat, w_o_t, "nt", name="out_proj_ln1", comm=comm, tiles=(256, w_o_t.shape[0], 2560),
                             extras=(xf, ln1_g[i][None], ln1_b[i][None]), out_dtypes=(F32, F32, BF16),
                             epilogue=lambda acc, x_t, g_t, b_t: _residual_ln(alpha * x_t + acc, g_t, b_t))
        w_1_t = gathered(weight_jobs[i][3], weight_jobs[i][4])
        h, hid = matmul(x1b, w_1_t, "nt", name="ff1", out_dtypes=(F32, BF16), comm=comm,
                        epilogue=lambda acc: (acc, jnp.square(jnp.maximum(acc, 0.0))))
        w_2 = gathered(weight_jobs[i][5], weight_jobs[i][6])
        y2 = matmul(hid, w_2, "nn", name="ff2", comm=comm)
        r2, x2, x2b = ln_residual(x1, y2, ln2_g[i], ln2_b[i], alpha, name="ln2", comm=comm)
        weights.append((w_in_t, w_o_t, w_1_t, w_2, w_kv))
        saved.append((xb, proj, kv, extra, qcol, cat, r1, x1b, h, hid, r2))
        xf, xb = x2, x2b

    loss_tile, dx = loss_head(xf, target, name="loss_head")
    loss = lax.psum(loss_tile[0, 0], ("x", "y", "c"))

    def exchange(dw, tag):
        rows, cols = dw.shape[0] // N_DEV, dw.shape[1]
        p = dw.reshape(N_DEV, rows, cols)
        whole_us = ExchangeJob.US_PER_MB * dw.size * dw.dtype.itemsize / 1e6
        counts = [q for q in range(1, rows // 16 + 1) if (rows // 16) % q == 0]
        n_chunks = next((q for q in counts if whole_us <= ExchangeJob.CHUNK_US * q), counts[-1])
        step = rows // n_chunks
        return [comm.submit(ExchangeJob(p, q * step, step, f"x{tag}{q}")) for q in range(n_chunks)]

    recv = {n: [None] * depth for n in ("w_in", "w_mem_kv", "w_out", "w_ff1", "w_ff2")}
    small = {n: [None] * depth for n in ("ln1_g", "ln1_b", "ln2_g", "ln2_b")}
    small_b = {n: [None] * n_b for n in ("w_s", "b_s", "vnorm_g", "vnorm_b")}
    for i in reversed(range(depth)):
        w_in_t, w_o_t, w_1_t, w_2, w_kv = weights[i]
        xb, proj, kv, extra, qcol, cat, r1, x1b, h, hid, r2 = saved[i]
        j = i // 2
        dr2, dr2b, small["ln2_g"][i], small["ln2_b"][i] = ln_backward(r2, ln2_g[i], dx, name="ln2_bwd", comm=comm)
        dh = matmul(dr2b, w_2, "nt", name="ff2_dx", extras=(h,), out_dtypes=(BF16,), comm=comm,
                    epilogue=lambda acc, h_t: (acc * (2.0 * jnp.maximum(h_t, 0.0)),))
        recv["w_ff2"][i] = exchange(matmul(hid, dr2b, "tn", name="ff2_dw", out_dtypes=(BF16,), comm=comm), f"ff2{i}")
        dx1 = matmul(dh, w_1_t, "nn", name="ff1_dx", extras=(dr2,), comm=comm, epilogue=lambda acc, res: (acc + alpha * res,))
        recv["w_ff1"][i] = exchange(matmul(dh, x1b, "tn", name="ff1_dw", out_dtypes=(BF16,), comm=comm), f"ff1{i}")
        dr1, dr1b, small["ln1_g"][i], small["ln1_b"][i] = ln_backward(r1, ln1_g[i], dx1, name="ln1_bwd", comm=comm)
        dcat = matmul(dr1b, w_o_t, "nn", name="out_proj_dx", comm=comm)
        recv["w_out"][i] = exchange(matmul(dr1b, cat, "tn", name="out_proj_dw", out_dtypes=(BF16,), comm=comm), f"out{i}")
        dq_mem, dkv = memory_attention_backward(proj, qcol, kv, dcat, mix_w, name=f"mem_attn_bwd_{i % 2}")
        recv["w_mem_kv"][i] = exchange(matmul(mem_b, dkv, "tn", name="mem_kv_dw", out_dtypes=(BF16,)), f"kv{i}")
        if i % 2 == 0:
            o, lse = extra
            pieces = []
            for g, d in enumerate(A_DILATIONS):
                pieces += [t.astype(BF16) for t in attention_backward(proj, o, lse, dcat, g, d, groups, name=f"attn_bwd_d{d}", comm=comm)]
            dproj = jnp.concatenate(pieces + [dq_mem], axis=1)
        else:
            dpu, dpv, small_b["w_s"][j], small_b["b_s"][j], small_b["vnorm_g"][j], small_b["vnorm_b"][j] = gmlp_backward(
                proj, w_s[j], b_s[j], vg_full[j], vb_full[j], dcat, name="gmlp_bwd")
            dproj = jnp.concatenate([dpu, dpv, dq_mem], axis=1)
        recv["w_in"][i] = exchange(matmul(dproj, xb, "tn", name=f"in_proj_dw_{i % 2}", out_dtypes=(BF16,), comm=comm), f"in{i}")
        dx = matmul(dproj, w_in_t, "nn", name=f"in_proj_dx_{i % 2}", extras=(dr1,), comm=comm, epilogue=lambda acc, res: (acc + alpha * res,))
    grad_x = dx[None]

    flat_small = jnp.concatenate(
        [jnp.concatenate(small[n], axis=0).reshape(-1) for n in ("ln1_g", "ln1_b", "ln2_g", "ln2_b")]
        + [jnp.stack(small_b[n]).reshape(-1) for n in ("w_s", "b_s", "vnorm_g", "vnorm_b")])
    pad = (-flat_small.size) % 1024
    small_job = comm.submit(GatherJob([jnp.pad(flat_small, (0, pad)).reshape(-1, 1024)], "gsmall"))

    results = {}

    def received(name, i):
        return comm.require(*recv[name][i])

    def update(name, layers, w_, m_, v_, transposed):
        sums = [sum_parts(received(name, i), name=f"sum_{name}_{i % 2}", comm=comm) for i in layers]
        grad = jnp.stack([t.T if transposed else t for t in sums])
        flat = lambda t: t.reshape(-1, t.shape[-1])
        upd = adamw(flat(grad)[None], flat(w_), flat(m_), flat(v_), name=f"adamw_{name}_{layers[0] % 2}", emit_grad=False)
        return [grad] + [t.reshape(w_.shape) for t in upd]

    results["w_ff2"] = update("w_ff2", list(range(depth)), w_ff2, m_w_ff2, v_w_ff2, False)
    results["w_ff1"] = update("w_ff1", list(range(depth)), w_ff1, m_w_ff1, v_w_ff1, True)
    results["w_out"] = update("w_out", list(range(depth)), w_out, m_w_out, v_w_out, True)
    results["w_mem_kv"] = update("w_mem_kv", list(range(depth)), w_mem_kv, m_w_mem_kv, v_w_mem_kv, False)
    results["w_in_b"] = update("w_in", list(range(1, depth, 2)), w_in_b, m_w_in_b, v_w_in_b, True)
    results["w_in_a"] = update("w_in", list(range(0, depth, 2)), w_in_a, m_w_in_a, v_w_in_a, True)

    gathered_small = comm.require(small_job)[0].reshape(N_DEV, -1)
    at = 0

    def take(shape):
        nonlocal at
        size = math.prod(shape)
        at += size
        return gathered_small[:, at - size:at].reshape((N_DEV,) + shape)

    for n, w_, m_, v_ in (("ln1_g", ln1_g, m_ln1_g, v_ln1_g), ("ln1_b", ln1_b, m_ln1_b, v_ln1_b),
                          ("ln2_g", ln2_g, m_ln2_g, v_ln2_g), ("ln2_b", ln2_b, m_ln2_b, v_ln2_b)):
        results[n] = adamw(take(w_.shape), w_, m_, v_, name="adamw_ln")
    lanes = lambda t: t.reshape(-1, 128)
    results["w_s"] = [t.reshape(w_s.shape) for t in adamw(take((w_s.size // 128, 128)), lanes(w_s), lanes(m_w_s), lanes(v_w_s), name="adamw_w_s")]
    results["b_s"] = [t.reshape(b_s.shape) for t in adamw(take((b_s.size // 128, 128)), lanes(b_s), lanes(m_b_s), lanes(v_b_s), name="adamw_b_s")]
    for n, w_, m_, v_ in (("vnorm_g", vnorm_g, m_vnorm_g, v_vnorm_g), ("vnorm_b", vnorm_b, m_vnorm_b, v_vnorm_b)):
        parts = lax.dynamic_slice_in_dim(take((n_b, mix_w)), me * w_.shape[1], w_.shape[1], axis=2)
        results[n] = adamw(parts, w_, m_, v_, name="adamw_vnorm")

    order =("w_in_a", "w_in_b", "w_s", "b_s", "vnorm_g", "vnorm_b", "w_mem_kv", "w_out", "ln1_g", "ln1_b", "w_ff1", "w_ff2", "ln2_g", "ln2_b")
    return (loss, grad_x, *[results[n][0] for n in order], *[results[n][1] for n in order],
            *[results[n][2] for n in order], *[results[n][3] for n in order])
```

```python
import math

import jax
import jax.numpy as jnp
from jax import lax
from jax.experimental import pallas as pl
from jax.experimental.pallas import tpu as pltpu

F32 = jnp.float32
BF16 = jnp.bfloat16
N_DEV = 8
HEAD_DIM = 128
MEM_HEADS = 4
MEM_WIDTH = MEM_HEADS * HEAD_DIM
A_DILATIONS = (1, 4, 16)
WEIGHT_PIECES = 4
LN_EPS = 1e-5
ADAM_LR, ADAM_B1, ADAM_B2, ADAM_EPS, ADAM_WD, ADAM_STEP = 0.001, 0.9, 0.999, 1e-08, 0.01, 10
VMEM_LIMIT_BYTES = 56 * 1024 * 1024
MESH = pl.DeviceIdType.MESH
ANY = pl.BlockSpec(memory_space=pl.ANY)


def _params(*sem):
    return pltpu.CompilerParams(dimension_semantics=sem, vmem_limit_bytes=VMEM_LIMIT_BYTES)


def _tile(dim, pref):
    if dim <= pref:
        return dim
    best = None
    for t in range(128, pref + 1, 128):
        if dim % t == 0:
            best = t
    return best if best is not None else dim


def _dot(a, b, dims):
    return lax.dot_general(a, b, (dims, ((), ())), preferred_element_type=F32)


NN = ((1,), (0,))
NT = ((1,), (1,))
TN = ((0,), (0,))


def _my_position():
    return lax.axis_index("x"), lax.axis_index("y"), lax.axis_index("c")


class GatherJob:
    US_PER_MB = 46.0
    OVERSHOOT = 1.05

    def __init__(self, arrays, tag):
        self.arrays, self.tag, self.outs = list(arrays), tag, None
        self.out_shapes = [jax.ShapeDtypeStruct((N_DEV,) + v.shape, v.dtype) for v in self.arrays]
        self.n_sems, self.n_local = 7 * len(self.arrays), len(self.arrays)
        self.est_us = self.US_PER_MB * sum(v.size * v.dtype.itemsize for v in self.arrays) / 1e6

    def phases(self, v_refs, out_refs, send_sems, recv_sems, local_sems, sem0, loc0):
        n = len(self.arrays)
        x, y, c = _my_position()
        me, sibling = (x, y, c), (x, y, 1 - c)
        chips = [(1 - x, y), (x, 1 - y), (1 - x, 1 - y)]

        def copy(a, k, block, to, from_input=False):
            px, py, pc = block
            slot = out_refs[a].at[4 * px + 2 * py + pc]
            return pltpu.make_async_remote_copy(
                src_ref=v_refs[a] if from_input else slot, dst_ref=slot, send_sem=send_sems.at[sem0 + 7 * a + k],
                recv_sem=recv_sems.at[sem0 + 7 * a + k], device_id=to, device_id_type=MESH)

        def mine(a):
            return pltpu.make_async_copy(v_refs[a], out_refs[a].at[4 * x + 2 * y + c], local_sems.at[loc0 + a])

        def first(a):
            return [copy(a, 0, me, sibling, True)] + [copy(a, 1 + j, me, (*chip, c), True) for j, chip in enumerate(chips)]

        def start():
            for a in range(n):
                mine(a).start()
                for cp in first(a):
                    cp.start()

        def middle():
            for j, chip in enumerate(chips):
                for a in range(n):
                    copy(a, 1 + j, (*chip, c), me).wait_recv()
                    copy(a, 4 + j, (*chip, c), sibling).start()

        def finish():
            for a in range(n):
                copy(a, 0, sibling, me).wait_recv()
                for j, chip in enumerate(chips):
                    copy(a, 4 + j, (*chip, 1 - c), me).wait_recv()
                for cp in first(a) + [copy(a, 4 + j, (*chip, c), sibling) for j, chip in enumerate(chips)]:
                    cp.wait_send()
                mine(a).wait()

        return start, middle, finish


class ExchangeJob:
    US_PER_MB = 11.0
    OVERSHOOT = 1.05
    CHUNK_US = 50.0
    RELATIONS = [(dx, dy, dc) for dx in (0, 1) for dy in (0, 1) for dc in (0, 1)][1:]

    def __init__(self, p, row0, rows, tag):
        self.arrays, self.tag, self.outs = [p], tag, None
        self.row0, self.rows = row0, rows
        self.out_shapes = [jax.ShapeDtypeStruct((N_DEV, rows, p.shape[2]), p.dtype)]
        self.n_sems, self.n_local = 7, 1
        self.est_us = self.US_PER_MB * N_DEV * rows * p.shape[2] * p.dtype.itemsize / 1e6

    def phases(self, p_refs, out_refs, send_sems, recv_sems, local_sems, sem0, loc0):
        (p_ref,), (out_ref,) = p_refs, out_refs
        x, y, c = _my_position()
        me = 4 * x + 2 * y + c
        chunk = pl.ds(self.row0, self.rows)

        def mine():
            return pltpu.make_async_copy(p_ref.at[me, chunk], out_ref.at[me], local_sems.at[loc0])

        def copies(arriving):
            made = []
            for k, (dx, dy, dc) in enumerate(self.RELATIONS):
                px, py, pc = (x + dx) % 2, (y + dy) % 2, (c + dc) % 2
                peer = 4 * px + 2 * py + pc
                made.append(pltpu.make_async_remote_copy(
                    src_ref=p_ref.at[peer, chunk], dst_ref=out_ref.at[peer if arriving else me],
                    send_sem=send_sems.at[sem0 + k], recv_sem=recv_sems.at[sem0 + k],
                    device_id=(px, py, pc), device_id_type=MESH))
            return made

        def start():
            mine().start()
            for send in copies(False):
                send.start()

        def finish():
            for arrival in copies(True):
                arrival.wait_recv()
            for send in copies(False):
                send.wait_send()
            mine().wait()

        return start, (lambda: None), finish


class Exchanges:
    MIN_HOST_US = 20.0
    MAX_JOBS = 8

    def __init__(self):
        self.queue = []

    def submit(self, job):
        self.queue.append(job)
        return job

    def take(self, host_us):
        jobs, used = [], 0.0
        if host_us >= self.MIN_HOST_US:
            for job in list(self.queue):
                if used + job.est_us <= job.OVERSHOOT * host_us and len(jobs) < self.MAX_JOBS:
                    used += job.est_us
                    jobs.append(job)
                    self.queue.remove(job)
        return jobs

    def require(self, *jobs):
        waiting = [job for job in jobs if job.outs is None]
        if waiting:
            for job in waiting:
                self.queue.remove(job)
            _call(lambda: None, grid=(), in_specs=[], out_specs=[], out_shape=[], scratch_shapes=[], args=[],
                  name="exchange", semantics=(), jobs=waiting)
        return [job.outs[0] for job in jobs]


def _call(body, *, grid, in_specs, out_specs, out_shape, scratch_shapes, args, name, semantics, jobs=()):
    if not jobs:
        return pl.pallas_call(body, grid=grid, in_specs=in_specs, out_specs=out_specs, out_shape=out_shape,
                              scratch_shapes=scratch_shapes, compiler_params=_params(*semantics), name=name)(*args)
    n_in, n_out, n_scr = len(in_specs), len(out_shape), len(scratch_shapes)
    j_in = [a for job in jobs for a in job.arrays]
    j_out = [s for job in jobs for s in job.out_shapes]
    n_sems, n_local = sum(job.n_sems for job in jobs), sum(job.n_local for job in jobs)
    steps = math.prod(grid)
    middle_step = (7 * steps) // 8

    def wrapped(*refs):
        ins, refs = refs[:n_in], refs[n_in:]
        j_ins, refs = refs[:len(j_in)], refs[len(j_in):]
        outs, refs = refs[:n_out], refs[n_out:]
        j_outs, refs = refs[:len(j_out)], refs[len(j_out):]
        scratch, (send_sems, recv_sems, local_sems) = refs[:n_scr], refs[n_scr:]
        step = 0
        for axis, extent in enumerate(grid):
            step = step * extent + pl.program_id(axis)
        phases, at_in, at_out, sem0, loc0 = [], 0, 0, 0, 0
        for job in jobs:
            k_in, k_out = len(job.arrays), len(job.out_shapes)
            phases.append(job.phases(j_ins[at_in:at_in + k_in], j_outs[at_out:at_out + k_out], send_sems, recv_sems, local_sems, sem0, loc0))
            at_in, at_out, sem0, loc0 = at_in + k_in, at_out + k_out, sem0 + job.n_sems, loc0 + job.n_local

        def run(which):
            for ph in phases:
                ph[which]()

        if steps == 1:
            run(0)
            body(*ins, *outs, *scratch)
            run(1)
            run(2)
        else:
            pl.when(step == 0)(lambda: run(0))
            body(*ins, *outs, *scratch)
            pl.when(step == middle_step)(lambda: run(1))
            pl.when(step == steps - 1)(lambda: run(2))

    res = pl.pallas_call(
        wrapped, grid=grid, in_specs=list(in_specs) + [ANY] * len(j_in), out_specs=list(out_specs) + [ANY] * len(j_out),
        out_shape=list(out_shape) + j_out,
        scratch_shapes=list(scratch_shapes) + [pltpu.SemaphoreType.DMA((n_sems,)), pltpu.SemaphoreType.DMA((n_sems,)),
                                               pltpu.SemaphoreType.DMA((n_local,))],
        compiler_params=_params(*(["arbitrary"] * len(grid))), name=name + "".join("__" + job.tag for job in jobs),
    )(*args, *j_in)
    at = n_out
    for job in jobs:
        job.outs = list(res[at:at + len(job.out_shapes)])
        at += len(job.out_shapes)
    return list(res[:n_out])


MATMUL_TILES = {"nn": (1024, 512, 2560), "nt": (1024, 512, 2560), "tn": (1024, 1024, 2048)}
MATMUL_FLOPS_PER_US = {"nn": 7.5e8, "nt": 7.5e8, "tn": 8.5e8}


def matmul(a, b, mode, name, epilogue=None, extras=(), out_dtypes=(F32,), comm=None, tiles=None):
    pieces = list(b) if isinstance(b, (list, tuple)) else [b]
    n_pc = len(pieces)
    b_rows, b_cols = pieces[0].shape[0], n_pc * pieces[0].shape[1]
    assert mode != "tn" or n_pc == 1
    if mode == "nn":
        (m, k), (k2, n) = a.shape, (b_rows, b_cols)
    elif mode == "nt":
        (m, k), (n, k2) = a.shape, (b_rows, b_cols)
    else:
        (k, m), (k2, n) = a.shape, (b_rows, b_cols)
    assert k == k2, (a.shape, (b_rows, b_cols), mode)
    tm_pref, tn_pref, tk_pref = tiles or MATMUL_TILES[mode]
    tm = _tile(m, tm_pref)
    if mode == "tn" and tm < tm_pref:
        tn_pref *= 2
    tn = _tile(n // n_pc if mode == "nn" else n, tn_pref)
    tk = _tile(k, tk_pref)
    nj, nk = n // tn, k // tk
    assert mode != "nt" or n_pc == 1 or nk == 1, (k, tk)
    per_piece = nj // n_pc
    k_piece = k // n_pc
    dims = {"nn": NN, "nt": NT, "tn": TN}[mode]
    a_spec = pl.BlockSpec((tk, tm), lambda i, j, kk: (kk, i)) if mode == "tn" else pl.BlockSpec((tm, tk), lambda i, j, kk: (i, kk))

    def b_spec(p):
        if mode == "nt" and n_pc > 1:
            return pl.BlockSpec((tn, k_piece), lambda i, j, kk: (j, 0))
        if mode == "nt":
            return pl.BlockSpec((tn, tk), lambda i, j, kk: (j, kk))
        if n_pc == 1:
            return pl.BlockSpec((tk, tn), lambda i, j, kk: (kk, j))
        return pl.BlockSpec((tk, tn), lambda i, j, kk: (jnp.where(j // per_piece == p, kk, 0),
                                                       jnp.where(j // per_piece == p, j - p * per_piece, 0)))

    tile_spec = pl.BlockSpec((tm, tn), lambda i, j, kk: (i, j))
    row_spec = pl.BlockSpec((1, tn), lambda i, j, kk: (0, j))
    n_ex, n_out = len(extras), len(out_dtypes)
    if epilogue is None:
        epilogue = lambda acc: (acc,) * n_out

    def body(a_ref, *rest):
        b_refs, rest = rest[:n_pc], rest[n_pc:]
        ex_refs, out_refs, acc_ref = rest[:n_ex], rest[n_ex:n_ex + n_out], rest[-1]
        j, kk = pl.program_id(1), pl.program_id(2)

        def finish(acc):
            for o_ref, val in zip(out_refs, epilogue(acc, *[e[...] for e in ex_refs])):
                o_ref[...] = val.astype(o_ref.dtype)

        if mode == "nt" and n_pc > 1:
            finish(sum(_dot(a_ref[:, p * k_piece:(p + 1) * k_piece].astype(BF16), b_ref[...].astype(BF16), NT)
                       for p, b_ref in enumerate(b_refs)))
            return

        for p, b_ref in enumerate(b_refs):
            in_use = True if n_pc == 1 else (j // per_piece == p)

            def product():
                return _dot(a_ref[...].astype(BF16), b_ref[...].astype(BF16), dims)

            if nk == 1:
                if n_pc == 1:
                    finish(product())
                else:
                    pl.when(in_use)(lambda: finish(product()))
                continue

            @pl.when(in_use & (kk == 0))
            def _():
                acc_ref[...] = product()

            @pl.when(in_use & (kk > 0) & (kk < nk - 1))
            def _():
                acc_ref[...] += product()

            @pl.when(in_use & (kk == nk - 1))
            def _():
                finish(acc_ref[...] + product())

    host_us = 2.0 * m * n * k / MATMUL_FLOPS_PER_US[mode]
    outs = _call(
        body,
        grid=(m // tm, nj, nk),
        in_specs=[a_spec] + [b_spec(p) for p in range(n_pc)] + [row_spec if e.shape[0] == 1 else tile_spec for e in extras],
        out_specs=[tile_spec] * n_out,
        out_shape=[jax.ShapeDtypeStruct((m, n), dt) for dt in out_dtypes],
        scratch_shapes=[pltpu.VMEM((tm, tn), F32)],
        args=[a, *pieces, *extras], name=name, semantics=("parallel", "parallel", "arbitrary"),
        jobs=comm.take(host_us) if comm is not None else (),
    )
    return outs[0] if n_out == 1 else outs


LN_BYTES_PER_US = 3.0e6


def _residual_ln(r, g, b):
    mu = jnp.mean(r, axis=-1, keepdims=True)
    var = jnp.mean(jnp.square(r - mu), axis=-1, keepdims=True)
    xn = (r - mu) * lax.rsqrt(var + LN_EPS) * g + b
    return r, xn, xn


def ln_residual(x, y, g, b, alpha, name, comm=None):
    s, w = x.shape
    tb = _tile(s, 256)
    row = pl.BlockSpec((tb, w), lambda i: (i, 0))
    vec = pl.BlockSpec((1, w), lambda i: (0, 0))

    def body(x_ref, y_ref, g_ref, b_ref, r_ref, xn_ref, xnb_ref):
        r, xn, _ = _residual_ln(alpha * x_ref[...] + y_ref[...], g_ref[...], b_ref[...])
        r_ref[...] = r
        xn_ref[...] = xn
        xnb_ref[...] = xn.astype(BF16)

    return _call(
        body, grid=(s // tb,), in_specs=[row, row, vec, vec], out_specs=[row, row, row],
        out_shape=[jax.ShapeDtypeStruct((s, w), F32), jax.ShapeDtypeStruct((s, w), F32), jax.ShapeDtypeStruct((s, w), BF16)],
        scratch_shapes=[], args=[x, y, g.reshape(1, w), b.reshape(1, w)], name=name, semantics=("parallel",),
        jobs=comm.take(18.0 * s * w / LN_BYTES_PER_US) if comm is not None else (),
    )


def _ln_bwd_tile(r, g, dxn):
    mu = jnp.mean(r, axis=-1, keepdims=True)
    cen = r - mu
    rstd = lax.rsqrt(jnp.mean(jnp.square(cen), axis=-1, keepdims=True) + LN_EPS)
    xhat = cen * rstd
    dxh = dxn * g
    dr = rstd * (dxh - jnp.mean(dxh, axis=-1, keepdims=True) - xhat * jnp.mean(dxh * xhat, axis=-1, keepdims=True))
    return dr, jnp.sum(dxn * xhat, axis=0, keepdims=True)


def ln_backward(r, g, dxn, name, comm=None):
    s, w = r.shape
    tb = _tile(s, 256)
    row = pl.BlockSpec((tb, w), lambda i: (i, 0))
    vec = pl.BlockSpec((1, w), lambda i: (0, 0))

    def body(r_ref, g_ref, d_ref, dr_ref, drb_ref, dg_ref, db_ref):
        @pl.when(pl.program_id(0) == 0)
        def _():
            dg_ref[...] = jnp.zeros_like(dg_ref)
            db_ref[...] = jnp.zeros_like(db_ref)

        dxn = d_ref[...]
        dr, dg = _ln_bwd_tile(r_ref[...], g_ref[...], dxn)
        dr_ref[...] = dr
        drb_ref[...] = dr.astype(BF16)
        dg_ref[...] += dg
        db_ref[...] += jnp.sum(dxn, axis=0, keepdims=True)

    return _call(
        body, grid=(s // tb,), in_specs=[row, vec, row], out_specs=[row, row, vec, vec],
        out_shape=[jax.ShapeDtypeStruct((s, w), F32), jax.ShapeDtypeStruct((s, w), BF16),
                   jax.ShapeDtypeStruct((1, w), F32), jax.ShapeDtypeStruct((1, w), F32)],
        scratch_shapes=[], args=[r, g.reshape(1, w), dxn], name=name, semantics=("arbitrary",),
        jobs=comm.take(14.0 * s * w / LN_BYTES_PER_US) if comm is not None else (),
    )


ATTENTION_US = {("fwd", 1): 55.0, ("fwd", 4): 80.0, ("fwd", 16): 100.0, ("bwd", 1): 115.0, ("bwd", 4): 130.0, ("bwd", 16): 190.0}
ATTENTION_HEADS_PER_STEP = {("fwd", 1): 8, ("fwd", 4): 1, ("fwd", 16): 1, ("bwd", 1): 8, ("bwd", 4): 1, ("bwd", 16): 1}


def _causal_masks():
    qi = lax.broadcasted_iota(jnp.int32, (128, 128), 0)
    kj = lax.broadcasted_iota(jnp.int32, (128, 128), 1)
    return kj <= qi, kj >= qi


def _window_mask(has_previous):
    qi = lax.broadcasted_iota(jnp.int32, (128, 256), 0)
    kj = lax.broadcasted_iota(jnp.int32, (128, 256), 1)
    return ((kj < 128) & (kj >= qi) & has_previous) | ((kj >= 128) & (kj - 128 <= qi))


def _sub_rows(d, r):
    return pl.ds(r, 128, stride=d) if d > 1 else pl.ds(0, 128)


def _for_each_residue(d, fn):
    if d <= 4:
        for r in range(d):
            fn(r)
    else:
        def step(r, carry):
            fn(r)
            return carry
        lax.fori_loop(0, d, step, 0, unroll=8)


def attention_forward(proj, group, d, heads, name, comm=None):
    s = proj.shape[0]
    rows = 128 * d
    nb = s // rows
    scale = HEAD_DIM ** -0.5
    hp = math.gcd(heads, ATTENTION_HEADS_PER_STEP["fwd", d])
    wide = 128 * hp
    qc, kc, vc = (group * 3) * heads // hp, (group * 3 + 1) * heads // hp, (group * 3 + 2) * heads // hp

    def cur(col):
        return pl.BlockSpec((rows, wide), lambda h, n: (n, col + h))

    def prev(col):
        return pl.BlockSpec((rows, wide), lambda h, n: (jnp.maximum(n - 1, 0), col + h))

    out = pl.BlockSpec((rows, wide), lambda h, n: (n, h))

    def body(q_ref, kc_ref, kp_ref, vc_ref, vp_ref, o_ref, l_ref):
        mask = _window_mask(pl.program_id(1) > 0)

        def one(r):
            rws = _sub_rows(d, r)
            for hh in range(hp):
                sl = (rws, pl.ds(hh * 128, 128))
                k = jnp.concatenate([kp_ref[sl].astype(BF16), kc_ref[sl].astype(BF16)], axis=0)
                v = jnp.concatenate([vp_ref[sl].astype(BF16), vc_ref[sl].astype(BF16)], axis=0)
                sc = jnp.where(mask, _dot(q_ref[sl].astype(BF16), k, NT) * scale, -jnp.inf)
                m = jnp.max(sc, axis=-1, keepdims=True)
                e = jnp.exp(sc - m)
                l = jnp.sum(e, axis=-1, keepdims=True)
                o_ref[sl] = _dot(e.astype(BF16), v, NN) / l
                l_ref[sl] = jnp.broadcast_to(m + jnp.log(l), (128, 128))

        _for_each_residue(d, one)

    return _call(
        body, grid=(heads // hp, nb),
        in_specs=[cur(qc), cur(kc), prev(kc), cur(vc), prev(vc)], out_specs=[out, out],
        out_shape=[jax.ShapeDtypeStruct((s, heads * 128), F32)] * 2, scratch_shapes=[],
        args=[proj] * 5, name=name, semantics=("parallel", "parallel"),
        jobs=comm.take(ATTENTION_US["fwd", d] * s * heads / (4096 * 8)) if comm is not None else (),
    )


def attention_combine(os_, ls_, name):
    s, w = os_[0].shape
    tb = _tile(s, 256)
    row = pl.BlockSpec((tb, w), lambda i: (i, 0))

    def body(o0, o1, o2, l0, l1, l2, o_ref, ob_ref, lse_ref):
        a, b, c = l0[...], l1[...], l2[...]
        m = jnp.maximum(jnp.maximum(a, b), c)
        ea, eb, ec = jnp.exp(a - m), jnp.exp(b - m), jnp.exp(c - m)
        tot = ea + eb + ec
        o = (ea / tot) * o0[...] + (eb / tot) * o1[...] + (ec / tot) * o2[...]
        o_ref[...] = o
        ob_ref[...] = o.astype(BF16)
        lse_ref[...] = m + jnp.log(tot)

    return pl.pallas_call(
        body, grid=(s // tb,), in_specs=[row] * 6, out_specs=[row] * 3,
        out_shape=[jax.ShapeDtypeStruct((s, w), F32), jax.ShapeDtypeStruct((s, w + MEM_WIDTH), BF16), jax.ShapeDtypeStruct((s, w), F32)],
        compiler_params=_params("parallel"), name=name,
    )(*os_, *ls_)


def attention_backward(proj, o, lse, dcat, group, d, heads, name, comm=None):
    s = proj.shape[0]
    rows = 128 * d
    nb = s // rows
    scale = HEAD_DIM ** -0.5
    hp = math.gcd(heads, ATTENTION_HEADS_PER_STEP["bwd", d])
    wide = 128 * hp
    qc, kc, vc = (group * 3) * heads // hp, (group * 3 + 1) * heads // hp, (group * 3 + 2) * heads // hp

    def at(col, shift):
        return pl.BlockSpec((rows, wide), lambda h, n: (jnp.clip(n + shift, 0, nb - 1), col + h))

    def out(col):
        return pl.BlockSpec((rows, wide), lambda h, n: (n, col + h))

    def body(qc_ref, qn_ref, kc_ref, kp_ref, vc_ref, vp_ref, doc_ref, don_ref, oc_ref, on_ref, lc_ref, ln_ref,
             dq_ref, dk_ref, dv_ref):
        n = pl.program_id(1)
        mask_w = _window_mask(n > 0)
        mask_n = _causal_masks()[1] & (n < nb - 1)

        def tile(q, k, v, do, lse_t, dsum, mask):
            p = jnp.where(mask, jnp.exp(_dot(q, k, NT) * scale - lse_t), 0.0)
            ds = (p * (_dot(do, v, NT) - dsum) * scale).astype(BF16)
            return p.astype(BF16), ds

        def one(r):
            rws = _sub_rows(d, r)
            for hh in range(hp):
                sl = (rws, pl.ds(hh * 128, 128))
                q_c, q_n = qc_ref[sl].astype(BF16), qn_ref[sl].astype(BF16)
                k_c, v_c = kc_ref[sl].astype(BF16), vc_ref[sl].astype(BF16)
                k_w = jnp.concatenate([kp_ref[sl].astype(BF16), k_c], axis=0)
                v_w = jnp.concatenate([vp_ref[sl].astype(BF16), v_c], axis=0)
                do_c, do_n = doc_ref[sl], don_ref[sl]
                dsum_c = jnp.sum(do_c * oc_ref[sl], axis=-1, keepdims=True)
                dsum_n = jnp.sum(do_n * on_ref[sl], axis=-1, keepdims=True)
                do_c, do_n = do_c.astype(BF16), do_n.astype(BF16)
                lse_c = lc_ref[sl]
                p_w, ds_w = tile(q_c, k_w, v_w, do_c, jnp.concatenate([lse_c, lse_c], axis=1), dsum_c, mask_w)
                p_n, ds_n = tile(q_n, k_c, v_c, do_n, ln_ref[sl], dsum_n, mask_n)
                dq_ref[sl] = _dot(ds_w, k_w, NN)
                both_q = jnp.concatenate([q_c, q_n], axis=0)
                both_do = jnp.concatenate([do_c, do_n], axis=0)
                dk_ref[sl] = _dot(jnp.concatenate([ds_w[:, 128:], ds_n], axis=0), both_q, TN)
                dv_ref[sl] = _dot(jnp.concatenate([p_w[:, 128:], p_n], axis=0), both_do, TN)

        _for_each_residue(d, one)

    w = heads * 128
    return _call(
        body, grid=(heads // hp, nb),
        in_specs=[at(qc, 0), at(qc, 1), at(kc, 0), at(kc, -1), at(vc, 0), at(vc, -1),
                  at(0, 0), at(0, 1), at(0, 0), at(0, 1), at(0, 0), at(0, 1)],
        out_specs=[out(0)] * 3,
        out_shape=[jax.ShapeDtypeStruct((s, w), F32)] * 3, scratch_shapes=[],
        args=[proj] * 6 + [dcat, dcat, o, o, lse, lse], name=name, semantics=("parallel", "parallel"),
        jobs=comm.take(ATTENTION_US["bwd", d] * s * heads / (4096 * 8)) if comm is not None else (),
    )


def _mem_softmax(q, kv, h, scale):
    k = kv[:, h * 128:(h + 1) * 128].astype(BF16)
    v = kv[:, MEM_WIDTH + h * 128:MEM_WIDTH + (h + 1) * 128].astype(BF16)
    sc = _dot(q, k, NT) * scale
    e = jnp.exp(sc - jnp.max(sc, axis=-1, keepdims=True))
    return e / jnp.sum(e, axis=-1, keepdims=True), k, v


def memory_attention(proj, qcol, kv, cat, name):
    s = proj.shape[0]
    tb = _tile(s, 512)
    scale = HEAD_DIM ** -0.5

    def body(q_ref, kv_ref, cat_ref, o_ref):
        kv_t = kv_ref[...]
        for h in range(MEM_HEADS):
            p, _, v = _mem_softmax(q_ref[:, h * 128:(h + 1) * 128].astype(BF16), kv_t, h, scale)
            o_ref[:, h * 128:(h + 1) * 128] = _dot(p.astype(BF16), v, NN).astype(BF16)

    return pl.pallas_call(
        body, grid=(s // tb,),
        in_specs=[pl.BlockSpec((tb, MEM_WIDTH), lambda i: (i, qcol // MEM_WIDTH)), pl.BlockSpec(kv.shape, lambda i: (0, 0)), ANY],
        out_specs=pl.BlockSpec((tb, MEM_WIDTH), lambda i: (i, cat.shape[1] // MEM_WIDTH - 1)),
        out_shape=jax.ShapeDtypeStruct(cat.shape, BF16), input_output_aliases={2: 0},
        compiler_params=_params("parallel"), name=name,
    )(proj, kv, cat)


def memory_attention_backward(proj, qcol, kv, dcat, dcol, name):
    s = proj.shape[0]
    tb = _tile(s, 512)
    scale = HEAD_DIM ** -0.5

    def body(q_ref, kv_ref, do_ref, dq_ref, dkv_ref):
        @pl.when(pl.program_id(0) == 0)
        def _():
            dkv_ref[...] = jnp.zeros_like(dkv_ref)

        kv_t = kv_ref[...]
        for h in range(MEM_HEADS):
            cols = slice(h * 128, (h + 1) * 128)
            q = q_ref[:, cols].astype(BF16)
            p, k, v = _mem_softmax(q, kv_t, h, scale)
            do = do_ref[:, cols].astype(BF16)
            dp = _dot(do, v, NT)
            ds = (p * (dp - jnp.sum(dp * p, axis=-1, keepdims=True)) * scale).astype(BF16)
            dq_ref[:, cols] = _dot(ds, k, NN).astype(BF16)
            dkv_ref[:, cols] += _dot(ds, q, TN)
            dkv_ref[:, MEM_WIDTH + h * 128:MEM_WIDTH + (h + 1) * 128] += _dot(p.astype(BF16), do, TN)

    return pl.pallas_call(
        body, grid=(s // tb,),
        in_specs=[pl.BlockSpec((tb, MEM_WIDTH), lambda i: (i, qcol // MEM_WIDTH)), pl.BlockSpec(kv.shape, lambda i: (0, 0)),
                  pl.BlockSpec((tb, MEM_WIDTH), lambda i: (i, dcol // MEM_WIDTH))],
        out_specs=[pl.BlockSpec((tb, MEM_WIDTH), lambda i: (i, 0)), pl.BlockSpec(kv.shape, lambda i: (0, 0))],
        out_shape=[jax.ShapeDtypeStruct((s, MEM_WIDTH), BF16), jax.ShapeDtypeStruct(kv.shape, F32)],
        compiler_params=_params("arbitrary"), name=name,
    )(proj, kv, dcat)


_SQRT_HALF = math.sqrt(0.5)
_INV_SQRT_2PI = 1.0 / math.sqrt(2.0 * math.pi)


def _gelu(x):
    return 0.5 * x * (1.0 + lax.erf(x * _SQRT_HALF))


def _gelu_grad(x):
    return 0.5 * (1.0 + lax.erf(x * _SQRT_HALF)) + x * (_INV_SQRT_2PI * jnp.exp(-0.5 * x * x))


def _gmlp_specs(s, wd, groups, tb):
    half = lambda c: pl.BlockSpec((tb, wd), lambda i: (i, c))
    ws_spec = pl.BlockSpec((groups, 128, 128), lambda i: (0, 0, 0))
    bs_spec = pl.BlockSpec((groups, 128, 1), lambda i: (0, 0, 0))
    vec = pl.BlockSpec((1, wd), lambda i: (0, 0))
    return half, ws_spec, bs_spec, vec


def _vnorm(zv, g, b):
    mu = jnp.mean(zv, axis=-1, keepdims=True)
    var = jnp.mean(jnp.square(zv - mu), axis=-1, keepdims=True)
    return (zv - mu) * lax.rsqrt(var + LN_EPS) * g + b


def gmlp_forward(proj, ws, bs, vg, vb, name):
    s = proj.shape[0]
    groups = ws.shape[0]
    wd = groups * 128
    tb = _tile(s, 512)
    half, ws_spec, bs_spec, vec = _gmlp_specs(s, wd, groups, tb)

    def body(pu_ref, pv_ref, ws_ref, bs_ref, vg_ref, vb_ref, o_ref, vn_ref):
        causal, _ = _causal_masks()
        vn_ref[...] = _vnorm(_gelu(pv_ref[...]), vg_ref[...], vb_ref[...]).astype(BF16)
        for g in range(groups):
            cols = slice(g * 128, (g + 1) * 128)
            wm = jnp.where(causal, ws_ref[g], 0.0).astype(BF16)
            for c in range(tb // 128):
                rws = slice(c * 128, (c + 1) * 128)
                sg = _dot(wm, vn_ref[rws, cols], NN) + bs_ref[g]
                o_ref[rws, cols] = (_gelu(pu_ref[rws, cols]) * sg).astype(BF16)

    return pl.pallas_call(
        body, grid=(s // tb,),
        in_specs=[half(0), half(1), ws_spec, bs_spec, vec, vec], out_specs=pl.BlockSpec((tb, wd), lambda i: (i, 0)),
        out_shape=jax.ShapeDtypeStruct((s, wd + MEM_WIDTH), BF16),
        scratch_shapes=[pltpu.VMEM((tb, wd), BF16)],
        compiler_params=_params("parallel"), name=name,
    )(proj, proj, ws, bs.reshape(groups, 128, 1), vg.reshape(1, wd), vb.reshape(1, wd))


def gmlp_backward(proj, ws, bs, vg, vb, dcat, name):
    s = proj.shape[0]
    groups = ws.shape[0]
    wd = groups * 128
    tb = _tile(s, 512)
    half, ws_spec, bs_spec, vec = _gmlp_specs(s, wd, groups, tb)

    def body(pu_ref, pv_ref, ws_ref, bs_ref, vg_ref, vb_ref, do_ref, dpu_ref, dpv_ref, dws_ref, dbs_ref, dvg_ref, dvb_ref,
             vn_ref, dvn_ref):
        @pl.when(pl.program_id(0) == 0)
        def _():
            dws_ref[...] = jnp.zeros_like(dws_ref)
            dbs_ref[...] = jnp.zeros_like(dbs_ref)
            dvg_ref[...] = jnp.zeros_like(dvg_ref)
            dvb_ref[...] = jnp.zeros_like(dvb_ref)

        causal, _ = _causal_masks()
        pv = pv_ref[...]
        zv = _gelu(pv)
        vn_ref[...] = _vnorm(zv, vg_ref[...], vb_ref[...]).astype(BF16)
        for g in range(groups):
            cols = slice(g * 128, (g + 1) * 128)
            wm = jnp.where(causal, ws_ref[g], 0.0).astype(BF16)
            dws_g = jnp.zeros((128, 128), F32)
            dbs_g = jnp.zeros((128, 1), F32)
            for c in range(tb // 128):
                rws = slice(c * 128, (c + 1) * 128)
                vn = vn_ref[rws, cols]
                pu = pu_ref[rws, cols]
                do = do_ref[rws, cols]
                sg = _dot(wm, vn, NN) + bs_ref[g]
                dpu_ref[rws, cols] = (do * sg * _gelu_grad(pu)).astype(BF16)
                dsg = do * _gelu(pu)
                dsg_b = dsg.astype(BF16)
                dws_g += _dot(dsg_b, vn, NT)
                dbs_g += jnp.sum(dsg, axis=-1, keepdims=True)
                dvn_ref[rws, cols] = _dot(wm, dsg_b, TN)
            dws_ref[g] += jnp.where(causal, dws_g, 0.0)
            dbs_ref[g] += dbs_g
        dvn = dvn_ref[...]
        dzv, dvg = _ln_bwd_tile(zv, vg_ref[...], dvn)
        dvg_ref[...] += dvg
        dvb_ref[...] += jnp.sum(dvn, axis=0, keepdims=True)
        dpv_ref[...] = (dzv * _gelu_grad(pv)).astype(BF16)

    row = pl.BlockSpec((tb, wd), lambda i: (i, 0))
    dpu, dpv, dws, dbs, dvg, dvb = pl.pallas_call(
        body, grid=(s // tb,),
        in_specs=[half(0), half(1), ws_spec, bs_spec, vec, vec, row],
        out_specs=[row, row, ws_spec, bs_spec, vec, vec],
        out_shape=[jax.ShapeDtypeStruct((s, wd), BF16), jax.ShapeDtypeStruct((s, wd), BF16),
                   jax.ShapeDtypeStruct((groups, 128, 128), F32), jax.ShapeDtypeStruct((groups, 128, 1), F32),
                   jax.ShapeDtypeStruct((1, wd), F32), jax.ShapeDtypeStruct((1, wd), F32)],
        scratch_shapes=[pltpu.VMEM((tb, wd), BF16), pltpu.VMEM((tb, wd), F32)],
        compiler_params=_params("arbitrary"), name=name,
    )(proj, proj, ws, bs.reshape(groups, 128, 1), vg.reshape(1, wd), vb.reshape(1, wd), dcat)
    return dpu, dpv, dws, dbs.reshape(groups, 128), dvg, dvb


def loss_head(y, target, name):
    s, w = y.shape
    tb = _tile(s, 256)
    row = pl.BlockSpec((tb, w), lambda i: (i, 0))
    nsteps = s // tb

    def body(y_ref, t_ref, loss_ref, dy_ref, acc_ref):
        i = pl.program_id(0)

        @pl.when(i == 0)
        def _():
            acc_ref[...] = jnp.zeros_like(acc_ref)

        err = y_ref[...] - t_ref[...]
        dy_ref[...] = err / w
        acc_ref[...] += jnp.sum(jnp.mean(jnp.square(err), axis=-1, keepdims=True), axis=0, keepdims=True)

        @pl.when(i == nsteps - 1)
        def _():
            loss_ref[...] = jnp.broadcast_to(0.5 * acc_ref[...], loss_ref.shape)

    return pl.pallas_call(
        body, grid=(nsteps,), in_specs=[row, row],
        out_specs=[pl.BlockSpec((8, 128), lambda i: (0, 0)), row],
        out_shape=[jax.ShapeDtypeStruct((8, 128), F32), jax.ShapeDtypeStruct((s, w), F32)],
        scratch_shapes=[pltpu.VMEM((1, 1), F32)],
        compiler_params=_params("arbitrary"), name=name,
    )(y, target)


PARTS_WINDOW_ELEMS = 1024 * 1024


def _row_tile(r, c, budget):
    if r * c <= budget or r % 16:
        return r
    fits = [t for t in range(16, r, 16) if r % t == 0 and t * c <= budget]
    return max(fits) if fits else 16


def _sum_in_device_order(p_ref):
    g = p_ref[0].astype(F32)
    for j in range(1, p_ref.shape[0]):
        g = g + p_ref[j].astype(F32)
    return g


def _adamw_update(g, w, m, v):
    nm = ADAM_B1 * m + (1.0 - ADAM_B1) * g
    nv = ADAM_B2 * v + (1.0 - ADAM_B2) * jnp.square(g)
    m_hat = nm / (1.0 - ADAM_B1 ** ADAM_STEP)
    v_hat = nv / (1.0 - ADAM_B2 ** ADAM_STEP)
    return -ADAM_LR * (m_hat / (jnp.sqrt(v_hat) + ADAM_EPS) + ADAM_WD * w), nm, nv


SUM_BYTES_PER_US = 1.7e6


def sum_parts(chunks, name, comm=None):
    p, r, c = chunks[0].shape
    tr = _row_tile(r, c, PARTS_WINDOW_ELEMS // len(chunks))
    nb = r // tr

    def chunk_spec(q):
        return pl.BlockSpec((p, tr, c), lambda ch, i: (0, jnp.where(ch == q, i, 0), 0))

    def body(*refs):
        for q in range(len(chunks)):
            @pl.when(pl.program_id(0) == q)
            def _():
                refs[-1][...] = _sum_in_device_order(refs[q])

    host_us = len(chunks) * r * c * (p * chunks[0].dtype.itemsize + 4) / SUM_BYTES_PER_US
    return _call(
        body, grid=(len(chunks), nb), in_specs=[chunk_spec(q) for q in range(len(chunks))],
        out_specs=[pl.BlockSpec((tr, c), lambda ch, i: (ch * nb + i, 0))],
        out_shape=[jax.ShapeDtypeStruct((len(chunks) * r, c), F32)], scratch_shapes=[],
        args=list(chunks), name=name, semantics=("arbitrary", "arbitrary"),
        jobs=comm.take(host_us) if comm is not None else (),
    )[0]


ELEMENTWISE_BYTES_PER_US = 2.0e6


def adamw(parts, w, m, v, name, emit_grad=True, comm=None):
    p, r, c = parts.shape
    tr = _row_tile(r, c, 160 * 1024)
    n_out = 4 if emit_grad else 3

    def body(p_ref, w_ref, m_ref, v_ref, *out_refs):
        g = _sum_in_device_order(p_ref)
        vals = _adamw_update(g, w_ref[...], m_ref[...], v_ref[...])
        for o_ref, val in zip(out_refs, ((g,) + vals) if emit_grad else vals):
            o_ref[...] = val

    row = pl.BlockSpec((tr, c), lambda i: (i, 0))
    host_us = (parts.size * parts.dtype.itemsize + (3 + n_out) * 4 * r * c) / ELEMENTWISE_BYTES_PER_US
    return _call(
        body, grid=(r // tr,), in_specs=[pl.BlockSpec((p, tr, c), lambda i: (0, i, 0)), row, row, row],
        out_specs=[row] * n_out, out_shape=[jax.ShapeDtypeStruct((r, c), F32)] * n_out, scratch_shapes=[],
        args=[parts, w, m, v], name=name, semantics=("parallel",), jobs=comm.take(host_us) if comm is not None else (),
    )


def kernel(x, mem, w_in_a, w_in_b, w_s, b_s, vnorm_g, vnorm_b, w_mem_kv, w_out, ln1_g, ln1_b, w_ff1, w_ff2, ln2_g, ln2_b, loss_target, m_w_in_a, m_w_in_b, m_w_s, m_b_s, m_vnorm_g, m_vnorm_b, m_w_mem_kv, m_w_out, m_ln1_g, m_ln1_b, m_w_ff1, m_w_ff2, m_ln2_g, m_ln2_b, v_w_in_a, v_w_in_b, v_w_s, v_b_s, v_vnorm_g, v_vnorm_b, v_w_mem_kv, v_w_out, v_ln1_g, v_ln1_b, v_w_ff1, v_w_ff2, v_ln2_g, v_ln2_b):
    depth = w_ff1.shape[0]
    groups = w_s.shape[1]
    mix_w = groups * HEAD_DIM
    n_b = vnorm_g.shape[0]
    alpha = (2.0 * depth) ** 0.25
    me = 4 * lax.axis_index("x") + 2 * lax.axis_index("y") + lax.axis_index("c")
    x0 = x[0]
    mem_b = mem[0].astype(BF16)
    target = loss_target[0]

    comm = Exchanges()
    weight_jobs = []
    for i in range(depth):
        w_in = (w_in_a if i % 2 == 0 else w_in_b)[i // 2]
        pieces = lambda t: jnp.split(t.astype(BF16), WEIGHT_PIECES, axis=1)
        small = [w_mem_kv[i].astype(BF16), w_out[i].T.astype(BF16)] + ([jnp.concatenate([vnorm_g, vnorm_b], axis=0)] if i == 0 else [])
        weight_jobs.append({
            "in": [GatherJob([t], f"g{i}in{q}") for q, t in enumerate(pieces(w_in.T))],
            "out": GatherJob(small, f"g{i}out"),
            "ff1": [GatherJob([t], f"g{i}ff1{q}") for q, t in enumerate(pieces(w_ff1[i].T))],
            "ff2": [GatherJob([t], f"g{i}ff2{q}") for q, t in enumerate(pieces(w_ff2[i]))]})
        for job in [*weight_jobs[i]["in"], weight_jobs[i]["out"], *weight_jobs[i]["ff1"], *weight_jobs[i]["ff2"]]:
            comm.submit(job)

    def gathered(*jobs):
        comm.require(*jobs)
        return [g.reshape(N_DEV * g.shape[1], g.shape[2]) for job in jobs for g in job.outs]


    saved = []
    weights = []
    xf, xb = x0, x0.astype(BF16)
    for i in range(depth):
        j = i // 2
        w_in_t = gathered(*weight_jobs[i]["in"])
        proj = matmul(xb, w_in_t, "nt", name=f"in_proj_{i % 2}", comm=comm)
        if i % 2 == 0:
            os_, ls_ = [], []
            for g, d in enumerate(A_DILATIONS):
                o_g, l_g = attention_forward(proj, g, d, groups, name=f"attn_fwd_d{d}", comm=comm)
                os_.append(o_g)
                ls_.append(l_g)
            o, cat, lse = attention_combine(os_, ls_, name="attn_combine")
            qcol = 9 * mix_w
            extra = (o, lse)
        else:
            cat = gmlp_forward(proj, w_s[j], b_s[j], vg_full[j], vb_full[j], name="gmlp_fwd")
            qcol = 2 * mix_w
            extra = ()
        w_kv, w_o_t = gathered(weight_jobs[i]["out"])[:2]
        if i == 0:
            vnorm = jnp.transpose(weight_jobs[0]["out"].outs[2], (1, 0, 2)).reshape(2 * n_b, mix_w)
            vg_full, vb_full = vnorm[:n_b], vnorm[n_b:]
        kv = matmul(mem_b, w_kv, "nn", name="mem_kv")
        cat = memory_attention(proj, qcol, kv, cat, name=f"mem_attn_{i % 2}")
        r1, x1, x1b = matmul(cat, w_o_t, "nt", name="out_proj_ln1", comm=comm, tiles=(256, w_o_t.shape[0], 2560),
                             extras=(xf, ln1_g[i][None], ln1_b[i][None]), out_dtypes=(F32, F32, BF16),
                             epilogue=lambda acc, x_t, g_t, b_t: _residual_ln(alpha * x_t + acc, g_t, b_t))
        w_1_t = gathered(*weight_jobs[i]["ff1"])
        h, hid = matmul(x1b, w_1_t, "nt", name="ff1", out_dtypes=(F32, BF16), comm=comm,
                        epilogue=lambda acc: (acc, jnp.square(jnp.maximum(acc, 0.0))))
        w_2 = gathered(*weight_jobs[i]["ff2"])
        y2 = matmul(hid, w_2, "nn", name="ff2", comm=comm)
        r2, x2, x2b = ln_residual(x1, y2, ln2_g[i], ln2_b[i], alpha, name="ln2", comm=comm)
        weights.append((w_in_t, w_o_t, w_1_t, w_2, w_kv))
        saved.append((xb, proj, kv, extra, qcol, cat, r1, x1b, h, hid, r2))
        xf, xb = x2, x2b

    loss_tile, dx = loss_head(xf, target, name="loss_head")
    loss = lax.psum(loss_tile[0, 0], ("x", "y", "c"))

    def exchange(dw, tag):
        rows, cols = dw.shape[0] // N_DEV, dw.shape[1]
        p = dw.reshape(N_DEV, rows, cols)
        whole_us = ExchangeJob.US_PER_MB * dw.size * dw.dtype.itemsize / 1e6
        counts = [q for q in range(1, rows // 16 + 1) if (rows // 16) % q == 0]
        n_chunks = next((q for q in counts if whole_us <= ExchangeJob.CHUNK_US * q), counts[-1])
        step = rows // n_chunks
        return [comm.submit(ExchangeJob(p, q * step, step, f"x{tag}{q}")) for q in range(n_chunks)]

    recv = {n: [None] * depth for n in ("w_in", "w_mem_kv", "w_out", "w_ff1", "w_ff2")}
    small = {n: [None] * depth for n in ("ln1_g", "ln1_b", "ln2_g", "ln2_b")}
    small_b = {n: [None] * n_b for n in ("w_s", "b_s", "vnorm_g", "vnorm_b")}
    for i in reversed(range(depth)):
        w_in_t, w_o_t, w_1_t, w_2, w_kv = weights[i]
        xb, proj, kv, extra, qcol, cat, r1, x1b, h, hid, r2 = saved[i]
        j = i // 2
        dr2, dr2b, small["ln2_g"][i], small["ln2_b"][i] = ln_backward(r2, ln2_g[i], dx, name="ln2_bwd", comm=comm)
        dh = matmul(dr2b, w_2, "nt", name="ff2_dx", extras=(h,), out_dtypes=(BF16,), comm=comm,
                    epilogue=lambda acc, h_t: (acc * (2.0 * jnp.maximum(h_t, 0.0)),))
        recv["w_ff2"][i] = exchange(matmul(hid, dr2b, "tn", name="ff2_dw", out_dtypes=(BF16,), comm=comm), f"ff2{i}")
        dx1 = matmul(dh, w_1_t, "nn", name="ff1_dx", extras=(dr2,), comm=comm, epilogue=lambda acc, res: (acc + alpha * res,))
        recv["w_ff1"][i] = exchange(matmul(dh, x1b, "tn", name="ff1_dw", out_dtypes=(BF16,), comm=comm), f"ff1{i}")
        dr1, dr1b, small["ln1_g"][i], small["ln1_b"][i] = ln_backward(r1, ln1_g[i], dx1, name="ln1_bwd", comm=comm)
        dcat = matmul(dr1b, w_o_t, "nn", name="out_proj_dx", comm=comm)
        recv["w_out"][i] = exchange(matmul(dr1b, cat, "tn", name="out_proj_dw", out_dtypes=(BF16,), comm=comm), f"out{i}")
        dq_mem, dkv = memory_attention_backward(proj, qcol, kv, dcat, mix_w, name=f"mem_attn_bwd_{i % 2}")
        recv["w_mem_kv"][i] = exchange(matmul(mem_b, dkv, "tn", name="mem_kv_dw", out_dtypes=(BF16,)), f"kv{i}")
        if i % 2 == 0:
            o, lse = extra
            pieces = []
            for g, d in enumerate(A_DILATIONS):
                pieces += [t.astype(BF16) for t in attention_backward(proj, o, lse, dcat, g, d, groups, name=f"attn_bwd_d{d}", comm=comm)]
            dproj = jnp.concatenate(pieces + [dq_mem], axis=1)
        else:
            dpu, dpv, small_b["w_s"][j], small_b["b_s"][j], small_b["vnorm_g"][j], small_b["vnorm_b"][j] = gmlp_backward(
                proj, w_s[j], b_s[j], vg_full[j], vb_full[j], dcat, name="gmlp_bwd")
            dproj = jnp.concatenate([dpu, dpv, dq_mem], axis=1)
        recv["w_in"][i] = exchange(matmul(dproj, xb, "tn", name=f"in_proj_dw_{i % 2}", out_dtypes=(BF16,), comm=comm), f"in{i}")
        dx = matmul(dproj, w_in_t, "nn", name=f"in_proj_dx_{i % 2}", extras=(dr1,), comm=comm, epilogue=lambda acc, res: (acc + alpha * res,))
    grad_x = dx[None]

    flat_small = jnp.concatenate(
        [jnp.concatenate(small[n], axis=0).reshape(-1) for n in ("ln1_g", "ln1_b", "ln2_g", "ln2_b")]
        + [jnp.stack(small_b[n]).reshape(-1) for n in ("w_s", "b_s", "vnorm_g", "vnorm_b")])
    pad = (-flat_small.size) % 1024
    small_job = comm.submit(GatherJob([jnp.pad(flat_small, (0, pad)).reshape(-1, 1024)], "gsmall"))

    results = {}

    def received(name, i):
        return comm.require(*recv[name][i])

    def update(name, layers, w_, m_, v_, transposed):
        sums = [sum_parts(received(name, i), name=f"sum_{name}_{i % 2}", comm=comm) for i in layers]
        grad = jnp.stack([t.T if transposed else t for t in sums])
        flat = lambda t: t.reshape(-1, t.shape[-1])
        upd = adamw(flat(grad)[None], flat(w_), flat(m_), flat(v_), name=f"adamw_{name}_{layers[0] % 2}", emit_grad=False)
        return [grad] + [t.reshape(w_.shape) for t in upd]

    results["w_ff2"] = update("w_ff2", list(range(depth)), w_ff2, m_w_ff2, v_w_ff2, False)
    results["w_ff1"] = update("w_ff1", list(range(depth)), w_ff1, m_w_ff1, v_w_ff1, True)
    results["w_out"] = update("w_out", list(range(depth)), w_out, m_w_out, v_w_out, True)
    results["w_mem_kv"] = update("w_mem_kv", list(range(depth)), w_mem_kv, m_w_mem_kv, v_w_mem_kv, False)
    results["w_in_b"] = update("w_in", list(range(1, depth, 2)), w_in_b, m_w_in_b, v_w_in_b, True)
    results["w_in_a"] = update("w_in", list(range(0, depth, 2)), w_in_a, m_w_in_a, v_w_in_a, True)

    gathered_small = comm.require(small_job)[0].reshape(N_DEV, -1)
    at = 0

    def take(shape):
        nonlocal at
        size = math.prod(shape)
        at += size
        return gathered_small[:, at - size:at].reshape((N_DEV,) + shape)

    for n, w_, m_, v_ in (("ln1_g", ln1_g, m_ln1_g, v_ln1_g), ("ln1_b", ln1_b, m_ln1_b, v_ln1_b),
                          ("ln2_g", ln2_g, m_ln2_g, v_ln2_g), ("ln2_b", ln2_b, m_ln2_b, v_ln2_b)):
        results[n] = adamw(take(w_.shape), w_, m_, v_, name="adamw_ln")
    lanes = lambda t: t.reshape(-1, 128)
    results["w_s"] = [t.reshape(w_s.shape) for t in adamw(take((w_s.size // 128, 128)), lanes(w_s), lanes(m_w_s), lanes(v_w_s), name="adamw_w_s")]
    results["b_s"] = [t.reshape(b_s.shape) for t in adamw(take((b_s.size // 128, 128)), lanes(b_s), lanes(m_b_s), lanes(v_b_s), name="adamw_b_s")]
    for n, w_, m_, v_ in (("vnorm_g", vnorm_g, m_vnorm_g, v_vnorm_g), ("vnorm_b", vnorm_b, m_vnorm_b, v_vnorm_b)):
        parts = lax.dynamic_slice_in_dim(take((n_b, mix_w)), me * w_.shape[1], w_.shape[1], axis=2)
        results[n] = adamw(parts, w_, m_, v_, name="adamw_vnorm")

    order =("w_in_a", "w_in_b", "w_s", "b_s", "vnorm_g", "vnorm_b", "w_mem_kv", "w_out", "ln1_g", "ln1_b", "w_ff1", "w_ff2", "ln2_g", "ln2_b")
    return (loss, grad_x, *[results[n][0] for n in order], *[results[n][1] for n in order],
            *[results[n][2] for n in order], *[results[n][3] for n in order])
```

```python
import math

import jax
import jax.numpy as jnp
from jax import lax
from jax.experimental import pallas as pl
from jax.experimental.pallas import tpu as pltpu

F32 = jnp.float32
BF16 = jnp.bfloat16
N_DEV = 8
HEAD_DIM = 128
MEM_HEADS = 4
MEM_WIDTH = MEM_HEADS * HEAD_DIM
A_DILATIONS = (1, 4, 16)
WEIGHT_PIECES = 4
LN_EPS = 1e-5
ADAM_LR, ADAM_B1, ADAM_B2, ADAM_EPS, ADAM_WD, ADAM_STEP = 0.001, 0.9, 0.999, 1e-08, 0.01, 10
VMEM_LIMIT_BYTES = 56 * 1024 * 1024
MESH = pl.DeviceIdType.MESH
ANY = pl.BlockSpec(memory_space=pl.ANY)


def _params(*sem):
    return pltpu.CompilerParams(dimension_semantics=sem, vmem_limit_bytes=VMEM_LIMIT_BYTES)


def _tile(dim, pref):
    if dim <= pref:
        return dim
    best = None
    for t in range(128, pref + 1, 128):
        if dim % t == 0:
            best = t
    return best if best is not None else dim


def _dot(a, b, dims):
    return lax.dot_general(a, b, (dims, ((), ())), preferred_element_type=F32)


NN = ((1,), (0,))
NT = ((1,), (1,))
TN = ((0,), (0,))


def _my_position():
    return lax.axis_index("x"), lax.axis_index("y"), lax.axis_index("c")


class GatherJob:
    US_PER_MB = 46.0
    OVERSHOOT = 1.05

    def __init__(self, arrays, tag):
        self.arrays, self.tag, self.outs = list(arrays), tag, None
        self.out_shapes = [jax.ShapeDtypeStruct((N_DEV,) + v.shape, v.dtype) for v in self.arrays]
        self.n_sems, self.n_local = 7 * len(self.arrays), len(self.arrays)
        self.est_us = self.US_PER_MB * sum(v.size * v.dtype.itemsize for v in self.arrays) / 1e6

    def phases(self, v_refs, out_refs, send_sems, recv_sems, local_sems, sem0, loc0):
        n = len(self.arrays)
        x, y, c = _my_position()
        me, sibling = (x, y, c), (x, y, 1 - c)
        chips = [(1 - x, y), (x, 1 - y), (1 - x, 1 - y)]

        def copy(a, k, block, to, from_input=False):
            px, py, pc = block
            slot = out_refs[a].at[4 * px + 2 * py + pc]
            return pltpu.make_async_remote_copy(
                src_ref=v_refs[a] if from_input else slot, dst_ref=slot, send_sem=send_sems.at[sem0 + 7 * a + k],
                recv_sem=recv_sems.at[sem0 + 7 * a + k], device_id=to, device_id_type=MESH)

        def mine(a):
            return pltpu.make_async_copy(v_refs[a], out_refs[a].at[4 * x + 2 * y + c], local_sems.at[loc0 + a])

        def first(a):
            return [copy(a, 0, me, sibling, True)] + [copy(a, 1 + j, me, (*chip, c), True) for j, chip in enumerate(chips)]

        def start():
            for a in range(n):
                mine(a).start()
                for cp in first(a):
                    cp.start()

        def middle():
            for j, chip in enumerate(chips):
                for a in range(n):
                    copy(a, 1 + j, (*chip, c), me).wait_recv()
                    copy(a, 4 + j, (*chip, c), sibling).start()

        def finish():
            for a in range(n):
                copy(a, 0, sibling, me).wait_recv()
                for j, chip in enumerate(chips):
                    copy(a, 4 + j, (*chip, 1 - c), me).wait_recv()
                for cp in first(a) + [copy(a, 4 + j, (*chip, c), sibling) for j, chip in enumerate(chips)]:
                    cp.wait_send()
                mine(a).wait()

        return start, middle, finish


class ExchangeJob:
    US_PER_MB = 11.0
    OVERSHOOT = 1.05
    CHUNK_US = 50.0
    RELATIONS = [(dx, dy, dc) for dx in (0, 1) for dy in (0, 1) for dc in (0, 1)][1:]

    def __init__(self, p, row0, rows, tag):
        self.arrays, self.tag, self.outs = [p], tag, None
        self.row0, self.rows = row0, rows
        self.out_shapes = [jax.ShapeDtypeStruct((N_DEV, rows, p.shape[2]), p.dtype)]
        self.n_sems, self.n_local = 7, 1
        self.est_us = self.US_PER_MB * N_DEV * rows * p.shape[2] * p.dtype.itemsize / 1e6

    def phases(self, p_refs, out_refs, send_sems, recv_sems, local_sems, sem0, loc0):
        (p_ref,), (out_ref,) = p_refs, out_refs
        x, y, c = _my_position()
        me = 4 * x + 2 * y + c
        chunk = pl.ds(self.row0, self.rows)

        def mine():
            return pltpu.make_async_copy(p_ref.at[me, chunk], out_ref.at[me], local_sems.at[loc0])

        def copies(arriving):
            made = []
            for k, (dx, dy, dc) in enumerate(self.RELATIONS):
                px, py, pc = (x + dx) % 2, (y + dy) % 2, (c + dc) % 2
                peer = 4 * px + 2 * py + pc
                made.append(pltpu.make_async_remote_copy(
                    src_ref=p_ref.at[peer, chunk], dst_ref=out_ref.at[peer if arriving else me],
                    send_sem=send_sems.at[sem0 + k], recv_sem=recv_sems.at[sem0 + k],
                    device_id=(px, py, pc), device_id_type=MESH))
            return made

        def start():
            mine().start()
            for send in copies(False):
                send.start()

        def finish():
            for arrival in copies(True):
                arrival.wait_recv()
            for send in copies(False):
                send.wait_send()
            mine().wait()

        return start, (lambda: None), finish


class Exchanges:
    MIN_HOST_US = 20.0
    MAX_JOBS = 8

    def __init__(self):
        self.queue = []

    def submit(self, job):
        self.queue.append(job)
        return job

    def take(self, host_us):
        jobs, used = [], 0.0
        if host_us >= self.MIN_HOST_US:
            for job in list(self.queue):
                if used + job.est_us <= job.OVERSHOOT * host_us and len(jobs) < self.MAX_JOBS:
                    used += job.est_us
                    jobs.append(job)
                    self.queue.remove(job)
        return jobs

    def require(self, *jobs):
        waiting = [job for job in jobs if job.outs is None]
        if waiting:
            for job in waiting:
                self.queue.remove(job)
            _call(lambda: None, grid=(), in_specs=[], out_specs=[], out_shape=[], scratch_shapes=[], args=[],
                  name="exchange", semantics=(), jobs=waiting)
        return [job.outs[0] for job in jobs]


def _call(body, *, grid, in_specs, out_specs, out_shape, scratch_shapes, args, name, semantics, jobs=()):
    if not jobs:
        return pl.pallas_call(body, grid=grid, in_specs=in_specs, out_specs=out_specs, out_shape=out_shape,
                              scratch_shapes=scratch_shapes, compiler_params=_params(*semantics), name=name)(*args)
    n_in, n_out, n_scr = len(in_specs), len(out_shape), len(scratch_shapes)
    j_in = [a for job in jobs for a in job.arrays]
    j_out = [s for job in jobs for s in job.out_shapes]
    n_sems, n_local = sum(job.n_sems for job in jobs), sum(job.n_local for job in jobs)
    steps = math.prod(grid)
    middle_step = (7 * steps) // 8

    def wrapped(*refs):
        ins, refs = refs[:n_in], refs[n_in:]
        j_ins, refs = refs[:len(j_in)], refs[len(j_in):]
        outs, refs = refs[:n_out], refs[n_out:]
        j_outs, refs = refs[:len(j_out)], refs[len(j_out):]
        scratch, (send_sems, recv_sems, local_sems) = refs[:n_scr], refs[n_scr:]
        step = 0
        for axis, extent in enumerate(grid):
            step = step * extent + pl.program_id(axis)
        phases, at_in, at_out, sem0, loc0 = [], 0, 0, 0, 0
        for job in jobs:
            k_in, k_out = len(job.arrays), len(job.out_shapes)
            phases.append(job.phases(j_ins[at_in:at_in + k_in], j_outs[at_out:at_out + k_out], send_sems, recv_sems, local_sems, sem0, loc0))
            at_in, at_out, sem0, loc0 = at_in + k_in, at_out + k_out, sem0 + job.n_sems, loc0 + job.n_local

        def run(which):
            for ph in phases:
                ph[which]()

        if steps == 1:
            run(0)
            body(*ins, *outs, *scratch)
            run(1)
            run(2)
        else:
            pl.when(step == 0)(lambda: run(0))
            body(*ins, *outs, *scratch)
            pl.when(step == middle_step)(lambda: run(1))
            pl.when(step == steps - 1)(lambda: run(2))

    res = pl.pallas_call(
        wrapped, grid=grid, in_specs=list(in_specs) + [ANY] * len(j_in), out_specs=list(out_specs) + [ANY] * len(j_out),
        out_shape=list(out_shape) + j_out,
        scratch_shapes=list(scratch_shapes) + [pltpu.SemaphoreType.DMA((n_sems,)), pltpu.SemaphoreType.DMA((n_sems,)),
                                               pltpu.SemaphoreType.DMA((n_local,))],
        compiler_params=_params(*(["arbitrary"] * len(grid))), name=name + "".join("__" + job.tag for job in jobs),
    )(*args, *j_in)
    at = n_out
    for job in jobs:
        job.outs = list(res[at:at + len(job.out_shapes)])
        at += len(job.out_shapes)
    return list(res[:n_out])


MATMUL_TILES = {"nn": (1024, 512, 2560), "nt": (1024, 512, 2560), "tn": (1024, 1024, 2048)}
MATMUL_FLOPS_PER_US = {"nn": 7.5e8, "nt": 7.5e8, "tn": 8.5e8}


def matmul(a, b, mode, name, epilogue=None, extras=(), out_dtypes=(F32,), comm=None, tiles=None):
    pieces = list(b) if isinstance(b, (list, tuple)) else [b]
    n_pc = len(pieces)
    b_rows, b_cols = pieces[0].shape[0], n_pc * pieces[0].shape[1]
    assert mode != "tn" or n_pc == 1
    if mode == "nn":
        (m, k), (k2, n) = a.shape, (b_rows, b_cols)
    elif mode == "nt":
        (m, k), (n, k2) = a.shape, (b_rows, b_cols)
    else:
        (k, m), (k2, n) = a.shape, (b_rows, b_cols)
    assert k == k2, (a.shape, (b_rows, b_cols), mode)
    tm_pref, tn_pref, tk_pref = tiles or MATMUL_TILES[mode]
    tm = _tile(m, tm_pref)
    if mode == "tn" and tm < tm_pref:
        tn_pref *= 2
    tn = _tile(n // n_pc if mode == "nn" else n, tn_pref)
    tk = _tile(k, tk_pref)
    nj, nk = n // tn, k // tk
    assert mode != "nt" or n_pc == 1 or nk == 1, (k, tk)
    per_piece = nj // n_pc
    k_piece = k // n_pc
    dims = {"nn": NN, "nt": NT, "tn": TN}[mode]
    a_spec = pl.BlockSpec((tk, tm), lambda i, j, kk: (kk, i)) if mode == "tn" else pl.BlockSpec((tm, tk), lambda i, j, kk: (i, kk))

    def b_spec(p):
        if mode == "nt" and n_pc > 1:
            return pl.BlockSpec((tn, k_piece), lambda i, j, kk: (j, 0))
        if mode == "nt":
            return pl.BlockSpec((tn, tk), lambda i, j, kk: (j, kk))
        if n_pc == 1:
            return pl.BlockSpec((tk, tn), lambda i, j, kk: (kk, j))
        return pl.BlockSpec((tk, tn), lambda i, j, kk: (jnp.where(j // per_piece == p, kk, 0),
                                                       jnp.where(j // per_piece == p, j - p * per_piece, 0)))

    tile_spec = pl.BlockSpec((tm, tn), lambda i, j, kk: (i, j))
    row_spec = pl.BlockSpec((1, tn), lambda i, j, kk: (0, j))
    n_ex, n_out = len(extras), len(out_dtypes)
    if epilogue is None:
        epilogue = lambda acc: (acc,) * n_out

    def body(a_ref, *rest):
        b_refs, rest = rest[:n_pc], rest[n_pc:]
        ex_refs, out_refs, acc_ref = rest[:n_ex], rest[n_ex:n_ex + n_out], rest[-1]
        j, kk = pl.program_id(1), pl.program_id(2)

        def finish(acc):
            for o_ref, val in zip(out_refs, epilogue(acc, *[e[...] for e in ex_refs])):
                o_ref[...] = val.astype(o_ref.dtype)

        if mode == "nt" and n_pc > 1:
            finish(sum(_dot(a_ref[:, p * k_piece:(p + 1) * k_piece].astype(BF16), b_ref[...].astype(BF16), NT)
                       for p, b_ref in enumerate(b_refs)))
            return

        for p, b_ref in enumerate(b_refs):
            in_use = True if n_pc == 1 else (j // per_piece == p)

            def product():
                return _dot(a_ref[...].astype(BF16), b_ref[...].astype(BF16), dims)

            if nk == 1:
                if n_pc == 1:
                    finish(product())
                else:
                    pl.when(in_use)(lambda: finish(product()))
                continue

            @pl.when(in_use & (kk == 0))
            def _():
                acc_ref[...] = product()

            @pl.when(in_use & (kk > 0) & (kk < nk - 1))
            def _():
                acc_ref[...] += product()

            @pl.when(in_use & (kk == nk - 1))
            def _():
                finish(acc_ref[...] + product())

    host_us = 2.0 * m * n * k / MATMUL_FLOPS_PER_US[mode]
    outs = _call(
        body,
        grid=(m // tm, nj, nk),
        in_specs=[a_spec] + [b_spec(p) for p in range(n_pc)] + [row_spec if e.shape[0] == 1 else tile_spec for e in extras],
        out_specs=[tile_spec] * n_out,
        out_shape=[jax.ShapeDtypeStruct((m, n), dt) for dt in out_dtypes],
        scratch_shapes=[pltpu.VMEM((tm, tn), F32)],
        args=[a, *pieces, *extras], name=name, semantics=("parallel", "parallel", "arbitrary"),
        jobs=comm.take(host_us) if comm is not None else (),
    )
    return outs[0] if n_out == 1 else outs


LN_BYTES_PER_US = 3.0e6


def _residual_ln(r, g, b):
    mu = jnp.mean(r, axis=-1, keepdims=True)
    var = jnp.mean(jnp.square(r - mu), axis=-1, keepdims=True)
    xn = (r - mu) * lax.rsqrt(var + LN_EPS) * g + b
    return r, xn, xn


def ln_residual(x, y, g, b, alpha, name, comm=None):
    s, w = x.shape
    tb = _tile(s, 256)
    row = pl.BlockSpec((tb, w), lambda i: (i, 0))
    vec = pl.BlockSpec((1, w), lambda i: (0, 0))

    def body(x_ref, y_ref, g_ref, b_ref, r_ref, xn_ref, xnb_ref):
        r, xn, _ = _residual_ln(alpha * x_ref[...] + y_ref[...], g_ref[...], b_ref[...])
        r_ref[...] = r
        xn_ref[...] = xn
        xnb_ref[...] = xn.astype(BF16)

    return _call(
        body, grid=(s // tb,), in_specs=[row, row, vec, vec], out_specs=[row, row, row],
        out_shape=[jax.ShapeDtypeStruct((s, w), F32), jax.ShapeDtypeStruct((s, w), F32), jax.ShapeDtypeStruct((s, w), BF16)],
        scratch_shapes=[], args=[x, y, g.reshape(1, w), b.reshape(1, w)], name=name, semantics=("parallel",),
        jobs=comm.take(18.0 * s * w / LN_BYTES_PER_US) if comm is not None else (),
    )


def _ln_bwd_tile(r, g, dxn):
    mu = jnp.mean(r, axis=-1, keepdims=True)
    cen = r - mu
    rstd = lax.rsqrt(jnp.mean(jnp.square(cen), axis=-1, keepdims=True) + LN_EPS)
    xhat = cen * rstd
    dxh = dxn * g
    dr = rstd * (dxh - jnp.mean(dxh, axis=-1, keepdims=True) - xhat * jnp.mean(dxh * xhat, axis=-1, keepdims=True))
    return dr, jnp.sum(dxn * xhat, axis=0, keepdims=True)


def ln_backward(r, g, dxn, name, comm=None):
    s, w = r.shape
    tb = _tile(s, 256)
    row = pl.BlockSpec((tb, w), lambda i: (i, 0))
    vec = pl.BlockSpec((1, w), lambda i: (0, 0))

    def body(r_ref, g_ref, d_ref, dr_ref, drb_ref, dg_ref, db_ref):
        @pl.when(pl.program_id(0) == 0)
        def _():
            dg_ref[...] = jnp.zeros_like(dg_ref)
            db_ref[...] = jnp.zeros_like(db_ref)

        dxn = d_ref[...]
        dr, dg = _ln_bwd_tile(r_ref[...], g_ref[...], dxn)
        dr_ref[...] = dr
        drb_ref[...] = dr.astype(BF16)
        dg_ref[...] += dg
        db_ref[...] += jnp.sum(dxn, axis=0, keepdims=True)

    return _call(
        body, grid=(s // tb,), in_specs=[row, vec, row], out_specs=[row, row, vec, vec],
        out_shape=[jax.ShapeDtypeStruct((s, w), F32), jax.ShapeDtypeStruct((s, w), BF16),
                   jax.ShapeDtypeStruct((1, w), F32), jax.ShapeDtypeStruct((1, w), F32)],
        scratch_shapes=[], args=[r, g.reshape(1, w), dxn], name=name, semantics=("arbitrary",),
        jobs=comm.take(14.0 * s * w / LN_BYTES_PER_US) if comm is not None else (),
    )


ATTENTION_US = {("fwd", 1): 55.0, ("fwd", 4): 80.0, ("fwd", 16): 100.0, ("bwd", 1): 115.0, ("bwd", 4): 130.0, ("bwd", 16): 190.0}
ATTENTION_HEADS_PER_STEP = {("fwd", 1): 8, ("fwd", 4): 1, ("fwd", 16): 1, ("bwd", 1): 8, ("bwd", 4): 1, ("bwd", 16): 1}


def _causal_masks():
    qi = lax.broadcasted_iota(jnp.int32, (128, 128), 0)
    kj = lax.broadcasted_iota(jnp.int32, (128, 128), 1)
    return kj <= qi, kj >= qi


def _window_mask(has_previous):
    qi = lax.broadcasted_iota(jnp.int32, (128, 256), 0)
    kj = lax.broadcasted_iota(jnp.int32, (128, 256), 1)
    return ((kj < 128) & (kj >= qi) & has_previous) | ((kj >= 128) & (kj - 128 <= qi))


def _sub_rows(d, r):
    return pl.ds(r, 128, stride=d) if d > 1 else pl.ds(0, 128)


def _for_each_residue(d, fn):
    if d <= 4:
        for r in range(d):
            fn(r)
    else:
        def step(r, carry):
            fn(r)
            return carry
        lax.fori_loop(0, d, step, 0, unroll=8)


def attention_forward(proj, group, d, heads, name, comm=None):
    s = proj.shape[0]
    rows = 128 * d
    nb = s // rows
    scale = HEAD_DIM ** -0.5
    hp = math.gcd(heads, ATTENTION_HEADS_PER_STEP["fwd", d])
    wide = 128 * hp
    qc, kc, vc = (group * 3) * heads // hp, (group * 3 + 1) * heads // hp, (group * 3 + 2) * heads // hp

    def cur(col):
        return pl.BlockSpec((rows, wide), lambda h, n: (n, col + h))

    def prev(col):
        return pl.BlockSpec((rows, wide), lambda h, n: (jnp.maximum(n - 1, 0), col + h))

    out = pl.BlockSpec((rows, wide), lambda h, n: (n, h))

    def body(q_ref, kc_ref, kp_ref, vc_ref, vp_ref, o_ref, l_ref):
        mask = _window_mask(pl.program_id(1) > 0)

        def one(r):
            rws = _sub_rows(d, r)
            for hh in range(hp):
                sl = (rws, pl.ds(hh * 128, 128))
                k = jnp.concatenate([kp_ref[sl].astype(BF16), kc_ref[sl].astype(BF16)], axis=0)
                v = jnp.concatenate([vp_ref[sl].astype(BF16), vc_ref[sl].astype(BF16)], axis=0)
                sc = jnp.where(mask, _dot(q_ref[sl].astype(BF16), k, NT) * scale, -jnp.inf)
                m = jnp.max(sc, axis=-1, keepdims=True)
                e = jnp.exp(sc - m)
                l = jnp.sum(e, axis=-1, keepdims=True)
                o_ref[sl] = _dot(e.astype(BF16), v, NN) / l
                l_ref[sl] = jnp.broadcast_to(m + jnp.log(l), (128, 128))

        _for_each_residue(d, one)

    return _call(
        body, grid=(heads // hp, nb),
        in_specs=[cur(qc), cur(kc), prev(kc), cur(vc), prev(vc)], out_specs=[out, out],
        out_shape=[jax.ShapeDtypeStruct((s, heads * 128), F32)] * 2, scratch_shapes=[],
        args=[proj] * 5, name=name, semantics=("parallel", "parallel"),
        jobs=comm.take(ATTENTION_US["fwd", d] * s * heads / (4096 * 8)) if comm is not None else (),
    )


def attention_combine(os_, ls_, name):
    s, w = os_[0].shape
    tb = _tile(s, 256)
    row = pl.BlockSpec((tb, w), lambda i: (i, 0))

    def body(o0, o1, o2, l0, l1, l2, o_ref, ob_ref, lse_ref):
        a, b, c = l0[...], l1[...], l2[...]
        m = jnp.maximum(jnp.maximum(a, b), c)
        ea, eb, ec = jnp.exp(a - m), jnp.exp(b - m), jnp.exp(c - m)
        tot = ea + eb + ec
        o = (ea / tot) * o0[...] + (eb / tot) * o1[...] + (ec / tot) * o2[...]
        o_ref[...] = o
        ob_ref[...] = o.astype(BF16)
        lse_ref[...] = m + jnp.log(tot)

    return pl.pallas_call(
        body, grid=(s // tb,), in_specs=[row] * 6, out_specs=[row] * 3,
        out_shape=[jax.ShapeDtypeStruct((s, w), F32), jax.ShapeDtypeStruct((s, w + MEM_WIDTH), BF16), jax.ShapeDtypeStruct((s, w), F32)],
        compiler_params=_params("parallel"), name=name,
    )(*os_, *ls_)


def attention_backward(proj, o, lse, dcat, group, d, heads, name, comm=None):
    s = proj.shape[0]
    rows = 128 * d
    nb = s // rows
    scale = HEAD_DIM ** -0.5
    hp = math.gcd(heads, ATTENTION_HEADS_PER_STEP["bwd", d])
    wide = 128 * hp
    qc, kc, vc = (group * 3) * heads // hp, (group * 3 + 1) * heads // hp, (group * 3 + 2) * heads // hp

    def at(col, shift):
        return pl.BlockSpec((rows, wide), lambda h, n: (jnp.clip(n + shift, 0, nb - 1), col + h))

    def out(col):
        return pl.BlockSpec((rows, wide), lambda h, n: (n, col + h))

    def body(qc_ref, qn_ref, kc_ref, kp_ref, vc_ref, vp_ref, doc_ref, don_ref, oc_ref, on_ref, lc_ref, ln_ref,
             dq_ref, dk_ref, dv_ref):
        n = pl.program_id(1)
        mask_w = _window_mask(n > 0)
        mask_n = _causal_masks()[1] & (n < nb - 1)

        def tile(q, k, v, do, lse_t, dsum, mask):
            p = jnp.where(mask, jnp.exp(_dot(q, k, NT) * scale - lse_t), 0.0)
            ds = (p * (_dot(do, v, NT) - dsum) * scale).astype(BF16)
            return p.astype(BF16), ds

        def one(r):
            rws = _sub_rows(d, r)
            for hh in range(hp):
                sl = (rws, pl.ds(hh * 128, 128))
                q_c, q_n = qc_ref[sl].astype(BF16), qn_ref[sl].astype(BF16)
                k_c, v_c = kc_ref[sl].astype(BF16), vc_ref[sl].astype(BF16)
                k_w = jnp.concatenate([kp_ref[sl].astype(BF16), k_c], axis=0)
                v_w = jnp.concatenate([vp_ref[sl].astype(BF16), v_c], axis=0)
                do_c, do_n = doc_ref[sl], don_ref[sl]
                dsum_c = jnp.sum(do_c * oc_ref[sl], axis=-1, keepdims=True)
                dsum_n = jnp.sum(do_n * on_ref[sl], axis=-1, keepdims=True)
                do_c, do_n = do_c.astype(BF16), do_n.astype(BF16)
                lse_c = lc_ref[sl]
                p_w, ds_w = tile(q_c, k_w, v_w, do_c, jnp.concatenate([lse_c, lse_c], axis=1), dsum_c, mask_w)
                p_n, ds_n = tile(q_n, k_c, v_c, do_n, ln_ref[sl], dsum_n, mask_n)
                dq_ref[sl] = _dot(ds_w, k_w, NN)
                both_q = jnp.concatenate([q_c, q_n], axis=0)
                both_do = jnp.concatenate([do_c, do_n], axis=0)
                dk_ref[sl] = _dot(jnp.concatenate([ds_w[:, 128:], ds_n], axis=0), both_q, TN)
                dv_ref[sl] = _dot(jnp.concatenate([p_w[:, 128:], p_n], axis=0), both_do, TN)

        _for_each_residue(d, one)

    w = heads * 128
    return _call(
        body, grid=(heads // hp, nb),
        in_specs=[at(qc, 0), at(qc, 1), at(kc, 0), at(kc, -1), at(vc, 0), at(vc, -1),
                  at(0, 0), at(0, 1), at(0, 0), at(0, 1), at(0, 0), at(0, 1)],
        out_specs=[out(0)] * 3,
        out_shape=[jax.ShapeDtypeStruct((s, w), F32)] * 3, scratch_shapes=[],
        args=[proj] * 6 + [dcat, dcat, o, o, lse, lse], name=name, semantics=("parallel", "parallel"),
        jobs=comm.take(ATTENTION_US["bwd", d] * s * heads / (4096 * 8)) if comm is not None else (),
    )


def _mem_softmax(q, kv, h, scale):
    k = kv[:, h * 128:(h + 1) * 128].astype(BF16)
    v = kv[:, MEM_WIDTH + h * 128:MEM_WIDTH + (h + 1) * 128].astype(BF16)
    sc = _dot(q, k, NT) * scale
    e = jnp.exp(sc - jnp.max(sc, axis=-1, keepdims=True))
    return e / jnp.sum(e, axis=-1, keepdims=True), k, v


def memory_attention(proj, qcol, kv, cat, name):
    s = proj.shape[0]
    tb = _tile(s, 512)
    scale = HEAD_DIM ** -0.5

    def body(q_ref, kv_ref, cat_ref, o_ref):
        kv_t = kv_ref[...]
        for h in range(MEM_HEADS):
            p, _, v = _mem_softmax(q_ref[:, h * 128:(h + 1) * 128].astype(BF16), kv_t, h, scale)
            o_ref[:, h * 128:(h + 1) * 128] = _dot(p.astype(BF16), v, NN).astype(BF16)

    return pl.pallas_call(
        body, grid=(s // tb,),
        in_specs=[pl.BlockSpec((tb, MEM_WIDTH), lambda i: (i, qcol // MEM_WIDTH)), pl.BlockSpec(kv.shape, lambda i: (0, 0)), ANY],
        out_specs=pl.BlockSpec((tb, MEM_WIDTH), lambda i: (i, cat.shape[1] // MEM_WIDTH - 1)),
        out_shape=jax.ShapeDtypeStruct(cat.shape, BF16), input_output_aliases={2: 0},
        compiler_params=_params("parallel"), name=name,
    )(proj, kv, cat)


def memory_attention_backward(proj, qcol, kv, dcat, dcol, name):
    s = proj.shape[0]
    tb = _tile(s, 512)
    scale = HEAD_DIM ** -0.5

    def body(q_ref, kv_ref, do_ref, dq_ref, dkv_ref):
        @pl.when(pl.program_id(0) == 0)
        def _():
            dkv_ref[...] = jnp.zeros_like(dkv_ref)

        kv_t = kv_ref[...]
        for h in range(MEM_HEADS):
            cols = slice(h * 128, (h + 1) * 128)
            q = q_ref[:, cols].astype(BF16)
            p, k, v = _mem_softmax(q, kv_t, h, scale)
            do = do_ref[:, cols].astype(BF16)
            dp = _dot(do, v, NT)
            ds = (p * (dp - jnp.sum(dp * p, axis=-1, keepdims=True)) * scale).astype(BF16)
            dq_ref[:, cols] = _dot(ds, k, NN).astype(BF16)
            dkv_ref[:, cols] += _dot(ds, q, TN)
            dkv_ref[:, MEM_WIDTH + h * 128:MEM_WIDTH + (h + 1) * 128] += _dot(p.astype(BF16), do, TN)

    return pl.pallas_call(
        body, grid=(s // tb,),
        in_specs=[pl.BlockSpec((tb, MEM_WIDTH), lambda i: (i, qcol // MEM_WIDTH)), pl.BlockSpec(kv.shape, lambda i: (0, 0)),
                  pl.BlockSpec((tb, MEM_WIDTH), lambda i: (i, dcol // MEM_WIDTH))],
        out_specs=[pl.BlockSpec((tb, MEM_WIDTH), lambda i: (i, 0)), pl.BlockSpec(kv.shape, lambda i: (0, 0))],
        out_shape=[jax.ShapeDtypeStruct((s, MEM_WIDTH), BF16), jax.ShapeDtypeStruct(kv.shape, F32)],
        compiler_params=_params("arbitrary"), name=name,
    )(proj, kv, dcat)


_SQRT_HALF = math.sqrt(0.5)
_INV_SQRT_2PI = 1.0 / math.sqrt(2.0 * math.pi)


def _gelu(x):
    return 0.5 * x * (1.0 + lax.erf(x * _SQRT_HALF))


def _gelu_grad(x):
    return 0.5 * (1.0 + lax.erf(x * _SQRT_HALF)) + x * (_INV_SQRT_2PI * jnp.exp(-0.5 * x * x))


def _gmlp_specs(s, wd, groups, tb):
    half = lambda c: pl.BlockSpec((tb, wd), lambda i: (i, c))
    ws_spec = pl.BlockSpec((groups, 128, 128), lambda i: (0, 0, 0))
    bs_spec = pl.BlockSpec((groups, 128, 1), lambda i: (0, 0, 0))
    vec = pl.BlockSpec((1, wd), lambda i: (0, 0))
    return half, ws_spec, bs_spec, vec


def _vnorm(zv, g, b):
    mu = jnp.mean(zv, axis=-1, keepdims=True)
    var = jnp.mean(jnp.square(zv - mu), axis=-1, keepdims=True)
    return (zv - mu) * lax.rsqrt(var + LN_EPS) * g + b


def gmlp_forward(proj, ws, bs, vg, vb, name):
    s = proj.shape[0]
    groups = ws.shape[0]
    wd = groups * 128
    tb = _tile(s, 512)
    half, ws_spec, bs_spec, vec = _gmlp_specs(s, wd, groups, tb)

    def body(pu_ref, pv_ref, ws_ref, bs_ref, vg_ref, vb_ref, o_ref, vn_ref):
        causal, _ = _causal_masks()
        vn_ref[...] = _vnorm(_gelu(pv_ref[...]), vg_ref[...], vb_ref[...]).astype(BF16)
        for g in range(groups):
            cols = slice(g * 128, (g + 1) * 128)
            wm = jnp.where(causal, ws_ref[g], 0.0).astype(BF16)
            for c in range(tb // 128):
                rws = slice(c * 128, (c + 1) * 128)
                sg = _dot(wm, vn_ref[rws, cols], NN) + bs_ref[g]
                o_ref[rws, cols] = (_gelu(pu_ref[rws, cols]) * sg).astype(BF16)

    return pl.pallas_call(
        body, grid=(s // tb,),
        in_specs=[half(0), half(1), ws_spec, bs_spec, vec, vec], out_specs=pl.BlockSpec((tb, wd), lambda i: (i, 0)),
        out_shape=jax.ShapeDtypeStruct((s, wd + MEM_WIDTH), BF16),
        scratch_shapes=[pltpu.VMEM((tb, wd), BF16)],
        compiler_params=_params("parallel"), name=name,
    )(proj, proj, ws, bs.reshape(groups, 128, 1), vg.reshape(1, wd), vb.reshape(1, wd))


def gmlp_backward(proj, ws, bs, vg, vb, dcat, name):
    s = proj.shape[0]
    groups = ws.shape[0]
    wd = groups * 128
    tb = _tile(s, 512)
    half, ws_spec, bs_spec, vec = _gmlp_specs(s, wd, groups, tb)

    def body(pu_ref, pv_ref, ws_ref, bs_ref, vg_ref, vb_ref, do_ref, dpu_ref, dpv_ref, dws_ref, dbs_ref, dvg_ref, dvb_ref,
             vn_ref, dvn_ref):
        @pl.when(pl.program_id(0) == 0)
        def _():
            dws_ref[...] = jnp.zeros_like(dws_ref)
            dbs_ref[...] = jnp.zeros_like(dbs_ref)
            dvg_ref[...] = jnp.zeros_like(dvg_ref)
            dvb_ref[...] = jnp.zeros_like(dvb_ref)

        causal, _ = _causal_masks()
        pv = pv_ref[...]
        zv = _gelu(pv)
        vn_ref[...] = _vnorm(zv, vg_ref[...], vb_ref[...]).astype(BF16)
        for g in range(groups):
            cols = slice(g * 128, (g + 1) * 128)
            wm = jnp.where(causal, ws_ref[g], 0.0).astype(BF16)
            dws_g = jnp.zeros((128, 128), F32)
            dbs_g = jnp.zeros((128, 1), F32)
            for c in range(tb // 128):
                rws = slice(c * 128, (c + 1) * 128)
                vn = vn_ref[rws, cols]
                pu = pu_ref[rws, cols]
                do = do_ref[rws, cols]
                sg = _dot(wm, vn, NN) + bs_ref[g]
                dpu_ref[rws, cols] = (do * sg * _gelu_grad(pu)).astype(BF16)
                dsg = do * _gelu(pu)
                dsg_b = dsg.astype(BF16)
                dws_g += _dot(dsg_b, vn, NT)
                dbs_g += jnp.sum(dsg, axis=-1, keepdims=True)
                dvn_ref[rws, cols] = _dot(wm, dsg_b, TN)
            dws_ref[g] += jnp.where(causal, dws_g, 0.0)
            dbs_ref[g] += dbs_g
        dvn = dvn_ref[...]
        dzv, dvg = _ln_bwd_tile(zv, vg_ref[...], dvn)
        dvg_ref[...] += dvg
        dvb_ref[...] += jnp.sum(dvn, axis=0, keepdims=True)
        dpv_ref[...] = (dzv * _gelu_grad(pv)).astype(BF16)

    row = pl.BlockSpec((tb, wd), lambda i: (i, 0))
    dpu, dpv, dws, dbs, dvg, dvb = pl.pallas_call(
        body, grid=(s // tb,),
        in_specs=[half(0), half(1), ws_spec, bs_spec, vec, vec, row],
        out_specs=[row, row, ws_spec, bs_spec, vec, vec],
        out_shape=[jax.ShapeDtypeStruct((s, wd), BF16), jax.ShapeDtypeStruct((s, wd), BF16),
                   jax.ShapeDtypeStruct((groups, 128, 128), F32), jax.ShapeDtypeStruct((groups, 128, 1), F32),
                   jax.ShapeDtypeStruct((1, wd), F32), jax.ShapeDtypeStruct((1, wd), F32)],
        scratch_shapes=[pltpu.VMEM((tb, wd), BF16), pltpu.VMEM((tb, wd), F32)],
        compiler_params=_params("arbitrary"), name=name,
    )(proj, proj, ws, bs.reshape(groups, 128, 1), vg.reshape(1, wd), vb.reshape(1, wd), dcat)
    return dpu, dpv, dws, dbs.reshape(groups, 128), dvg, dvb


def loss_head(y, target, name):
    s, w = y.shape
    tb = _tile(s, 256)
    row = pl.BlockSpec((tb, w), lambda i: (i, 0))
    nsteps = s // tb

    def body(y_ref, t_ref, loss_ref, dy_ref, acc_ref):
        i = pl.program_id(0)

        @pl.when(i == 0)
        def _():
            acc_ref[...] = jnp.zeros_like(acc_ref)

        err = y_ref[...] - t_ref[...]
        dy_ref[...] = err / w
        acc_ref[...] += jnp.sum(jnp.mean(jnp.square(err), axis=-1, keepdims=True), axis=0, keepdims=True)

        @pl.when(i == nsteps - 1)
        def _():
            loss_ref[...] = jnp.broadcast_to(0.5 * acc_ref[...], loss_ref.shape)

    return pl.pallas_call(
        body, grid=(nsteps,), in_specs=[row, row],
        out_specs=[pl.BlockSpec((8, 128), lambda i: (0, 0)), row],
        out_shape=[jax.ShapeDtypeStruct((8, 128), F32), jax.ShapeDtypeStruct((s, w), F32)],
        scratch_shapes=[pltpu.VMEM((1, 1), F32)],
        compiler_params=_params("arbitrary"), name=name,
    )(y, target)


PARTS_WINDOW_ELEMS = 1024 * 1024


def _row_tile(r, c, budget):
    if r * c <= budget or r % 16:
        return r
    fits = [t for t in range(16, r, 16) if r % t == 0 and t * c <= budget]
    return max(fits) if fits else 16


def _sum_in_device_order(p_ref):
    g = p_ref[0].astype(F32)
    for j in range(1, p_ref.shape[0]):
        g = g + p_ref[j].astype(F32)
    return g


def _adamw_update(g, w, m, v):
    nm = ADAM_B1 * m + (1.0 - ADAM_B1) * g
    nv = ADAM_B2 * v + (1.0 - ADAM_B2) * jnp.square(g)
    m_hat = nm / (1.0 - ADAM_B1 ** ADAM_STEP)
    v_hat = nv / (1.0 - ADAM_B2 ** ADAM_STEP)
    return -ADAM_LR * (m_hat / (jnp.sqrt(v_hat) + ADAM_EPS) + ADAM_WD * w), nm, nv


SUM_BYTES_PER_US = 1.7e6


def sum_parts(chunks, name, comm=None):
    p, r, c = chunks[0].shape
    tr = _row_tile(r, c, PARTS_WINDOW_ELEMS // len(chunks))
    nb = r // tr

    def chunk_spec(q):
        return pl.BlockSpec((p, tr, c), lambda ch, i: (0, jnp.where(ch == q, i, 0), 0))

    def body(*refs):
        for q in range(len(chunks)):
            @pl.when(pl.program_id(0) == q)
            def _():
                refs[-1][...] = _sum_in_device_order(refs[q])

    host_us = len(chunks) * r * c * (p * chunks[0].dtype.itemsize + 4) / SUM_BYTES_PER_US
    return _call(
        body, grid=(len(chunks), nb), in_specs=[chunk_spec(q) for q in range(len(chunks))],
        out_specs=[pl.BlockSpec((tr, c), lambda ch, i: (ch * nb + i, 0))],
        out_shape=[jax.ShapeDtypeStruct((len(chunks) * r, c), F32)], scratch_shapes=[],
        args=list(chunks), name=name, semantics=("arbitrary", "arbitrary"),
        jobs=comm.take(host_us) if comm is not None else (),
    )[0]


ELEMENTWISE_BYTES_PER_US = 2.0e6


def adamw(parts, w, m, v, name, emit_grad=True, comm=None):
    p, r, c = parts.shape
    tr = _row_tile(r, c, 160 * 1024)
    n_out = 4 if emit_grad else 3

    def body(p_ref, w_ref, m_ref, v_ref, *out_refs):
        g = _sum_in_device_order(p_ref)
        vals = _adamw_update(g, w_ref[...], m_ref[...], v_ref[...])
        for o_ref, val in zip(out_refs, ((g,) + vals) if emit_grad else vals):
            o_ref[...] = val

    row = pl.BlockSpec((tr, c), lambda i: (i, 0))
    host_us = (parts.size * parts.dtype.itemsize + (3 + n_out) * 4 * r * c) / ELEMENTWISE_BYTES_PER_US
    return _call(
        body, grid=(r // tr,), in_specs=[pl.BlockSpec((p, tr, c), lambda i: (0, i, 0)), row, row, row],
        out_specs=[row] * n_out, out_shape=[jax.ShapeDtypeStruct((r, c), F32)] * n_out, scratch_shapes=[],
        args=[parts, w, m, v], name=name, semantics=("parallel",), jobs=comm.take(host_us) if comm is not None else (),
    )


def kernel(x, mem, w_in_a, w_in_b, w_s, b_s, vnorm_g, vnorm_b, w_mem_kv, w_out, ln1_g, ln1_b, w_ff1, w_ff2, ln2_g, ln2_b, loss_target, m_w_in_a, m_w_in_b, m_w_s, m_b_s, m_vnorm_g, m_vnorm_b, m_w_mem_kv, m_w_out, m_ln1_g, m_ln1_b, m_w_ff1, m_w_ff2, m_ln2_g, m_ln2_b, v_w_in_a, v_w_in_b, v_w_s, v_b_s, v_vnorm_g, v_vnorm_b, v_w_mem_kv, v_w_out, v_ln1_g, v_ln1_b, v_w_ff1, v_w_ff2, v_ln2_g, v_ln2_b):
    depth = w_ff1.shape[0]
    groups = w_s.shape[1]
    mix_w = groups * HEAD_DIM
    n_b = vnorm_g.shape[0]
    alpha = (2.0 * depth) ** 0.25
    me = 4 * lax.axis_index("x") + 2 * lax.axis_index("y") + lax.axis_index("c")
    x0 = x[0]
    mem_b = mem[0].astype(BF16)
    target = loss_target[0]

    comm = Exchanges()
    weight_jobs = []
    for i in range(depth):
        w_in = (w_in_a if i % 2 == 0 else w_in_b)[i // 2]
        pieces = lambda t: jnp.split(t.astype(BF16), WEIGHT_PIECES, axis=1)
        small = [w_mem_kv[i].astype(BF16), w_out[i].T.astype(BF16)] + ([jnp.concatenate([vnorm_g, vnorm_b], axis=0)] if i == 0 else [])
        weight_jobs.append({
            "in": [GatherJob([t], f"g{i}in{q}") for q, t in enumerate(pieces(w_in.T))],
            "out": GatherJob(small, f"g{i}out"),
            "ff1": [GatherJob([t], f"g{i}ff1{q}") for q, t in enumerate(pieces(w_ff1[i].T))],
            "ff2": [GatherJob([t], f"g{i}ff2{q}") for q, t in enumerate(pieces(w_ff2[i]))]})
        for job in [*weight_jobs[i]["in"], weight_jobs[i]["out"], *weight_jobs[i]["ff1"], *weight_jobs[i]["ff2"]]:
            comm.submit(job)

    def gathered(*jobs):
        comm.require(*jobs)
        return [g.reshape(N_DEV * g.shape[1], g.shape[2]) for job in jobs for g in job.outs]


    saved = []
    weights = []
    xf, xb = x0, x0.astype(BF16)
    for i in range(depth):
        j = i // 2
        w_in_t = gathered(*weight_jobs[i]["in"])
        proj = matmul(xb, w_in_t, "nt", name=f"in_proj_{i % 2}", comm=comm)
        if i % 2 == 0:
            os_, ls_ = [], []
            for g, d in enumerate(A_DILATIONS):
                o_g, l_g = attention_forward(proj, g, d, groups, name=f"attn_fwd_d{d}", comm=comm)
                os_.append(o_g)
                ls_.append(l_g)
            o, cat, lse = attention_combine(os_, ls_, name="attn_combine")
            qcol = 9 * mix_w
            extra = (o, lse)
        else:
            cat = gmlp_forward(proj, w_s[j], b_s[j], vg_full[j], vb_full[j], name="gmlp_fwd")
            qcol = 2 * mix_w
            extra = ()
        w_kv, w_o_t = gathered(weight_jobs[i]["out"])[:2]
        if i == 0:
            vnorm = jnp.transpose(weight_jobs[0]["out"].outs[2], (1, 0, 2)).reshape(2 * n_b, mix_w)
            vg_full, vb_full = vnorm[:n_b], vnorm[n_b:]
        kv = matmul(mem_b, w_kv, "nn", name="mem_kv")
        cat = memory_attention(proj, qcol, kv, cat, name=f"mem_attn_{i % 2}")
        r1, x1, x1b = matmul(cat, w_o_t, "nt", name="out_proj_ln1", comm=comm, tiles=(256, w_o_t.shape[0], 2560),
                             extras=(xf, ln1_g[i][None], ln1_b[i][None]), out_dtypes=(F32, F32, BF16),
                             epilogue=lambda acc, x_t, g_t, b_t: _residual_ln(alpha * x_t + acc, g_t, b_t))
        w_1_t = gathered(*weight_jobs[i]["ff1"])
        h, hid = matmul(x1b, w_1_t, "nt", name="ff1", out_dtypes=(F32, BF16), comm=comm,
                        epilogue=lambda acc: (acc, jnp.square(jnp.maximum(acc, 0.0))))
        w_2 = gathered(*weight_jobs[i]["ff2"])
        y2 = matmul(hid, w_2, "nn", name="ff2", comm=comm)
        r2, x2, x2b = ln_residual(x1, y2, ln2_g[i], ln2_b[i], alpha, name="ln2", comm=comm)
        weights.append((w_in_t, w_o_t, w_1_t, w_2, w_kv))
        saved.append((xb, proj, kv, extra, qcol, cat, r1, x1b, h, hid, r2))
        xf, xb = x2, x2b

    loss_tile, dx = loss_head(xf, target, name="loss_head")
    loss = lax.psum(loss_tile[0, 0], ("x", "y", "c"))

    def exchange(dw, tag):
        rows, cols = dw.shape[0] // N_DEV, dw.shape[1]
        p = dw.reshape(N_DEV, rows, cols)
        whole_us = ExchangeJob.US_PER_MB * dw.size * dw.dtype.itemsize / 1e6
        counts = [q for q in range(1, rows // 16 + 1) if (rows // 16) % q == 0]
        n_chunks = next((q for q in counts if whole_us <= ExchangeJob.CHUNK_US * q), counts[-1])
        step = rows // n_chunks
        return [comm.submit(ExchangeJob(p, q * step, step, f"x{tag}{q}")) for q in range(n_chunks)]

    recv = {n: [None] * depth for n in ("w_in", "w_mem_kv", "w_out", "w_ff1", "w_ff2")}
    small = {n: [None] * depth for n in ("ln1_g", "ln1_b", "ln2_g", "ln2_b")}
    small_b = {n: [None] * n_b for n in ("w_s", "b_s", "vnorm_g", "vnorm_b")}
    for i in reversed(range(depth)):
        w_in_t, w_o_t, w_1_t, w_2, w_kv = weights[i]
        xb, proj, kv, extra, qcol, cat, r1, x1b, h, hid, r2 = saved[i]
        j = i // 2
        dr2, dr2b, small["ln2_g"][i], small["ln2_b"][i] = ln_backward(r2, ln2_g[i], dx, name="ln2_bwd", comm=comm)
        dh = matmul(dr2b, w_2, "nt", name="ff2_dx", extras=(h,), out_dtypes=(BF16,), comm=comm,
                    epilogue=lambda acc, h_t: (acc * (2.0 * jnp.maximum(h_t, 0.0)),))
        recv["w_ff2"][i] = exchange(matmul(hid, dr2b, "tn", name="ff2_dw", out_dtypes=(BF16,), comm=comm), f"ff2{i}")
        dx1 = matmul(dh, w_1_t, "nn", name="ff1_dx", extras=(dr2,), comm=comm, epilogue=lambda acc, res: (acc + alpha * res,))
        recv["w_ff1"][i] = exchange(matmul(dh, x1b, "tn", name="ff1_dw", out_dtypes=(BF16,), comm=comm), f"ff1{i}")
        dr1, dr1b, small["ln1_g"][i], small["ln1_b"][i] = ln_backward(r1, ln1_g[i], dx1, name="ln1_bwd", comm=comm)
        dcat = matmul(dr1b, w_o_t, "nn", name="out_proj_dx", comm=comm)
        recv["w_out"][i] = exchange(matmul(dr1b, cat, "tn", name="out_proj_dw", out_dtypes=(BF16,), comm=comm), f"out{i}")
        dq_mem, dkv = memory_attention_backward(proj, qcol, kv, dcat, mix_w, name=f"mem_attn_bwd_{i % 2}")
        recv["w_mem_kv"][i] = exchange(matmul(mem_b, dkv, "tn", name="mem_kv_dw", out_dtypes=(BF16,)), f"kv{i}")
        if i % 2 == 0:
            o, lse = extra
            pieces = []
            for g, d in enumerate(A_DILATIONS):
                pieces += [t.astype(BF16) for t in attention_backward(proj, o, lse, dcat, g, d, groups, name=f"attn_bwd_d{d}", comm=comm)]
            dproj = jnp.concatenate(pieces + [dq_mem], axis=1)
        else:
            dpu, dpv, small_b["w_s"][j], small_b["b_s"][j], small_b["vnorm_g"][j], small_b["vnorm_b"][j] = gmlp_backward(
                proj, w_s[j], b_s[j], vg_full[j], vb_full[j], dcat, name="gmlp_bwd")
            dproj = jnp.concatenate([dpu, dpv, dq_mem], axis=1)
        recv["w_in"][i] = exchange(matmul(dproj, xb, "tn", name=f"in_proj_dw_{i % 2}", out_dtypes=(BF16,), comm=comm), f"in{i}")
        dx = matmul(dproj, w_in_t, "nn", name=f"in_proj_dx_{i % 2}", extras=(dr1,), comm=comm, epilogue=lambda acc, res: (acc + alpha * res,))
    grad_x = dx[None]

    flat_small = jnp.concatenate(
        [jnp.concatenate(small[n], axis=0).reshape(-1) for n in ("ln1_g", "ln1_b", "ln2_g", "ln2_b")]
        + [jnp.stack(small_b[n]).reshape(-1) for n in ("w_s", "b_s", "vnorm_g", "vnorm_b")])
    pad = (-flat_small.size) % 1024
    small_job = comm.submit(GatherJob([jnp.pad(flat_small, (0, pad)).reshape(-1, 1024)], "gsmall"))

    results = {}

    def received(name, i):
        return comm.require(*recv[name][i])

    def update(name, layers, w_, m_, v_, transposed):
        sums = [sum_parts(received(name, i), name=f"sum_{name}_{i % 2}", comm=comm) for i in layers]
        grad = jnp.stack([t.T if transposed else t for t in sums])
        flat = lambda t: t.reshape(-1, t.shape[-1])
        upd = adamw(flat(grad)[None], flat(w_), flat(m_), flat(v_), name=f"adamw_{name}_{layers[0] % 2}", emit_grad=False)
        return [grad] + [t.reshape(w_.shape) for t in upd]

    results["w_ff2"] = update("w_ff2", list(range(depth)), w_ff2, m_w_ff2, v_w_ff2, False)
    results["w_ff1"] = update("w_ff1", list(range(depth)), w_ff1, m_w_ff1, v_w_ff1, True)
    results["w_out"] = update("w_out", list(range(depth)), w_out, m_w_out, v_w_out, True)
    results["w_mem_kv"] = update("w_mem_kv", list(range(depth)), w_mem_kv, m_w_mem_kv, v_w_mem_kv, False)
    results["w_in_b"] = update("w_in", list(range(1, depth, 2)), w_in_b, m_w_in_b, v_w_in_b, True)
    comm.require(*[job for i in range(0, depth, 2) for job in recv["w_in"][i]], small_job)
    results["w_in_a"] = update("w_in", list(range(0, depth, 2)), w_in_a, m_w_in_a, v_w_in_a, True)

    gathered_small = comm.require(small_job)[0].reshape(N_DEV, -1)
    at = 0

    def take(shape):
        nonlocal at
        size = math.prod(shape)
        at += size
        return gathered_small[:, at - size:at].reshape((N_DEV,) + shape)

    for n, w_, m_, v_ in (("ln1_g", ln1_g, m_ln1_g, v_ln1_g), ("ln1_b", ln1_b, m_ln1_b, v_ln1_b),
                          ("ln2_g", ln2_g, m_ln2_g, v_ln2_g), ("ln2_b", ln2_b, m_ln2_b, v_ln2_b)):
        results[n] = adamw(take(w_.shape), w_, m_, v_, name="adamw_ln")
    lanes = lambda t: t.reshape(-1, 128)
    results["w_s"] = [t.reshape(w_s.shape) for t in adamw(take((w_s.size // 128, 128)), lanes(w_s), lanes(m_w_s), lanes(v_w_s), name="adamw_w_s")]
    results["b_s"] = [t.reshape(b_s.shape) for t in adamw(take((b_s.size // 128, 128)), lanes(b_s), lanes(m_b_s), lanes(v_b_s), name="adamw_b_s")]
    for n, w_, m_, v_ in (("vnorm_g", vnorm_g, m_vnorm_g, v_vnorm_g), ("vnorm_b", vnorm_b, m_vnorm_b, v_vnorm_b)):
        parts = lax.dynamic_slice_in_dim(take((n_b, mix_w)), me * w_.shape[1], w_.shape[1], axis=2)
        results[n] = adamw(parts, w_, m_, v_, name="adamw_vnorm")

    order =("w_in_a", "w_in_b", "w_s", "b_s", "vnorm_g", "vnorm_b", "w_mem_kv", "w_out", "ln1_g", "ln1_b", "w_ff1", "w_ff2", "ln2_g", "ln2_b")
    return (loss, grad_x, *[results[n][0] for n in order], *[results[n][1] for n in order],
            *[results[n][2] for n in order], *[results[n][3] for n in order])
```

```python
import math

import jax
import jax.numpy as jnp
from jax import lax
from jax.experimental import pallas as pl
from jax.experimental.pallas import tpu as pltpu

F32 = jnp.float32
BF16 = jnp.bfloat16
N_DEV = 8
HEAD_DIM = 128
MEM_HEADS = 4
MEM_WIDTH = MEM_HEADS * HEAD_DIM
A_DILATIONS = (1, 4, 16)
WEIGHT_PIECES = 4
LN_EPS = 1e-5
ADAM_LR, ADAM_B1, ADAM_B2, ADAM_EPS, ADAM_WD, ADAM_STEP = 0.001, 0.9, 0.999, 1e-08, 0.01, 10
VMEM_LIMIT_BYTES = 56 * 1024 * 1024
MESH = pl.DeviceIdType.MESH
ANY = pl.BlockSpec(memory_space=pl.ANY)


def _params(*sem):
    return pltpu.CompilerParams(dimension_semantics=sem, vmem_limit_bytes=VMEM_LIMIT_BYTES)


def _tile(dim, pref):
    if dim <= pref:
        return dim
    best = None
    for t in range(128, pref + 1, 128):
        if dim % t == 0:
            best = t
    return best if best is not None else dim


def _dot(a, b, dims):
    return lax.dot_general(a, b, (dims, ((), ())), preferred_element_type=F32)


NN = ((1,), (0,))
NT = ((1,), (1,))
TN = ((0,), (0,))


def _my_position():
    return lax.axis_index("x"), lax.axis_index("y"), lax.axis_index("c")


class GatherJob:
    US_PER_MB = 46.0
    OVERSHOOT = 1.05

    def __init__(self, arrays, tag):
        self.arrays, self.tag, self.outs = list(arrays), tag, None
        self.out_shapes = [jax.ShapeDtypeStruct((N_DEV,) + v.shape, v.dtype) for v in self.arrays]
        self.n_sems, self.n_local = 7 * len(self.arrays), len(self.arrays)
        self.est_us = self.US_PER_MB * sum(v.size * v.dtype.itemsize for v in self.arrays) / 1e6

    def phases(self, v_refs, out_refs, send_sems, recv_sems, local_sems, sem0, loc0):
        n = len(self.arrays)
        x, y, c = _my_position()
        me, sibling = (x, y, c), (x, y, 1 - c)
        chips = [(1 - x, y), (x, 1 - y), (1 - x, 1 - y)]

        def copy(a, k, block, to, from_input=False):
            px, py, pc = block
            slot = out_refs[a].at[4 * px + 2 * py + pc]
            return pltpu.make_async_remote_copy(
                src_ref=v_refs[a] if from_input else slot, dst_ref=slot, send_sem=send_sems.at[sem0 + 7 * a + k],
                recv_sem=recv_sems.at[sem0 + 7 * a + k], device_id=to, device_id_type=MESH)

        def mine(a):
            return pltpu.make_async_copy(v_refs[a], out_refs[a].at[4 * x + 2 * y + c], local_sems.at[loc0 + a])

        def first(a):
            return [copy(a, 0, me, sibling, True)] + [copy(a, 1 + j, me, (*chip, c), True) for j, chip in enumerate(chips)]

        def start():
            for a in range(n):
                mine(a).start()
                for cp in first(a):
                    cp.start()

        def middle():
            for j, chip in enumerate(chips):
                for a in range(n):
                    copy(a, 1 + j, (*chip, c), me).wait_recv()
                    copy(a, 4 + j, (*chip, c), sibling).start()

        def finish():
            for a in range(n):
                copy(a, 0, sibling, me).wait_recv()
                for j, chip in enumerate(chips):
                    copy(a, 4 + j, (*chip, 1 - c), me).wait_recv()
                for cp in first(a) + [copy(a, 4 + j, (*chip, c), sibling) for j, chip in enumerate(chips)]:
                    cp.wait_send()
                mine(a).wait()

        return start, middle, finish


class ExchangeJob:
    US_PER_MB = 11.0
    OVERSHOOT = 1.15
    CHUNK_US = 50.0
    RELATIONS = [(dx, dy, dc) for dx in (0, 1) for dy in (0, 1) for dc in (0, 1)][1:]

    def __init__(self, p, row0, rows, tag):
        self.arrays, self.tag, self.outs = [p], tag, None
        self.row0, self.rows = row0, rows
        self.out_shapes = [jax.ShapeDtypeStruct((N_DEV, rows, p.shape[2]), p.dtype)]
        self.n_sems, self.n_local = 7, 1
        self.est_us = self.US_PER_MB * N_DEV * rows * p.shape[2] * p.dtype.itemsize / 1e6

    def phases(self, p_refs, out_refs, send_sems, recv_sems, local_sems, sem0, loc0):
        (p_ref,), (out_ref,) = p_refs, out_refs
        x, y, c = _my_position()
        me = 4 * x + 2 * y + c
        chunk = pl.ds(self.row0, self.rows)

        def mine():
            return pltpu.make_async_copy(p_ref.at[me, chunk], out_ref.at[me], local_sems.at[loc0])

        def copies(arriving):
            made = []
            for k, (dx, dy, dc) in enumerate(self.RELATIONS):
                px, py, pc = (x + dx) % 2, (y + dy) % 2, (c + dc) % 2
                peer = 4 * px + 2 * py + pc
                made.append(pltpu.make_async_remote_copy(
                    src_ref=p_ref.at[peer, chunk], dst_ref=out_ref.at[peer if arriving else me],
                    send_sem=send_sems.at[sem0 + k], recv_sem=recv_sems.at[sem0 + k],
                    device_id=(px, py, pc), device_id_type=MESH))
            return made

        def start():
            mine().start()
            for send in copies(False):
                send.start()

        def finish():
            for arrival in copies(True):
                arrival.wait_recv()
            for send in copies(False):
                send.wait_send()
            mine().wait()

        return start, (lambda: None), finish


class Exchanges:
    MIN_HOST_US = 20.0
    MAX_JOBS = 8

    def __init__(self):
        self.queue = []

    def submit(self, job):
        self.queue.append(job)
        return job

    def take(self, host_us):
        jobs, used = [], 0.0
        if host_us >= self.MIN_HOST_US:
            for job in list(self.queue):
                if used + job.est_us <= job.OVERSHOOT * host_us and len(jobs) < self.MAX_JOBS:
                    used += job.est_us
                    jobs.append(job)
                    self.queue.remove(job)
        return jobs

    def require(self, *jobs):
        waiting = [job for job in jobs if job.outs is None]
        if waiting:
            for job in waiting:
                self.queue.remove(job)
            _call(lambda: None, grid=(), in_specs=[], out_specs=[], out_shape=[], scratch_shapes=[], args=[],
                  name="exchange", semantics=(), jobs=waiting)
        return [job.outs[0] for job in jobs]


def _call(body, *, grid, in_specs, out_specs, out_shape, scratch_shapes, args, name, semantics, jobs=()):
    if not jobs:
        return pl.pallas_call(body, grid=grid, in_specs=in_specs, out_specs=out_specs, out_shape=out_shape,
                              scratch_shapes=scratch_shapes, compiler_params=_params(*semantics), name=name)(*args)
    n_in, n_out, n_scr = len(in_specs), len(out_shape), len(scratch_shapes)
    j_in = [a for job in jobs for a in job.arrays]
    j_out = [s for job in jobs for s in job.out_shapes]
    n_sems, n_local = sum(job.n_sems for job in jobs), sum(job.n_local for job in jobs)
    steps = math.prod(grid)
    middle_step = (7 * steps) // 8

    def wrapped(*refs):
        ins, refs = refs[:n_in], refs[n_in:]
        j_ins, refs = refs[:len(j_in)], refs[len(j_in):]
        outs, refs = refs[:n_out], refs[n_out:]
        j_outs, refs = refs[:len(j_out)], refs[len(j_out):]
        scratch, (send_sems, recv_sems, local_sems) = refs[:n_scr], refs[n_scr:]
        step = 0
        for axis, extent in enumerate(grid):
            step = step * extent + pl.program_id(axis)
        phases, at_in, at_out, sem0, loc0 = [], 0, 0, 0, 0
        for job in jobs:
            k_in, k_out = len(job.arrays), len(job.out_shapes)
            phases.append(job.phases(j_ins[at_in:at_in + k_in], j_outs[at_out:at_out + k_out], send_sems, recv_sems, local_sems, sem0, loc0))
            at_in, at_out, sem0, loc0 = at_in + k_in, at_out + k_out, sem0 + job.n_sems, loc0 + job.n_local

        def run(which):
            for ph in phases:
                ph[which]()

        if steps == 1:
            run(0)
            body(*ins, *outs, *scratch)
            run(1)
            run(2)
        else:
            pl.when(step == 0)(lambda: run(0))
            body(*ins, *outs, *scratch)
            pl.when(step == middle_step)(lambda: run(1))
            pl.when(step == steps - 1)(lambda: run(2))

    res = pl.pallas_call(
        wrapped, grid=grid, in_specs=list(in_specs) + [ANY] * len(j_in), out_specs=list(out_specs) + [ANY] * len(j_out),
        out_shape=list(out_shape) + j_out,
        scratch_shapes=list(scratch_shapes) + [pltpu.SemaphoreType.DMA((n_sems,)), pltpu.SemaphoreType.DMA((n_sems,)),
                                               pltpu.SemaphoreType.DMA((n_local,))],
        compiler_params=_params(*(["arbitrary"] * len(grid))), name=name + "".join("__" + job.tag for job in jobs),
    )(*args, *j_in)
    at = n_out
    for job in jobs:
        job.outs = list(res[at:at + len(job.out_shapes)])
        at += len(job.out_shapes)
    return list(res[:n_out])


MATMUL_TILES = {"nn": (1024, 512, 2560), "nt": (1024, 512, 2560), "tn": (1024, 1024, 2048)}
MATMUL_FLOPS_PER_US = {"nn": 7.5e8, "nt": 7.5e8, "tn": 8.5e8}


def matmul(a, b, mode, name, epilogue=None, extras=(), out_dtypes=(F32,), comm=None, tiles=None):
    pieces = list(b) if isinstance(b, (list, tuple)) else [b]
    n_pc = len(pieces)
    b_rows, b_cols = pieces[0].shape[0], n_pc * pieces[0].shape[1]
    assert mode != "tn" or n_pc == 1
    if mode == "nn":
        (m, k), (k2, n) = a.shape, (b_rows, b_cols)
    elif mode == "nt":
        (m, k), (n, k2) = a.shape, (b_rows, b_cols)
    else:
        (k, m), (k2, n) = a.shape, (b_rows, b_cols)
    assert k == k2, (a.shape, (b_rows, b_cols), mode)
    tm_pref, tn_pref, tk_pref = tiles or MATMUL_TILES[mode]
    tm = _tile(m, tm_pref)
    if mode == "tn" and tm < tm_pref:
        tn_pref *= 2
    tn = _tile(n // n_pc if mode == "nn" else n, tn_pref)
    tk = _tile(k, tk_pref)
    nj, nk = n // tn, k // tk
    assert mode != "nt" or n_pc == 1 or nk == 1, (k, tk)
    per_piece = nj // n_pc
    k_piece = k // n_pc
    dims = {"nn": NN, "nt": NT, "tn": TN}[mode]
    a_spec = pl.BlockSpec((tk, tm), lambda i, j, kk: (kk, i)) if mode == "tn" else pl.BlockSpec((tm, tk), lambda i, j, kk: (i, kk))

    def b_spec(p):
        if mode == "nt" and n_pc > 1:
            return pl.BlockSpec((tn, k_piece), lambda i, j, kk: (j, 0))
        if mode == "nt":
            return pl.BlockSpec((tn, tk), lambda i, j, kk: (j, kk))
        if n_pc == 1:
            return pl.BlockSpec((tk, tn), lambda i, j, kk: (kk, j))
        return pl.BlockSpec((tk, tn), lambda i, j, kk: (jnp.where(j // per_piece == p, kk, 0),
                                                       jnp.where(j // per_piece == p, j - p * per_piece, 0)))

    tile_spec = pl.BlockSpec((tm, tn), lambda i, j, kk: (i, j))
    row_spec = pl.BlockSpec((1, tn), lambda i, j, kk: (0, j))
    n_ex, n_out = len(extras), len(out_dtypes)
    if epilogue is None:
        epilogue = lambda acc: (acc,) * n_out

    def body(a_ref, *rest):
        b_refs, rest = rest[:n_pc], rest[n_pc:]
        ex_refs, out_refs, acc_ref = rest[:n_ex], rest[n_ex:n_ex + n_out], rest[-1]
        j, kk = pl.program_id(1), pl.program_id(2)

        def finish(acc):
            for o_ref, val in zip(out_refs, epilogue(acc, *[e[...] for e in ex_refs])):
                o_ref[...] = val.astype(o_ref.dtype)

        if mode == "nt" and n_pc > 1:
            finish(sum(_dot(a_ref[:, p * k_piece:(p + 1) * k_piece].astype(BF16), b_ref[...].astype(BF16), NT)
                       for p, b_ref in enumerate(b_refs)))
            return

        for p, b_ref in enumerate(b_refs):
            in_use = True if n_pc == 1 else (j // per_piece == p)

            def product():
                return _dot(a_ref[...].astype(BF16), b_ref[...].astype(BF16), dims)

            if nk == 1:
                if n_pc == 1:
                    finish(product())
                else:
                    pl.when(in_use)(lambda: finish(product()))
                continue

            @pl.when(in_use & (kk == 0))
            def _():
                acc_ref[...] = product()

            @pl.when(in_use & (kk > 0) & (kk < nk - 1))
            def _():
                acc_ref[...] += product()

            @pl.when(in_use & (kk == nk - 1))
            def _():
                finish(acc_ref[...] + product())

    host_us = 2.0 * m * n * k / MATMUL_FLOPS_PER_US[mode]
    outs = _call(
        body,
        grid=(m // tm, nj, nk),
        in_specs=[a_spec] + [b_spec(p) for p in range(n_pc)] + [row_spec if e.shape[0] == 1 else tile_spec for e in extras],
        out_specs=[tile_spec] * n_out,
        out_shape=[jax.ShapeDtypeStruct((m, n), dt) for dt in out_dtypes],
        scratch_shapes=[pltpu.VMEM((tm, tn), F32)],
        args=[a, *pieces, *extras], name=name, semantics=("parallel", "parallel", "arbitrary"),
        jobs=comm.take(host_us) if comm is not None else (),
    )
    return outs[0] if n_out == 1 else outs


LN_BYTES_PER_US = 3.0e6


def _residual_ln(r, g, b):
    mu = jnp.mean(r, axis=-1, keepdims=True)
    var = jnp.mean(jnp.square(r - mu), axis=-1, keepdims=True)
    xn = (r - mu) * lax.rsqrt(var + LN_EPS) * g + b
    return r, xn, xn


def ln_residual(x, y, g, b, alpha, name, comm=None):
    s, w = x.shape
    tb = _tile(s, 256)
    row = pl.BlockSpec((tb, w), lambda i: (i, 0))
    vec = pl.BlockSpec((1, w), lambda i: (0, 0))

    def body(x_ref, y_ref, g_ref, b_ref, r_ref, xn_ref, xnb_ref):
        r, xn, _ = _residual_ln(alpha * x_ref[...] + y_ref[...], g_ref[...], b_ref[...])
        r_ref[...] = r
        xn_ref[...] = xn
        xnb_ref[...] = xn.astype(BF16)

    return _call(
        body, grid=(s // tb,), in_specs=[row, row, vec, vec], out_specs=[row, row, row],
        out_shape=[jax.ShapeDtypeStruct((s, w), F32), jax.ShapeDtypeStruct((s, w), F32), jax.ShapeDtypeStruct((s, w), BF16)],
        scratch_shapes=[], args=[x, y, g.reshape(1, w), b.reshape(1, w)], name=name, semantics=("parallel",),
        jobs=comm.take(18.0 * s * w / LN_BYTES_PER_US) if comm is not None else (),
    )


def _ln_bwd_tile(r, g, dxn):
    mu = jnp.mean(r, axis=-1, keepdims=True)
    cen = r - mu
    rstd = lax.rsqrt(jnp.mean(jnp.square(cen), axis=-1, keepdims=True) + LN_EPS)
    xhat = cen * rstd
    dxh = dxn * g
    dr = rstd * (dxh - jnp.mean(dxh, axis=-1, keepdims=True) - xhat * jnp.mean(dxh * xhat, axis=-1, keepdims=True))
    return dr, jnp.sum(dxn * xhat, axis=0, keepdims=True)


def ln_backward(r, g, dxn, name, comm=None):
    s, w = r.shape
    tb = _tile(s, 256)
    row = pl.BlockSpec((tb, w), lambda i: (i, 0))
    vec = pl.BlockSpec((1, w), lambda i: (0, 0))

    def body(r_ref, g_ref, d_ref, dr_ref, drb_ref, dg_ref, db_ref):
        @pl.when(pl.program_id(0) == 0)
        def _():
            dg_ref[...] = jnp.zeros_like(dg_ref)
            db_ref[...] = jnp.zeros_like(db_ref)

        dxn = d_ref[...]
        dr, dg = _ln_bwd_tile(r_ref[...], g_ref[...], dxn)
        dr_ref[...] = dr
        drb_ref[...] = dr.astype(BF16)
        dg_ref[...] += dg
        db_ref[...] += jnp.sum(dxn, axis=0, keepdims=True)

    return _call(
        body, grid=(s // tb,), in_specs=[row, vec, row], out_specs=[row, row, vec, vec],
        out_shape=[jax.ShapeDtypeStruct((s, w), F32), jax.ShapeDtypeStruct((s, w), BF16),
                   jax.ShapeDtypeStruct((1, w), F32), jax.ShapeDtypeStruct((1, w), F32)],
        scratch_shapes=[], args=[r, g.reshape(1, w), dxn], name=name, semantics=("arbitrary",),
        jobs=comm.take(14.0 * s * w / LN_BYTES_PER_US) if comm is not None else (),
    )


ATTENTION_US = {("fwd", 1): 55.0, ("fwd", 4): 80.0, ("fwd", 16): 100.0, ("bwd", 1): 115.0, ("bwd", 4): 130.0, ("bwd", 16): 190.0}
ATTENTION_HEADS_PER_STEP = {("fwd", 1): 8, ("fwd", 4): 1, ("fwd", 16): 1, ("bwd", 1): 8, ("bwd", 4): 1, ("bwd", 16): 1}


def _causal_masks():
    qi = lax.broadcasted_iota(jnp.int32, (128, 128), 0)
    kj = lax.broadcasted_iota(jnp.int32, (128, 128), 1)
    return kj <= qi, kj >= qi


def _window_mask(has_previous):
    qi = lax.broadcasted_iota(jnp.int32, (128, 256), 0)
    kj = lax.broadcasted_iota(jnp.int32, (128, 256), 1)
    return ((kj < 128) & (kj >= qi) & has_previous) | ((kj >= 128) & (kj - 128 <= qi))


def _sub_rows(d, r):
    return pl.ds(r, 128, stride=d) if d > 1 else pl.ds(0, 128)


def _for_each_residue(d, fn):
    if d <= 4:
        for r in range(d):
            fn(r)
    else:
        def step(r, carry):
            fn(r)
            return carry
        lax.fori_loop(0, d, step, 0, unroll=8)


def attention_forward(proj, group, d, heads, name, comm=None):
    s = proj.shape[0]
    rows = 128 * d
    nb = s // rows
    scale = HEAD_DIM ** -0.5
    hp = math.gcd(heads, ATTENTION_HEADS_PER_STEP["fwd", d])
    wide = 128 * hp
    qc, kc, vc = (group * 3) * heads // hp, (group * 3 + 1) * heads // hp, (group * 3 + 2) * heads // hp

    def cur(col):
        return pl.BlockSpec((rows, wide), lambda h, n: (n, col + h))

    def prev(col):
        return pl.BlockSpec((rows, wide), lambda h, n: (jnp.maximum(n - 1, 0), col + h))

    out = pl.BlockSpec((rows, wide), lambda h, n: (n, h))

    def body(q_ref, kc_ref, kp_ref, vc_ref, vp_ref, o_ref, l_ref):
        mask = _window_mask(pl.program_id(1) > 0)

        def one(r):
            rws = _sub_rows(d, r)
            for hh in range(hp):
                sl = (rws, pl.ds(hh * 128, 128))
                k = jnp.concatenate([kp_ref[sl].astype(BF16), kc_ref[sl].astype(BF16)], axis=0)
                v = jnp.concatenate([vp_ref[sl].astype(BF16), vc_ref[sl].astype(BF16)], axis=0)
                sc = jnp.where(mask, _dot(q_ref[sl].astype(BF16), k, NT) * scale, -jnp.inf)
                m = jnp.max(sc, axis=-1, keepdims=True)
                e = jnp.exp(sc - m)
                l = jnp.sum(e, axis=-1, keepdims=True)
                o_ref[sl] = _dot(e.astype(BF16), v, NN) / l
                l_ref[sl] = jnp.broadcast_to(m + jnp.log(l), (128, 128))

        _for_each_residue(d, one)

    return _call(
        body, grid=(heads // hp, nb),
        in_specs=[cur(qc), cur(kc), prev(kc), cur(vc), prev(vc)], out_specs=[out, out],
        out_shape=[jax.ShapeDtypeStruct((s, heads * 128), F32)] * 2, scratch_shapes=[],
        args=[proj] * 5, name=name, semantics=("parallel", "parallel"),
        jobs=comm.take(ATTENTION_US["fwd", d] * s * heads / (4096 * 8)) if comm is not None else (),
    )


def attention_combine(os_, ls_, name):
    s, w = os_[0].shape
    tb = _tile(s, 256)
    row = pl.BlockSpec((tb, w), lambda i: (i, 0))

    def body(o0, o1, o2, l0, l1, l2, o_ref, ob_ref, lse_ref):
        a, b, c = l0[...], l1[...], l2[...]
        m = jnp.maximum(jnp.maximum(a, b), c)
        ea, eb, ec = jnp.exp(a - m), jnp.exp(b - m), jnp.exp(c - m)
        tot = ea + eb + ec
        o = (ea / tot) * o0[...] + (eb / tot) * o1[...] + (ec / tot) * o2[...]
        o_ref[...] = o
        ob_ref[...] = o.astype(BF16)
        lse_ref[...] = m + jnp.log(tot)

    return pl.pallas_call(
        body, grid=(s // tb,), in_specs=[row] * 6, out_specs=[row] * 3,
        out_shape=[jax.ShapeDtypeStruct((s, w), F32), jax.ShapeDtypeStruct((s, w + MEM_WIDTH), BF16), jax.ShapeDtypeStruct((s, w), F32)],
        compiler_params=_params("parallel"), name=name,
    )(*os_, *ls_)


def attention_backward(proj, o, lse, dcat, group, d, heads, name, comm=None):
    s = proj.shape[0]
    rows = 128 * d
    nb = s // rows
    scale = HEAD_DIM ** -0.5
    hp = math.gcd(heads, ATTENTION_HEADS_PER_STEP["bwd", d])
    wide = 128 * hp
    qc, kc, vc = (group * 3) * heads // hp, (group * 3 + 1) * heads // hp, (group * 3 + 2) * heads // hp

    def at(col, shift):
        return pl.BlockSpec((rows, wide), lambda h, n: (jnp.clip(n + shift, 0, nb - 1), col + h))

    def out(col):
        return pl.BlockSpec((rows, wide), lambda h, n: (n, col + h))

    def body(qc_ref, qn_ref, kc_ref, kp_ref, vc_ref, vp_ref, doc_ref, don_ref, oc_ref, on_ref, lc_ref, ln_ref,
             dq_ref, dk_ref, dv_ref):
        n = pl.program_id(1)
        mask_w = _window_mask(n > 0)
        mask_n = _causal_masks()[1] & (n < nb - 1)

        def tile(q, k, v, do, lse_t, dsum, mask):
            p = jnp.where(mask, jnp.exp(_dot(q, k, NT) * scale - lse_t), 0.0)
            ds = (p * (_dot(do, v, NT) - dsum) * scale).astype(BF16)
            return p.astype(BF16), ds

        def one(r):
            rws = _sub_rows(d, r)
            for hh in range(hp):
                sl = (rws, pl.ds(hh * 128, 128))
                q_c, q_n = qc_ref[sl].astype(BF16), qn_ref[sl].astype(BF16)
                k_c, v_c = kc_ref[sl].astype(BF16), vc_ref[sl].astype(BF16)
                k_w = jnp.concatenate([kp_ref[sl].astype(BF16), k_c], axis=0)
                v_w = jnp.concatenate([vp_ref[sl].astype(BF16), v_c], axis=0)
                do_c, do_n = doc_ref[sl], don_ref[sl]
                dsum_c = jnp.sum(do_c * oc_ref[sl], axis=-1, keepdims=True)
                dsum_n = jnp.sum(do_n * on_ref[sl], axis=-1, keepdims=True)
                do_c, do_n = do_c.astype(BF16), do_n.astype(BF16)
                lse_c = lc_ref[sl]
                p_w, ds_w = tile(q_c, k_w, v_w, do_c, jnp.concatenate([lse_c, lse_c], axis=1), dsum_c, mask_w)
                p_n, ds_n = tile(q_n, k_c, v_c, do_n, ln_ref[sl], dsum_n, mask_n)
                dq_ref[sl] = _dot(ds_w, k_w, NN)
                both_q = jnp.concatenate([q_c, q_n], axis=0)
                both_do = jnp.concatenate([do_c, do_n], axis=0)
                dk_ref[sl] = _dot(jnp.concatenate([ds_w[:, 128:], ds_n], axis=0), both_q, TN)
                dv_ref[sl] = _dot(jnp.concatenate([p_w[:, 128:], p_n], axis=0), both_do, TN)

        _for_each_residue(d, one)

    w = heads * 128
    return _call(
        body, grid=(heads // hp, nb),
        in_specs=[at(qc, 0), at(qc, 1), at(kc, 0), at(kc, -1), at(vc, 0), at(vc, -1),
                  at(0, 0), at(0, 1), at(0, 0), at(0, 1), at(0, 0), at(0, 1)],
        out_specs=[out(0)] * 3,
        out_shape=[jax.ShapeDtypeStruct((s, w), F32)] * 3, scratch_shapes=[],
        args=[proj] * 6 + [dcat, dcat, o, o, lse, lse], name=name, semantics=("parallel", "parallel"),
        jobs=comm.take(ATTENTION_US["bwd", d] * s * heads / (4096 * 8)) if comm is not None else (),
    )


def _mem_softmax(q, kv, h, scale):
    k = kv[:, h * 128:(h + 1) * 128].astype(BF16)
    v = kv[:, MEM_WIDTH + h * 128:MEM_WIDTH + (h + 1) * 128].astype(BF16)
    sc = _dot(q, k, NT) * scale
    e = jnp.exp(sc - jnp.max(sc, axis=-1, keepdims=True))
    return e / jnp.sum(e, axis=-1, keepdims=True), k, v


def memory_attention(proj, qcol, kv, cat, name):
    s = proj.shape[0]
    tb = _tile(s, 512)
    scale = HEAD_DIM ** -0.5

    def body(q_ref, kv_ref, cat_ref, o_ref):
        kv_t = kv_ref[...]
        for h in range(MEM_HEADS):
            p, _, v = _mem_softmax(q_ref[:, h * 128:(h + 1) * 128].astype(BF16), kv_t, h, scale)
            o_ref[:, h * 128:(h + 1) * 128] = _dot(p.astype(BF16), v, NN).astype(BF16)

    return pl.pallas_call(
        body, grid=(s // tb,),
        in_specs=[pl.BlockSpec((tb, MEM_WIDTH), lambda i: (i, qcol // MEM_WIDTH)), pl.BlockSpec(kv.shape, lambda i: (0, 0)), ANY],
        out_specs=pl.BlockSpec((tb, MEM_WIDTH), lambda i: (i, cat.shape[1] // MEM_WIDTH - 1)),
        out_shape=jax.ShapeDtypeStruct(cat.shape, BF16), input_output_aliases={2: 0},
        compiler_params=_params("parallel"), name=name,
    )(proj, kv, cat)


def memory_attention_backward(proj, qcol, kv, dcat, dcol, name):
    s = proj.shape[0]
    tb = _tile(s, 512)
    scale = HEAD_DIM ** -0.5

    def body(q_ref, kv_ref, do_ref, dq_ref, dkv_ref):
        @pl.when(pl.program_id(0) == 0)
        def _():
            dkv_ref[...] = jnp.zeros_like(dkv_ref)

        kv_t = kv_ref[...]
        for h in range(MEM_HEADS):
            cols = slice(h * 128, (h + 1) * 128)
            q = q_ref[:, cols].astype(BF16)
            p, k, v = _mem_softmax(q, kv_t, h, scale)
            do = do_ref[:, cols].astype(BF16)
            dp = _dot(do, v, NT)
            ds = (p * (dp - jnp.sum(dp * p, axis=-1, keepdims=True)) * scale).astype(BF16)
            dq_ref[:, cols] = _dot(ds, k, NN).astype(BF16)
            dkv_ref[:, cols] += _dot(ds, q, TN)
            dkv_ref[:, MEM_WIDTH + h * 128:MEM_WIDTH + (h + 1) * 128] += _dot(p.astype(BF16), do, TN)

    return pl.pallas_call(
        body, grid=(s // tb,),
        in_specs=[pl.BlockSpec((tb, MEM_WIDTH), lambda i: (i, qcol // MEM_WIDTH)), pl.BlockSpec(kv.shape, lambda i: (0, 0)),
                  pl.BlockSpec((tb, MEM_WIDTH), lambda i: (i, dcol // MEM_WIDTH))],
        out_specs=[pl.BlockSpec((tb, MEM_WIDTH), lambda i: (i, 0)), pl.BlockSpec(kv.shape, lambda i: (0, 0))],
        out_shape=[jax.ShapeDtypeStruct((s, MEM_WIDTH), BF16), jax.ShapeDtypeStruct(kv.shape, F32)],
        compiler_params=_params("arbitrary"), name=name,
    )(proj, kv, dcat)


_SQRT_HALF = math.sqrt(0.5)
_INV_SQRT_2PI = 1.0 / math.sqrt(2.0 * math.pi)


def _gelu(x):
    return 0.5 * x * (1.0 + lax.erf(x * _SQRT_HALF))


def _gelu_grad(x):
    return 0.5 * (1.0 + lax.erf(x * _SQRT_HALF)) + x * (_INV_SQRT_2PI * jnp.exp(-0.5 * x * x))


def _gmlp_specs(s, wd, groups, tb):
    half = lambda c: pl.BlockSpec((tb, wd), lambda i: (i, c))
    ws_spec = pl.BlockSpec((groups, 128, 128), lambda i: (0, 0, 0))
    bs_spec = pl.BlockSpec((groups, 128, 1), lambda i: (0, 0, 0))
    vec = pl.BlockSpec((1, wd), lambda i: (0, 0))
    return half, ws_spec, bs_spec, vec


def _vnorm(zv, g, b):
    mu = jnp.mean(zv, axis=-1, keepdims=True)
    var = jnp.mean(jnp.square(zv - mu), axis=-1, keepdims=True)
    return (zv - mu) * lax.rsqrt(var + LN_EPS) * g + b


def gmlp_forward(proj, ws, bs, vg, vb, name):
    s = proj.shape[0]
    groups = ws.shape[0]
    wd = groups * 128
    tb = _tile(s, 512)
    half, ws_spec, bs_spec, vec = _gmlp_specs(s, wd, groups, tb)

    def body(pu_ref, pv_ref, ws_ref, bs_ref, vg_ref, vb_ref, o_ref, vn_ref):
        causal, _ = _causal_masks()
        vn_ref[...] = _vnorm(_gelu(pv_ref[...]), vg_ref[...], vb_ref[...]).astype(BF16)
        for g in range(groups):
            cols = slice(g * 128, (g + 1) * 128)
            wm = jnp.where(causal, ws_ref[g], 0.0).astype(BF16)
            for c in range(tb // 128):
                rws = slice(c * 128, (c + 1) * 128)
                sg = _dot(wm, vn_ref[rws, cols], NN) + bs_ref[g]
                o_ref[rws, cols] = (_gelu(pu_ref[rws, cols]) * sg).astype(BF16)

    return pl.pallas_call(
        body, grid=(s // tb,),
        in_specs=[half(0), half(1), ws_spec, bs_spec, vec, vec], out_specs=pl.BlockSpec((tb, wd), lambda i: (i, 0)),
        out_shape=jax.ShapeDtypeStruct((s, wd + MEM_WIDTH), BF16),
        scratch_shapes=[pltpu.VMEM((tb, wd), BF16)],
        compiler_params=_params("parallel"), name=name,
    )(proj, proj, ws, bs.reshape(groups, 128, 1), vg.reshape(1, wd), vb.reshape(1, wd))


def gmlp_backward(proj, ws, bs, vg, vb, dcat, name):
    s = proj.shape[0]
    groups = ws.shape[0]
    wd = groups * 128
    tb = _tile(s, 512)
    half, ws_spec, bs_spec, vec = _gmlp_specs(s, wd, groups, tb)

    def body(pu_ref, pv_ref, ws_ref, bs_ref, vg_ref, vb_ref, do_ref, dpu_ref, dpv_ref, dws_ref, dbs_ref, dvg_ref, dvb_ref,
             vn_ref, dvn_ref):
        @pl.when(pl.program_id(0) == 0)
        def _():
            dws_ref[...] = jnp.zeros_like(dws_ref)
            dbs_ref[...] = jnp.zeros_like(dbs_ref)
            dvg_ref[...] = jnp.zeros_like(dvg_ref)
            dvb_ref[...] = jnp.zeros_like(dvb_ref)

        causal, _ = _causal_masks()
        pv = pv_ref[...]
        zv = _gelu(pv)
        vn_ref[...] = _vnorm(zv, vg_ref[...], vb_ref[...]).astype(BF16)
        for g in range(groups):
            cols = slice(g * 128, (g + 1) * 128)
            wm = jnp.where(causal, ws_ref[g], 0.0).astype(BF16)
            dws_g = jnp.zeros((128, 128), F32)
            dbs_g = jnp.zeros((128, 1), F32)
            for c in range(tb // 128):
                rws = slice(c * 128, (c + 1) * 128)
                vn = vn_ref[rws, cols]
                pu = pu_ref[rws, cols]
                do = do_ref[rws, cols]
                sg = _dot(wm, vn, NN) + bs_ref[g]
                dpu_ref[rws, cols] = (do * sg * _gelu_grad(pu)).astype(BF16)
                dsg = do * _gelu(pu)
                dsg_b = dsg.astype(BF16)
                dws_g += _dot(dsg_b, vn, NT)
                dbs_g += jnp.sum(dsg, axis=-1, keepdims=True)
                dvn_ref[rws, cols] = _dot(wm, dsg_b, TN)
            dws_ref[g] += jnp.where(causal, dws_g, 0.0)
            dbs_ref[g] += dbs_g
        dvn = dvn_ref[...]
        dzv, dvg = _ln_bwd_tile(zv, vg_ref[...], dvn)
        dvg_ref[...] += dvg
        dvb_ref[...] += jnp.sum(dvn, axis=0, keepdims=True)
        dpv_ref[...] = (dzv * _gelu_grad(pv)).astype(BF16)

    row = pl.BlockSpec((tb, wd), lambda i: (i, 0))
    dpu, dpv, dws, dbs, dvg, dvb = pl.pallas_call(
        body, grid=(s // tb,),
        in_specs=[half(0), half(1), ws_spec, bs_spec, vec, vec, row],
        out_specs=[row, row, ws_spec, bs_spec, vec, vec],
        out_shape=[jax.ShapeDtypeStruct((s, wd), BF16), jax.ShapeDtypeStruct((s, wd), BF16),
                   jax.ShapeDtypeStruct((groups, 128, 128), F32), jax.ShapeDtypeStruct((groups, 128, 1), F32),
                   jax.ShapeDtypeStruct((1, wd), F32), jax.ShapeDtypeStruct((1, wd), F32)],
        scratch_shapes=[pltpu.VMEM((tb, wd), BF16), pltpu.VMEM((tb, wd), F32)],
        compiler_params=_params("arbitrary"), name=name,
    )(proj, proj, ws, bs.reshape(groups, 128, 1), vg.reshape(1, wd), vb.reshape(1, wd), dcat)
    return dpu, dpv, dws, dbs.reshape(groups, 128), dvg, dvb


def loss_head(y, target, name):
    s, w = y.shape
    tb = _tile(s, 256)
    row = pl.BlockSpec((tb, w), lambda i: (i, 0))
    nsteps = s // tb

    def body(y_ref, t_ref, loss_ref, dy_ref, acc_ref):
        i = pl.program_id(0)

        @pl.when(i == 0)
        def _():
            acc_ref[...] = jnp.zeros_like(acc_ref)

        err = y_ref[...] - t_ref[...]
        dy_ref[...] = err / w
        acc_ref[...] += jnp.sum(jnp.mean(jnp.square(err), axis=-1, keepdims=True), axis=0, keepdims=True)

        @pl.when(i == nsteps - 1)
        def _():
            loss_ref[...] = jnp.broadcast_to(0.5 * acc_ref[...], loss_ref.shape)

    return pl.pallas_call(
        body, grid=(nsteps,), in_specs=[row, row],
        out_specs=[pl.BlockSpec((8, 128), lambda i: (0, 0)), row],
        out_shape=[jax.ShapeDtypeStruct((8, 128), F32), jax.ShapeDtypeStruct((s, w), F32)],
        scratch_shapes=[pltpu.VMEM((1, 1), F32)],
        compiler_params=_params("arbitrary"), name=name,
    )(y, target)


PARTS_WINDOW_ELEMS = 1024 * 1024


def _row_tile(r, c, budget):
    if r * c <= budget or r % 16:
        return r
    fits = [t for t in range(16, r, 16) if r % t == 0 and t * c <= budget]
    return max(fits) if fits else 16


def _sum_in_device_order(p_ref):
    g = p_ref[0].astype(F32)
    for j in range(1, p_ref.shape[0]):
        g = g + p_ref[j].astype(F32)
    return g


def _adamw_update(g, w, m, v):
    nm = ADAM_B1 * m + (1.0 - ADAM_B1) * g
    nv = ADAM_B2 * v + (1.0 - ADAM_B2) * jnp.square(g)
    m_hat = nm / (1.0 - ADAM_B1 ** ADAM_STEP)
    v_hat = nv / (1.0 - ADAM_B2 ** ADAM_STEP)
    return -ADAM_LR * (m_hat / (jnp.sqrt(v_hat) + ADAM_EPS) + ADAM_WD * w), nm, nv


SUM_BYTES_PER_US = 1.7e6


def sum_parts(chunks, name, comm=None):
    p, r, c = chunks[0].shape
    tr = _row_tile(r, c, PARTS_WINDOW_ELEMS // len(chunks))
    nb = r // tr

    def chunk_spec(q):
        return pl.BlockSpec((p, tr, c), lambda ch, i: (0, jnp.where(ch == q, i, 0), 0))

    def body(*refs):
        for q in range(len(chunks)):
            @pl.when(pl.program_id(0) == q)
            def _():
                refs[-1][...] = _sum_in_device_order(refs[q])

    host_us = len(chunks) * r * c * (p * chunks[0].dtype.itemsize + 4) / SUM_BYTES_PER_US
    return _call(
        body, grid=(len(chunks), nb), in_specs=[chunk_spec(q) for q in range(len(chunks))],
        out_specs=[pl.BlockSpec((tr, c), lambda ch, i: (ch * nb + i, 0))],
        out_shape=[jax.ShapeDtypeStruct((len(chunks) * r, c), F32)], scratch_shapes=[],
        args=list(chunks), name=name, semantics=("arbitrary", "arbitrary"),
        jobs=comm.take(host_us) if comm is not None else (),
    )[0]


ELEMENTWISE_BYTES_PER_US = 2.0e6


def adamw(parts, w, m, v, name, emit_grad=True, comm=None):
    p, r, c = parts.shape
    tr = _row_tile(r, c, 160 * 1024)
    n_out = 4 if emit_grad else 3

    def body(p_ref, w_ref, m_ref, v_ref, *out_refs):
        g = _sum_in_device_order(p_ref)
        vals = _adamw_update(g, w_ref[...], m_ref[...], v_ref[...])
        for o_ref, val in zip(out_refs, ((g,) + vals) if emit_grad else vals):
            o_ref[...] = val

    row = pl.BlockSpec((tr, c), lambda i: (i, 0))
    host_us = (parts.size * parts.dtype.itemsize + (3 + n_out) * 4 * r * c) / ELEMENTWISE_BYTES_PER_US
    return _call(
        body, grid=(r // tr,), in_specs=[pl.BlockSpec((p, tr, c), lambda i: (0, i, 0)), row, row, row],
        out_specs=[row] * n_out, out_shape=[jax.ShapeDtypeStruct((r, c), F32)] * n_out, scratch_shapes=[],
        args=[parts, w, m, v], name=name, semantics=("parallel",), jobs=comm.take(host_us) if comm is not None else (),
    )


def kernel(x, mem, w_in_a, w_in_b, w_s, b_s, vnorm_g, vnorm_b, w_mem_kv, w_out, ln1_g, ln1_b, w_ff1, w_ff2, ln2_g, ln2_b, loss_target, m_w_in_a, m_w_in_b, m_w_s, m_b_s, m_vnorm_g, m_vnorm_b, m_w_mem_kv, m_w_out, m_ln1_g, m_ln1_b, m_w_ff1, m_w_ff2, m_ln2_g, m_ln2_b, v_w_in_a, v_w_in_b, v_w_s, v_b_s, v_vnorm_g, v_vnorm_b, v_w_mem_kv, v_w_out, v_ln1_g, v_ln1_b, v_w_ff1, v_w_ff2, v_ln2_g, v_ln2_b):
    depth = w_ff1.shape[0]
    groups = w_s.shape[1]
    mix_w = groups * HEAD_DIM
    n_b = vnorm_g.shape[0]
    alpha = (2.0 * depth) ** 0.25
    me = 4 * lax.axis_index("x") + 2 * lax.axis_index("y") + lax.axis_index("c")
    x0 = x[0]
    mem_b = mem[0].astype(BF16)
    target = loss_target[0]

    comm = Exchanges()
    weight_jobs = []
    for i in range(depth):
        w_in = (w_in_a if i % 2 == 0 else w_in_b)[i // 2]
        pieces = lambda t: jnp.split(t.astype(BF16), WEIGHT_PIECES, axis=1)
        small = [w_mem_kv[i].astype(BF16), w_out[i].T.astype(BF16)] + ([jnp.concatenate([vnorm_g, vnorm_b], axis=0)] if i == 0 else [])
        weight_jobs.append({
            "in": [GatherJob([t], f"g{i}in{q}") for q, t in enumerate(pieces(w_in.T))],
            "out": GatherJob(small, f"g{i}out"),
            "ff1": [GatherJob([t], f"g{i}ff1{q}") for q, t in enumerate(pieces(w_ff1[i].T))],
            "ff2": [GatherJob([t], f"g{i}ff2{q}") for q, t in enumerate(pieces(w_ff2[i]))]})
        for job in [*weight_jobs[i]["in"], weight_jobs[i]["out"], *weight_jobs[i]["ff1"], *weight_jobs[i]["ff2"]]:
            comm.submit(job)

    def gathered(*jobs):
        comm.require(*jobs)
        return [g.reshape(N_DEV * g.shape[1], g.shape[2]) for job in jobs for g in job.outs]


    saved = []
    weights = []
    xf, xb = x0, x0.astype(BF16)
    for i in range(depth):
        j = i // 2
        w_in_t = gathered(*weight_jobs[i]["in"])
        proj = matmul(xb, w_in_t, "nt", name=f"in_proj_{i % 2}", comm=comm)
        if i % 2 == 0:
            os_, ls_ = [], []
            for g, d in enumerate(A_DILATIONS):
                o_g, l_g = attention_forward(proj, g, d, groups, name=f"attn_fwd_d{d}", comm=comm)
                os_.append(o_g)
                ls_.append(l_g)
            o, cat, lse = attention_combine(os_, ls_, name="attn_combine")
            qcol = 9 * mix_w
            extra = (o, lse)
        else:
            cat = gmlp_forward(proj, w_s[j], b_s[j], vg_full[j], vb_full[j], name="gmlp_fwd")
            qcol = 2 * mix_w
            extra = ()
        w_kv, w_o_t = gathered(weight_jobs[i]["out"])[:2]
        if i == 0:
            vnorm = jnp.transpose(weight_jobs[0]["out"].outs[2], (1, 0, 2)).reshape(2 * n_b, mix_w)
            vg_full, vb_full = vnorm[:n_b], vnorm[n_b:]
        kv = matmul(mem_b, w_kv, "nn", name="mem_kv")
        cat = memory_attention(proj, qcol, kv, cat, name=f"mem_attn_{i % 2}")
        r1, x1, x1b = matmul(cat, w_o_t, "nt", name="out_proj_ln1", comm=comm, tiles=(256, w_o_t.shape[0], 2560),
                             extras=(xf, ln1_g[i][None], ln1_b[i][None]), out_dtypes=(F32, F32, BF16),
                             epilogue=lambda acc, x_t, g_t, b_t: _residual_ln(alpha * x_t + acc, g_t, b_t))
        w_1_t = gathered(*weight_jobs[i]["ff1"])
        h, hid = matmul(x1b, w_1_t, "nt", name="ff1", out_dtypes=(F32, BF16), comm=comm,
                        epilogue=lambda acc: (acc, jnp.square(jnp.maximum(acc, 0.0))))
        w_2 = gathered(*weight_jobs[i]["ff2"])
        y2 = matmul(hid, w_2, "nn", name="ff2", comm=comm)
        r2, x2, x2b = ln_residual(x1, y2, ln2_g[i], ln2_b[i], alpha, name="ln2", comm=comm)
        weights.append((w_in_t, w_o_t, w_1_t, w_2, w_kv))
        saved.append((xb, proj, kv, extra, qcol, cat, r1, x1b, h, hid, r2))
        xf, xb = x2, x2b

    loss_tile, dx = loss_head(xf, target, name="loss_head")
    loss = lax.psum(loss_tile[0, 0], ("x", "y", "c"))

    def exchange(dw, tag):
        rows, cols = dw.shape[0] // N_DEV, dw.shape[1]
        p = dw.reshape(N_DEV, rows, cols)
        whole_us = ExchangeJob.US_PER_MB * dw.size * dw.dtype.itemsize / 1e6
        counts = [q for q in range(1, rows // 16 + 1) if (rows // 16) % q == 0]
        n_chunks = next((q for q in counts if whole_us <= ExchangeJob.CHUNK_US * q), counts[-1])
        step = rows // n_chunks
        return [comm.submit(ExchangeJob(p, q * step, step, f"x{tag}{q}")) for q in range(n_chunks)]

    recv = {n: [None] * depth for n in ("w_in", "w_mem_kv", "w_out", "w_ff1", "w_ff2")}
    small = {n: [None] * depth for n in ("ln1_g", "ln1_b", "ln2_g", "ln2_b")}
    small_b = {n: [None] * n_b for n in ("w_s", "b_s", "vnorm_g", "vnorm_b")}
    for i in reversed(range(depth)):
        w_in_t, w_o_t, w_1_t, w_2, w_kv = weights[i]
        xb, proj, kv, extra, qcol, cat, r1, x1b, h, hid, r2 = saved[i]
        j = i // 2
        dr2, dr2b, small["ln2_g"][i], small["ln2_b"][i] = ln_backward(r2, ln2_g[i], dx, name="ln2_bwd", comm=comm)
        dh = matmul(dr2b, w_2, "nt", name="ff2_dx", extras=(h,), out_dtypes=(BF16,), comm=comm,
                    epilogue=lambda acc, h_t: (acc * (2.0 * jnp.maximum(h_t, 0.0)),))
        recv["w_ff2"][i] = exchange(matmul(hid, dr2b, "tn", name="ff2_dw", out_dtypes=(BF16,), comm=comm), f"ff2{i}")
        dx1 = matmul(dh, w_1_t, "nn", name="ff1_dx", extras=(dr2,), comm=comm, epilogue=lambda acc, res: (acc + alpha * res,))
        recv["w_ff1"][i] = exchange(matmul(dh, x1b, "tn", name="ff1_dw", out_dtypes=(BF16,), comm=comm), f"ff1{i}")
        dr1, dr1b, small["ln1_g"][i], small["ln1_b"][i] = ln_backward(r1, ln1_g[i], dx1, name="ln1_bwd", comm=comm)
        dcat = matmul(dr1b, w_o_t, "nn", name="out_proj_dx", comm=comm)
        recv["w_out"][i] = exchange(matmul(dr1b, cat, "tn", name="out_proj_dw", out_dtypes=(BF16,), comm=comm), f"out{i}")
        dq_mem, dkv = memory_attention_backward(proj, qcol, kv, dcat, mix_w, name=f"mem_attn_bwd_{i % 2}")
        recv["w_mem_kv"][i] = exchange(matmul(mem_b, dkv, "tn", name="mem_kv_dw", out_dtypes=(BF16,)), f"kv{i}")
        if i % 2 == 0:
            o, lse = extra
            pieces = []
            for g, d in enumerate(A_DILATIONS):
                pieces += [t.astype(BF16) for t in attention_backward(proj, o, lse, dcat, g, d, groups, name=f"attn_bwd_d{d}", comm=comm)]
            dproj = jnp.concatenate(pieces + [dq_mem], axis=1)
        else:
            dpu, dpv, small_b["w_s"][j], small_b["b_s"][j], small_b["vnorm_g"][j], small_b["vnorm_b"][j] = gmlp_backward(
                proj, w_s[j], b_s[j], vg_full[j], vb_full[j], dcat, name="gmlp_bwd")
            dproj = jnp.concatenate([dpu, dpv, dq_mem], axis=1)
        recv["w_in"][i] = exchange(matmul(dproj, xb, "tn", name=f"in_proj_dw_{i % 2}", out_dtypes=(BF16,), comm=comm), f"in{i}")
        dx = matmul(dproj, w_in_t, "nn", name=f"in_proj_dx_{i % 2}", extras=(dr1,), comm=comm, epilogue=lambda acc, res: (acc + alpha * res,))
    grad_x = dx[None]

    flat_small = jnp.concatenate(
        [jnp.concatenate(small[n], axis=0).reshape(-1) for n in ("ln1_g", "ln1_b", "ln2_g", "ln2_b")]
        + [jnp.stack(small_b[n]).reshape(-1) for n in ("w_s", "b_s", "vnorm_g", "vnorm_b")])
    pad = (-flat_small.size) % 1024
    small_job = comm.submit(GatherJob([jnp.pad(flat_small, (0, pad)).reshape(-1, 1024)], "gsmall"))

    results = {}

    def received(name, i):
        return comm.require(*recv[name][i])

    def update(name, layers, w_, m_, v_, transposed):
        sums = [sum_parts(received(name, i), name=f"sum_{name}_{i % 2}", comm=comm) for i in layers]
        grad = jnp.stack([t.T if transposed else t for t in sums])
        flat = lambda t: t.reshape(-1, t.shape[-1])
        upd = adamw(flat(grad)[None], flat(w_), flat(m_), flat(v_), name=f"adamw_{name}_{layers[0] % 2}", emit_grad=False)
        return [grad] + [t.reshape(w_.shape) for t in upd]

    results["w_ff2"] = update("w_ff2", list(range(depth)), w_ff2, m_w_ff2, v_w_ff2, False)
    results["w_ff1"] = update("w_ff1", list(range(depth)), w_ff1, m_w_ff1, v_w_ff1, True)
    results["w_out"] = update("w_out", list(range(depth)), w_out, m_w_out, v_w_out, True)
    results["w_mem_kv"] = update("w_mem_kv", list(range(depth)), w_mem_kv, m_w_mem_kv, v_w_mem_kv, False)
    results["w_in_b"] = update("w_in", list(range(1, depth, 2)), w_in_b, m_w_in_b, v_w_in_b, True)
    comm.require(*[job for i in range(0, depth, 2) for job in recv["w_in"][i]], small_job)
    results["w_in_a"] = update("w_in", list(range(0, depth, 2)), w_in_a, m_w_in_a, v_w_in_a, True)

    gathered_small = comm.require(small_job)[0].reshape(N_DEV, -1)
    at = 0

    def take(shape):
        nonlocal at
        size = math.prod(shape)
        at += size
        return gathered_small[:, at - size:at].reshape((N_DEV,) + shape)

    for n, w_, m_, v_ in (("ln1_g", ln1_g, m_ln1_g, v_ln1_g), ("ln1_b", ln1_b, m_ln1_b, v_ln1_b),
                          ("ln2_g", ln2_g, m_ln2_g, v_ln2_g), ("ln2_b", ln2_b, m_ln2_b, v_ln2_b)):
        results[n] = adamw(take(w_.shape), w_, m_, v_, name="adamw_ln")
    lanes = lambda t: t.reshape(-1, 128)
    results["w_s"] = [t.reshape(w_s.shape) for t in adamw(take((w_s.size // 128, 128)), lanes(w_s), lanes(m_w_s), lanes(v_w_s), name="adamw_w_s")]
    results["b_s"] = [t.reshape(b_s.shape) for t in adamw(take((b_s.size // 128, 128)), lanes(b_s), lanes(m_b_s), lanes(v_b_s), name="adamw_b_s")]
    for n, w_, m_, v_ in (("vnorm_g", vnorm_g, m_vnorm_g, v_vnorm_g), ("vnorm_b", vnorm_b, m_vnorm_b, v_vnorm_b)):
        parts = lax.dynamic_slice_in_dim(take((n_b, mix_w)), me * w_.shape[1], w_.shape[1], axis=2)
        results[n] = adamw(parts, w_, m_, v_, name="adamw_vnorm")

    order =("w_in_a", "w_in_b", "w_s", "b_s", "vnorm_g", "vnorm_b", "w_mem_kv", "w_out", "ln1_g", "ln1_b", "w_ff1", "w_ff2", "ln2_g", "ln2_b")
    return (loss, grad_x, *[results[n][0] for n in order], *[results[n][1] for n in order],
            *[results[n][2] for n in order], *[results[n][3] for n in order])
```
